```python
import jax
import jax.numpy as jnp
from jax import lax
import numpy as np

D_MODEL = 2048
BATCH = 4
SEQ = 4096
DEPTH = 1

EPS = 1e-6
ROPE_THETA = 10000.0
NEG = -1e30
BIG = 1e9

HG_HEADS = 8
HG_KDIM = 128
HG_VDIM = 128
HG_QK = HG_HEADS * HG_KDIM
HG_WIDTH = HG_HEADS * HG_VDIM
HG_CHUNK = 64

NSA_HEADS = 16
NSA_KV_GROUPS = 4
NSA_HEAD_DIM = 64
NSA_WIDTH = NSA_HEADS * NSA_HEAD_DIM
NSA_KV = NSA_KV_GROUPS * NSA_HEAD_DIM
CMP_BLOCK = 32
CMP_STRIDE = 16
CMP_HIDDEN = 256
SLC_BLOCK = 64
SLC_TOPK = 16
SLC_LOCAL = 2
WIN = 512
NSA_Q_BLOCK = 64

D_MIX = HG_WIDTH + NSA_WIDTH
IN_SIZES = (HG_QK, HG_QK, HG_WIDTH, HG_WIDTH, NSA_WIDTH, NSA_KV, NSA_KV, NSA_KV, NSA_KV, NSA_KV, NSA_KV, 3 * NSA_HEADS)
IN_COLS = sum(IN_SIZES)

MOE_GROUPS = 4
MOE_EXPERTS_PER_GROUP = 8
N_EXPERTS = MOE_GROUPS * MOE_EXPERTS_PER_GROUP
MOE_TOPK = 2
EXPERT_FF = 512
MOE_BLOCK = 128

kernel_name = 'hymba_hgrn2_nsa_hier_moe'


def rmsnorm(x, gain):
    xf = x.astype(jnp.float32)
    y = xf * lax.rsqrt(jnp.mean(xf * xf, axis=-1, keepdims=True) + EPS)
    return (y * gain.astype(jnp.float32)).astype(x.dtype)


def rope_tables(seq, dim):
    inv = 1.0 / (ROPE_THETA ** (jnp.arange(0, dim, 2, dtype=jnp.float32) / dim))
    ang = jnp.arange(seq, dtype=jnp.float32)[:, None] * inv[None, :]
    return jnp.cos(ang), jnp.sin(ang)


def apply_rope(x, cos, sin):
    x1, x2 = jnp.split(x.astype(jnp.float32), 2, axis=-1)
    c, s = cos[None, :, None, :], sin[None, :, None, :]
    return jnp.concatenate([x1 * c - x2 * s, x2 * c + x1 * s], axis=-1).astype(x.dtype)


def hgrn2_mixer(q, f_raw, i, g, lb, out_gain):
    B, S, _ = q.shape
    C = HG_CHUNK
    nc = S // C
    f = lb + (1.0 - lb) * jax.nn.sigmoid(f_raw.astype(jnp.float32))
    log_f = jnp.log(f)
    k = 1.0 - f

    def chunks(t, d):
        return t.astype(jnp.float32).reshape(B, nc, C, HG_HEADS, d).transpose(1, 0, 3, 2, 4)

    xs = (chunks(q, HG_KDIM) * HG_KDIM ** -0.5, chunks(k, HG_KDIM), chunks(i, HG_VDIM), chunks(log_f, HG_KDIM))
    causal = jnp.tril(jnp.ones((C, C), dtype=bool))[:, :, None]

    def step(state, inp):
        qb, kb, vb, lfb = inp
        bcum = jnp.cumsum(lfb, axis=2)
        diff = bcum[:, :, :, None, :] - bcum[:, :, None, :, :]
        decay = jnp.exp(jnp.where(causal, diff, -jnp.inf))
        scores = jnp.einsum('bhtk,bhtsk->bhts', qb, decay * kb[:, :, None, :, :])
        o = jnp.einsum('bhts,bhsv->bhtv', scores, vb) + jnp.einsum('bhtk,bhkv->bhtv', qb * jnp.exp(bcum), state)
        blast = bcum[:, :, -1:, :]
        state = jnp.exp(blast[:, :, 0, :])[..., None] * state + jnp.einsum('bhsk,bhsv->bhkv', kb * jnp.exp(blast - bcum), vb)
        return state, o

    s0 = jnp.zeros((B, HG_HEADS, HG_KDIM, HG_VDIM), jnp.float32)
    _, ys = lax.scan(step, s0, xs)
    o = ys.transpose(1, 0, 3, 2, 4).reshape(B, S, HG_HEADS, HG_VDIM)
    gate = jax.nn.silu(g.astype(jnp.float32)).reshape(B, S, HG_HEADS, HG_VDIM)
    o = rmsnorm(o, out_gain) * gate
    return o.reshape(B, S, HG_WIDTH).astype(q.dtype)


def compress_blocks(kv, pos, w1, w2):
    B, S, G, dk = kv.shape
    n_cmp = (S - CMP_BLOCK) // CMP_STRIDE + 1
    idx = np.arange(n_cmp)[:, None] * CMP_STRIDE + np.arange(CMP_BLOCK)[None, :]
    blk = kv[:, idx] + pos[None, None, :, None, :]
    flat = blk.transpose(0, 1, 3, 2, 4).reshape(B, n_cmp, G, CMP_BLOCK * dk)
    return jax.nn.gelu(flat @ w1) @ w2


def nsa_mixer(q, k_cmp, v_cmp, k_slc, v_slc, k_win, v_win, gate_raw,
              cmp_pos_k, cmp_w1_k, cmp_w2_k, cmp_pos_v, cmp_w1_v, cmp_w2_v, cos, sin):
    B, S, _ = q.shape
    H, G, dk = NSA_HEADS, NSA_KV_GROUPS, NSA_HEAD_DIM
    R = H // G
    Qb = NSA_Q_BLOCK
    scale = dk ** -0.5
    q = q.reshape(B, S, H, dk)
    q_rot = apply_rope(q, cos, sin)
    k_cmp, v_cmp, v_slc, v_win = [t.reshape(B, S, G, dk) for t in (k_cmp, v_cmp, v_slc, v_win)]
    k_slc = apply_rope(k_slc.reshape(B, S, G, dk), cos, sin)
    k_win = apply_rope(k_win.reshape(B, S, G, dk), cos, sin)

    kc = compress_blocks(k_cmp, cmp_pos_k, cmp_w1_k, cmp_w2_k)
    vc = compress_blocks(v_cmp, cmp_pos_v, cmp_w1_v, cmp_w2_v)
    n_cmp = kc.shape[1]
    cmp_end = jnp.arange(n_cmp) * CMP_STRIDE + CMP_BLOCK - 1

    n_slc = S // SLC_BLOCK
    topk = min(SLC_TOPK, n_slc)
    ci = np.arange(n_cmp)[:, None]
    sj = np.arange(n_slc)[None, :]
    overlap = (ci * CMP_STRIDE < (sj + 1) * SLC_BLOCK) & (ci * CMP_STRIDE + CMP_BLOCK > sj * SLC_BLOCK)
    agg = jnp.asarray(overlap, jnp.float32)
    kb = k_slc.reshape(B, n_slc, SLC_BLOCK, G, dk).transpose(0, 3, 1, 2, 4)
    vb = v_slc.reshape(B, n_slc, SLC_BLOCK, G, dk).transpose(0, 3, 1, 2, 4)
    blk_j = jnp.arange(n_slc)
    b_ix = jnp.arange(B)[:, None, None, None]
    g_ix = jnp.arange(G)[None, :, None, None]

    kw = jnp.pad(k_win, ((0, 0), (WIN, 0), (0, 0), (0, 0)))
    vw = jnp.pad(v_win, ((0, 0), (WIN, 0), (0, 0), (0, 0)))
    gates = jax.nn.sigmoid(gate_raw.astype(jnp.float32)).reshape(B, S, H, 3)

    def heads_by_group(t):
        return t.reshape(B, Qb, G, R, t.shape[-1]).transpose(0, 2, 3, 1, 4)

    def block(qs):
        t = qs + jnp.arange(Qb)
        qr = heads_by_group(lax.dynamic_slice_in_dim(q, qs, Qb, 1)).astype(jnp.float32)
        qo = heads_by_group(lax.dynamic_slice_in_dim(q_rot, qs, Qb, 1)).astype(jnp.float32)

        m_c = cmp_end[None, :] <= t[:, None]
        s_c = jnp.einsum('bgrqd,bcgd->bgrqc', qr, kc) * scale
        p_c = jax.nn.softmax(jnp.where(m_c, s_c, NEG), axis=-1) * m_c
        o_c = jnp.einsum('bgrqc,bcgd->bgrqd', p_c, vc)

        imp = jnp.einsum('bgrqc,cj->bgqj', p_c, agg)
        cur = t // SLC_BLOCK
        dj = cur[:, None] - blk_j[None, :]
        forced = (blk_j[None, :] == 0) | ((dj >= 0) & (dj < SLC_LOCAL))
        causal_blk = blk_j[None, :] * SLC_BLOCK <= t[:, None]
        imp = jnp.where(forced, BIG, jnp.where(causal_blk, imp, -BIG))
        _, idx = lax.top_k(imp, topk)
        kg = kb[b_ix, g_ix, idx]
        vg = vb[b_ix, g_ix, idx]
        pos = idx[..., None] * SLC_BLOCK + jnp.arange(SLC_BLOCK)
        m_s = (pos <= t[None, None, :, None, None])[:, :, None]
        s_s = jnp.einsum('bgrqd,bgqnld->bgrqnl', qo, kg) * scale
        s_s = jnp.where(m_s, s_s, NEG).reshape(B, G, R, Qb, topk * SLC_BLOCK)
        p_s = jax.nn.softmax(s_s, axis=-1).reshape(B, G, R, Qb, topk, SLC_BLOCK)
        o_s = jnp.einsum('bgrqnl,bgqnld->bgrqd', p_s, vg)

        kwin = lax.dynamic_slice_in_dim(kw, qs, WIN + Qb, 1)
        vwin = lax.dynamic_slice_in_dim(vw, qs, WIN + Qb, 1)
        kpos = qs - WIN + jnp.arange(WIN + Qb)
        dlt = t[:, None] - kpos[None, :]
        m_w = (dlt >= 0) & (dlt < WIN) & (kpos[None, :] >= 0)
        s_w = jnp.einsum('bgrqd,bkgd->bgrqk', qo, kwin) * scale
        p_w = jax.nn.softmax(jnp.where(m_w, s_w, NEG), axis=-1)
        o_w = jnp.einsum('bgrqk,bkgd->bgrqd', p_w, vwin)

        gq = heads_by_group(lax.dynamic_slice_in_dim(gates, qs, Qb, 1))
        o = gq[..., 0:1] * o_c + gq[..., 1:2] * o_s + gq[..., 2:3] * o_w
        return o.transpose(0, 3, 1, 2, 4).reshape(B, Qb, H * dk)

    out = lax.map(block, jnp.arange(S // Qb) * Qb)
    return out.transpose(1, 0, 2, 3).reshape(B, S, NSA_WIDTH).astype(q.dtype)


def hier_moe(x, w_group, b_group, w_router, b_router, w_gate, w_up, w_down):
    B, S, D = x.shape
    n_tok = B * S
    xt = x.reshape(n_tok, D)
    g_prob = jax.nn.softmax((xt @ w_group).astype(jnp.float32) + b_group.astype(jnp.float32), axis=-1)
    g_sel = jnp.argmax(g_prob, axis=-1)
    g_w = jnp.take_along_axis(g_prob, g_sel[:, None], axis=-1)
    e_logits = ((xt @ w_router).astype(jnp.float32) + b_router.astype(jnp.float32)).reshape(n_tok, MOE_GROUPS, MOE_EXPERTS_PER_GROUP)
    e_in = jnp.take_along_axis(e_logits, g_sel[:, None, None], axis=1)[:, 0]
    e_val, e_loc = lax.top_k(e_in, MOE_TOPK)
    e_w = jax.nn.softmax(e_val, axis=-1) * g_w
    e_id = g_sel[:, None] * MOE_EXPERTS_PER_GROUP + e_loc

    n_asg = n_tok * MOE_TOPK
    flat_e = e_id.reshape(n_asg).astype(jnp.int32)
    flat_t = jnp.repeat(jnp.arange(n_tok, dtype=jnp.int32), MOE_TOPK)
    flat_w = e_w.reshape(n_asg)
    order = jnp.argsort(flat_e)
    se, st, sw = flat_e[order], flat_t[order], flat_w[order]
    counts = jnp.bincount(flat_e, length=N_EXPERTS)
    start = jnp.cumsum(counts) - counts
    padded = (counts + MOE_BLOCK - 1) // MOE_BLOCK * MOE_BLOCK
    pend = jnp.cumsum(padded)
    pstart = pend - padded
    dest = pstart[se] + (jnp.arange(n_asg) - start[se])
    n_blocks = (n_asg + MOE_BLOCK - 1) // MOE_BLOCK + N_EXPERTS
    buf_tok = jnp.zeros((n_blocks * MOE_BLOCK,), jnp.int32).at[dest].set(st)
    buf_w = jnp.zeros((n_blocks * MOE_BLOCK,), jnp.float32).at[dest].set(sw)
    block_e = jnp.minimum(jnp.searchsorted(pend, jnp.arange(n_blocks) * MOE_BLOCK, side='right'), N_EXPERTS - 1)

    def expert_block(args):
        tok, w, e = args
        xb = xt[tok]
        hid = jax.nn.silu(xb @ w_gate[e]) * (xb @ w_up[e])
        return (hid @ w_down[e]) * w[:, None].astype(xb.dtype)

    ys = lax.map(expert_block, (buf_tok.reshape(n_blocks, MOE_BLOCK), buf_w.reshape(n_blocks, MOE_BLOCK), block_e))
    out = jnp.zeros((n_tok, D), x.dtype).at[buf_tok].add(ys.reshape(-1, D))
    return out.reshape(B, S, D)


def setup_inputs(seed: int = 0) -> dict:
    key = jax.random.key(seed)
    ks = jax.random.split(key, 24)
    L = DEPTH

    def nrm(k, shape, scale):
        return jax.random.normal(k, shape, jnp.float32) * scale

    return {
        'x': nrm(ks[0], (BATCH, SEQ, D_MODEL), 1.0),
        'attn_norm': 1.0 + nrm(ks[1], (L, D_MODEL), 0.02),
        'w_in': nrm(ks[2], (L, D_MODEL, IN_COLS), D_MODEL ** -0.5),
        'hg_lb_logits': nrm(ks[3], (L + 1, HG_QK), 0.5),
        'hg_out_norm': 1.0 + nrm(ks[4], (L, HG_VDIM), 0.02),
        'cmp_pos_k': nrm(ks[5], (L, CMP_BLOCK, NSA_HEAD_DIM), 0.1),
        'cmp_w1_k': nrm(ks[6], (L, CMP_BLOCK * NSA_HEAD_DIM, CMP_HIDDEN), (CMP_BLOCK * NSA_HEAD_DIM) ** -0.5),
        'cmp_w2_k': nrm(ks[7], (L, CMP_HIDDEN, NSA_HEAD_DIM), CMP_HIDDEN ** -0.5),
        'cmp_pos_v': nrm(ks[8], (L, CMP_BLOCK, NSA_HEAD_DIM), 0.1),
        'cmp_w1_v': nrm(ks[9], (L, CMP_BLOCK * NSA_HEAD_DIM, CMP_HIDDEN), (CMP_BLOCK * NSA_HEAD_DIM) ** -0.5),
        'cmp_w2_v': nrm(ks[10], (L, CMP_HIDDEN, NSA_HEAD_DIM), CMP_HIDDEN ** -0.5),
        'nsa_out_norm': 1.0 + nrm(ks[11], (L, NSA_WIDTH), 0.02),
        'w_out': nrm(ks[12], (L, D_MIX, D_MODEL), D_MIX ** -0.5),
        'ffn_norm': 1.0 + nrm(ks[13], (L, D_MODEL), 0.02),
        'moe_w_group': nrm(ks[14], (L, D_MODEL, MOE_GROUPS), D_MODEL ** -0.5),
        'moe_b_group': nrm(ks[15], (L, MOE_GROUPS), 0.01),
        'moe_w_router': nrm(ks[16], (L, D_MODEL, N_EXPERTS), D_MODEL ** -0.5),
        'moe_b_router': nrm(ks[17], (L, N_EXPERTS), 0.01),
        'moe_w_gate': nrm(ks[18], (L, N_EXPERTS, D_MODEL, EXPERT_FF), D_MODEL ** -0.5),
        'moe_w_up': nrm(ks[19], (L, N_EXPERTS, D_MODEL, EXPERT_FF), D_MODEL ** -0.5),
        'moe_w_down': nrm(ks[20], (L, N_EXPERTS, EXPERT_FF, D_MODEL), EXPERT_FF ** -0.5),
        'final_norm': 1.0 + nrm(ks[21], (D_MODEL,), 0.02),
    }


def reference(x, attn_norm, w_in, hg_lb_logits, hg_out_norm, cmp_pos_k, cmp_w1_k, cmp_w2_k,
              cmp_pos_v, cmp_w1_v, cmp_w2_v, nsa_out_norm, w_out, ffn_norm, moe_w_group, moe_b_group,
              moe_w_router, moe_b_router, moe_w_gate, moe_w_up, moe_w_down, final_norm):
    S = x.shape[1]
    cos, sin = rope_tables(S, NSA_HEAD_DIM)
    lower_bounds = jnp.cumsum(jax.nn.softmax(hg_lb_logits.astype(jnp.float32), axis=0), axis=0)
    split_at = [int(v) for v in np.cumsum(IN_SIZES)[:-1]]
    for l in range(DEPTH):
        h = rmsnorm(x, attn_norm[l])
        proj = jnp.einsum('bsd,de->bse', h, w_in[l])
        hq, hf, hi, hg, nq, kc, vc, ksl, vsl, kwn, vwn, gr = jnp.split(proj, split_at, axis=-1)
        y_hg = hgrn2_mixer(hq, hf, hi, hg, lower_bounds[l], hg_out_norm[l])
        y_nsa = nsa_mixer(nq, kc, vc, ksl, vsl, kwn, vwn, gr, cmp_pos_k[l], cmp_w1_k[l], cmp_w2_k[l],
                          cmp_pos_v[l], cmp_w1_v[l], cmp_w2_v[l], cos, sin)
        y_nsa = rmsnorm(y_nsa, nsa_out_norm[l])
        y = jnp.concatenate([y_hg.astype(x.dtype), y_nsa.astype(x.dtype)], axis=-1)
        x = x + jnp.einsum('bse,ed->bsd', y, w_out[l])
        x = x + hier_moe(rmsnorm(x, ffn_norm[l]), moe_w_group[l], moe_b_group[l], moe_w_router[l],
                         moe_b_router[l], moe_w_gate[l], moe_w_up[l], moe_w_down[l])
    return rmsnorm(x, final_norm)
```

```python
import functools

import jax
import jax.numpy as jnp
import numpy as np
from jax import lax
from jax.experimental import pallas as pl
from jax.experimental.pallas import tpu as pltpu

F32 = jnp.float32
BF16 = jnp.bfloat16

EPS = 1e-6
ROPE_THETA = 10000.0
NEG = -1e30
BIG = 1e9

HG_HEADS = 8
HG_DIM = 128
HG_QK = HG_HEADS * HG_DIM
HG_CHUNK = 64
HG_SUB = 16

NSA_HEADS = 16
NSA_GROUPS = 4
NSA_REP = NSA_HEADS // NSA_GROUPS
NSA_DIM = 64
NSA_WIDTH = NSA_HEADS * NSA_DIM
NSA_KV = NSA_GROUPS * NSA_DIM
CMP_BLOCK = 32
CMP_STRIDE = 16
CMP_HIDDEN = 256
SLC_BLOCK = 64
SLC_TOPK = 16
SLC_LOCAL = 2
WIN = 512

MOE_GROUPS = 4
MOE_EPG = 8
N_EXPERTS = MOE_GROUPS * MOE_EPG
EXPERT_FF = 512

LANES = 128
VMEM_LIMIT = 56 * 1024 * 1024


def _params(semantics, **kw):
    return pltpu.CompilerParams(dimension_semantics=semantics, vmem_limit_bytes=VMEM_LIMIT, **kw)


def _split2(a):
    hi = a.astype(BF16)
    return hi, (a - hi.astype(F32)).astype(BF16)


def _split3(a):
    hi = a.astype(BF16)
    r = a - hi.astype(F32)
    mid = r.astype(BF16)
    return hi, mid, (r - mid.astype(F32)).astype(BF16)


def _dot(a, b):
    return jnp.dot(a, b, preferred_element_type=F32)


def _dot_nt(a, b):
    return lax.dot_general(a, b, (((1,), (1,)), ((), ())), preferred_element_type=F32)


def _dot3(a, b):
    a_hi, a_lo = _split2(a)
    b_hi, b_lo = _split2(b)
    return _dot(a_hi, b_hi) + (_dot(a_hi, b_lo) + _dot(a_lo, b_hi))


def _dot3_nt(a, b):
    a_hi, a_lo = _split2(a)
    b_hi, b_lo = _split2(b)
    return _dot_nt(a_hi, b_hi) + (_dot_nt(a_hi, b_lo) + _dot_nt(a_lo, b_hi))


def _rms(x, gain):
    return x * lax.rsqrt(jnp.mean(x * x, axis=-1, keepdims=True) + EPS) * gain


def _normed_matmul_kernel(x_ref, g_ref, w_ref, o_ref, h_ref):
    parts = w_ref.shape[0]

    @pl.when(pl.program_id(1) == 0)
    def _():
        y = _rms(x_ref[...], g_ref[...])
        hi = y.astype(BF16)
        h_ref[0] = hi
        if parts == 2:
            h_ref[1] = (y - hi.astype(F32)).astype(BF16)

    acc = _dot(h_ref[0], w_ref[0])
    if parts == 2:
        acc = acc + (_dot(h_ref[0], w_ref[1]) + _dot(h_ref[1], w_ref[0]))
    o_ref[...] = acc


def _normed_matmul(x, gain, w_parts, tm, tn):
    m, k = x.shape
    parts, _, n = w_parts.shape
    return pl.pallas_call(
        _normed_matmul_kernel,
        grid=(m // tm, n // tn),
        in_specs=[
            pl.BlockSpec((tm, k), lambda i, j: (i, 0)),
            pl.BlockSpec((1, k), lambda i, j: (0, 0)),
            pl.BlockSpec((parts, k, tn), lambda i, j: (0, 0, j)),
        ],
        out_specs=pl.BlockSpec((tm, tn), lambda i, j: (i, j)),
        out_shape=jax.ShapeDtypeStruct((m, n), F32),
        scratch_shapes=[pltpu.VMEM((parts, tm, k), BF16)],
        compiler_params=_params(("parallel", "arbitrary")),
        name="normed_matmul",
    )(x, gain.reshape(1, k), w_parts)


def _hgrn_consts():
    c, sub = HG_CHUNK, HG_SUB
    t = np.arange(c)[:, None]
    r = np.arange(c)[None, :]
    mats = [r <= t, (r >= (t // sub) * sub) & (r <= t)]
    for i in range(1, c // sub + 1):
        mats.append((r > t) & (r <= i * sub - 1))
    wst = np.concatenate(mats, axis=0).astype(np.float32)
    gsum = (np.arange(c * sub)[None, :] // sub == np.arange(c)[:, None]).astype(np.float32)
    return jnp.asarray(wst, BF16), jnp.asarray(gsum, BF16)


def _hgrn_kernel(q_ref, f_ref, i_ref, g_ref, lbl_ref, gain_ref, wst_ref, gsum_ref, o_ref,
                 st_ref, p_ref, cl_ref, qs_ref, k_ref):
    c, sub, d = HG_CHUNK, HG_SUB, HG_DIM
    nsub = c // sub

    @pl.when(pl.program_id(2) == 0)
    def _():
        st_ref[...] = jnp.zeros_like(st_ref)

    l0 = lbl_ref[0:1, :]
    l1 = lbl_ref[1:2, :]
    lmax = jnp.maximum(l0, l1)
    e0 = jnp.exp(l0 - lmax)
    lb = e0 / (e0 + jnp.exp(l1 - lmax))
    row = lax.broadcasted_iota(jnp.int32, (c, d), 0)
    srow = lax.broadcasted_iota(jnp.int32, (sub, d), 0)
    ones = jnp.ones((d, d), BF16)

    def chunk(ci, carry):
        rows = pl.ds(pl.multiple_of(ci * c, c), c)
        q = q_ref[0, rows, :] * (d ** -0.5)
        f = lb + (1.0 - lb) * jax.nn.sigmoid(f_ref[0, rows, :])
        lf = jnp.log(f)
        k = 1.0 - f
        v = i_ref[0, rows, :]
        vb = v.astype(BF16)
        hi, mid, lo = _split3(lf)
        y = _dot(wst_ref[...], jnp.concatenate([hi, mid, lo], axis=1))
        y = y[:, :d] + y[:, d:2 * d] + y[:, 2 * d:]
        b = y[0:c]
        cl = y[c:2 * c]
        cl_ref[...] = cl
        qs_ref[...] = q
        k_ref[...] = k
        qe = q * jnp.exp(cl)
        qcat = jnp.concatenate([jnp.where(row // sub == i, qe, 0.0) for i in range(1, nsub)], axis=1)
        kcat = jnp.concatenate(
            [jnp.where(row < i * sub, k * jnp.exp(y[(1 + i) * c:(2 + i) * c]), 0.0) for i in range(1, nsub)], axis=1)
        a_off = _dot_nt(qcat.astype(BF16), kcat.astype(BF16))
        o = _dot(a_off.astype(BF16), vb)
        for t in range(c):
            j0 = (t // sub) * sub
            dlt = cl_ref[t:t + 1, :] - cl_ref[j0:j0 + sub, :]
            e = jnp.where(srow + j0 <= t, jnp.exp(jnp.minimum(dlt, 0.0)), 0.0)
            p_ref[t * sub:(t + 1) * sub, :] = (qs_ref[t:t + 1, :] * k_ref[j0:j0 + sub, :] * e).astype(BF16)
        r2 = _dot(p_ref[...], ones)
        x = r2.reshape(nsub, sub, sub, d) * v.reshape(nsub, 1, sub, d)
        o = o + _dot(gsum_ref[...], x.reshape(c * sub, d).astype(BF16))
        st = st_ref[...]
        o = o + _dot_nt((q * jnp.exp(b)).astype(BF16), st.astype(BF16))
        kd = k * jnp.exp(y[(1 + nsub) * c:(2 + nsub) * c])
        st_ref[...] = st * jnp.exp(b[c - 1:c, :]) + _dot(v.T.astype(BF16), kd.astype(BF16))
        gate = jax.nn.silu(g_ref[0, rows, :])
        o_ref[0, rows, :] = _rms(o, gain_ref[...]) * gate
        return carry

    lax.fori_loop(0, q_ref.shape[1] // c, chunk, 0)


def _hgrn(proj3, lb_logits, out_gain, tseq):
    bsz, seq, _ = proj3.shape
    d, c, sub = HG_DIM, HG_CHUNK, HG_SUB
    wst, gsum = _hgrn_consts()

    def col(off):
        return pl.BlockSpec((1, tseq, d), lambda b, h, t: (b, t, off + h))

    return pl.pallas_call(
        _hgrn_kernel,
        grid=(bsz, HG_HEADS, seq // tseq),
        in_specs=[
            col(0), col(HG_HEADS), col(2 * HG_HEADS), col(3 * HG_HEADS),
            pl.BlockSpec((2, d), lambda b, h, t: (0, h)),
            pl.BlockSpec((1, d), lambda b, h, t: (0, 0)),
            pl.BlockSpec(wst.shape, lambda b, h, t: (0, 0)),
            pl.BlockSpec(gsum.shape, lambda b, h, t: (0, 0)),
        ],
        out_specs=pl.BlockSpec((1, tseq, d), lambda b, h, t: (b, t, h)),
        out_shape=jax.ShapeDtypeStruct((bsz, seq, HG_QK), F32),
        scratch_shapes=[
            pltpu.VMEM((d, d), F32),
            pltpu.VMEM((c * sub, d), BF16),
            pltpu.VMEM((c, d), F32),
            pltpu.VMEM((c, d), F32),
            pltpu.VMEM((c, d), F32),
        ],
        compiler_params=_params(("parallel", "parallel", "arbitrary")),
        name="hgrn2",
    )(proj3, proj3, proj3, proj3, lb_logits, out_gain.reshape(1, d), wst, gsum)


def _rope(x, cs, sn):
    lane = lax.broadcasted_iota(jnp.int32, x.shape, 1)
    partner = jnp.where(lane % NSA_DIM < NSA_DIM // 2, pltpu.roll(x, LANES - NSA_DIM // 2, 1),
                        pltpu.roll(x, NSA_DIM // 2, 1))
    return x * cs + partner * sn


def _nsa_prep_kernel(q_ref, ksl_ref, vsl_ref, kwn_ref, vwn_ref, cs_ref, sn_ref,
                     qrot_ref, kslo_ref, vslo_ref, kwno_ref, vwno_ref):
    cs = cs_ref[...]
    sn = sn_ref[...]
    for c in range(NSA_WIDTH // LANES):
        cols = slice(c * LANES, (c + 1) * LANES)
        qrot_ref[0, :, cols] = (_rope(q_ref[0, :, cols], cs, sn) * NSA_DIM ** -0.5).astype(BF16)
    for c in range(NSA_KV // LANES):
        cols = slice(c * LANES, (c + 1) * LANES)
        ks = _rope(ksl_ref[0, :, cols], cs, sn).astype(BF16)
        kw = _rope(kwn_ref[0, :, cols], cs, sn).astype(BF16)
        vs = vsl_ref[0, :, cols].astype(BF16)
        vw = vwn_ref[0, :, cols].astype(BF16)
        for half in range(LANES // NSA_DIM):
            g = c * (LANES // NSA_DIM) + half
            hs = slice(half * NSA_DIM, (half + 1) * NSA_DIM)
            kslo_ref[0, g] = ks[:, hs]
            kwno_ref[0, g] = kw[:, hs]
            vslo_ref[0, g] = vs[:, hs]
            vwno_ref[0, g] = vw[:, hs]


def _nsa_prep(prec3, rest3, kv_off, tseq):
    bsz, seq, _ = prec3.shape
    half = NSA_DIM // 2
    inv = 1.0 / (ROPE_THETA ** (jnp.arange(0, NSA_DIM, 2, dtype=F32) / NSA_DIM))
    ang = jnp.arange(seq, dtype=F32)[:, None] * inv[None, :]
    cs = jnp.tile(jnp.cos(ang), (1, LANES // half))
    sn = jnp.tile(jnp.concatenate([-jnp.sin(ang), jnp.sin(ang)], axis=1), (1, LANES // NSA_DIM))
    kvb = kv_off // NSA_KV

    def kv_in(i):
        return pl.BlockSpec((1, tseq, NSA_KV), lambda b, t: (b, t, kvb + i))

    kv_out = pl.BlockSpec((1, NSA_GROUPS, tseq, NSA_DIM), lambda b, t: (b, 0, t, 0))
    kv_shape = jax.ShapeDtypeStruct((bsz, NSA_GROUPS, seq, NSA_DIM), BF16)
    tab = pl.BlockSpec((tseq, LANES), lambda b, t: (t, 0))
    return pl.pallas_call(
        _nsa_prep_kernel,
        grid=(bsz, seq // tseq),
        in_specs=[pl.BlockSpec((1, tseq, NSA_WIDTH), lambda b, t: (b, t, 0)), kv_in(0), kv_in(1), kv_in(2), kv_in(3),
                  tab, tab],
        out_specs=[pl.BlockSpec((1, tseq, NSA_WIDTH), lambda b, t: (b, t, 0)), kv_out, kv_out, kv_out, kv_out],
        out_shape=[jax.ShapeDtypeStruct((bsz, seq, NSA_WIDTH), BF16), kv_shape, kv_shape, kv_shape, kv_shape],
        compiler_params=_params(("parallel", "parallel")),
        name="nsa_prep",
    )(prec3, rest3, rest3, rest3, rest3, cs, sn)


def _compress_kernel(u_ref, pos_ref, w1_ref, w2_ref, o_ref, *, precise):
    mm = _dot3 if precise else (lambda a, b: _dot(a.astype(BF16), b.astype(BF16)))
    u = u_ref[0]
    nu = u.shape[0]
    ya = mm(u + pos_ref[0:1, :], w1_ref[0])
    yb = mm(u + pos_ref[1:2, :], w1_ref[1])
    hid = ya + pltpu.roll(yb, nu - 1, 0)
    o_ref[0] = mm(jax.nn.gelu(hid), w2_ref[...])


def _compress(kv, pos, w1, w2, precise):
    bsz, seq, _ = kv.shape
    nu = seq // CMP_STRIDE
    width = CMP_STRIDE * NSA_DIM
    u = kv.reshape(bsz, nu, CMP_STRIDE, NSA_GROUPS, NSA_DIM).transpose(0, 3, 1, 2, 4).reshape(bsz * NSA_GROUPS, nu, width)
    return pl.pallas_call(
        functools.partial(_compress_kernel, precise=precise),
        grid=(bsz * NSA_GROUPS,),
        in_specs=[
            pl.BlockSpec((1, nu, width), lambda i: (i, 0, 0)),
            pl.BlockSpec((2, width), lambda i: (0, 0)),
            pl.BlockSpec((2, width, CMP_HIDDEN), lambda i: (0, 0, 0)),
            pl.BlockSpec((CMP_HIDDEN, NSA_DIM), lambda i: (0, 0)),
        ],
        out_specs=pl.BlockSpec((1, nu, NSA_DIM), lambda i: (i, 0, 0)),
        out_shape=jax.ShapeDtypeStruct((bsz * NSA_GROUPS, nu, NSA_DIM), F32),
        compiler_params=_params(("parallel",)),
        name="nsa_compress",
    )(u, pos.reshape(2, width), w1.reshape(2, width, CMP_HIDDEN), w2)


def _online_softmax_step(s, v, m, l, acc):
    rep, tq, tk = s.shape
    m_new = jnp.maximum(m, jnp.max(s, axis=-1, keepdims=True))
    alpha = jnp.exp(m - m_new)
    p = jnp.exp(s - m_new)
    l = l * alpha + jnp.sum(p, axis=-1, keepdims=True)
    pv = _dot(p.reshape(rep * tq, tk).astype(BF16), v).reshape(rep, tq, v.shape[-1])
    return m_new, l, acc * alpha + pv


def _nsa_attn_kernel(qraw_ref, qrot_ref, kc_ref, vc_ref, ksl_ref, vsl_ref, kwn_ref, vwn_ref, gate_ref, agg_ref,
                     exp_ref, o_ref, bias_ref, *, topk, tk):
    tq = qraw_ref.shape[1]
    seq = ksl_ref.shape[2]
    nu = kc_ref.shape[2]
    ns = agg_ref.shape[1]
    rep, dk = NSA_REP, NSA_DIM
    qs = pl.program_id(2) * tq
    tpos = qs + lax.broadcasted_iota(jnp.int32, (tq, 1), 0)

    qr = jnp.concatenate([qraw_ref[0, :, r * dk:(r + 1) * dk] for r in range(rep)], axis=0) * dk ** -0.5
    s_c = _dot3_nt(qr, kc_ref[0, 0]).reshape(rep, tq, nu)
    cidx = lax.broadcasted_iota(jnp.int32, (tq, nu), 1)
    m_c = ((cidx * CMP_STRIDE + CMP_BLOCK - 1 <= tpos) & (cidx < nu - 1))[None]
    s_c = jnp.where(m_c, s_c, NEG)
    e_c = jnp.exp(s_c - jnp.max(s_c, axis=-1, keepdims=True))
    p_c = jnp.where(m_c, e_c / jnp.sum(e_c, axis=-1, keepdims=True), 0.0)
    o_c = _dot(p_c.reshape(rep * tq, nu).astype(BF16), vc_ref[0, 0].astype(BF16)).reshape(rep, tq, dk)

    p_hi, p_lo = _split2(jnp.sum(p_c, axis=0))
    imp = _dot(p_hi, agg_ref[...]) + _dot(p_lo, agg_ref[...])
    jidx = lax.broadcasted_iota(jnp.int32, (tq, ns), 1)
    dj = jnp.right_shift(tpos, SLC_BLOCK.bit_length() - 1) - jidx
    forced = (jidx == 0) | ((dj >= 0) & (dj < SLC_LOCAL))
    imp = jnp.where(forced, BIG, jnp.where(jidx * SLC_BLOCK <= tpos, imp, -BIG))
    rank = jnp.zeros((tq, ns), F32)
    for jp in range(ns):
        col = imp[:, jp:jp + 1]
        ahead = (col > imp) | ((col == imp) & (jidx > jp))
        rank = rank + jnp.where(ahead, 1.0, 0.0)
    selneg = jnp.where(rank < topk, 0.0, NEG).astype(BF16)
    kidx = lax.broadcasted_iota(jnp.int32, (tq, seq), 1)
    bias_ref[...] = jnp.where(kidx <= tpos, _dot(selneg, exp_ref[...]), NEG)

    q2 = jnp.concatenate([qrot_ref[0, :, r * dk:(r + 1) * dk] for r in range(rep)], axis=0)
    init = (jnp.full((rep, tq, 1), NEG, F32), jnp.zeros((rep, tq, 1), F32), jnp.zeros((rep, tq, dk), F32))

    def sel_step(kt, carry):
        ks = pl.ds(pl.multiple_of(kt * tk, tk), tk)
        s = _dot_nt(q2, ksl_ref[0, 0, ks, :]).reshape(rep, tq, tk) + bias_ref[:, ks][None]
        return _online_softmax_step(s, vsl_ref[0, 0, ks, :], *carry)

    _, l_s, acc_s = lax.fori_loop(0, (qs + tq + tk - 1) // tk, sel_step, init)
    o_s = acc_s / l_s

    def win_step(kt, carry):
        ks = pl.ds(pl.multiple_of(kt * tk, tk), tk)
        dlt = tpos - (kt * tk + lax.broadcasted_iota(jnp.int32, (tq, tk), 1))
        s = _dot_nt(q2, kwn_ref[0, 0, ks, :]).reshape(rep, tq, tk)
        s = jnp.where(((dlt >= 0) & (dlt < WIN))[None], s, NEG)
        return _online_softmax_step(s, vwn_ref[0, 0, ks, :], *carry)

    _, l_w, acc_w = lax.fori_loop(jnp.maximum(qs - WIN, 0) // tk, (qs + tq + tk - 1) // tk, win_step, init)
    o_w = acc_w / l_w

    gate = jax.nn.sigmoid(gate_ref[0, 0])
    for r in range(rep):
        o_ref[0, :, r * dk:(r + 1) * dk] = (gate[:, 3 * r:3 * r + 1] * o_c[r] + gate[:, 3 * r + 1:3 * r + 2] * o_s[r]
                                            + gate[:, 3 * r + 2:3 * r + 3] * o_w[r])


def _nsa_attn(prec3, qrot, kc, vc, ksl, vsl, kwn, vwn, gates, tq, tk):
    bsz, seq, _ = qrot.shape
    nu = seq // CMP_STRIDE
    ns = seq // SLC_BLOCK
    ci = np.arange(nu)[:, None]
    sj = np.arange(ns)[None, :]
    overlap = (ci * CMP_STRIDE < (sj + 1) * SLC_BLOCK) & (ci * CMP_STRIDE + CMP_BLOCK > sj * SLC_BLOCK) & (ci < nu - 1)
    agg = jnp.asarray(overlap, BF16)
    expand = jnp.asarray(np.arange(seq)[None, :] // SLC_BLOCK == np.arange(ns)[:, None], BF16)
    gw = NSA_REP * NSA_DIM

    def q_spec():
        return pl.BlockSpec((1, tq, gw), lambda b, g, t: (b, t, g))

    def cmp_spec():
        return pl.BlockSpec((1, 1, nu, NSA_DIM), lambda b, g, t: (b, g, 0, 0))

    def kv_spec():
        return pl.BlockSpec((1, 1, seq, NSA_DIM), lambda b, g, t: (b, g, 0, 0))

    return pl.pallas_call(
        functools.partial(_nsa_attn_kernel, topk=min(SLC_TOPK, ns), tk=tk),
        grid=(bsz, NSA_GROUPS, seq // tq),
        in_specs=[q_spec(), q_spec(), cmp_spec(), cmp_spec(), kv_spec(), kv_spec(), kv_spec(), kv_spec(),
                  pl.BlockSpec((1, 1, tq, 3 * NSA_REP), lambda b, g, t: (b, g, t, 0)),
                  pl.BlockSpec((nu, ns), lambda b, g, t: (0, 0)),
                  pl.BlockSpec((ns, seq), lambda b, g, t: (0, 0))],
        out_specs=q_spec(),
        out_shape=jax.ShapeDtypeStruct((bsz, seq, NSA_WIDTH), F32),
        scratch_shapes=[pltpu.VMEM((tq, seq), F32)],
        compiler_params=_params(("parallel", "parallel", "arbitrary")),
        name="nsa_attention",
    )(prec3, qrot, kc, vc, ksl, vsl, kwn, vwn, gates, agg, expand)


def _nsa(prec3, rest3, kv_off, gate_off, cmp_k, cmp_v):
    bsz, seq, _ = prec3.shape
    nu = seq // CMP_STRIDE
    qrot, ksl, vsl, kwn, vwn = _nsa_prep(prec3, rest3, kv_off + NSA_KV, min(seq, 512))
    kc = _compress(prec3[:, :, NSA_WIDTH:NSA_WIDTH + NSA_KV], *cmp_k, precise=True)
    vc = _compress(rest3[:, :, kv_off:kv_off + NSA_KV], *cmp_v, precise=False)
    kc = kc.reshape(bsz, NSA_GROUPS, nu, NSA_DIM)
    vc = vc.reshape(bsz, NSA_GROUPS, nu, NSA_DIM)
    gates = rest3[:, :, gate_off:gate_off + 3 * NSA_HEADS].reshape(bsz, seq, NSA_GROUPS, 3 * NSA_REP).transpose(0, 2, 1, 3)
    return _nsa_attn(prec3, qrot, kc, vc, ksl, vsl, kwn, vwn, gates, 128, 256)


def _out_proj_kernel(yh_ref, yn_ref, g_ref, w_ref, x_ref, o_ref, y_ref):
    @pl.when(pl.program_id(1) == 0)
    def _():
        wh = yh_ref.shape[1]
        y_ref[:, :wh] = yh_ref[...].astype(BF16)
        y_ref[:, wh:] = _rms(yn_ref[...], g_ref[...]).astype(BF16)

    o_ref[...] = x_ref[...] + _dot(y_ref[...], w_ref[...])


def _out_proj(y_hg, y_nsa, nsa_gain, w_out, x, tm, tn):
    m, dm = x.shape
    wh, wn = y_hg.shape[1], y_nsa.shape[1]
    return pl.pallas_call(
        _out_proj_kernel,
        grid=(m // tm, dm // tn),
        in_specs=[
            pl.BlockSpec((tm, wh), lambda i, j: (i, 0)),
            pl.BlockSpec((tm, wn), lambda i, j: (i, 0)),
            pl.BlockSpec((1, wn), lambda i, j: (0, 0)),
            pl.BlockSpec((wh + wn, tn), lambda i, j: (0, j)),
            pl.BlockSpec((tm, tn), lambda i, j: (i, j)),
        ],
        out_specs=pl.BlockSpec((tm, tn), lambda i, j: (i, j)),
        out_shape=jax.ShapeDtypeStruct((m, dm), F32),
        scratch_shapes=[pltpu.VMEM((tm, wh + wn), BF16)],
        compiler_params=_params(("parallel", "arbitrary")),
        name="out_proj",
    )(y_hg, y_nsa, nsa_gain.reshape(1, wn), w_out.astype(BF16), x)


MOE_ROWS = 256


def _router_kernel(x_ref, g_ref, w_ref, b_ref, tri_ref, xn_ref, info_ref, cnt_ref, carry_ref):
    @pl.when(pl.program_id(0) == 0)
    def _():
        carry_ref[...] = jnp.zeros_like(carry_ref)

    xn = _rms(x_ref[...], g_ref[...])
    hi, lo = _split2(xn)
    xn_ref[...] = hi
    logits = _dot(hi, w_ref[0]) + (_dot(hi, w_ref[1]) + _dot(lo, w_ref[0])) + b_ref[...]
    lane = lax.broadcasted_iota(jnp.int32, logits.shape, 1).astype(F32)
    none = float(LANES)

    def first_max(mask):
        top = jnp.max(jnp.where(mask, logits, -jnp.inf), axis=-1, keepdims=True)
        return top, jnp.min(jnp.where(mask & (logits == top), lane, none), axis=-1, keepdims=True)

    is_g = lane < MOE_GROUPS
    gmax, gsel = first_max(is_g)
    gw = 1.0 / jnp.sum(jnp.where(is_g, jnp.exp(logits - gmax), 0.0), axis=-1, keepdims=True)
    lo_lane = MOE_GROUPS + gsel * MOE_EPG
    in_grp = (lane >= lo_lane) & (lane < lo_lane + MOE_EPG)
    v1, i1 = first_max(in_grp)
    v2, i2 = first_max(in_grp & (lane != i1))
    e = jnp.exp(v2 - v1)
    w1 = gw / (1.0 + e)
    w2 = gw * e / (1.0 + e)
    e1 = i1 - MOE_GROUPS
    e2 = i2 - MOE_GROUPS
    onehot = jnp.where((lane == e1) | (lane == e2), 1.0, 0.0)
    before = _dot(tri_ref[...], onehot.astype(BF16)) + carry_ref[...]
    r1 = jnp.sum(jnp.where(lane == e1, before, 0.0), axis=-1, keepdims=True)
    r2 = jnp.sum(jnp.where(lane == e2, before, 0.0), axis=-1, keepdims=True)
    carry_ref[...] = carry_ref[...] + jnp.sum(onehot, axis=0, keepdims=True)
    cnt_ref[...] = carry_ref[...]
    info = jnp.zeros_like(logits)
    for idx, val in enumerate((e1, e2, w1, w2, r1, r2)):
        info = jnp.where(lane == idx, val, info)
    info_ref[...] = info


def _router(x2, gain, w_group, b_group, w_router, b_router, tm):
    m, dm = x2.shape
    wcat = jnp.pad(jnp.concatenate([w_group, w_router], axis=1), ((0, 0), (0, LANES - MOE_GROUPS - N_EXPERTS)))
    bcat = jnp.pad(jnp.concatenate([b_group, b_router]), (0, LANES - MOE_GROUPS - N_EXPERTS)).reshape(1, LANES)
    tri = jnp.asarray(np.tril(np.ones((tm, tm), np.float32), -1), BF16)
    return pl.pallas_call(
        _router_kernel,
        grid=(m // tm,),
        in_specs=[
            pl.BlockSpec((tm, dm), lambda i: (i, 0)),
            pl.BlockSpec((1, dm), lambda i: (0, 0)),
            pl.BlockSpec((2, dm, LANES), lambda i: (0, 0, 0)),
            pl.BlockSpec((1, LANES), lambda i: (0, 0)),
            pl.BlockSpec((tm, tm), lambda i: (0, 0)),
        ],
        out_specs=[
            pl.BlockSpec((tm, dm), lambda i: (i, 0)),
            pl.BlockSpec((tm, LANES), lambda i: (i, 0)),
            pl.BlockSpec((1, LANES), lambda i: (0, 0)),
        ],
        out_shape=[
            jax.ShapeDtypeStruct((m, dm), BF16),
            jax.ShapeDtypeStruct((m, LANES), F32),
            jax.ShapeDtypeStruct((1, LANES), F32),
        ],
        scratch_shapes=[pltpu.VMEM((1, LANES), F32)],
        compiler_params=_params(("arbitrary",)),
        name="moe_router",
    )(x2, gain.reshape(1, dm), jnp.stack(_split2(wcat)), bcat, tri)


def _dispatch_kernel(d1_ref, d2_ref, xn_ref, xs_in_ref, xs_ref, sem):
    del xs_in_ref
    tm = xn_ref.shape[0]
    base = pl.program_id(0) * tm

    def row_copy(i, dest):
        return pltpu.make_async_copy(xn_ref.at[i], xs_ref.at[dest], sem)

    def issue(i, c):
        row_copy(i, d1_ref[base + i]).start()
        row_copy(i, d2_ref[base + i]).start()
        return c

    def drain(i, c):
        row_copy(i, d1_ref[base + i]).wait()
        row_copy(i, d2_ref[base + i]).wait()
        return c

    lax.fori_loop(0, tm, issue, 0)
    lax.fori_loop(0, tm, drain, 0)


def _dispatch(xn, d1, d2, rows, tm):
    m, dm = xn.shape
    sub = dm // LANES
    return pl.pallas_call(
        _dispatch_kernel,
        grid_spec=pltpu.PrefetchScalarGridSpec(
            num_scalar_prefetch=2,
            grid=(m // tm,),
            in_specs=[pl.BlockSpec((tm, sub, LANES), lambda i, *_: (i, 0, 0)), pl.BlockSpec(memory_space=pl.ANY)],
            out_specs=pl.BlockSpec(memory_space=pl.ANY),
            scratch_shapes=[pltpu.SemaphoreType.DMA(())],
        ),
        out_shape=jax.ShapeDtypeStruct((rows, sub, LANES), BF16),
        input_output_aliases={3: 0},
        compiler_params=_params(("arbitrary",)),
        name="moe_dispatch",
    )(d1, d2, xn.reshape(m, sub, LANES), jnp.zeros((rows, sub, LANES), BF16)).reshape(rows, dm)


def _expert_kernel(be_ref, nused_ref, x_ref, wg_ref, wu_ref, wd_ref, o_ref):
    del be_ref
    i = pl.program_id(0)

    @pl.when(i < nused_ref[0])
    def _():
        x = x_ref[...]
        hid = jax.nn.silu(_dot(x, wg_ref[0])) * _dot(x, wu_ref[0])
        o_ref[...] = _dot(hid.astype(BF16), wd_ref[0])

    @pl.when(i >= nused_ref[0])
    def _():
        o_ref[...] = jnp.zeros_like(o_ref)


def _experts(xs, block_e, nused, w_gate, w_up, w_down):
    rows, dm = xs.shape
    ff = w_gate.shape[2]
    return pl.pallas_call(
        _expert_kernel,
        grid_spec=pltpu.PrefetchScalarGridSpec(
            num_scalar_prefetch=2,
            grid=(rows // MOE_ROWS,),
            in_specs=[
                pl.BlockSpec((MOE_ROWS, dm), lambda i, be, nu: (i, 0)),
                pl.BlockSpec((1, dm, ff), lambda i, be, nu: (be[i], 0, 0)),
                pl.BlockSpec((1, dm, ff), lambda i, be, nu: (be[i], 0, 0)),
                pl.BlockSpec((1, ff, dm), lambda i, be, nu: (be[i], 0, 0)),
            ],
            out_specs=pl.BlockSpec((MOE_ROWS, dm), lambda i, be, nu: (i, 0)),
        ),
        out_shape=jax.ShapeDtypeStruct((rows, dm), F32),
        compiler_params=_params(("arbitrary",)),
        name="moe_experts",
    )(block_e, nused, xs, w_gate.astype(BF16), w_up.astype(BF16), w_down.astype(BF16))


def _combine_kernel(d1_ref, d2_ref, x_ref, info_ref, g_ref, ys_ref, o_ref, buf_ref, sem):
    tm = x_ref.shape[0]
    base = pl.program_id(0) * tm

    def row_copy(i, slot, src):
        return pltpu.make_async_copy(ys_ref.at[pl.ds(src, 1)], buf_ref.at[slot, pl.ds(i, 1)], sem)

    def issue(i, c):
        row_copy(i, 0, d1_ref[base + i]).start()
        row_copy(i, 1, d2_ref[base + i]).start()
        return c

    def drain(i, c):
        row_copy(i, 0, d1_ref[base + i]).wait()
        row_copy(i, 1, d2_ref[base + i]).wait()
        return c

    lax.fori_loop(0, tm, issue, 0)
    lax.fori_loop(0, tm, drain, 0)
    info = info_ref[...]
    y = x_ref[...] + (info[:, 2:3] * buf_ref[0] + info[:, 3:4] * buf_ref[1])
    o_ref[...] = _rms(y, g_ref[...])


def _combine(x2, info, gain, ys, d1, d2, tm):
    m, dm = x2.shape
    return pl.pallas_call(
        _combine_kernel,
        grid_spec=pltpu.PrefetchScalarGridSpec(
            num_scalar_prefetch=2,
            grid=(m // tm,),
            in_specs=[
                pl.BlockSpec((tm, dm), lambda i, *_: (i, 0)),
                pl.BlockSpec((tm, LANES), lambda i, *_: (i, 0)),
                pl.BlockSpec((1, dm), lambda i, *_: (0, 0)),
                pl.BlockSpec(memory_space=pl.ANY),
            ],
            out_specs=pl.BlockSpec((tm, dm), lambda i, *_: (i, 0)),
            scratch_shapes=[pltpu.VMEM((2, tm, dm), F32), pltpu.SemaphoreType.DMA(())],
        ),
        out_shape=jax.ShapeDtypeStruct((m, dm), F32),
        compiler_params=_params(("arbitrary",)),
        name="moe_combine",
    )(d1, d2, x2, info, gain.reshape(1, dm), ys)


def _moe_and_final_norm(x2, ffn_gain, w_group, b_group, w_router, b_router, w_gate, w_up, w_down, final_gain, tm):
    m, _ = x2.shape
    xn, info, cnt = _router(x2, ffn_gain, w_group, b_group, w_router, b_router, tm)
    counts = cnt[0, :N_EXPERTS].astype(jnp.int32)
    padded = (counts + MOE_ROWS - 1) // MOE_ROWS * MOE_ROWS
    pend = jnp.cumsum(padded)
    pstart = pend - padded
    e1, e2 = info[:, 0].astype(jnp.int32), info[:, 1].astype(jnp.int32)
    d1 = pstart[e1] + info[:, 4].astype(jnp.int32)
    d2 = pstart[e2] + info[:, 5].astype(jnp.int32)
    nblocks = 2 * m // MOE_ROWS + N_EXPERTS
    block_e = jnp.minimum(jnp.searchsorted(pend, jnp.arange(nblocks, dtype=jnp.int32) * MOE_ROWS, side='right'),
                          N_EXPERTS - 1).astype(jnp.int32)
    nused = (pend[-1:] // MOE_ROWS).astype(jnp.int32)
    xs = _dispatch(xn, d1, d2, nblocks * MOE_ROWS, tm)
    ys = _experts(xs, block_e, nused, w_gate, w_up, w_down)
    return _combine(x2, info, final_gain, ys, d1, d2, tm)


def kernel(x, attn_norm, w_in, hg_lb_logits, hg_out_norm, cmp_pos_k, cmp_w1_k, cmp_w2_k, cmp_pos_v, cmp_w1_v,
           cmp_w2_v, nsa_out_norm, w_out, ffn_norm, moe_w_group, moe_b_group, moe_w_router, moe_b_router,
           moe_w_gate, moe_w_up, moe_w_down, final_norm):
    bsz, seq, dm = x.shape
    xt = x.reshape(bsz * seq, dm)
    w = w_in[0]
    w_rest = jnp.concatenate([w[:, :4 * HG_QK], w[:, 4 * HG_QK + NSA_WIDTH + NSA_KV:]], axis=1)
    pad = (-w_rest.shape[1]) % 512
    w_rest = jnp.pad(w_rest, ((0, 0), (0, pad))).astype(BF16)[None]
    rest = _normed_matmul(xt, attn_norm[0], w_rest, 512, 512)
    w_prec = jnp.stack(_split2(w[:, 4 * HG_QK:4 * HG_QK + NSA_WIDTH + NSA_KV]))
    prec = _normed_matmul(xt, attn_norm[0], w_prec, 512, 256)
    rest3 = rest.reshape(bsz, seq, -1)
    prec3 = prec.reshape(bsz, seq, -1)
    y_hg = _hgrn(rest3, hg_lb_logits, hg_out_norm[0], 256)
    y_nsa = _nsa(prec3, rest3, 4 * HG_QK, 4 * HG_QK + 5 * NSA_KV, (cmp_pos_k[0], cmp_w1_k[0], cmp_w2_k[0]),
                 (cmp_pos_v[0], cmp_w1_v[0], cmp_w2_v[0]))
    x2 = _out_proj(y_hg.reshape(bsz * seq, -1), y_nsa.reshape(bsz * seq, -1), nsa_out_norm[0], w_out[0], xt, 512, 512)
    out = _moe_and_final_norm(x2, ffn_norm[0], moe_w_group[0], moe_b_group[0], moe_w_router[0], moe_b_router[0],
                              moe_w_gate[0], moe_w_up[0], moe_w_down[0], final_norm, 256)
    return out.reshape(bsz, seq, dm)
```

```python
import functools

import jax
import jax.numpy as jnp
import numpy as np
from jax import lax
from jax.experimental import pallas as pl
from jax.experimental.pallas import tpu as pltpu

F32 = jnp.float32
BF16 = jnp.bfloat16

EPS = 1e-6
ROPE_THETA = 10000.0
NEG = -1e30
BIG = 1e9

HG_HEADS = 8
HG_DIM = 128
HG_QK = HG_HEADS * HG_DIM
HG_CHUNK = 64
HG_SUB = 16

NSA_HEADS = 16
NSA_GROUPS = 4
NSA_REP = NSA_HEADS // NSA_GROUPS
NSA_DIM = 64
NSA_WIDTH = NSA_HEADS * NSA_DIM
NSA_KV = NSA_GROUPS * NSA_DIM
CMP_BLOCK = 32
CMP_STRIDE = 16
CMP_HIDDEN = 256
SLC_BLOCK = 64
SLC_TOPK = 16
SLC_LOCAL = 2
WIN = 512

MOE_GROUPS = 4
MOE_EPG = 8
N_EXPERTS = MOE_GROUPS * MOE_EPG
EXPERT_FF = 512

LANES = 128
VMEM_LIMIT = 56 * 1024 * 1024


def _params(semantics, **kw):
    return pltpu.CompilerParams(dimension_semantics=semantics, vmem_limit_bytes=VMEM_LIMIT, **kw)


def _split2(a):
    hi = a.astype(BF16)
    return hi, (a - hi.astype(F32)).astype(BF16)


def _split3(a):
    hi = a.astype(BF16)
    r = a - hi.astype(F32)
    mid = r.astype(BF16)
    return hi, mid, (r - mid.astype(F32)).astype(BF16)


def _dot(a, b):
    return jnp.dot(a, b, preferred_element_type=F32)


def _dot_nt(a, b):
    return lax.dot_general(a, b, (((1,), (1,)), ((), ())), preferred_element_type=F32)


def _dot3(a, b):
    a_hi, a_lo = _split2(a)
    b_hi, b_lo = _split2(b)
    return _dot(a_hi, b_hi) + (_dot(a_hi, b_lo) + _dot(a_lo, b_hi))


def _dot3_nt(a, b):
    a_hi, a_lo = _split2(a)
    b_hi, b_lo = _split2(b)
    return _dot_nt(a_hi, b_hi) + (_dot_nt(a_hi, b_lo) + _dot_nt(a_lo, b_hi))


def _rms(x, gain):
    return x * lax.rsqrt(jnp.mean(x * x, axis=-1, keepdims=True) + EPS) * gain


def _normed_matmul_kernel(x_ref, g_ref, w_ref, o_ref, h_ref):
    parts = w_ref.shape[0]

    @pl.when(pl.program_id(1) == 0)
    def _():
        y = _rms(x_ref[...], g_ref[...])
        hi = y.astype(BF16)
        h_ref[0] = hi
        if parts == 2:
            h_ref[1] = (y - hi.astype(F32)).astype(BF16)

    acc = _dot(h_ref[0], w_ref[0])
    if parts == 2:
        acc = acc + (_dot(h_ref[0], w_ref[1]) + _dot(h_ref[1], w_ref[0]))
    o_ref[...] = acc


def _normed_matmul(x, gain, w_parts, tm, tn):
    m, k = x.shape
    parts, _, n = w_parts.shape
    return pl.pallas_call(
        _normed_matmul_kernel,
        grid=(m // tm, n // tn),
        in_specs=[
            pl.BlockSpec((tm, k), lambda i, j: (i, 0)),
            pl.BlockSpec((1, k), lambda i, j: (0, 0)),
            pl.BlockSpec((parts, k, tn), lambda i, j: (0, 0, j)),
        ],
        out_specs=pl.BlockSpec((tm, tn), lambda i, j: (i, j)),
        out_shape=jax.ShapeDtypeStruct((m, n), F32),
        scratch_shapes=[pltpu.VMEM((parts, tm, k), BF16)],
        compiler_params=_params(("parallel", "arbitrary")),
        name="normed_matmul",
    )(x, gain.reshape(1, k), w_parts)


def _hgrn_consts():
    c, sub = HG_CHUNK, HG_SUB
    t = np.arange(c)[:, None]
    r = np.arange(c)[None, :]
    mats = [r <= t, (r >= (t // sub) * sub) & (r <= t)]
    for i in range(1, c // sub + 1):
        mats.append((r > t) & (r <= i * sub - 1))
    wst = np.concatenate(mats, axis=0).astype(np.float32)
    gsum = (np.arange(c * sub)[None, :] // sub == np.arange(c)[:, None]).astype(np.float32)
    return jnp.asarray(wst, BF16), jnp.asarray(gsum, BF16)


def _hgrn_kernel(q_ref, f_ref, i_ref, g_ref, lbl_ref, gain_ref, wst_ref, gsum_ref, o_ref,
                 st_ref, p_ref, cl_ref, qs_ref, k_ref):
    c, sub, d = HG_CHUNK, HG_SUB, HG_DIM
    nsub = c // sub

    @pl.when(pl.program_id(2) == 0)
    def _():
        st_ref[...] = jnp.zeros_like(st_ref)

    l0 = lbl_ref[0:1, :]
    l1 = lbl_ref[1:2, :]
    lmax = jnp.maximum(l0, l1)
    e0 = jnp.exp(l0 - lmax)
    lb = e0 / (e0 + jnp.exp(l1 - lmax))
    row = lax.broadcasted_iota(jnp.int32, (c, d), 0)
    srow = lax.broadcasted_iota(jnp.int32, (sub, d), 0)
    ones = jnp.ones((d, d), BF16)

    def chunk(ci, carry):
        rows = pl.ds(pl.multiple_of(ci * c, c), c)
        q = q_ref[0, rows, :] * (d ** -0.5)
        f = lb + (1.0 - lb) * jax.nn.sigmoid(f_ref[0, rows, :])
        lf = jnp.log(f)
        k = 1.0 - f
        v = i_ref[0, rows, :]
        vb = v.astype(BF16)
        hi, mid, lo = _split3(lf)
        y = _dot(wst_ref[...], jnp.concatenate([hi, mid, lo], axis=1))
        y = y[:, :d] + y[:, d:2 * d] + y[:, 2 * d:]
        b = y[0:c]
        cl = y[c:2 * c]
        cl_ref[...] = cl
        qs_ref[...] = q
        k_ref[...] = k
        qe = q * jnp.exp(cl)
        qcat = jnp.concatenate([jnp.where(row // sub == i, qe, 0.0) for i in range(1, nsub)], axis=1)
        kcat = jnp.concatenate(
            [jnp.where(row < i * sub, k * jnp.exp(y[(1 + i) * c:(2 + i) * c]), 0.0) for i in range(1, nsub)], axis=1)
        a_off = _dot_nt(qcat.astype(BF16), kcat.astype(BF16))
        o = _dot(a_off.astype(BF16), vb)
        for t in range(c):
            j0 = (t // sub) * sub
            dlt = cl_ref[t:t + 1, :] - cl_ref[j0:j0 + sub, :]
            e = jnp.where(srow + j0 <= t, jnp.exp(jnp.minimum(dlt, 0.0)), 0.0)
            p_ref[t * sub:(t + 1) * sub, :] = (qs_ref[t:t + 1, :] * k_ref[j0:j0 + sub, :] * e).astype(BF16)
        r2 = _dot(p_ref[...], ones)
        x = r2.reshape(nsub, sub, sub, d) * v.reshape(nsub, 1, sub, d)
        o = o + _dot(gsum_ref[...], x.reshape(c * sub, d).astype(BF16))
        st = st_ref[...]
        o = o + _dot_nt((q * jnp.exp(b)).astype(BF16), st.astype(BF16))
        kd = k * jnp.exp(y[(1 + nsub) * c:(2 + nsub) * c])
        st_ref[...] = st * jnp.exp(b[c - 1:c, :]) + _dot(v.T.astype(BF16), kd.astype(BF16))
        gate = jax.nn.silu(g_ref[0, rows, :])
        o_ref[0, rows, :] = _rms(o, gain_ref[...]) * gate
        return carry

    lax.fori_loop(0, q_ref.shape[1] // c, chunk, 0)


def _hgrn(proj3, lb_logits, out_gain, tseq):
    bsz, seq, _ = proj3.shape
    d, c, sub = HG_DIM, HG_CHUNK, HG_SUB
    wst, gsum = _hgrn_consts()

    def col(off):
        return pl.BlockSpec((1, tseq, d), lambda b, h, t: (b, t, off + h))

    return pl.pallas_call(
        _hgrn_kernel,
        grid=(bsz, HG_HEADS, seq // tseq),
        in_specs=[
            col(0), col(HG_HEADS), col(2 * HG_HEADS), col(3 * HG_HEADS),
            pl.BlockSpec((2, d), lambda b, h, t: (0, h)),
            pl.BlockSpec((1, d), lambda b, h, t: (0, 0)),
            pl.BlockSpec(wst.shape, lambda b, h, t: (0, 0)),
            pl.BlockSpec(gsum.shape, lambda b, h, t: (0, 0)),
        ],
        out_specs=pl.BlockSpec((1, tseq, d), lambda b, h, t: (b, t, h)),
        out_shape=jax.ShapeDtypeStruct((bsz, seq, HG_QK), F32),
        scratch_shapes=[
            pltpu.VMEM((d, d), F32),
            pltpu.VMEM((c * sub, d), BF16),
            pltpu.VMEM((c, d), F32),
            pltpu.VMEM((c, d), F32),
            pltpu.VMEM((c, d), F32),
        ],
        compiler_params=_params(("parallel", "parallel", "arbitrary")),
        name="hgrn2",
    )(proj3, proj3, proj3, proj3, lb_logits, out_gain.reshape(1, d), wst, gsum)


def _rope(x, cs, sn):
    lane = lax.broadcasted_iota(jnp.int32, x.shape, 1)
    partner = jnp.where(lane % NSA_DIM < NSA_DIM // 2, pltpu.roll(x, LANES - NSA_DIM // 2, 1),
                        pltpu.roll(x, NSA_DIM // 2, 1))
    return x * cs + partner * sn


def _nsa_prep_kernel(q_ref, ksl_ref, vsl_ref, kwn_ref, vwn_ref, cs_ref, sn_ref,
                     qrot_ref, kslo_ref, vslo_ref, kwno_ref, vwno_ref):
    cs = cs_ref[...]
    sn = sn_ref[...]
    for c in range(NSA_WIDTH // LANES):
        cols = slice(c * LANES, (c + 1) * LANES)
        qrot_ref[0, :, cols] = (_rope(q_ref[0, :, cols], cs, sn) * NSA_DIM ** -0.5).astype(BF16)
    for c in range(NSA_KV // LANES):
        cols = slice(c * LANES, (c + 1) * LANES)
        ks = _rope(ksl_ref[0, :, cols], cs, sn).astype(BF16)
        kw = _rope(kwn_ref[0, :, cols], cs, sn).astype(BF16)
        vs = vsl_ref[0, :, cols].T.astype(BF16)
        vw = vwn_ref[0, :, cols].T.astype(BF16)
        for half in range(LANES // NSA_DIM):
            g = c * (LANES // NSA_DIM) + half
            hs = slice(half * NSA_DIM, (half + 1) * NSA_DIM)
            kslo_ref[0, g] = ks[:, hs]
            kwno_ref[0, g] = kw[:, hs]
            vslo_ref[0, g] = vs[hs, :]
            vwno_ref[0, g] = vw[hs, :]


def _nsa_prep(prec3, rest3, kv_off, tseq):
    bsz, seq, _ = prec3.shape
    half = NSA_DIM // 2
    inv = 1.0 / (ROPE_THETA ** (jnp.arange(0, NSA_DIM, 2, dtype=F32) / NSA_DIM))
    ang = jnp.arange(seq, dtype=F32)[:, None] * inv[None, :]
    cs = jnp.tile(jnp.cos(ang), (1, LANES // half))
    sn = jnp.tile(jnp.concatenate([-jnp.sin(ang), jnp.sin(ang)], axis=1), (1, LANES // NSA_DIM))
    kvb = kv_off // NSA_KV

    def kv_in(i):
        return pl.BlockSpec((1, tseq, NSA_KV), lambda b, t: (b, t, kvb + i))

    k_out = pl.BlockSpec((1, NSA_GROUPS, tseq, NSA_DIM), lambda b, t: (b, 0, t, 0))
    k_shape = jax.ShapeDtypeStruct((bsz, NSA_GROUPS, seq, NSA_DIM), BF16)
    v_out = pl.BlockSpec((1, NSA_GROUPS, NSA_DIM, tseq), lambda b, t: (b, 0, 0, t))
    v_shape = jax.ShapeDtypeStruct((bsz, NSA_GROUPS, NSA_DIM, seq), BF16)
    tab = pl.BlockSpec((tseq, LANES), lambda b, t: (t, 0))
    return pl.pallas_call(
        _nsa_prep_kernel,
        grid=(bsz, seq // tseq),
        in_specs=[pl.BlockSpec((1, tseq, NSA_WIDTH), lambda b, t: (b, t, 0)), kv_in(0), kv_in(1), kv_in(2), kv_in(3),
                  tab, tab],
        out_specs=[pl.BlockSpec((1, tseq, NSA_WIDTH), lambda b, t: (b, t, 0)), k_out, v_out, k_out, v_out],
        out_shape=[jax.ShapeDtypeStruct((bsz, seq, NSA_WIDTH), BF16), k_shape, v_shape, k_shape, v_shape],
        compiler_params=_params(("parallel", "parallel")),
        name="nsa_prep",
    )(prec3, rest3, rest3, rest3, rest3, cs, sn)


def _compress_kernel(u_ref, pos_ref, w1_ref, w2_ref, o_ref, *, precise):
    mm = _dot3 if precise else (lambda a, b: _dot(a.astype(BF16), b.astype(BF16)))
    u = u_ref[0]
    nu = u.shape[0]
    ya = mm(u + pos_ref[0:1, :], w1_ref[0])
    yb = mm(u + pos_ref[1:2, :], w1_ref[1])
    hid = ya + pltpu.roll(yb, nu - 1, 0)
    o_ref[0] = mm(jax.nn.gelu(hid), w2_ref[...])


def _compress(kv, pos, w1, w2, precise):
    bsz, seq, _ = kv.shape
    nu = seq // CMP_STRIDE
    width = CMP_STRIDE * NSA_DIM
    u = kv.reshape(bsz, nu, CMP_STRIDE, NSA_GROUPS, NSA_DIM).transpose(0, 3, 1, 2, 4).reshape(bsz * NSA_GROUPS, nu, width)
    return pl.pallas_call(
        functools.partial(_compress_kernel, precise=precise),
        grid=(bsz * NSA_GROUPS,),
        in_specs=[
            pl.BlockSpec((1, nu, width), lambda i: (i, 0, 0)),
            pl.BlockSpec((2, width), lambda i: (0, 0)),
            pl.BlockSpec((2, width, CMP_HIDDEN), lambda i: (0, 0, 0)),
            pl.BlockSpec((CMP_HIDDEN, NSA_DIM), lambda i: (0, 0)),
        ],
        out_specs=pl.BlockSpec((1, nu, NSA_DIM), lambda i: (i, 0, 0)),
        out_shape=jax.ShapeDtypeStruct((bsz * NSA_GROUPS, nu, NSA_DIM), F32),
        compiler_params=_params(("parallel",)),
        name="nsa_compress",
    )(u, pos.reshape(2, width), w1.reshape(2, width, CMP_HIDDEN), w2)


def _nsa_attn_kernel(qraw_ref, qrot_ref, kc_ref, vct_ref, ksl_ref, vslt_ref, kwn_ref, vwnt_ref, gate_ref, aggt_ref,
                     o_ref, selt_ref, sa_ref, sb_ref, *, topk, tk):
    tq = qraw_ref.shape[1]
    nu = kc_ref.shape[2]
    ns = aggt_ref.shape[0]
    rep, dk = NSA_REP, NSA_DIM
    qs = pl.program_id(2) * tq
    tpos = qs + lax.broadcasted_iota(jnp.int32, (1, tq), 1)

    qrt = (qraw_ref[0] * dk ** -0.5).T
    kc_hi, kc_lo = _split2(kc_ref[0, 0])
    vct = vct_ref[0, 0].astype(BF16)
    crow = lax.broadcasted_iota(jnp.int32, (nu, tq), 0)
    m_c = (crow * CMP_STRIDE + CMP_BLOCK - 1 <= tpos) & (crow < nu - 1)
    q_hi, q_lo = _split2(jnp.concatenate([qrt[r * dk:(r + 1) * dk] for r in range(rep)], axis=1))
    s_all = _dot(kc_hi, q_hi) + (_dot(kc_hi, q_lo) + _dot(kc_lo, q_hi))
    psum = jnp.zeros((nu, tq), F32)
    p_all = []
    for r in range(rep):
        s = jnp.where(m_c, s_all[:, r * tq:(r + 1) * tq], NEG)
        e = jnp.exp(s - jnp.max(s, axis=0, keepdims=True))
        p = jnp.where(m_c, e * (1.0 / jnp.sum(e, axis=0, keepdims=True)), 0.0)
        psum = psum + p
        p_all.append(p.astype(BF16))
    o_c_all = _dot(vct, jnp.concatenate(p_all, axis=1))
    o_c = [o_c_all[:, r * tq:(r + 1) * tq] for r in range(rep)]

    p_hi, p_lo = _split2(psum)
    imp = _dot(aggt_ref[...], p_hi) + _dot(aggt_ref[...], p_lo)
    jrow = lax.broadcasted_iota(jnp.int32, (ns, tq), 0)
    dj = jnp.right_shift(tpos, SLC_BLOCK.bit_length() - 1) - jrow
    forced = (jrow == 0) | ((dj >= 0) & (dj < SLC_LOCAL))
    imp = jnp.where(forced, BIG, jnp.where(jrow * SLC_BLOCK <= tpos, imp, -BIG))
    sub8 = lax.broadcasted_iota(jnp.int32, (8, tq), 0)
    chunks = [imp[c * 8:(c + 1) * 8] for c in range(ns // 8)]
    ranks = [jnp.zeros((8, tq), F32) for _ in range(ns // 8)]
    for jp in range(ns):
        row = chunks[jp // 8][jp % 8:jp % 8 + 1]
        for c in range(ns // 8):
            if c < jp // 8:
                ahead = jnp.where(row > chunks[c], 1.0, 0.0)
            elif c > jp // 8:
                ahead = jnp.where(row >= chunks[c], 1.0, 0.0)
            else:
                tie = jnp.where(sub8 > jp % 8, 1.0, 0.0)
                ahead = jnp.where(row > chunks[c], 1.0, jnp.where(row == chunks[c], tie, 0.0))
            ranks[c] = ranks[c] + ahead
    for c in range(ns // 8):
        selt_ref[c * 8:(c + 1) * 8] = jnp.where(ranks[c] < topk, 0.0, NEG)

    qt = qrot_ref[0].astype(F32).T.astype(BF16)

    qt_all = jnp.concatenate([qt[r * dk:(r + 1) * dk] for r in range(rep)], axis=1)

    def attend(k_ref, vt_ref, lo, hi, bias_fn):
        def tile(kt):
            return pl.ds(pl.multiple_of(jnp.minimum(kt, hi - 1) * tk, tk), tk)

        def scores(kt, dst_ref):
            dst_ref[...] = _dot(k_ref[0, 0, tile(kt), :], qt_all)

        def consume(kt, src_ref, carry):
            vt = vt_ref[0, 0, :, tile(kt)]
            bias = bias_fn(jnp.minimum(kt, hi - 1)) + jnp.where(kt < hi, 0.0, NEG)
            out = []
            for r in range(rep):
                m_old, l_old, acc = carry[r]
                s = src_ref[:, r * tq:(r + 1) * tq] + bias
                m_new = jnp.maximum(m_old, jnp.max(s, axis=0, keepdims=True))
                alpha = jnp.exp(m_old - m_new)
                p = jnp.exp(s - m_new)
                out.append((m_new, l_old * alpha + jnp.sum(p, axis=0, keepdims=True),
                            acc * alpha + _dot(vt, p.astype(BF16))))
            return tuple(out)

        def pair(i, carry):
            kt = lo + 2 * i
            scores(kt + 1, sb_ref)
            carry = consume(kt, sa_ref, carry)
            scores(kt + 2, sa_ref)
            return consume(kt + 1, sb_ref, carry)

        init = tuple((jnp.full((1, tq), NEG, F32), jnp.zeros((1, tq), F32), jnp.zeros((dk, tq), F32))
                     for _ in range(rep))
        scores(lo, sa_ref)
        final = lax.fori_loop(0, (hi - lo + 1) // 2, pair, init)
        return [acc * (1.0 / l) for _, l, acc in final]

    def sel_bias(kt):
        kpos = kt * tk + lax.broadcasted_iota(jnp.int32, (tk, tq), 0)
        per_blk = tk // SLC_BLOCK
        rows = [jnp.broadcast_to(selt_ref[pl.ds(kt * per_blk + b, 1)], (SLC_BLOCK, tq)) for b in range(per_blk)]
        return jnp.where(kpos <= tpos, jnp.concatenate(rows, axis=0), NEG)

    def win_bias(kt):
        dlt = tpos - (kt * tk + lax.broadcasted_iota(jnp.int32, (tk, tq), 0))
        return jnp.where(dlt >= 0, jnp.where(dlt < WIN, 0.0, NEG), NEG)

    hi = (qs + tq + tk - 1) // tk
    o_s = attend(ksl_ref, vslt_ref, 0, hi, sel_bias)
    o_w = attend(kwn_ref, vwnt_ref, jnp.maximum(qs - WIN, 0) // tk, hi, win_bias)

    gate = jax.nn.sigmoid(gate_ref[0, 0])
    o_t = [gate[3 * r:3 * r + 1] * o_c[r] + gate[3 * r + 1:3 * r + 2] * o_s[r] + gate[3 * r + 2:3 * r + 3] * o_w[r]
           for r in range(rep)]
    o_ref[0] = jnp.concatenate(o_t, axis=0).T


def _nsa_attn(prec3, qrot, kc, vct, ksl, vslt, kwn, vwnt, gates_t, tq, tk):
    bsz, seq, _ = qrot.shape
    nu = seq // CMP_STRIDE
    ns = seq // SLC_BLOCK
    ci = np.arange(nu)[None, :]
    sj = np.arange(ns)[:, None]
    overlap = (ci * CMP_STRIDE < (sj + 1) * SLC_BLOCK) & (ci * CMP_STRIDE + CMP_BLOCK > sj * SLC_BLOCK) & (ci < nu - 1)
    aggt = jnp.asarray(overlap, BF16)
    gw = NSA_REP * NSA_DIM

    def q_spec():
        return pl.BlockSpec((1, tq, gw), lambda b, g, t: (b, t, g))

    def per_group(rows, cols):
        return pl.BlockSpec((1, 1, rows, cols), lambda b, g, t: (b, g, 0, 0))

    return pl.pallas_call(
        functools.partial(_nsa_attn_kernel, topk=min(SLC_TOPK, ns), tk=tk),
        grid=(bsz, NSA_GROUPS, seq // tq),
        in_specs=[q_spec(), q_spec(), per_group(nu, NSA_DIM), per_group(NSA_DIM, nu),
                  per_group(seq, NSA_DIM), per_group(NSA_DIM, seq), per_group(seq, NSA_DIM), per_group(NSA_DIM, seq),
                  pl.BlockSpec((1, 1, 3 * NSA_REP, tq), lambda b, g, t: (b, g, 0, t)),
                  pl.BlockSpec((ns, nu), lambda b, g, t: (0, 0))],
        out_specs=q_spec(),
        out_shape=jax.ShapeDtypeStruct((bsz, seq, NSA_WIDTH), F32),
        scratch_shapes=[pltpu.VMEM((ns, tq), F32), pltpu.VMEM((tk, NSA_REP * tq), F32),
                        pltpu.VMEM((tk, NSA_REP * tq), F32)],
        compiler_params=_params(("parallel", "parallel", "arbitrary")),
        name="nsa_attention",
    )(prec3, qrot, kc, vct, ksl, vslt, kwn, vwnt, gates_t, aggt)


def _nsa(prec3, rest3, kv_off, gate_off, cmp_k, cmp_v):
    bsz, seq, _ = prec3.shape
    nu = seq // CMP_STRIDE
    qrot, ksl, vslt, kwn, vwnt = _nsa_prep(prec3, rest3, kv_off + NSA_KV, min(seq, 512))
    kc = _compress(prec3[:, :, NSA_WIDTH:NSA_WIDTH + NSA_KV], *cmp_k, precise=True)
    vc = _compress(rest3[:, :, kv_off:kv_off + NSA_KV], *cmp_v, precise=False)
    kc = kc.reshape(bsz, NSA_GROUPS, nu, NSA_DIM)
    vct = vc.reshape(bsz, NSA_GROUPS, nu, NSA_DIM).transpose(0, 1, 3, 2)
    gates_t = rest3[:, :, gate_off:gate_off + 3 * NSA_HEADS].reshape(bsz, seq, NSA_GROUPS, 3 * NSA_REP).transpose(0, 2, 3, 1)
    return _nsa_attn(prec3, qrot, kc, vct, ksl, vslt, kwn, vwnt, gates_t, 128, 256)


def _out_proj_kernel(yh_ref, yn_ref, g_ref, w_ref, x_ref, o_ref, y_ref):
    @pl.when(pl.program_id(1) == 0)
    def _():
        wh = yh_ref.shape[1]
        y_ref[:, :wh] = yh_ref[...].astype(BF16)
        y_ref[:, wh:] = _rms(yn_ref[...], g_ref[...]).astype(BF16)

    o_ref[...] = x_ref[...] + _dot(y_ref[...], w_ref[...])


def _out_proj(y_hg, y_nsa, nsa_gain, w_out, x, tm, tn):
    m, dm = x.shape
    wh, wn = y_hg.shape[1], y_nsa.shape[1]
    return pl.pallas_call(
        _out_proj_kernel,
        grid=(m // tm, dm // tn),
        in_specs=[
            pl.BlockSpec((tm, wh), lambda i, j: (i, 0)),
            pl.BlockSpec((tm, wn), lambda i, j: (i, 0)),
            pl.BlockSpec((1, wn), lambda i, j: (0, 0)),
            pl.BlockSpec((wh + wn, tn), lambda i, j: (0, j)),
            pl.BlockSpec((tm, tn), lambda i, j: (i, j)),
        ],
        out_specs=pl.BlockSpec((tm, tn), lambda i, j: (i, j)),
        out_shape=jax.ShapeDtypeStruct((m, dm), F32),
        scratch_shapes=[pltpu.VMEM((tm, wh + wn), BF16)],
        compiler_params=_params(("parallel", "arbitrary")),
        name="out_proj",
    )(y_hg, y_nsa, nsa_gain.reshape(1, wn), w_out.astype(BF16), x)


MOE_ROWS = 256


def _router_kernel(x_ref, g_ref, w_ref, b_ref, tri_ref, xn_ref, info_ref, cnt_ref, carry_ref):
    @pl.when(pl.program_id(0) == 0)
    def _():
        carry_ref[...] = jnp.zeros_like(carry_ref)

    xn = _rms(x_ref[...], g_ref[...])
    hi, lo = _split2(xn)
    xn_ref[...] = hi
    logits = _dot(hi, w_ref[0]) + (_dot(hi, w_ref[1]) + _dot(lo, w_ref[0])) + b_ref[...]
    lane = lax.broadcasted_iota(jnp.int32, logits.shape, 1).astype(F32)
    none = float(LANES)

    def first_max(mask):
        top = jnp.max(jnp.where(mask, logits, -jnp.inf), axis=-1, keepdims=True)
        return top, jnp.min(jnp.where(mask & (logits == top), lane, none), axis=-1, keepdims=True)

    is_g = lane < MOE_GROUPS
    gmax, gsel = first_max(is_g)
    gw = 1.0 / jnp.sum(jnp.where(is_g, jnp.exp(logits - gmax), 0.0), axis=-1, keepdims=True)
    lo_lane = MOE_GROUPS + gsel * MOE_EPG
    in_grp = (lane >= lo_lane) & (lane < lo_lane + MOE_EPG)
    v1, i1 = first_max(in_grp)
    v2, i2 = first_max(in_grp & (lane != i1))
    e = jnp.exp(v2 - v1)
    w1 = gw / (1.0 + e)
    w2 = gw * e / (1.0 + e)
    e1 = i1 - MOE_GROUPS
    e2 = i2 - MOE_GROUPS
    onehot = jnp.where((lane == e1) | (lane == e2), 1.0, 0.0)
    before = _dot(tri_ref[...], onehot.astype(BF16)) + carry_ref[...]
    r1 = jnp.sum(jnp.where(lane == e1, before, 0.0), axis=-1, keepdims=True)
    r2 = jnp.sum(jnp.where(lane == e2, before, 0.0), axis=-1, keepdims=True)
    carry_ref[...] = carry_ref[...] + jnp.sum(onehot, axis=0, keepdims=True)
    cnt_ref[...] = carry_ref[...]
    info = jnp.zeros_like(logits)
    for idx, val in enumerate((e1, e2, w1, w2, r1, r2)):
        info = jnp.where(lane == idx, val, info)
    info_ref[...] = info


def _router(x2, gain, w_group, b_group, w_router, b_router, tm):
    m, dm = x2.shape
    wcat = jnp.pad(jnp.concatenate([w_group, w_router], axis=1), ((0, 0), (0, LANES - MOE_GROUPS - N_EXPERTS)))
    bcat = jnp.pad(jnp.concatenate([b_group, b_router]), (0, LANES - MOE_GROUPS - N_EXPERTS)).reshape(1, LANES)
    tri = jnp.asarray(np.tril(np.ones((tm, tm), np.float32), -1), BF16)
    return pl.pallas_call(
        _router_kernel,
        grid=(m // tm,),
        in_specs=[
            pl.BlockSpec((tm, dm), lambda i: (i, 0)),
            pl.BlockSpec((1, dm), lambda i: (0, 0)),
            pl.BlockSpec((2, dm, LANES), lambda i: (0, 0, 0)),
            pl.BlockSpec((1, LANES), lambda i: (0, 0)),
            pl.BlockSpec((tm, tm), lambda i: (0, 0)),
        ],
        out_specs=[
            pl.BlockSpec((tm, dm), lambda i: (i, 0)),
            pl.BlockSpec((tm, LANES), lambda i: (i, 0)),
            pl.BlockSpec((1, LANES), lambda i: (0, 0)),
        ],
        out_shape=[
            jax.ShapeDtypeStruct((m, dm), BF16),
            jax.ShapeDtypeStruct((m, LANES), F32),
            jax.ShapeDtypeStruct((1, LANES), F32),
        ],
        scratch_shapes=[pltpu.VMEM((1, LANES), F32)],
        compiler_params=_params(("arbitrary",)),
        name="moe_router",
    )(x2, gain.reshape(1, dm), jnp.stack(_split2(wcat)), bcat, tri)


def _dispatch_kernel(d1_ref, d2_ref, xn_ref, xs_in_ref, xs_ref, sem):
    del xs_in_ref
    tm = xn_ref.shape[0]
    base = pl.program_id(0) * tm

    def row_copy(i, dest):
        return pltpu.make_async_copy(xn_ref.at[i], xs_ref.at[dest], sem)

    def issue(i, c):
        row_copy(i, d1_ref[base + i]).start()
        row_copy(i, d2_ref[base + i]).start()
        return c

    def drain(i, c):
        row_copy(i, d1_ref[base + i]).wait()
        row_copy(i, d2_ref[base + i]).wait()
        return c

    lax.fori_loop(0, tm, issue, 0)
    lax.fori_loop(0, tm, drain, 0)


def _dispatch(xn, d1, d2, rows, tm):
    m, dm = xn.shape
    sub = dm // LANES
    return pl.pallas_call(
        _dispatch_kernel,
        grid_spec=pltpu.PrefetchScalarGridSpec(
            num_scalar_prefetch=2,
            grid=(m // tm,),
            in_specs=[pl.BlockSpec((tm, sub, LANES), lambda i, *_: (i, 0, 0)), pl.BlockSpec(memory_space=pl.ANY)],
            out_specs=pl.BlockSpec(memory_space=pl.ANY),
            scratch_shapes=[pltpu.SemaphoreType.DMA(())],
        ),
        out_shape=jax.ShapeDtypeStruct((rows, sub, LANES), BF16),
        input_output_aliases={3: 0},
        compiler_params=_params(("arbitrary",)),
        name="moe_dispatch",
    )(d1, d2, xn.reshape(m, sub, LANES), jnp.zeros((rows, sub, LANES), BF16)).reshape(rows, dm)


def _expert_kernel(be_ref, nused_ref, x_ref, wg_ref, wu_ref, wd_ref, o_ref):
    del be_ref
    i = pl.program_id(0)

    @pl.when(i < nused_ref[0])
    def _():
        x = x_ref[...]
        hid = jax.nn.silu(_dot(x, wg_ref[0])) * _dot(x, wu_ref[0])
        o_ref[...] = _dot(hid.astype(BF16), wd_ref[0])

    @pl.when(i >= nused_ref[0])
    def _():
        o_ref[...] = jnp.zeros_like(o_ref)


def _experts(xs, block_e, nused, w_gate, w_up, w_down):
    rows, dm = xs.shape
    ff = w_gate.shape[2]
    return pl.pallas_call(
        _expert_kernel,
        grid_spec=pltpu.PrefetchScalarGridSpec(
            num_scalar_prefetch=2,
            grid=(rows // MOE_ROWS,),
            in_specs=[
                pl.BlockSpec((MOE_ROWS, dm), lambda i, be, nu: (i, 0)),
                pl.BlockSpec((1, dm, ff), lambda i, be, nu: (be[i], 0, 0)),
                pl.BlockSpec((1, dm, ff), lambda i, be, nu: (be[i], 0, 0)),
                pl.BlockSpec((1, ff, dm), lambda i, be, nu: (be[i], 0, 0)),
            ],
            out_specs=pl.BlockSpec((MOE_ROWS, dm), lambda i, be, nu: (i, 0)),
        ),
        out_shape=jax.ShapeDtypeStruct((rows, dm), F32),
        compiler_params=_params(("arbitrary",)),
        name="moe_experts",
    )(block_e, nused, xs, w_gate.astype(BF16), w_up.astype(BF16), w_down.astype(BF16))


def _combine_kernel(d1_ref, d2_ref, x_ref, info_ref, g_ref, ys_ref, o_ref, buf_ref, sem):
    tm = x_ref.shape[0]
    base = pl.program_id(0) * tm

    def row_copy(i, slot, src):
        return pltpu.make_async_copy(ys_ref.at[pl.ds(src, 1)], buf_ref.at[slot, pl.ds(i, 1)], sem)

    def issue(i, c):
        row_copy(i, 0, d1_ref[base + i]).start()
        row_copy(i, 1, d2_ref[base + i]).start()
        return c

    def drain(i, c):
        row_copy(i, 0, d1_ref[base + i]).wait()
        row_copy(i, 1, d2_ref[base + i]).wait()
        return c

    lax.fori_loop(0, tm, issue, 0)
    lax.fori_loop(0, tm, drain, 0)
    info = info_ref[...]
    y = x_ref[...] + (info[:, 2:3] * buf_ref[0] + info[:, 3:4] * buf_ref[1])
    o_ref[...] = _rms(y, g_ref[...])


def _combine(x2, info, gain, ys, d1, d2, tm):
    m, dm = x2.shape
    return pl.pallas_call(
        _combine_kernel,
        grid_spec=pltpu.PrefetchScalarGridSpec(
            num_scalar_prefetch=2,
            grid=(m // tm,),
            in_specs=[
                pl.BlockSpec((tm, dm), lambda i, *_: (i, 0)),
                pl.BlockSpec((tm, LANES), lambda i, *_: (i, 0)),
                pl.BlockSpec((1, dm), lambda i, *_: (0, 0)),
                pl.BlockSpec(memory_space=pl.ANY),
            ],
            out_specs=pl.BlockSpec((tm, dm), lambda i, *_: (i, 0)),
            scratch_shapes=[pltpu.VMEM((2, tm, dm), F32), pltpu.SemaphoreType.DMA(())],
        ),
        out_shape=jax.ShapeDtypeStruct((m, dm), F32),
        compiler_params=_params(("arbitrary",)),
        name="moe_combine",
    )(d1, d2, x2, info, gain.reshape(1, dm), ys)


def _moe_and_final_norm(x2, ffn_gain, w_group, b_group, w_router, b_router, w_gate, w_up, w_down, final_gain, tm):
    m, _ = x2.shape
    xn, info, cnt = _router(x2, ffn_gain, w_group, b_group, w_router, b_router, tm)
    counts = cnt[0, :N_EXPERTS].astype(jnp.int32)
    padded = (counts + MOE_ROWS - 1) // MOE_ROWS * MOE_ROWS
    pend = jnp.cumsum(padded)
    pstart = pend - padded
    e1, e2 = info[:, 0].astype(jnp.int32), info[:, 1].astype(jnp.int32)
    d1 = pstart[e1] + info[:, 4].astype(jnp.int32)
    d2 = pstart[e2] + info[:, 5].astype(jnp.int32)
    nblocks = 2 * m // MOE_ROWS + N_EXPERTS
    block_e = jnp.minimum(jnp.searchsorted(pend, jnp.arange(nblocks, dtype=jnp.int32) * MOE_ROWS, side='right'),
                          N_EXPERTS - 1).astype(jnp.int32)
    nused = (pend[-1:] // MOE_ROWS).astype(jnp.int32)
    xs = _dispatch(xn, d1, d2, nblocks * MOE_ROWS, tm)
    ys = _experts(xs, block_e, nused, w_gate, w_up, w_down)
    return _combine(x2, info, final_gain, ys, d1, d2, tm)


def kernel(x, attn_norm, w_in, hg_lb_logits, hg_out_norm, cmp_pos_k, cmp_w1_k, cmp_w2_k, cmp_pos_v, cmp_w1_v,
           cmp_w2_v, nsa_out_norm, w_out, ffn_norm, moe_w_group, moe_b_group, moe_w_router, moe_b_router,
           moe_w_gate, moe_w_up, moe_w_down, final_norm):
    bsz, seq, dm = x.shape
    xt = x.reshape(bsz * seq, dm)
    w = w_in[0]
    w_rest = jnp.concatenate([w[:, :4 * HG_QK], w[:, 4 * HG_QK + NSA_WIDTH + NSA_KV:]], axis=1)
    pad = (-w_rest.shape[1]) % 512
    w_rest = jnp.pad(w_rest, ((0, 0), (0, pad))).astype(BF16)[None]
    rest = _normed_matmul(xt, attn_norm[0], w_rest, 512, 512)
    w_prec = jnp.stack(_split2(w[:, 4 * HG_QK:4 * HG_QK + NSA_WIDTH + NSA_KV]))
    prec = _normed_matmul(xt, attn_norm[0], w_prec, 512, 256)
    rest3 = rest.reshape(bsz, seq, -1)
    prec3 = prec.reshape(bsz, seq, -1)
    y_hg = _hgrn(rest3, hg_lb_logits, hg_out_norm[0], 256)
    y_nsa = _nsa(prec3, rest3, 4 * HG_QK, 4 * HG_QK + 5 * NSA_KV, (cmp_pos_k[0], cmp_w1_k[0], cmp_w2_k[0]),
                 (cmp_pos_v[0], cmp_w1_v[0], cmp_w2_v[0]))
    x2 = _out_proj(y_hg.reshape(bsz * seq, -1), y_nsa.reshape(bsz * seq, -1), nsa_out_norm[0], w_out[0], xt, 512, 512)
    out = _moe_and_final_norm(x2, ffn_norm[0], moe_w_group[0], moe_b_group[0], moe_w_router[0], moe_b_router[0],
                              moe_w_gate[0], moe_w_up[0], moe_w_down[0], final_norm, 256)
    return out.reshape(bsz, seq, dm)
```

```python
import functools

import jax
import jax.numpy as jnp
import numpy as np
from jax import lax
from jax.experimental import pallas as pl
from jax.experimental.pallas import tpu as pltpu

F32 = jnp.float32
BF16 = jnp.bfloat16

EPS = 1e-6
ROPE_THETA = 10000.0
NEG = -1e30
BIG = 1e9
LOG2E = 1.4426950408889634

HG_HEADS = 8
HG_DIM = 128
HG_QK = HG_HEADS * HG_DIM
HG_CHUNK = 64
HG_SUB = 8

NSA_HEADS = 16
NSA_GROUPS = 4
NSA_REP = NSA_HEADS // NSA_GROUPS
NSA_DIM = 64
NSA_WIDTH = NSA_HEADS * NSA_DIM
NSA_KV = NSA_GROUPS * NSA_DIM
CMP_BLOCK = 32
CMP_STRIDE = 16
CMP_HIDDEN = 256
SLC_BLOCK = 64
SLC_TOPK = 16
SLC_LOCAL = 2
WIN = 512

MOE_GROUPS = 4
MOE_EPG = 8
N_EXPERTS = MOE_GROUPS * MOE_EPG
EXPERT_FF = 512

LANES = 128
VMEM_LIMIT = 56 * 1024 * 1024


def _params(semantics, **kw):
    return pltpu.CompilerParams(dimension_semantics=semantics, vmem_limit_bytes=VMEM_LIMIT, **kw)


def _split2(a):
    hi = a.astype(BF16)
    return hi, (a - hi.astype(F32)).astype(BF16)


def _split3(a):
    hi = a.astype(BF16)
    r = a - hi.astype(F32)
    mid = r.astype(BF16)
    return hi, mid, (r - mid.astype(F32)).astype(BF16)


def _dot(a, b):
    return jnp.dot(a, b, preferred_element_type=F32)


def _dot_nt(a, b):
    return lax.dot_general(a, b, (((1,), (1,)), ((), ())), preferred_element_type=F32)


def _dot3(a, b):
    a_hi, a_lo = _split2(a)
    b_hi, b_lo = _split2(b)
    return _dot(a_hi, b_hi) + (_dot(a_hi, b_lo) + _dot(a_lo, b_hi))


def _dot3_nt(a, b):
    a_hi, a_lo = _split2(a)
    b_hi, b_lo = _split2(b)
    return _dot_nt(a_hi, b_hi) + (_dot_nt(a_hi, b_lo) + _dot_nt(a_lo, b_hi))


def _rms(x, gain):
    return x * lax.rsqrt(jnp.mean(x * x, axis=-1, keepdims=True) + EPS) * gain


def _normed_matmul_kernel(x_ref, g_ref, w_ref, o_ref, h_ref):
    parts = w_ref.shape[0]

    @pl.when(pl.program_id(1) == 0)
    def _():
        y = _rms(x_ref[...], g_ref[...])
        hi = y.astype(BF16)
        h_ref[0] = hi
        if parts == 2:
            h_ref[1] = (y - hi.astype(F32)).astype(BF16)

    acc = _dot(h_ref[0], w_ref[0])
    if parts == 2:
        acc = acc + (_dot(h_ref[0], w_ref[1]) + _dot(h_ref[1], w_ref[0]))
    o_ref[...] = acc


def _normed_matmul(x, gain, w_parts, tm, tn):
    m, k = x.shape
    parts, _, n = w_parts.shape
    return pl.pallas_call(
        _normed_matmul_kernel,
        grid=(m // tm, n // tn),
        in_specs=[
            pl.BlockSpec((tm, k), lambda i, j: (i, 0)),
            pl.BlockSpec((1, k), lambda i, j: (0, 0)),
            pl.BlockSpec((parts, k, tn), lambda i, j: (0, 0, j)),
        ],
        out_specs=pl.BlockSpec((tm, tn), lambda i, j: (i, j)),
        out_shape=jax.ShapeDtypeStruct((m, n), F32),
        scratch_shapes=[pltpu.VMEM((parts, tm, k), BF16)],
        compiler_params=_params(("parallel", "arbitrary")),
        name="normed_matmul",
    )(x, gain.reshape(1, k), w_parts)


def _hgrn_consts():
    c, sub = HG_CHUNK, HG_SUB
    tri = np.tile(np.tril(np.ones((c, c), np.float32)), (1, 3))
    gsum = (np.arange(c * sub)[None, :] // sub == np.arange(c)[:, None]).astype(np.float32)
    return jnp.asarray(tri, BF16), jnp.asarray(gsum, BF16)


HG_PAR = 2


def _hgrn_kernel(q_ref, f_ref, i_ref, g_ref, lbl_ref, gain_ref, tri_ref, gsum_ref, o_ref, *scratch):
    c, sub, d = HG_CHUNK, HG_SUB, HG_DIM
    nsub = c // sub
    heads = range(HG_PAR)
    st_refs, p_refs, cl_refs, qs_refs, k_refs = (scratch[i * HG_PAR:(i + 1) * HG_PAR] for i in range(5))

    @pl.when(pl.program_id(2) == 0)
    def _():
        for h in heads:
            st_refs[h][...] = jnp.zeros_like(st_refs[h])

    l0 = lbl_ref[0:1, :]
    l1 = lbl_ref[1:2, :]
    lmax = jnp.maximum(l0, l1)
    e0 = jnp.exp(l0 - lmax)
    lb_all = e0 / (e0 + jnp.exp(l1 - lmax))
    srow = lax.broadcasted_iota(jnp.int32, (sub, d), 0)
    ones = jnp.ones((d, d), BF16)

    def rows_at(x, start):
        parts = ([jnp.zeros((start, d), F32)] if start else []) + [x]
        if start + x.shape[0] < c:
            parts.append(jnp.zeros((c - start - x.shape[0], d), F32))
        return jnp.concatenate(parts, axis=0)

    nchunks = q_ref.shape[1] // c
    cols = [slice(h * d, (h + 1) * d) for h in heads]

    def load(ci):
        rows = pl.ds(pl.multiple_of(ci * c, c), c)
        out = []
        for h in heads:
            lb = lb_all[:, cols[h]]
            f = lb + (1.0 - lb) * jax.nn.sigmoid(f_ref[0, rows, cols[h]])
            bcum = _dot(tri_ref[...], jnp.concatenate(_split3(jnp.log(f)), axis=0))
            out.append((q_ref[0, rows, cols[h]] * (d ** -0.5), 1.0 - f, i_ref[0, rows, cols[h]], bcum))
        return tuple(out)

    def chunk(ci, cur):
        nxt = load(jnp.minimum(ci + 1, nchunks - 1))
        rows = pl.ds(pl.multiple_of(ci * c, c), c)
        q, k, v, bcum = ([cur[h][i] for h in heads] for i in range(4))
        a_off, o_inter = [], []
        for h in heads:
            b = bcum[h]
            edge = [b[i * sub - 1:i * sub] for i in range(1, nsub + 1)]
            cl = b - jnp.concatenate([jnp.zeros((sub, d), F32)] + [jnp.broadcast_to(e, (sub, d)) for e in edge[:-1]],
                                     axis=0)
            cl_refs[h][...] = cl
            qs_refs[h][...] = q[h]
            k_refs[h][...] = k[h]
            qe = q[h] * jnp.exp(cl)
            qcat, kcat = [], []
            for i in range(1, nsub):
                qcat.append(rows_at(qe[i * sub:(i + 1) * sub], i * sub))
                kcat.append(rows_at(k[h][:i * sub] * jnp.exp(edge[i - 1] - b[:i * sub]), 0))
            a_off.append(_dot_nt(jnp.concatenate(qcat, axis=1).astype(BF16),
                                 jnp.concatenate(kcat, axis=1).astype(BF16)))
            st = st_refs[h][...]
            o_inter.append(_dot_nt((q[h] * jnp.exp(b)).astype(BF16), st.astype(BF16)))
            kd = k[h] * jnp.exp(edge[-1] - b)
            st_refs[h][...] = st * jnp.exp(edge[-1]) + _dot(v[h].T.astype(BF16), kd.astype(BF16))
        r2 = []
        for h in heads:
            for t in range(c):
                j0 = (t // sub) * sub
                dlt = cl_refs[h][t:t + 1, :] - cl_refs[h][j0:j0 + sub, :]
                e = jnp.where(srow + j0 <= t, jnp.exp(jnp.minimum(dlt, 0.0)), 0.0)
                p_refs[h][t * sub:(t + 1) * sub, :] = (qs_refs[h][t:t + 1, :] * k_refs[h][j0:j0 + sub, :] * e).astype(BF16)
            r2.append(_dot(p_refs[h][...], ones))
        o = []
        for h in heads:
            o.append(o_inter[h] + _dot(a_off[h].astype(BF16), v[h].astype(BF16)))
        for h in heads:
            x = r2[h].reshape(nsub, sub, sub, d) * v[h].reshape(nsub, 1, sub, d)
            o[h] = o[h] + _dot(gsum_ref[...], x.reshape(c * sub, d).astype(BF16))
        for h in heads:
            gate = jax.nn.silu(g_ref[0, rows, cols[h]])
            o_ref[0, rows, cols[h]] = _rms(o[h], gain_ref[...]) * gate
        return nxt

    lax.fori_loop(0, nchunks, chunk, load(0), unroll=2)


def _hgrn(proj3, lb_logits, out_gain, tseq):
    bsz, seq, _ = proj3.shape
    d, c, sub = HG_DIM, HG_CHUNK, HG_SUB
    wst, gsum = _hgrn_consts()
    groups = HG_HEADS // HG_PAR
    width = HG_PAR * d

    def col(off):
        return pl.BlockSpec((1, tseq, width), lambda b, h, t: (b, t, off * groups + h))

    per_head = [pltpu.VMEM((d, d), F32), pltpu.VMEM((c * sub, d), BF16), pltpu.VMEM((c, d), F32),
                pltpu.VMEM((c, d), F32), pltpu.VMEM((c, d), F32)]
    return pl.pallas_call(
        _hgrn_kernel,
        grid=(bsz, groups, seq // tseq),
        in_specs=[
            col(0), col(1), col(2), col(3),
            pl.BlockSpec((2, width), lambda b, h, t: (0, h)),
            pl.BlockSpec((1, d), lambda b, h, t: (0, 0)),
            pl.BlockSpec(wst.shape, lambda b, h, t: (0, 0)),
            pl.BlockSpec(gsum.shape, lambda b, h, t: (0, 0)),
        ],
        out_specs=pl.BlockSpec((1, tseq, width), lambda b, h, t: (b, t, h)),
        out_shape=jax.ShapeDtypeStruct((bsz, seq, HG_QK), F32),
        scratch_shapes=[s for s in per_head for _ in range(HG_PAR)],
        compiler_params=_params(("parallel", "parallel", "arbitrary")),
        name="hgrn2",
    )(proj3, proj3, proj3, proj3, lb_logits, out_gain.reshape(1, d), wst, gsum)


def _rope(x, cs, sn):
    lane = lax.broadcasted_iota(jnp.int32, x.shape, 1)
    partner = jnp.where(lane % NSA_DIM < NSA_DIM // 2, pltpu.roll(x, LANES - NSA_DIM // 2, 1),
                        pltpu.roll(x, NSA_DIM // 2, 1))
    return x * cs + partner * sn


def _nsa_prep_kernel(q_ref, ksl_ref, vsl_ref, kwn_ref, vwn_ref, cs_ref, sn_ref,
                     qrot_ref, kslo_ref, vslo_ref, kwno_ref, vwno_ref):
    cs = cs_ref[...]
    sn = sn_ref[...]
    for c in range(NSA_WIDTH // LANES):
        cols = slice(c * LANES, (c + 1) * LANES)
        qrot_ref[0, :, cols] = (_rope(q_ref[0, :, cols], cs, sn) * (NSA_DIM ** -0.5 * LOG2E)).astype(BF16)
    for c in range(NSA_KV // LANES):
        cols = slice(c * LANES, (c + 1) * LANES)
        ks = _rope(ksl_ref[0, :, cols], cs, sn).astype(BF16)
        kw = _rope(kwn_ref[0, :, cols], cs, sn).astype(BF16)
        vs = vsl_ref[0, :, cols].T.astype(BF16)
        vw = vwn_ref[0, :, cols].T.astype(BF16)
        for half in range(LANES // NSA_DIM):
            g = c * (LANES // NSA_DIM) + half
            hs = slice(half * NSA_DIM, (half + 1) * NSA_DIM)
            kslo_ref[0, g] = ks[:, hs]
            kwno_ref[0, g] = kw[:, hs]
            vslo_ref[0, g] = vs[hs, :]
            vwno_ref[0, g] = vw[hs, :]


def _nsa_prep(prec3, rest3, kv_off, tseq):
    bsz, seq, _ = prec3.shape
    half = NSA_DIM // 2
    inv = 1.0 / (ROPE_THETA ** (jnp.arange(0, NSA_DIM, 2, dtype=F32) / NSA_DIM))
    ang = jnp.arange(seq, dtype=F32)[:, None] * inv[None, :]
    cs = jnp.tile(jnp.cos(ang), (1, LANES // half))
    sn = jnp.tile(jnp.concatenate([-jnp.sin(ang), jnp.sin(ang)], axis=1), (1, LANES // NSA_DIM))
    kvb = kv_off // NSA_KV

    def kv_in(i):
        return pl.BlockSpec((1, tseq, NSA_KV), lambda b, t: (b, t, kvb + i))

    k_out = pl.BlockSpec((1, NSA_GROUPS, tseq, NSA_DIM), lambda b, t: (b, 0, t, 0))
    k_shape = jax.ShapeDtypeStruct((bsz, NSA_GROUPS, seq, NSA_DIM), BF16)
    v_out = pl.BlockSpec((1, NSA_GROUPS, NSA_DIM, tseq), lambda b, t: (b, 0, 0, t))
    v_shape = jax.ShapeDtypeStruct((bsz, NSA_GROUPS, NSA_DIM, seq), BF16)
    tab = pl.BlockSpec((tseq, LANES), lambda b, t: (t, 0))
    return pl.pallas_call(
        _nsa_prep_kernel,
        grid=(bsz, seq // tseq),
        in_specs=[pl.BlockSpec((1, tseq, NSA_WIDTH), lambda b, t: (b, t, 0)), kv_in(0), kv_in(1), kv_in(2), kv_in(3),
                  tab, tab],
        out_specs=[pl.BlockSpec((1, tseq, NSA_WIDTH), lambda b, t: (b, t, 0)), k_out, v_out, k_out, v_out],
        out_shape=[jax.ShapeDtypeStruct((bsz, seq, NSA_WIDTH), BF16), k_shape, v_shape, k_shape, v_shape],
        compiler_params=_params(("parallel", "parallel")),
        name="nsa_prep",
    )(prec3, rest3, rest3, rest3, rest3, cs, sn)


def _compress_kernel(u_ref, pos_ref, w1_ref, w2_ref, o_ref, *, precise):
    mm = _dot3 if precise else (lambda a, b: _dot(a.astype(BF16), b.astype(BF16)))
    u = u_ref[0]
    nu = u.shape[0]
    ya = mm(u + pos_ref[0:1, :], w1_ref[0])
    yb = mm(u + pos_ref[1:2, :], w1_ref[1])
    hid = ya + pltpu.roll(yb, nu - 1, 0)
    o_ref[0] = mm(jax.nn.gelu(hid), w2_ref[...])


def _compress(kv, pos, w1, w2, precise):
    bsz, seq, _ = kv.shape
    nu = seq // CMP_STRIDE
    width = CMP_STRIDE * NSA_DIM
    u = kv.reshape(bsz, nu, CMP_STRIDE, NSA_GROUPS, NSA_DIM).transpose(0, 3, 1, 2, 4).reshape(bsz * NSA_GROUPS, nu, width)
    return pl.pallas_call(
        functools.partial(_compress_kernel, precise=precise),
        grid=(bsz * NSA_GROUPS,),
        in_specs=[
            pl.BlockSpec((1, nu, width), lambda i: (i, 0, 0)),
            pl.BlockSpec((2, width), lambda i: (0, 0)),
            pl.BlockSpec((2, width, CMP_HIDDEN), lambda i: (0, 0, 0)),
            pl.BlockSpec((CMP_HIDDEN, NSA_DIM), lambda i: (0, 0)),
        ],
        out_specs=pl.BlockSpec((1, nu, NSA_DIM), lambda i: (i, 0, 0)),
        out_shape=jax.ShapeDtypeStruct((bsz * NSA_GROUPS, nu, NSA_DIM), F32),
        compiler_params=_params(("parallel",)),
        name="nsa_compress",
    )(u, pos.reshape(2, width), w1.reshape(2, width, CMP_HIDDEN), w2)


def _nsa_attn_kernel(qraw_ref, qrot_ref, kc_ref, vct_ref, ksl_ref, vslt_ref, kwn_ref, vwnt_ref, gate_ref, aggt_ref,
                     o_ref, selt_ref, sa_ref, sb_ref, *, topk, tk):
    tq = qraw_ref.shape[1]
    nu = kc_ref.shape[2]
    ns = aggt_ref.shape[0]
    rep, dk = NSA_REP, NSA_DIM
    qs = pl.program_id(2) * tq
    tpos = qs + lax.broadcasted_iota(jnp.int32, (1, tq), 1)

    qrt = (qraw_ref[0] * dk ** -0.5).T
    kc_hi, kc_lo = _split2(kc_ref[0, 0])
    vct = vct_ref[0, 0].astype(BF16)
    crow = lax.broadcasted_iota(jnp.int32, (nu, tq), 0)
    m_c = (crow * CMP_STRIDE + CMP_BLOCK - 1 <= tpos) & (crow < nu - 1)
    q_hi, q_lo = _split2(jnp.concatenate([qrt[r * dk:(r + 1) * dk] for r in range(rep)], axis=1))
    s_all = _dot(kc_hi, q_hi) + (_dot(kc_hi, q_lo) + _dot(kc_lo, q_hi))
    psum = jnp.zeros((nu, tq), F32)
    p_all = []
    for r in range(rep):
        s = jnp.where(m_c, s_all[:, r * tq:(r + 1) * tq], NEG)
        e = jnp.exp(s - jnp.max(s, axis=0, keepdims=True))
        p = jnp.where(m_c, e * (1.0 / jnp.sum(e, axis=0, keepdims=True)), 0.0)
        psum = psum + p
        p_all.append(p.astype(BF16))
    o_c_all = _dot(vct, jnp.concatenate(p_all, axis=1))
    o_c = [o_c_all[:, r * tq:(r + 1) * tq] for r in range(rep)]

    p_hi, p_lo = _split2(psum)
    imp = _dot(aggt_ref[...], p_hi) + _dot(aggt_ref[...], p_lo)
    jrow = lax.broadcasted_iota(jnp.int32, (ns, tq), 0)
    dj = jnp.right_shift(tpos, SLC_BLOCK.bit_length() - 1) - jrow
    forced = (jrow == 0) | ((dj >= 0) & (dj < SLC_LOCAL))
    imp = jnp.where(forced, BIG, jnp.where(jrow * SLC_BLOCK <= tpos, imp, -BIG))
    sub8 = lax.broadcasted_iota(jnp.int32, (8, tq), 0)
    chunks = [imp[c * 8:(c + 1) * 8] for c in range(ns // 8)]
    ranks = [jnp.zeros((8, tq), F32) for _ in range(ns // 8)]
    for jp in range(ns):
        row = chunks[jp // 8][jp % 8:jp % 8 + 1]
        for c in range(ns // 8):
            if c < jp // 8:
                ahead = jnp.where(row > chunks[c], 1.0, 0.0)
            elif c > jp // 8:
                ahead = jnp.where(row >= chunks[c], 1.0, 0.0)
            else:
                tie = jnp.where(sub8 > jp % 8, 1.0, 0.0)
                ahead = jnp.where(row > chunks[c], 1.0, jnp.where(row == chunks[c], tie, 0.0))
            ranks[c] = ranks[c] + ahead
    for c in range(ns // 8):
        selt_ref[c * 8:(c + 1) * 8] = jnp.where(ranks[c] < topk, 0.0, NEG)

    qt = qrot_ref[0].astype(F32).T.astype(BF16)

    qt_all = jnp.concatenate([qt[r * dk:(r + 1) * dk] for r in range(rep)], axis=1)

    def attend(k_ref, vt_ref, lo, hi, bias_fn):
        def tile(kt):
            return pl.ds(pl.multiple_of(jnp.minimum(kt, hi - 1) * tk, tk), tk)

        def scores(kt, dst_ref):
            dst_ref[...] = _dot(k_ref[0, 0, tile(kt), :], qt_all)

        def consume(kt, src_ref, carry):
            vt = vt_ref[0, 0, :, tile(kt)]
            bias = bias_fn(kt, jnp.where(kt < hi, kt * tk, 1 << 30))
            out = []
            for r in range(rep):
                m_old, l_old, acc = carry[r]
                s = src_ref[:, r * tq:(r + 1) * tq] + bias
                m_new = jnp.maximum(m_old, jnp.max(s, axis=0, keepdims=True))
                alpha = jnp.exp2(m_old - m_new)
                p = jnp.exp2(s - m_new)
                out.append((m_new, l_old * alpha + jnp.sum(p, axis=0, keepdims=True),
                            acc * alpha + _dot(vt, p.astype(BF16))))
            return tuple(out)

        def pair(i, carry):
            kt = lo + 2 * i
            scores(kt + 1, sb_ref)
            carry = consume(kt, sa_ref, carry)
            scores(kt + 2, sa_ref)
            return consume(kt + 1, sb_ref, carry)

        init = tuple((jnp.full((1, tq), NEG, F32), jnp.zeros((1, tq), F32), jnp.zeros((dk, tq), F32))
                     for _ in range(rep))
        scores(lo, sa_ref)
        final = lax.fori_loop(0, (hi - lo + 1) // 2, pair, init)
        return [acc * (1.0 / l) for _, l, acc in final]

    def sel_bias(kt, key0):
        kpos = key0 + lax.broadcasted_iota(jnp.int32, (tk, tq), 0)
        per_blk = tk // SLC_BLOCK
        blk0 = jnp.minimum(kt, hi - 1) * per_blk
        rows = [jnp.broadcast_to(selt_ref[pl.ds(blk0 + b, 1)], (SLC_BLOCK, tq)) for b in range(per_blk)]
        return jnp.where(kpos <= tpos, jnp.concatenate(rows, axis=0), NEG)

    def win_bias(kt, key0):
        dlt = tpos - (key0 + lax.broadcasted_iota(jnp.int32, (tk, tq), 0))
        return jnp.where(dlt >= 0, jnp.where(dlt < WIN, 0.0, NEG), NEG)

    hi = (qs + tq + tk - 1) // tk
    o_s = attend(ksl_ref, vslt_ref, 0, hi, sel_bias)
    o_w = attend(kwn_ref, vwnt_ref, jnp.maximum(qs - WIN, 0) // tk, hi, win_bias)

    gate = jax.nn.sigmoid(gate_ref[0, 0])
    o_t = [gate[3 * r:3 * r + 1] * o_c[r] + gate[3 * r + 1:3 * r + 2] * o_s[r] + gate[3 * r + 2:3 * r + 3] * o_w[r]
           for r in range(rep)]
    o_ref[0] = jnp.concatenate(o_t, axis=0).T


def _nsa_attn(prec3, qrot, kc, vct, ksl, vslt, kwn, vwnt, gates_t, tq, tk):
    bsz, seq, _ = qrot.shape
    nu = seq // CMP_STRIDE
    ns = seq // SLC_BLOCK
    ci = np.arange(nu)[None, :]
    sj = np.arange(ns)[:, None]
    overlap = (ci * CMP_STRIDE < (sj + 1) * SLC_BLOCK) & (ci * CMP_STRIDE + CMP_BLOCK > sj * SLC_BLOCK) & (ci < nu - 1)
    aggt = jnp.asarray(overlap, BF16)
    gw = NSA_REP * NSA_DIM

    def q_spec():
        return pl.BlockSpec((1, tq, gw), lambda b, g, t: (b, t, g))

    def per_group(rows, cols):
        return pl.BlockSpec((1, 1, rows, cols), lambda b, g, t: (b, g, 0, 0))

    return pl.pallas_call(
        functools.partial(_nsa_attn_kernel, topk=min(SLC_TOPK, ns), tk=tk),
        grid=(bsz, NSA_GROUPS, seq // tq),
        in_specs=[q_spec(), q_spec(), per_group(nu, NSA_DIM), per_group(NSA_DIM, nu),
                  per_group(seq, NSA_DIM), per_group(NSA_DIM, seq), per_group(seq, NSA_DIM), per_group(NSA_DIM, seq),
                  pl.BlockSpec((1, 1, 3 * NSA_REP, tq), lambda b, g, t: (b, g, 0, t)),
                  pl.BlockSpec((ns, nu), lambda b, g, t: (0, 0))],
        out_specs=q_spec(),
        out_shape=jax.ShapeDtypeStruct((bsz, seq, NSA_WIDTH), F32),
        scratch_shapes=[pltpu.VMEM((ns, tq), F32), pltpu.VMEM((tk, NSA_REP * tq), F32),
                        pltpu.VMEM((tk, NSA_REP * tq), F32)],
        compiler_params=_params(("parallel", "parallel", "arbitrary")),
        name="nsa_attention",
    )(prec3, qrot, kc, vct, ksl, vslt, kwn, vwnt, gates_t, aggt)


def _nsa(prec3, rest3, kv_off, gate_off, cmp_k, cmp_v):
    bsz, seq, _ = prec3.shape
    nu = seq // CMP_STRIDE
    qrot, ksl, vslt, kwn, vwnt = _nsa_prep(prec3, rest3, kv_off + NSA_KV, min(seq, 512))
    kc = _compress(prec3[:, :, NSA_WIDTH:NSA_WIDTH + NSA_KV], *cmp_k, precise=True)
    vc = _compress(rest3[:, :, kv_off:kv_off + NSA_KV], *cmp_v, precise=False)
    kc = kc.reshape(bsz, NSA_GROUPS, nu, NSA_DIM)
    vct = vc.reshape(bsz, NSA_GROUPS, nu, NSA_DIM).transpose(0, 1, 3, 2)
    gates_t = rest3[:, :, gate_off:gate_off + 3 * NSA_HEADS].reshape(bsz, seq, NSA_GROUPS, 3 * NSA_REP).transpose(0, 2, 3, 1)
    return _nsa_attn(prec3, qrot, kc, vct, ksl, vslt, kwn, vwnt, gates_t, 128, 256)


def _out_proj_kernel(yh_ref, yn_ref, g_ref, w_ref, x_ref, o_ref, y_ref):
    @pl.when(pl.program_id(1) == 0)
    def _():
        wh = yh_ref.shape[1]
        y_ref[:, :wh] = yh_ref[...].astype(BF16)
        y_ref[:, wh:] = _rms(yn_ref[...], g_ref[...]).astype(BF16)

    o_ref[...] = x_ref[...] + _dot(y_ref[...], w_ref[...])


def _out_proj(y_hg, y_nsa, nsa_gain, w_out, x, tm, tn):
    m, dm = x.shape
    wh, wn = y_hg.shape[1], y_nsa.shape[1]
    return pl.pallas_call(
        _out_proj_kernel,
        grid=(m // tm, dm // tn),
        in_specs=[
            pl.BlockSpec((tm, wh), lambda i, j: (i, 0)),
            pl.BlockSpec((tm, wn), lambda i, j: (i, 0)),
            pl.BlockSpec((1, wn), lambda i, j: (0, 0)),
            pl.BlockSpec((wh + wn, tn), lambda i, j: (0, j)),
            pl.BlockSpec((tm, tn), lambda i, j: (i, j)),
        ],
        out_specs=pl.BlockSpec((tm, tn), lambda i, j: (i, j)),
        out_shape=jax.ShapeDtypeStruct((m, dm), F32),
        scratch_shapes=[pltpu.VMEM((tm, wh + wn), BF16)],
        compiler_params=_params(("parallel", "arbitrary")),
        name="out_proj",
    )(y_hg, y_nsa, nsa_gain.reshape(1, wn), w_out.astype(BF16), x)


MOE_ROWS = 256


def _pack_bf16_pairs(hi):
    n = hi.shape[1] // 2
    bits = pltpu.bitcast(hi.astype(F32), jnp.uint32)
    return jnp.right_shift(bits[:, :n], jnp.uint32(16)) | (bits[:, n:] & jnp.uint32(0xFFFF0000))


def _unpack_f32_pairs(words):
    lo = pltpu.bitcast(jnp.left_shift(words, jnp.uint32(16)), F32)
    hi = pltpu.bitcast(words & jnp.uint32(0xFFFF0000), F32)
    return lo, hi


def _unpack_bf16_pairs(words):
    lo, hi = _unpack_f32_pairs(words)
    return lo.astype(BF16), hi.astype(BF16)


def _router_kernel(x_ref, g_ref, w_ref, b_ref, tri_ref, xn_ref, info_ref, infot_ref, cnt_ref, carry_ref):
    @pl.when(pl.program_id(0) == 0)
    def _():
        carry_ref[...] = jnp.zeros_like(carry_ref)

    xn = _rms(x_ref[...], g_ref[...])
    hi, lo = _split2(xn)
    xn_ref[...] = _pack_bf16_pairs(hi)
    logits = _dot(hi, w_ref[0]) + (_dot(hi, w_ref[1]) + _dot(lo, w_ref[0])) + b_ref[...]
    lane = lax.broadcasted_iota(jnp.int32, logits.shape, 1).astype(F32)
    none = float(LANES)

    def first_max(mask):
        top = jnp.max(jnp.where(mask, logits, -jnp.inf), axis=-1, keepdims=True)
        return top, jnp.min(jnp.where(mask & (logits == top), lane, none), axis=-1, keepdims=True)

    is_g = lane < MOE_GROUPS
    gmax, gsel = first_max(is_g)
    gw = 1.0 / jnp.sum(jnp.where(is_g, jnp.exp(logits - gmax), 0.0), axis=-1, keepdims=True)
    lo_lane = MOE_GROUPS + gsel * MOE_EPG
    in_grp = (lane >= lo_lane) & (lane < lo_lane + MOE_EPG)
    v1, i1 = first_max(in_grp)
    v2, i2 = first_max(in_grp & (lane != i1))
    e = jnp.exp(v2 - v1)
    w1 = gw / (1.0 + e)
    w2 = gw * e / (1.0 + e)
    e1 = i1 - MOE_GROUPS
    e2 = i2 - MOE_GROUPS
    onehot = jnp.where((lane == e1) | (lane == e2), 1.0, 0.0)
    before = _dot(tri_ref[...], onehot.astype(BF16)) + carry_ref[...]
    r1 = jnp.sum(jnp.where(lane == e1, before, 0.0), axis=-1, keepdims=True)
    r2 = jnp.sum(jnp.where(lane == e2, before, 0.0), axis=-1, keepdims=True)
    carry_ref[...] = carry_ref[...] + jnp.sum(onehot, axis=0, keepdims=True)
    cnt_ref[...] = carry_ref[...]
    info = jnp.zeros_like(logits)
    for idx, val in enumerate((e1, e2, w1, w2, r1, r2)):
        info = jnp.where(lane == idx, val, info)
    info_ref[...] = info
    infot_ref[...] = info.T[:8]


def _router(x2, gain, w_group, b_group, w_router, b_router, tm):
    m, dm = x2.shape
    wcat = jnp.pad(jnp.concatenate([w_group, w_router], axis=1), ((0, 0), (0, LANES - MOE_GROUPS - N_EXPERTS)))
    bcat = jnp.pad(jnp.concatenate([b_group, b_router]), (0, LANES - MOE_GROUPS - N_EXPERTS)).reshape(1, LANES)
    tri = jnp.asarray(np.tril(np.ones((tm, tm), np.float32), -1), BF16)
    return pl.pallas_call(
        _router_kernel,
        grid=(m // tm,),
        in_specs=[
            pl.BlockSpec((tm, dm), lambda i: (i, 0)),
            pl.BlockSpec((1, dm), lambda i: (0, 0)),
            pl.BlockSpec((2, dm, LANES), lambda i: (0, 0, 0)),
            pl.BlockSpec((1, LANES), lambda i: (0, 0)),
            pl.BlockSpec((tm, tm), lambda i: (0, 0)),
        ],
        out_specs=[
            pl.BlockSpec((tm, dm // 2), lambda i: (i, 0)),
            pl.BlockSpec((tm, LANES), lambda i: (i, 0)),
            pl.BlockSpec((8, tm), lambda i: (0, i)),
            pl.BlockSpec((1, LANES), lambda i: (0, 0)),
        ],
        out_shape=[
            jax.ShapeDtypeStruct((m, dm // 2), jnp.uint32),
            jax.ShapeDtypeStruct((m, LANES), F32),
            jax.ShapeDtypeStruct((8, m), F32),
            jax.ShapeDtypeStruct((1, LANES), F32),
        ],
        scratch_shapes=[pltpu.VMEM((1, LANES), F32)],
        compiler_params=_params(("arbitrary",)),
        name="moe_router",
    )(x2, gain.reshape(1, dm), jnp.stack(_split2(wcat)), bcat, tri)


def _dispatch_kernel(d1_ref, d2_ref, xn_ref, xs_in_ref, xs_ref, sem):
    del xs_in_ref
    tm = xn_ref.shape[0]
    base = pl.program_id(0) * tm

    def row_copy(i, dest):
        return pltpu.make_async_copy(xn_ref.at[pl.ds(i, 1)], xs_ref.at[pl.ds(dest, 1)], sem)

    def issue(i, c):
        row_copy(i, d1_ref[base + i]).start()
        row_copy(i, d2_ref[base + i]).start()
        return c

    def drain(i, c):
        row_copy(i, d1_ref[base + i]).wait()
        row_copy(i, d2_ref[base + i]).wait()
        return c

    lax.fori_loop(0, tm, issue, 0)
    lax.fori_loop(0, tm, drain, 0)


def _dispatch(xn, d1, d2, rows, tm):
    m, words = xn.shape
    return pl.pallas_call(
        _dispatch_kernel,
        grid_spec=pltpu.PrefetchScalarGridSpec(
            num_scalar_prefetch=2,
            grid=(m // tm,),
            in_specs=[pl.BlockSpec((tm, words), lambda i, *_: (i, 0)), pl.BlockSpec(memory_space=pl.ANY)],
            out_specs=pl.BlockSpec(memory_space=pl.ANY),
            scratch_shapes=[pltpu.SemaphoreType.DMA(())],
        ),
        out_shape=jax.ShapeDtypeStruct((rows, words), jnp.uint32),
        input_output_aliases={3: 0},
        compiler_params=_params(("arbitrary",)),
        name="moe_dispatch",
    )(d1, d2, xn, jnp.zeros((rows, words), jnp.uint32))


def _expert_kernel(be_ref, nused_ref, x_ref, wg_ref, wu_ref, wd_ref, o_ref, wgb_ref, wub_ref, wdb_ref):
    i = pl.program_id(0)

    @pl.when((i == 0) | (be_ref[i] != be_ref[jnp.maximum(i - 1, 0)]))
    def _():
        wgb_ref[...] = wg_ref[0].astype(BF16)
        wub_ref[...] = wu_ref[0].astype(BF16)
        wdb_ref[...] = wd_ref[0].astype(BF16)

    @pl.when(i < nused_ref[0])
    def _():
        half = wgb_ref.shape[0] // 2
        x_lo, x_hi = _unpack_bf16_pairs(x_ref[...])

        def up(w_ref):
            return _dot(x_lo, w_ref[:half]) + _dot(x_hi, w_ref[half:])

        hid = (jax.nn.silu(up(wgb_ref)) * up(wub_ref)).astype(BF16)
        y = _dot(hid, wdb_ref[...])
        o_ref[...] = _pack_bf16_pairs(y.astype(BF16))

    @pl.when(i >= nused_ref[0])
    def _():
        o_ref[...] = jnp.zeros_like(o_ref)


def _experts(xs, block_e, nused, w_gate, w_up, w_down):
    rows, words = xs.shape
    _, dm, ff = w_gate.shape
    return pl.pallas_call(
        _expert_kernel,
        grid_spec=pltpu.PrefetchScalarGridSpec(
            num_scalar_prefetch=2,
            grid=(rows // MOE_ROWS,),
            in_specs=[
                pl.BlockSpec((MOE_ROWS, words), lambda i, be, nu: (i, 0)),
                pl.BlockSpec((1, dm, ff), lambda i, be, nu: (be[i], 0, 0)),
                pl.BlockSpec((1, dm, ff), lambda i, be, nu: (be[i], 0, 0)),
                pl.BlockSpec((1, ff, dm), lambda i, be, nu: (be[i], 0, 0)),
            ],
            out_specs=pl.BlockSpec((MOE_ROWS, words), lambda i, be, nu: (i, 0)),
            scratch_shapes=[pltpu.VMEM((dm, ff), BF16), pltpu.VMEM((dm, ff), BF16), pltpu.VMEM((ff, dm), BF16)],
        ),
        out_shape=jax.ShapeDtypeStruct((rows, words), jnp.uint32),
        compiler_params=_params(("arbitrary",)),
        name="moe_experts",
    )(block_e, nused, xs, w_gate, w_up, w_down)


def _combine_kernel(d1_ref, d2_ref, x_ref, info_ref, g_ref, ys_ref, o_ref, buf_ref, sem):
    tm, dm = x_ref.shape
    base = pl.program_id(0) * tm

    def row_copy(i, slot, src):
        return pltpu.make_async_copy(ys_ref.at[pl.ds(src, 1)], buf_ref.at[slot, pl.ds(i, 1)], sem)

    def issue(i, c):
        row_copy(i, 0, d1_ref[base + i]).start()
        row_copy(i, 1, d2_ref[base + i]).start()
        return c

    def drain(i, c):
        row_copy(i, 0, d1_ref[base + i]).wait()
        row_copy(i, 1, d2_ref[base + i]).wait()
        return c

    lax.fori_loop(0, tm, issue, 0)
    lax.fori_loop(0, tm, drain, 0)
    info = info_ref[...]
    w1, w2 = info[:, 2:3], info[:, 3:4]
    a_lo, a_hi = _unpack_f32_pairs(buf_ref[0])
    b_lo, b_hi = _unpack_f32_pairs(buf_ref[1])
    y_lo = x_ref[:, :dm // 2] + (w1 * a_lo + w2 * b_lo)
    y_hi = x_ref[:, dm // 2:] + (w1 * a_hi + w2 * b_hi)
    ssq = jnp.sum(y_lo * y_lo, axis=-1, keepdims=True) + jnp.sum(y_hi * y_hi, axis=-1, keepdims=True)
    inv = lax.rsqrt(ssq / dm + EPS)
    o_ref[:, :dm // 2] = y_lo * inv * g_ref[:, :dm // 2]
    o_ref[:, dm // 2:] = y_hi * inv * g_ref[:, dm // 2:]


def _combine(x2, info, gain, ys, d1, d2, tm):
    m, dm = x2.shape
    return pl.pallas_call(
        _combine_kernel,
        grid_spec=pltpu.PrefetchScalarGridSpec(
            num_scalar_prefetch=2,
            grid=(m // tm,),
            in_specs=[
                pl.BlockSpec((tm, dm), lambda i, *_: (i, 0)),
                pl.BlockSpec((tm, LANES), lambda i, *_: (i, 0)),
                pl.BlockSpec((1, dm), lambda i, *_: (0, 0)),
                pl.BlockSpec(memory_space=pl.ANY),
            ],
            out_specs=pl.BlockSpec((tm, dm), lambda i, *_: (i, 0)),
            scratch_shapes=[pltpu.VMEM((2, tm, dm // 2), jnp.uint32), pltpu.SemaphoreType.DMA(())],
        ),
        out_shape=jax.ShapeDtypeStruct((m, dm), F32),
        compiler_params=_params(("arbitrary",)),
        name="moe_combine",
    )(d1, d2, x2, info, gain.reshape(1, dm), ys)


def _moe_and_final_norm(x2, ffn_gain, w_group, b_group, w_router, b_router, w_gate, w_up, w_down, final_gain, tm):
    m, _ = x2.shape
    xn, info, infot, cnt = _router(x2, ffn_gain, w_group, b_group, w_router, b_router, tm)
    counts = cnt[0, :N_EXPERTS].astype(jnp.int32)
    padded = (counts + MOE_ROWS - 1) // MOE_ROWS * MOE_ROWS
    pend = jnp.cumsum(padded)
    pstart = pend - padded
    fields = infot.astype(jnp.int32)
    d1 = pstart[fields[0]] + fields[4]
    d2 = pstart[fields[1]] + fields[5]
    nblocks = 2 * m // MOE_ROWS + N_EXPERTS
    first_row = jnp.arange(nblocks, dtype=jnp.int32) * MOE_ROWS
    block_e = jnp.minimum(jnp.sum(pend[None, :] <= first_row[:, None], axis=1), N_EXPERTS - 1).astype(jnp.int32)
    nused = (pend[-1:] // MOE_ROWS).astype(jnp.int32)
    xs = _dispatch(xn, d1, d2, nblocks * MOE_ROWS, tm)
    ys = _experts(xs, block_e, nused, w_gate, w_up, w_down)
    return _combine(x2, info, final_gain, ys, d1, d2, tm)


def kernel(x, attn_norm, w_in, hg_lb_logits, hg_out_norm, cmp_pos_k, cmp_w1_k, cmp_w2_k, cmp_pos_v, cmp_w1_v,
           cmp_w2_v, nsa_out_norm, w_out, ffn_norm, moe_w_group, moe_b_group, moe_w_router, moe_b_router,
           moe_w_gate, moe_w_up, moe_w_down, final_norm):
    bsz, seq, dm = x.shape
    xt = x.reshape(bsz * seq, dm)
    w = w_in[0]
    w_rest = jnp.concatenate([w[:, :4 * HG_QK], w[:, 4 * HG_QK + NSA_WIDTH + NSA_KV:]], axis=1)
    pad = (-w_rest.shape[1]) % 512
    w_rest = jnp.pad(w_rest, ((0, 0), (0, pad))).astype(BF16)[None]
    rest = _normed_matmul(xt, attn_norm[0], w_rest, 512, 512)
    w_prec = jnp.stack(_split2(w[:, 4 * HG_QK:4 * HG_QK + NSA_WIDTH + NSA_KV]))
    prec = _normed_matmul(xt, attn_norm[0], w_prec, 512, 256)
    rest3 = rest.reshape(bsz, seq, -1)
    prec3 = prec.reshape(bsz, seq, -1)
    y_hg = _hgrn(rest3, hg_lb_logits, hg_out_norm[0], 256)
    y_nsa = _nsa(prec3, rest3, 4 * HG_QK, 4 * HG_QK + 5 * NSA_KV, (cmp_pos_k[0], cmp_w1_k[0], cmp_w2_k[0]),
                 (cmp_pos_v[0], cmp_w1_v[0], cmp_w2_v[0]))
    x2 = _out_proj(y_hg.reshape(bsz * seq, -1), y_nsa.reshape(bsz * seq, -1), nsa_out_norm[0], w_out[0], xt, 512, 512)
    out = _moe_and_final_norm(x2, ffn_norm[0], moe_w_group[0], moe_b_group[0], moe_w_router[0], moe_b_router[0],
                              moe_w_gate[0], moe_w_up[0], moe_w_down[0], final_norm, 256)
    return out.reshape(bsz, seq, dm)
```

```python
import functools

import jax
import jax.numpy as jnp
import numpy as np
from jax import lax
from jax.experimental import pallas as pl
from jax.experimental.pallas import tpu as pltpu

F32 = jnp.float32
BF16 = jnp.bfloat16

EPS = 1e-6
ROPE_THETA = 10000.0
NEG = -1e30
BIG = 1e9
LOG2E = 1.4426950408889634

HG_HEADS = 8
HG_DIM = 128
HG_QK = HG_HEADS * HG_DIM
HG_CHUNK = 64
HG_SUB = 8

NSA_HEADS = 16
NSA_GROUPS = 4
NSA_REP = NSA_HEADS // NSA_GROUPS
NSA_DIM = 64
NSA_WIDTH = NSA_HEADS * NSA_DIM
NSA_KV = NSA_GROUPS * NSA_DIM
NSA_VPAD = 16
CMP_BLOCK = 32
CMP_STRIDE = 16
CMP_HIDDEN = 256
SLC_BLOCK = 64
SLC_TOPK = 16
SLC_LOCAL = 2
WIN = 512

MOE_GROUPS = 4
MOE_EPG = 8
N_EXPERTS = MOE_GROUPS * MOE_EPG
EXPERT_FF = 512

LANES = 128
VMEM_LIMIT = 56 * 1024 * 1024


def _params(semantics, **kw):
    return pltpu.CompilerParams(dimension_semantics=semantics, vmem_limit_bytes=VMEM_LIMIT, **kw)


def _split2(a):
    hi = a.astype(BF16)
    return hi, (a - hi.astype(F32)).astype(BF16)


def _split3(a):
    hi = a.astype(BF16)
    r = a - hi.astype(F32)
    mid = r.astype(BF16)
    return hi, mid, (r - mid.astype(F32)).astype(BF16)


def _dot(a, b):
    return jnp.dot(a, b, preferred_element_type=F32)


def _dot_nt(a, b):
    return lax.dot_general(a, b, (((1,), (1,)), ((), ())), preferred_element_type=F32)


def _dot3(a, b):
    a_hi, a_lo = _split2(a)
    b_hi, b_lo = _split2(b)
    return _dot(a_hi, b_hi) + (_dot(a_hi, b_lo) + _dot(a_lo, b_hi))


def _dot3_nt(a, b):
    a_hi, a_lo = _split2(a)
    b_hi, b_lo = _split2(b)
    return _dot_nt(a_hi, b_hi) + (_dot_nt(a_hi, b_lo) + _dot_nt(a_lo, b_hi))


def _rms(x, gain):
    return x * lax.rsqrt(jnp.mean(x * x, axis=-1, keepdims=True) + EPS) * gain


def _normed_matmul_kernel(x_ref, g_ref, w_ref, o_ref, h_ref):
    parts = w_ref.shape[0]

    @pl.when(pl.program_id(1) == 0)
    def _():
        y = _rms(x_ref[...], g_ref[...])
        hi = y.astype(BF16)
        h_ref[0] = hi
        if parts == 2:
            h_ref[1] = (y - hi.astype(F32)).astype(BF16)

    acc = _dot(h_ref[0], w_ref[0])
    if parts == 2:
        acc = acc + (_dot(h_ref[0], w_ref[1]) + _dot(h_ref[1], w_ref[0]))
    o_ref[...] = acc


def _normed_matmul(x, gain, w_parts, tm, tn):
    m, k = x.shape
    parts, _, n = w_parts.shape
    return pl.pallas_call(
        _normed_matmul_kernel,
        grid=(m // tm, n // tn),
        in_specs=[
            pl.BlockSpec((tm, k), lambda i, j: (i, 0)),
            pl.BlockSpec((1, k), lambda i, j: (0, 0)),
            pl.BlockSpec((parts, k, tn), lambda i, j: (0, 0, j)),
        ],
        out_specs=pl.BlockSpec((tm, tn), lambda i, j: (i, j)),
        out_shape=jax.ShapeDtypeStruct((m, n), F32),
        scratch_shapes=[pltpu.VMEM((parts, tm, k), BF16)],
        compiler_params=_params(("parallel", "arbitrary")),
        name="normed_matmul",
    )(x, gain.reshape(1, k), w_parts)


def _hgrn_consts():
    c, sub = HG_CHUNK, HG_SUB
    tri = np.tile(np.tril(np.ones((c, c), np.float32)), (1, 3))
    gsum = (np.arange(c * sub)[None, :] // sub == np.arange(c)[:, None]).astype(np.float32)
    return jnp.asarray(tri, BF16), jnp.asarray(gsum, BF16)


HG_PAR = 2


def _hgrn_kernel(q_ref, f_ref, i_ref, g_ref, lbl_ref, gain_ref, tri_ref, gsum_ref, o_ref, *scratch):
    c, sub, d = HG_CHUNK, HG_SUB, HG_DIM
    nsub = c // sub
    heads = range(HG_PAR)
    st_refs, p_refs, cl_refs, qs_refs, k_refs = (scratch[i * HG_PAR:(i + 1) * HG_PAR] for i in range(5))

    @pl.when(pl.program_id(2) == 0)
    def _():
        for h in heads:
            st_refs[h][...] = jnp.zeros_like(st_refs[h])

    l0 = lbl_ref[0:1, :]
    l1 = lbl_ref[1:2, :]
    lmax = jnp.maximum(l0, l1)
    e0 = jnp.exp(l0 - lmax)
    lb_all = e0 / (e0 + jnp.exp(l1 - lmax))
    srow = lax.broadcasted_iota(jnp.int32, (sub, d), 0)
    ones = jnp.ones((d, d), BF16)

    def rows_at(x, start):
        parts = ([jnp.zeros((start, d), F32)] if start else []) + [x]
        if start + x.shape[0] < c:
            parts.append(jnp.zeros((c - start - x.shape[0], d), F32))
        return jnp.concatenate(parts, axis=0)

    nchunks = q_ref.shape[1] // c
    cols = [slice(h * d, (h + 1) * d) for h in heads]

    def load(ci):
        rows = pl.ds(pl.multiple_of(ci * c, c), c)
        out = []
        for h in heads:
            lb = lb_all[:, cols[h]]
            f = lb + (1.0 - lb) * jax.nn.sigmoid(f_ref[0, rows, cols[h]])
            bcum = _dot(tri_ref[...], jnp.concatenate(_split3(jnp.log(f)), axis=0))
            out.append((q_ref[0, rows, cols[h]] * (d ** -0.5), 1.0 - f, i_ref[0, rows, cols[h]], bcum))
        return tuple(out)

    def chunk(ci, cur):
        nxt = load(jnp.minimum(ci + 1, nchunks - 1))
        rows = pl.ds(pl.multiple_of(ci * c, c), c)
        q, k, v, bcum = ([cur[h][i] for h in heads] for i in range(4))
        a_off, o_inter = [], []
        for h in heads:
            b = bcum[h]
            edge = [b[i * sub - 1:i * sub] for i in range(1, nsub + 1)]
            cl = b - jnp.concatenate([jnp.zeros((sub, d), F32)] + [jnp.broadcast_to(e, (sub, d)) for e in edge[:-1]],
                                     axis=0)
            cl_refs[h][...] = cl
            qs_refs[h][...] = q[h]
            k_refs[h][...] = k[h]
            qe = q[h] * jnp.exp(cl)
            qcat, kcat = [], []
            for i in range(1, nsub):
                qcat.append(rows_at(qe[i * sub:(i + 1) * sub], i * sub))
                kcat.append(rows_at(k[h][:i * sub] * jnp.exp(edge[i - 1] - b[:i * sub]), 0))
            a_off.append(_dot_nt(jnp.concatenate(qcat, axis=1).astype(BF16),
                                 jnp.concatenate(kcat, axis=1).astype(BF16)))
            st = st_refs[h][...]
            o_inter.append(_dot_nt((q[h] * jnp.exp(b)).astype(BF16), st.astype(BF16)))
            kd = k[h] * jnp.exp(edge[-1] - b)
            st_refs[h][...] = st * jnp.exp(edge[-1]) + _dot(v[h].T.astype(BF16), kd.astype(BF16))
        r2 = []
        for h in heads:
            for t in range(c):
                j0 = (t // sub) * sub
                dlt = cl_refs[h][t:t + 1, :] - cl_refs[h][j0:j0 + sub, :]
                e = jnp.where(srow + j0 <= t, jnp.exp(jnp.minimum(dlt, 0.0)), 0.0)
                p_refs[h][t * sub:(t + 1) * sub, :] = (qs_refs[h][t:t + 1, :] * k_refs[h][j0:j0 + sub, :] * e).astype(BF16)
            r2.append(_dot(p_refs[h][...], ones))
        o = []
        for h in heads:
            o.append(o_inter[h] + _dot(a_off[h].astype(BF16), v[h].astype(BF16)))
        for h in heads:
            x = r2[h].reshape(nsub, sub, sub, d) * v[h].reshape(nsub, 1, sub, d)
            o[h] = o[h] + _dot(gsum_ref[...], x.reshape(c * sub, d).astype(BF16))
        for h in heads:
            gate = jax.nn.silu(g_ref[0, rows, cols[h]])
            o_ref[0, rows, cols[h]] = _rms(o[h], gain_ref[...]) * gate
        return nxt

    lax.fori_loop(0, nchunks, chunk, load(0), unroll=2)


def _hgrn(proj3, lb_logits, out_gain, tseq):
    bsz, seq, _ = proj3.shape
    d, c, sub = HG_DIM, HG_CHUNK, HG_SUB
    wst, gsum = _hgrn_consts()
    groups = HG_HEADS // HG_PAR
    width = HG_PAR * d

    def col(off):
        return pl.BlockSpec((1, tseq, width), lambda b, h, t: (b, t, off * groups + h))

    per_head = [pltpu.VMEM((d, d), F32), pltpu.VMEM((c * sub, d), BF16), pltpu.VMEM((c, d), F32),
                pltpu.VMEM((c, d), F32), pltpu.VMEM((c, d), F32)]
    return pl.pallas_call(
        _hgrn_kernel,
        grid=(bsz, groups, seq // tseq),
        in_specs=[
            col(0), col(1), col(2), col(3),
            pl.BlockSpec((2, width), lambda b, h, t: (0, h)),
            pl.BlockSpec((1, d), lambda b, h, t: (0, 0)),
            pl.BlockSpec(wst.shape, lambda b, h, t: (0, 0)),
            pl.BlockSpec(gsum.shape, lambda b, h, t: (0, 0)),
        ],
        out_specs=pl.BlockSpec((1, tseq, width), lambda b, h, t: (b, t, h)),
        out_shape=jax.ShapeDtypeStruct((bsz, seq, HG_QK), F32),
        scratch_shapes=[s for s in per_head for _ in range(HG_PAR)],
        compiler_params=_params(("parallel", "parallel", "arbitrary")),
        name="hgrn2",
    )(proj3, proj3, proj3, proj3, lb_logits, out_gain.reshape(1, d), wst, gsum)


def _rope(x, cs, sn):
    lane = lax.broadcasted_iota(jnp.int32, x.shape, 1)
    partner = jnp.where(lane % NSA_DIM < NSA_DIM // 2, pltpu.roll(x, LANES - NSA_DIM // 2, 1),
                        pltpu.roll(x, NSA_DIM // 2, 1))
    return x * cs + partner * sn


def _nsa_prep_kernel(q_ref, ksl_ref, vsl_ref, kwn_ref, vwn_ref, cs_ref, sn_ref,
                     qrot_ref, kslo_ref, vslo_ref, kwno_ref, vwno_ref):
    cs = cs_ref[...]
    sn = sn_ref[...]
    for c in range(NSA_WIDTH // LANES):
        cols = slice(c * LANES, (c + 1) * LANES)
        qrot_ref[0, :, cols] = (_rope(q_ref[0, :, cols], cs, sn) * (NSA_DIM ** -0.5 * LOG2E)).astype(BF16)
    tseq = q_ref.shape[1]
    lane = lax.broadcasted_iota(jnp.int32, (tseq, LANES), 1)
    block = (pl.program_id(1) * tseq + lax.broadcasted_iota(jnp.int32, (tseq, LANES), 0)) // SLC_BLOCK
    block_onehot = jnp.where(lane - NSA_DIM == block, 1.0, 0.0)
    ones_rows = jnp.where(lax.broadcasted_iota(jnp.int32, (NSA_VPAD, tseq), 0) == 0, 1.0, 0.0).astype(BF16)
    for c in range(NSA_KV // LANES):
        cols = slice(c * LANES, (c + 1) * LANES)
        ks = _rope(ksl_ref[0, :, cols], cs, sn)
        kw = _rope(kwn_ref[0, :, cols], cs, sn).astype(BF16)
        vs = vsl_ref[0, :, cols].T.astype(BF16)
        vw = vwn_ref[0, :, cols].T.astype(BF16)
        for half in range(LANES // NSA_DIM):
            g = c * (LANES // NSA_DIM) + half
            hs = slice(half * NSA_DIM, (half + 1) * NSA_DIM)
            ks_g = ks if half == 0 else pltpu.roll(ks, NSA_DIM, 1)
            kslo_ref[0, g] = jnp.where(lane < NSA_DIM, ks_g, block_onehot).astype(BF16)
            kwno_ref[0, g] = kw[:, hs]
            vslo_ref[0, g, :NSA_DIM] = vs[hs, :]
            vslo_ref[0, g, NSA_DIM:] = ones_rows
            vwno_ref[0, g, :NSA_DIM] = vw[hs, :]
            vwno_ref[0, g, NSA_DIM:] = ones_rows


def _nsa_prep(prec3, rest3, kv_off, tseq):
    bsz, seq, _ = prec3.shape
    half = NSA_DIM // 2
    inv = 1.0 / (ROPE_THETA ** (jnp.arange(0, NSA_DIM, 2, dtype=F32) / NSA_DIM))
    ang = jnp.arange(seq, dtype=F32)[:, None] * inv[None, :]
    cs = jnp.tile(jnp.cos(ang), (1, LANES // half))
    sn = jnp.tile(jnp.concatenate([-jnp.sin(ang), jnp.sin(ang)], axis=1), (1, LANES // NSA_DIM))
    kvb = kv_off // NSA_KV

    def kv_in(i):
        return pl.BlockSpec((1, tseq, NSA_KV), lambda b, t: (b, t, kvb + i))

    assert NSA_DIM + seq // SLC_BLOCK <= LANES

    def k_out(width):
        return (pl.BlockSpec((1, NSA_GROUPS, tseq, width), lambda b, t: (b, 0, t, 0)),
                jax.ShapeDtypeStruct((bsz, NSA_GROUPS, seq, width), BF16))

    (ksl_out, ksl_shape), (kwn_out, kwn_shape) = k_out(LANES), k_out(NSA_DIM)
    v_out = pl.BlockSpec((1, NSA_GROUPS, NSA_DIM + NSA_VPAD, tseq), lambda b, t: (b, 0, 0, t))
    v_shape = jax.ShapeDtypeStruct((bsz, NSA_GROUPS, NSA_DIM + NSA_VPAD, seq), BF16)
    tab = pl.BlockSpec((tseq, LANES), lambda b, t: (t, 0))
    return pl.pallas_call(
        _nsa_prep_kernel,
        grid=(bsz, seq // tseq),
        in_specs=[pl.BlockSpec((1, tseq, NSA_WIDTH), lambda b, t: (b, t, 0)), kv_in(0), kv_in(1), kv_in(2), kv_in(3),
                  tab, tab],
        out_specs=[pl.BlockSpec((1, tseq, NSA_WIDTH), lambda b, t: (b, t, 0)), ksl_out, v_out, kwn_out, v_out],
        out_shape=[jax.ShapeDtypeStruct((bsz, seq, NSA_WIDTH), BF16), ksl_shape, v_shape, kwn_shape, v_shape],
        compiler_params=_params(("parallel", "parallel")),
        name="nsa_prep",
    )(prec3, rest3, rest3, rest3, rest3, cs, sn)


def _compress_kernel(u_ref, pos_ref, w1_ref, w2_ref, o_ref, *, precise):
    mm = _dot3 if precise else (lambda a, b: _dot(a.astype(BF16), b.astype(BF16)))
    u = u_ref[0]
    nu = u.shape[0]
    ya = mm(u + pos_ref[0:1, :], w1_ref[0])
    yb = mm(u + pos_ref[1:2, :], w1_ref[1])
    hid = ya + pltpu.roll(yb, nu - 1, 0)
    o_ref[0] = mm(jax.nn.gelu(hid), w2_ref[...])


def _compress(kv, pos, w1, w2, precise):
    bsz, seq, _ = kv.shape
    nu = seq // CMP_STRIDE
    width = CMP_STRIDE * NSA_DIM
    u = kv.reshape(bsz, nu, CMP_STRIDE, NSA_GROUPS, NSA_DIM).transpose(0, 3, 1, 2, 4).reshape(bsz * NSA_GROUPS, nu, width)
    return pl.pallas_call(
        functools.partial(_compress_kernel, precise=precise),
        grid=(bsz * NSA_GROUPS,),
        in_specs=[
            pl.BlockSpec((1, nu, width), lambda i: (i, 0, 0)),
            pl.BlockSpec((2, width), lambda i: (0, 0)),
            pl.BlockSpec((2, width, CMP_HIDDEN), lambda i: (0, 0, 0)),
            pl.BlockSpec((CMP_HIDDEN, NSA_DIM), lambda i: (0, 0)),
        ],
        out_specs=pl.BlockSpec((1, nu, NSA_DIM), lambda i: (i, 0, 0)),
        out_shape=jax.ShapeDtypeStruct((bsz * NSA_GROUPS, nu, NSA_DIM), F32),
        compiler_params=_params(("parallel",)),
        name="nsa_compress",
    )(u, pos.reshape(2, width), w1.reshape(2, width, CMP_HIDDEN), w2)


def _nsa_attn_kernel(qraw_ref, qrot_ref, kc_ref, vct_ref, ksl_ref, vslt_ref, kwn_ref, vwnt_ref, gate_ref, aggt_ref,
                     o_ref, sa_ref, sb_ref, *win_refs, topk, tk):
    tq = qraw_ref.shape[1]
    nu = kc_ref.shape[2]
    ns = aggt_ref.shape[0]
    rep, dk = NSA_REP, NSA_DIM
    qs = pl.program_id(2) * tq
    tpos = qs + lax.broadcasted_iota(jnp.int32, (1, tq), 1)

    qrt = (qraw_ref[0] * dk ** -0.5).T
    kc_hi, kc_lo = _split2(kc_ref[0, 0])
    vct = vct_ref[0, 0].astype(BF16)
    crow = lax.broadcasted_iota(jnp.int32, (nu, tq), 0)
    m_c = (crow * CMP_STRIDE + CMP_BLOCK - 1 <= tpos) & (crow < nu - 1)
    q_hi, q_lo = _split2(jnp.concatenate([qrt[r * dk:(r + 1) * dk] for r in range(rep)], axis=1))
    s_all = _dot(jnp.concatenate([kc_hi, kc_hi, kc_lo], axis=1),
                 jnp.concatenate([q_hi, q_lo, q_hi], axis=0))

    qt = qrot_ref[0].astype(F32).T.astype(BF16)
    qt_all = jnp.concatenate([qt[r * dk:(r + 1) * dk] for r in range(rep)], axis=1)
    hi = (qs + tq) // tk

    def key_tile(ktc):
        return pl.ds(pl.multiple_of(ktc * tk, tk), tk)

    def scores(k_ref, ktc, dst_ref):
        dst_ref[...] = _dot(k_ref[0, 0, key_tile(ktc), :], qt_all)

    win_tiles = [hi - len(win_refs) + j for j in range(len(win_refs))]
    for kt, dst_ref in zip(win_tiles, win_refs):
        scores(kwn_ref, jnp.maximum(kt, 0), dst_ref)
    psum = jnp.zeros((nu, tq), F32)
    p_all = []
    for r in range(rep):
        s = jnp.where(m_c, s_all[:, r * tq:(r + 1) * tq], NEG)
        e = jnp.exp(s - jnp.max(s, axis=0, keepdims=True))
        p = jnp.where(m_c, e * (1.0 / jnp.sum(e, axis=0, keepdims=True)), 0.0)
        psum = psum + p
        p_all.append(p.astype(BF16))
    o_c_all = _dot(vct, jnp.concatenate(p_all, axis=1))
    o_c = [o_c_all[:, r * tq:(r + 1) * tq] for r in range(rep)]

    p_hi, p_lo = _split2(psum)
    imp = _dot(aggt_ref[...], p_hi) + _dot(aggt_ref[...], p_lo)
    jrow = lax.broadcasted_iota(jnp.int32, (ns, tq), 0)
    dj = jnp.right_shift(tpos, SLC_BLOCK.bit_length() - 1) - jrow
    forced = (jrow == 0) | ((dj >= 0) & (dj < SLC_LOCAL))
    imp = jnp.where(forced, BIG, jnp.where(jrow * SLC_BLOCK <= tpos, imp, -BIG))
    sub8 = lax.broadcasted_iota(jnp.int32, (8, tq), 0)
    chunks = [imp[c * 8:(c + 1) * 8] for c in range(ns // 8)]
    ranks = [jnp.zeros((8, tq), F32) for _ in range(ns // 8)]
    for jp in range(ns):
        row = chunks[jp // 8][jp % 8:jp % 8 + 1]
        for c in range(ns // 8):
            if c < jp // 8:
                ahead = jnp.where(row > chunks[c], 1.0, 0.0)
            elif c > jp // 8:
                ahead = jnp.where(row >= chunks[c], 1.0, 0.0)
            else:
                tie = jnp.where(sub8 > jp % 8, 1.0, 0.0)
                ahead = jnp.where(row > chunks[c], 1.0, jnp.where(row == chunks[c], tie, 0.0))
            ranks[c] = ranks[c] + ahead
    selt = [jnp.where(ranks[c] < topk, 0.0, NEG) for c in range(ns // 8)]

    def consume(vt_ref, ktc, src_ref, carry, mask=None):
        vt = vt_ref[0, 0, :, key_tile(ktc)]
        out = []
        for r in range(rep):
            m_old, acc = carry[r]
            s = src_ref[:, r * tq:(r + 1) * tq]
            if mask is not None:
                s = jnp.where(mask, s, NEG)
            m_new = jnp.maximum(m_old, jnp.max(s, axis=0, keepdims=True))
            alpha = jnp.exp2(m_old - m_new)
            p = jnp.exp2(s - m_new).astype(BF16)
            out.append((m_new, acc * alpha + _dot(vt, p)))
        return tuple(out)

    def normalised(carry):
        return [acc[:dk] * (1.0 / acc[dk:dk + 1]) for _, acc in carry]

    init = tuple((jnp.full((1, tq), NEG, F32), jnp.zeros((dk + NSA_VPAD, tq), F32)) for _ in range(rep))
    krow = lax.broadcasted_iota(jnp.int32, (tk, tq), 0)
    unseen = 1 << 30

    carry = init
    for kt, src_ref in zip(win_tiles, win_refs):
        ktc = jnp.maximum(kt, 0)
        dlt = tpos - (jnp.where(kt >= 0, ktc * tk, unseen) + krow)
        carry = consume(vwnt_ref, ktc, src_ref, carry, mask=pltpu.bitcast(dlt, jnp.uint32) < jnp.uint32(WIN))
    o_w = normalised(carry)

    selb = jnp.concatenate([jnp.concatenate(selt, axis=0)] * rep, axis=1).astype(BF16)
    pad = jnp.zeros((LANES - dk - ns, rep * tq), BF16)
    qt_sel = jnp.concatenate([qt_all, selb] + ([pad] if LANES > dk + ns else []), axis=0)

    def sel_scores(ktc, dst_ref):
        dst_ref[...] = _dot(ksl_ref[0, 0, key_tile(ktc), :], qt_sel)

    def pair(i, carry):
        kt = 2 * i
        sel_scores(kt + 1, sb_ref)
        carry = consume(vslt_ref, kt, sa_ref, carry)
        sel_scores(kt + 2, sa_ref)
        return consume(vslt_ref, kt + 1, sb_ref, carry)

    past = hi - 1
    sel_scores(0, sa_ref)
    carry = lax.fori_loop(0, past // 2, pair, init)
    carry = lax.cond(past % 2 == 1, lambda c: consume(vslt_ref, past - 1, sa_ref, c), lambda c: c, carry)
    sel_scores(past, sb_ref)
    carry = consume(vslt_ref, past, sb_ref, carry, mask=past * tk + krow <= tpos)
    o_s = normalised(carry)

    gate = jax.nn.sigmoid(gate_ref[0, 0])
    o_t = [gate[3 * r:3 * r + 1] * o_c[r] + gate[3 * r + 1:3 * r + 2] * o_s[r] + gate[3 * r + 2:3 * r + 3] * o_w[r]
           for r in range(rep)]
    o_ref[0] = jnp.concatenate(o_t, axis=0).T


def _nsa_attn(prec3, qrot, kc, vct, ksl, vslt, kwn, vwnt, gates_t, tq, tk):
    bsz, seq, _ = qrot.shape
    nu = seq // CMP_STRIDE
    ns = seq // SLC_BLOCK
    ci = np.arange(nu)[None, :]
    sj = np.arange(ns)[:, None]
    overlap = (ci * CMP_STRIDE < (sj + 1) * SLC_BLOCK) & (ci * CMP_STRIDE + CMP_BLOCK > sj * SLC_BLOCK) & (ci < nu - 1)
    aggt = jnp.asarray(overlap, BF16)
    gw = NSA_REP * NSA_DIM
    assert tq % tk == 0 and seq % tq == 0
    win_tiles = -(-(WIN - 1) // tk) + tq // tk

    def q_spec():
        return pl.BlockSpec((1, tq, gw), lambda b, g, t: (b, t, g))

    def per_group(rows, cols):
        return pl.BlockSpec((1, 1, rows, cols), lambda b, g, t: (b, g, 0, 0))

    return pl.pallas_call(
        functools.partial(_nsa_attn_kernel, topk=min(SLC_TOPK, ns), tk=tk),
        grid=(bsz, NSA_GROUPS, seq // tq),
        in_specs=[q_spec(), q_spec(), per_group(nu, NSA_DIM), per_group(NSA_DIM, nu),
                  per_group(seq, LANES), per_group(NSA_DIM + NSA_VPAD, seq),
                  per_group(seq, NSA_DIM), per_group(NSA_DIM + NSA_VPAD, seq),
                  pl.BlockSpec((1, 1, 3 * NSA_REP, tq), lambda b, g, t: (b, g, 0, t)),
                  pl.BlockSpec((ns, nu), lambda b, g, t: (0, 0))],
        out_specs=q_spec(),
        out_shape=jax.ShapeDtypeStruct((bsz, seq, NSA_WIDTH), F32),
        scratch_shapes=[pltpu.VMEM((tk, NSA_REP * tq), F32)] * (2 + win_tiles),
        compiler_params=_params(("parallel", "parallel", "arbitrary")),
        name="nsa_attention",
    )(prec3, qrot, kc, vct, ksl, vslt, kwn, vwnt, gates_t, aggt)


def _nsa(prec3, rest3, kv_off, gate_off, cmp_k, cmp_v):
    bsz, seq, _ = prec3.shape
    nu = seq // CMP_STRIDE
    qrot, ksl, vslt, kwn, vwnt = _nsa_prep(prec3, rest3, kv_off + NSA_KV, min(seq, 512))
    kc = _compress(prec3[:, :, NSA_WIDTH:NSA_WIDTH + NSA_KV], *cmp_k, precise=True)
    vc = _compress(rest3[:, :, kv_off:kv_off + NSA_KV], *cmp_v, precise=False)
    kc = kc.reshape(bsz, NSA_GROUPS, nu, NSA_DIM)
    vct = vc.reshape(bsz, NSA_GROUPS, nu, NSA_DIM).transpose(0, 1, 3, 2)
    gates_t = rest3[:, :, gate_off:gate_off + 3 * NSA_HEADS].reshape(bsz, seq, NSA_GROUPS, 3 * NSA_REP).transpose(0, 2, 3, 1)
    return _nsa_attn(prec3, qrot, kc, vct, ksl, vslt, kwn, vwnt, gates_t, 256, 256)


def _out_proj_kernel(yh_ref, yn_ref, g_ref, w_ref, x_ref, o_ref, y_ref):
    @pl.when(pl.program_id(1) == 0)
    def _():
        wh = yh_ref.shape[1]
        y_ref[:, :wh] = yh_ref[...].astype(BF16)
        y_ref[:, wh:] = _rms(yn_ref[...], g_ref[...]).astype(BF16)

    o_ref[...] = x_ref[...] + _dot(y_ref[...], w_ref[...])


def _out_proj(y_hg, y_nsa, nsa_gain, w_out, x, tm, tn):
    m, dm = x.shape
    wh, wn = y_hg.shape[1], y_nsa.shape[1]
    return pl.pallas_call(
        _out_proj_kernel,
        grid=(m // tm, dm // tn),
        in_specs=[
            pl.BlockSpec((tm, wh), lambda i, j: (i, 0)),
            pl.BlockSpec((tm, wn), lambda i, j: (i, 0)),
            pl.BlockSpec((1, wn), lambda i, j: (0, 0)),
            pl.BlockSpec((wh + wn, tn), lambda i, j: (0, j)),
            pl.BlockSpec((tm, tn), lambda i, j: (i, j)),
        ],
        out_specs=pl.BlockSpec((tm, tn), lambda i, j: (i, j)),
        out_shape=jax.ShapeDtypeStruct((m, dm), F32),
        scratch_shapes=[pltpu.VMEM((tm, wh + wn), BF16)],
        compiler_params=_params(("parallel", "arbitrary")),
        name="out_proj",
    )(y_hg, y_nsa, nsa_gain.reshape(1, wn), w_out.astype(BF16), x)


MOE_ROWS = 256


def _pack_bf16_pairs(hi):
    n = hi.shape[1] // 2
    bits = pltpu.bitcast(hi.astype(F32), jnp.uint32)
    return jnp.right_shift(bits[:, :n], jnp.uint32(16)) | (bits[:, n:] & jnp.uint32(0xFFFF0000))


def _unpack_f32_pairs(words):
    lo = pltpu.bitcast(jnp.left_shift(words, jnp.uint32(16)), F32)
    hi = pltpu.bitcast(words & jnp.uint32(0xFFFF0000), F32)
    return lo, hi


def _unpack_bf16_pairs(words):
    lo, hi = _unpack_f32_pairs(words)
    return lo.astype(BF16), hi.astype(BF16)


def _router_kernel(x_ref, g_ref, w_ref, b_ref, tri_ref, xn_ref, info_ref, infot_ref, cnt_ref, carry_ref):
    @pl.when(pl.program_id(0) == 0)
    def _():
        carry_ref[...] = jnp.zeros_like(carry_ref)

    xn = _rms(x_ref[...], g_ref[...])
    hi, lo = _split2(xn)
    xn_ref[...] = _pack_bf16_pairs(hi)
    logits = _dot(hi, w_ref[0]) + (_dot(hi, w_ref[1]) + _dot(lo, w_ref[0])) + b_ref[...]
    lane = lax.broadcasted_iota(jnp.int32, logits.shape, 1).astype(F32)
    none = float(LANES)

    def first_max(mask):
        top = jnp.max(jnp.where(mask, logits, -jnp.inf), axis=-1, keepdims=True)
        return top, jnp.min(jnp.where(mask & (logits == top), lane, none), axis=-1, keepdims=True)

    is_g = lane < MOE_GROUPS
    gmax, gsel = first_max(is_g)
    gw = 1.0 / jnp.sum(jnp.where(is_g, jnp.exp(logits - gmax), 0.0), axis=-1, keepdims=True)
    lo_lane = MOE_GROUPS + gsel * MOE_EPG
    in_grp = (lane >= lo_lane) & (lane < lo_lane + MOE_EPG)
    v1, i1 = first_max(in_grp)
    v2, i2 = first_max(in_grp & (lane != i1))
    e = jnp.exp(v2 - v1)
    w1 = gw / (1.0 + e)
    w2 = gw * e / (1.0 + e)
    e1 = i1 - MOE_GROUPS
    e2 = i2 - MOE_GROUPS
    onehot = jnp.where((lane == e1) | (lane == e2), 1.0, 0.0)
    before = _dot(tri_ref[...], onehot.astype(BF16)) + carry_ref[...]
    r1 = jnp.sum(jnp.where(lane == e1, before, 0.0), axis=-1, keepdims=True)
    r2 = jnp.sum(jnp.where(lane == e2, before, 0.0), axis=-1, keepdims=True)
    carry_ref[...] = carry_ref[...] + jnp.sum(onehot, axis=0, keepdims=True)
    cnt_ref[...] = carry_ref[...]
    info = jnp.zeros_like(logits)
    for idx, val in enumerate((e1, e2, w1, w2, r1, r2)):
        info = jnp.where(lane == idx, val, info)
    info_ref[...] = info
    infot_ref[...] = info.T[:8]


def _router(x2, gain, w_group, b_group, w_router, b_router, tm):
    m, dm = x2.shape
    wcat = jnp.pad(jnp.concatenate([w_group, w_router], axis=1), ((0, 0), (0, LANES - MOE_GROUPS - N_EXPERTS)))
    bcat = jnp.pad(jnp.concatenate([b_group, b_router]), (0, LANES - MOE_GROUPS - N_EXPERTS)).reshape(1, LANES)
    tri = jnp.asarray(np.tril(np.ones((tm, tm), np.float32), -1), BF16)
    return pl.pallas_call(
        _router_kernel,
        grid=(m // tm,),
        in_specs=[
            pl.BlockSpec((tm, dm), lambda i: (i, 0)),
            pl.BlockSpec((1, dm), lambda i: (0, 0)),
            pl.BlockSpec((2, dm, LANES), lambda i: (0, 0, 0)),
            pl.BlockSpec((1, LANES), lambda i: (0, 0)),
            pl.BlockSpec((tm, tm), lambda i: (0, 0)),
        ],
        out_specs=[
            pl.BlockSpec((tm, dm // 2), lambda i: (i, 0)),
            pl.BlockSpec((tm, LANES), lambda i: (i, 0)),
            pl.BlockSpec((8, tm), lambda i: (0, i)),
            pl.BlockSpec((1, LANES), lambda i: (0, 0)),
        ],
        out_shape=[
            jax.ShapeDtypeStruct((m, dm // 2), jnp.uint32),
            jax.ShapeDtypeStruct((m, LANES), F32),
            jax.ShapeDtypeStruct((8, m), F32),
            jax.ShapeDtypeStruct((1, LANES), F32),
        ],
        scratch_shapes=[pltpu.VMEM((1, LANES), F32)],
        compiler_params=_params(("arbitrary",)),
        name="moe_router",
    )(x2, gain.reshape(1, dm), jnp.stack(_split2(wcat)), bcat, tri)


def _dispatch_kernel(d1_ref, d2_ref, xn_ref, xs_in_ref, xs_ref, sem):
    del xs_in_ref
    tm = xn_ref.shape[0]
    base = pl.program_id(0) * tm

    def row_copy(i, dest):
        return pltpu.make_async_copy(xn_ref.at[pl.ds(i, 1)], xs_ref.at[pl.ds(dest, 1)], sem)

    def issue(i, c):
        row_copy(i, d1_ref[base + i]).start()
        row_copy(i, d2_ref[base + i]).start()
        return c

    lax.fori_loop(0, tm, issue, 0, unroll=8)
    for _ in range(2):
        pltpu.make_async_copy(xn_ref, xs_ref.at[pl.ds(0, tm)], sem).wait()


def _dispatch(xn, d1, d2, rows, tm):
    m, words = xn.shape
    return pl.pallas_call(
        _dispatch_kernel,
        grid_spec=pltpu.PrefetchScalarGridSpec(
            num_scalar_prefetch=2,
            grid=(m // tm,),
            in_specs=[pl.BlockSpec((tm, words), lambda i, *_: (i, 0)), pl.BlockSpec(memory_space=pl.ANY)],
            out_specs=pl.BlockSpec(memory_space=pl.ANY),
            scratch_shapes=[pltpu.SemaphoreType.DMA(())],
        ),
        out_shape=jax.ShapeDtypeStruct((rows, words), jnp.uint32),
        input_output_aliases={3: 0},
        compiler_params=_params(("arbitrary",)),
        name="moe_dispatch",
    )(d1, d2, xn, jnp.zeros((rows, words), jnp.uint32))


def _expert_kernel(be_ref, nused_ref, x_ref, wg_ref, wu_ref, wd_ref, o_ref, wgb_ref, wub_ref, wdb_ref):
    i = pl.program_id(0)

    @pl.when((i == 0) | (be_ref[i] != be_ref[jnp.maximum(i - 1, 0)]))
    def _():
        wgb_ref[...] = wg_ref[0].astype(BF16)
        wub_ref[...] = wu_ref[0].astype(BF16)
        wdb_ref[...] = wd_ref[0].astype(BF16)

    @pl.when(i < nused_ref[0])
    def _():
        half = wgb_ref.shape[0] // 2
        x_lo, x_hi = _unpack_bf16_pairs(x_ref[...])

        def up(w_ref):
            return _dot(x_lo, w_ref[:half]) + _dot(x_hi, w_ref[half:])

        hid = (jax.nn.silu(up(wgb_ref)) * up(wub_ref)).astype(BF16)
        y = _dot(hid, wdb_ref[...])
        o_ref[...] = _pack_bf16_pairs(y.astype(BF16))

    @pl.when(i >= nused_ref[0])
    def _():
        o_ref[...] = jnp.zeros_like(o_ref)


def _experts(xs, block_e, nused, w_gate, w_up, w_down):
    rows, words = xs.shape
    _, dm, ff = w_gate.shape
    return pl.pallas_call(
        _expert_kernel,
        grid_spec=pltpu.PrefetchScalarGridSpec(
            num_scalar_prefetch=2,
            grid=(rows // MOE_ROWS,),
            in_specs=[
                pl.BlockSpec((MOE_ROWS, words), lambda i, be, nu: (i, 0)),
                pl.BlockSpec((1, dm, ff), lambda i, be, nu: (be[i], 0, 0)),
                pl.BlockSpec((1, dm, ff), lambda i, be, nu: (be[i], 0, 0)),
                pl.BlockSpec((1, ff, dm), lambda i, be, nu: (be[i], 0, 0)),
            ],
            out_specs=pl.BlockSpec((MOE_ROWS, words), lambda i, be, nu: (i, 0)),
            scratch_shapes=[pltpu.VMEM((dm, ff), BF16), pltpu.VMEM((dm, ff), BF16), pltpu.VMEM((ff, dm), BF16)],
        ),
        out_shape=jax.ShapeDtypeStruct((rows, words), jnp.uint32),
        compiler_params=_params(("arbitrary",)),
        name="moe_experts",
    )(block_e, nused, xs, w_gate, w_up, w_down)


def _combine_kernel(d1_ref, d2_ref, x_ref, info_ref, g_ref, ys_ref, o_ref, buf_ref, sem):
    tm, dm = x_ref.shape
    base = pl.program_id(0) * tm

    def row_copy(i, slot, src):
        return pltpu.make_async_copy(ys_ref.at[pl.ds(src, 1)], buf_ref.at[slot, pl.ds(i, 1)], sem)

    def issue(i, c):
        row_copy(i, 0, d1_ref[base + i]).start()
        row_copy(i, 1, d2_ref[base + i]).start()
        return c

    lax.fori_loop(0, tm, issue, 0, unroll=8)
    for slot in range(2):
        pltpu.make_async_copy(ys_ref.at[pl.ds(0, tm)], buf_ref.at[slot], sem).wait()
    info = info_ref[...]
    w1, w2 = info[:, 2:3], info[:, 3:4]
    a_lo, a_hi = _unpack_f32_pairs(buf_ref[0])
    b_lo, b_hi = _unpack_f32_pairs(buf_ref[1])
    y_lo = x_ref[:, :dm // 2] + (w1 * a_lo + w2 * b_lo)
    y_hi = x_ref[:, dm // 2:] + (w1 * a_hi + w2 * b_hi)
    ssq = jnp.sum(y_lo * y_lo, axis=-1, keepdims=True) + jnp.sum(y_hi * y_hi, axis=-1, keepdims=True)
    inv = lax.rsqrt(ssq / dm + EPS)
    o_ref[:, :dm // 2] = y_lo * inv * g_ref[:, :dm // 2]
    o_ref[:, dm // 2:] = y_hi * inv * g_ref[:, dm // 2:]


def _combine(x2, info, gain, ys, d1, d2, tm):
    m, dm = x2.shape
    return pl.pallas_call(
        _combine_kernel,
        grid_spec=pltpu.PrefetchScalarGridSpec(
            num_scalar_prefetch=2,
            grid=(m // tm,),
            in_specs=[
                pl.BlockSpec((tm, dm), lambda i, *_: (i, 0)),
                pl.BlockSpec((tm, LANES), lambda i, *_: (i, 0)),
                pl.BlockSpec((1, dm), lambda i, *_: (0, 0)),
                pl.BlockSpec(memory_space=pl.ANY),
            ],
            out_specs=pl.BlockSpec((tm, dm), lambda i, *_: (i, 0)),
            scratch_shapes=[pltpu.VMEM((2, tm, dm // 2), jnp.uint32), pltpu.SemaphoreType.DMA(())],
        ),
        out_shape=jax.ShapeDtypeStruct((m, dm), F32),
        compiler_params=_params(("arbitrary",)),
        name="moe_combine",
    )(d1, d2, x2, info, gain.reshape(1, dm), ys)


def _moe_and_final_norm(x2, ffn_gain, w_group, b_group, w_router, b_router, w_gate, w_up, w_down, final_gain, tm):
    m, _ = x2.shape
    xn, info, infot, cnt = _router(x2, ffn_gain, w_group, b_group, w_router, b_router, tm)
    counts = cnt[0, :N_EXPERTS].astype(jnp.int32)
    padded = (counts + MOE_ROWS - 1) // MOE_ROWS * MOE_ROWS
    pend = jnp.cumsum(padded)
    pstart = pend - padded
    fields = infot.astype(jnp.int32)
    d1 = pstart[fields[0]] + fields[4]
    d2 = pstart[fields[1]] + fields[5]
    nblocks = 2 * m // MOE_ROWS + N_EXPERTS
    first_row = jnp.arange(nblocks, dtype=jnp.int32) * MOE_ROWS
    block_e = jnp.minimum(jnp.sum(pend[None, :] <= first_row[:, None], axis=1), N_EXPERTS - 1).astype(jnp.int32)
    nused = (pend[-1:] // MOE_ROWS).astype(jnp.int32)
    xs = _dispatch(xn, d1, d2, nblocks * MOE_ROWS, tm)
    ys = _experts(xs, block_e, nused, w_gate, w_up, w_down)
    return _combine(x2, info, final_gain, ys, d1, d2, tm)


def kernel(x, attn_norm, w_in, hg_lb_logits, hg_out_norm, cmp_pos_k, cmp_w1_k, cmp_w2_k, cmp_pos_v, cmp_w1_v,
           cmp_w2_v, nsa_out_norm, w_out, ffn_norm, moe_w_group, moe_b_group, moe_w_router, moe_b_router,
           moe_w_gate, moe_w_up, moe_w_down, final_norm):
    bsz, seq, dm = x.shape
    xt = x.reshape(bsz * seq, dm)
    w = w_in[0]
    w_rest = jnp.concatenate([w[:, :4 * HG_QK], w[:, 4 * HG_QK + NSA_WIDTH + NSA_KV:]], axis=1)
    pad = (-w_rest.shape[1]) % 512
    w_rest = jnp.pad(w_rest, ((0, 0), (0, pad))).astype(BF16)[None]
    rest = _normed_matmul(xt, attn_norm[0], w_rest, 512, 512)
    w_prec = jnp.stack(_split2(w[:, 4 * HG_QK:4 * HG_QK + NSA_WIDTH + NSA_KV]))
    prec = _normed_matmul(xt, attn_norm[0], w_prec, 512, 256)
    rest3 = rest.reshape(bsz, seq, -1)
    prec3 = prec.reshape(bsz, seq, -1)
    y_hg = _hgrn(rest3, hg_lb_logits, hg_out_norm[0], 256)
    y_nsa = _nsa(prec3, rest3, 4 * HG_QK, 4 * HG_QK + 5 * NSA_KV, (cmp_pos_k[0], cmp_w1_k[0], cmp_w2_k[0]),
                 (cmp_pos_v[0], cmp_w1_v[0], cmp_w2_v[0]))
    x2 = _out_proj(y_hg.reshape(bsz * seq, -1), y_nsa.reshape(bsz * seq, -1), nsa_out_norm[0], w_out[0], xt, 512, 512)
    out = _moe_and_final_norm(x2, ffn_norm[0], moe_w_group[0], moe_b_group[0], moe_w_router[0], moe_b_router[0],
                              moe_w_gate[0], moe_w_up[0], moe_w_down[0], final_norm, 256)
    return out.reshape(bsz, seq, dm)
```

```python
import functools

import jax
import jax.numpy as jnp
import numpy as np
from jax import lax
from jax.experimental import pallas as pl
from jax.experimental.pallas import tpu as pltpu

F32 = jnp.float32
BF16 = jnp.bfloat16

EPS = 1e-6
ROPE_THETA = 10000.0
NEG = -1e30
BIG = 1e9
LOG2E = 1.4426950408889634

HG_HEADS = 8
HG_DIM = 128
HG_QK = HG_HEADS * HG_DIM
HG_CHUNK = 64
HG_SUB = 8

NSA_HEADS = 16
NSA_GROUPS = 4
NSA_REP = NSA_HEADS // NSA_GROUPS
NSA_DIM = 64
NSA_WIDTH = NSA_HEADS * NSA_DIM
NSA_KV = NSA_GROUPS * NSA_DIM
NSA_VPAD = 16
CMP_BLOCK = 32
CMP_STRIDE = 16
CMP_HIDDEN = 256
SLC_BLOCK = 64
SLC_TOPK = 16
SLC_LOCAL = 2
WIN = 512

MOE_GROUPS = 4
MOE_EPG = 8
N_EXPERTS = MOE_GROUPS * MOE_EPG
EXPERT_FF = 512

LANES = 128
VMEM_LIMIT = 56 * 1024 * 1024


def _params(semantics, **kw):
    return pltpu.CompilerParams(dimension_semantics=semantics, vmem_limit_bytes=VMEM_LIMIT, **kw)


def _split2(a):
    hi = a.astype(BF16)
    return hi, (a - hi.astype(F32)).astype(BF16)


def _split3(a):
    hi = a.astype(BF16)
    r = a - hi.astype(F32)
    mid = r.astype(BF16)
    return hi, mid, (r - mid.astype(F32)).astype(BF16)


def _dot(a, b):
    return jnp.dot(a, b, preferred_element_type=F32)


def _dot_nt(a, b):
    return lax.dot_general(a, b, (((1,), (1,)), ((), ())), preferred_element_type=F32)


def _dot3(a, b):
    a_hi, a_lo = _split2(a)
    b_hi, b_lo = _split2(b)
    return _dot(a_hi, b_hi) + (_dot(a_hi, b_lo) + _dot(a_lo, b_hi))


def _dot3_nt(a, b):
    a_hi, a_lo = _split2(a)
    b_hi, b_lo = _split2(b)
    return _dot_nt(a_hi, b_hi) + (_dot_nt(a_hi, b_lo) + _dot_nt(a_lo, b_hi))


def _rms(x, gain):
    return x * lax.rsqrt(jnp.mean(x * x, axis=-1, keepdims=True) + EPS) * gain


def _normed_matmul_kernel(x_ref, g_ref, w_ref, o_ref, h_ref):
    parts = w_ref.shape[0]

    @pl.when(pl.program_id(1) == 0)
    def _():
        y = _rms(x_ref[...], g_ref[...])
        hi = y.astype(BF16)
        h_ref[0] = hi
        if parts == 2:
            h_ref[1] = (y - hi.astype(F32)).astype(BF16)

    acc = _dot(h_ref[0], w_ref[0])
    if parts == 2:
        acc = acc + (_dot(h_ref[0], w_ref[1]) + _dot(h_ref[1], w_ref[0]))
    o_ref[...] = acc


def _normed_matmul(x, gain, w_parts, tm, tn, skip=None):
    m, k = x.shape
    parts, _, n = w_parts.shape
    first, count = skip if skip else (n // tn, 0)
    return pl.pallas_call(
        _normed_matmul_kernel,
        grid=(m // tm, n // tn - count),
        in_specs=[
            pl.BlockSpec((tm, k), lambda i, j: (i, 0)),
            pl.BlockSpec((1, k), lambda i, j: (0, 0)),
            pl.BlockSpec((parts, k, tn), lambda i, j: (0, 0, jnp.where(j < first, j, j + count))),
        ],
        out_specs=pl.BlockSpec((tm, tn), lambda i, j: (i, j)),
        out_shape=jax.ShapeDtypeStruct((m, n - count * tn), F32),
        scratch_shapes=[pltpu.VMEM((parts, tm, k), BF16)],
        compiler_params=_params(("parallel", "arbitrary")),
        name="normed_matmul",
    )(x, gain.reshape(1, k), w_parts)


def _hgrn_consts():
    c, sub = HG_CHUNK, HG_SUB
    tri = np.tile(np.tril(np.ones((c, c), np.float32)), (1, 3))
    gsum = (np.arange(c * sub)[None, :] // sub == np.arange(c)[:, None]).astype(np.float32)
    return jnp.asarray(tri, BF16), jnp.asarray(gsum, BF16)


HG_PAR = 2


def _hgrn_kernel(q_ref, f_ref, i_ref, g_ref, lbl_ref, gain_ref, tri_ref, gsum_ref, o_ref, *scratch):
    c, sub, d = HG_CHUNK, HG_SUB, HG_DIM
    nsub = c // sub
    heads = range(HG_PAR)
    st_refs, p_refs, cl_refs, qs_refs, k_refs = (scratch[i * HG_PAR:(i + 1) * HG_PAR] for i in range(5))

    @pl.when(pl.program_id(2) == 0)
    def _():
        for h in heads:
            st_refs[h][...] = jnp.zeros_like(st_refs[h])

    l0 = lbl_ref[0:1, :]
    l1 = lbl_ref[1:2, :]
    lmax = jnp.maximum(l0, l1)
    e0 = jnp.exp(l0 - lmax)
    lb_all = e0 / (e0 + jnp.exp(l1 - lmax))
    srow = lax.broadcasted_iota(jnp.int32, (sub, d), 0)
    ones = jnp.ones((d, d), BF16)

    def rows_at(x, start):
        parts = ([jnp.zeros((start, d), F32)] if start else []) + [x]
        if start + x.shape[0] < c:
            parts.append(jnp.zeros((c - start - x.shape[0], d), F32))
        return jnp.concatenate(parts, axis=0)

    nchunks = q_ref.shape[1] // c
    cols = [slice(h * d, (h + 1) * d) for h in heads]

    def load(ci):
        rows = pl.ds(pl.multiple_of(ci * c, c), c)
        out = []
        for h in heads:
            lb = lb_all[:, cols[h]]
            f = lb + (1.0 - lb) * jax.nn.sigmoid(f_ref[0, rows, cols[h]])
            bcum = _dot(tri_ref[...], jnp.concatenate(_split3(jnp.log(f)), axis=0))
            out.append((q_ref[0, rows, cols[h]] * (d ** -0.5), 1.0 - f, i_ref[0, rows, cols[h]], bcum))
        return tuple(out)

    def chunk(ci, cur):
        nxt = load(jnp.minimum(ci + 1, nchunks - 1))
        rows = pl.ds(pl.multiple_of(ci * c, c), c)
        q, k, v, bcum = ([cur[h][i] for h in heads] for i in range(4))
        a_off, o_inter = [], []
        for h in heads:
            b = bcum[h]
            edge = [b[i * sub - 1:i * sub] for i in range(1, nsub + 1)]
            cl = b - jnp.concatenate([jnp.zeros((sub, d), F32)] + [jnp.broadcast_to(e, (sub, d)) for e in edge[:-1]],
                                     axis=0)
            cl_refs[h][...] = cl
            qs_refs[h][...] = q[h]
            k_refs[h][...] = k[h]
            qe = q[h] * jnp.exp(cl)
            qcat, kcat = [], []
            for i in range(1, nsub):
                qcat.append(rows_at(qe[i * sub:(i + 1) * sub], i * sub))
                kcat.append(rows_at(k[h][:i * sub] * jnp.exp(edge[i - 1] - b[:i * sub]), 0))
            a_off.append(_dot_nt(jnp.concatenate(qcat, axis=1).astype(BF16),
                                 jnp.concatenate(kcat, axis=1).astype(BF16)))
            st = st_refs[h][...]
            o_inter.append(_dot_nt((q[h] * jnp.exp(b)).astype(BF16), st.astype(BF16)))
            kd = k[h] * jnp.exp(edge[-1] - b)
            st_refs[h][...] = st * jnp.exp(edge[-1]) + _dot(v[h].T.astype(BF16), kd.astype(BF16))
        r2 = []
        for h in heads:
            for t in range(c):
                j0 = (t // sub) * sub
                dlt = cl_refs[h][t:t + 1, :] - cl_refs[h][j0:j0 + sub, :]
                e = jnp.where(srow + j0 <= t, jnp.exp(jnp.minimum(dlt, 0.0)), 0.0)
                p_refs[h][t * sub:(t + 1) * sub, :] = (qs_refs[h][t:t + 1, :] * k_refs[h][j0:j0 + sub, :] * e).astype(BF16)
            r2.append(_dot(p_refs[h][...], ones))
        o = []
        for h in heads:
            o.append(o_inter[h] + _dot(a_off[h].astype(BF16), v[h].astype(BF16)))
        for h in heads:
            x = r2[h].reshape(nsub, sub, sub, d) * v[h].reshape(nsub, 1, sub, d)
            o[h] = o[h] + _dot(gsum_ref[...], x.reshape(c * sub, d).astype(BF16))
        for h in heads:
            gate = jax.nn.silu(g_ref[0, rows, cols[h]])
            o_ref[0, rows, cols[h]] = (_rms(o[h], gain_ref[...]) * gate).astype(o_ref.dtype)
        return nxt

    lax.fori_loop(0, nchunks, chunk, load(0), unroll=2)


def _hgrn(proj3, lb_logits, out_gain, tseq):
    bsz, seq, _ = proj3.shape
    d, c, sub = HG_DIM, HG_CHUNK, HG_SUB
    wst, gsum = _hgrn_consts()
    groups = HG_HEADS // HG_PAR
    width = HG_PAR * d

    def col(off):
        return pl.BlockSpec((1, tseq, width), lambda b, h, t: (b, t, off * groups + h))

    per_head = [pltpu.VMEM((d, d), F32), pltpu.VMEM((c * sub, d), BF16), pltpu.VMEM((c, d), F32),
                pltpu.VMEM((c, d), F32), pltpu.VMEM((c, d), F32)]
    return pl.pallas_call(
        _hgrn_kernel,
        grid=(bsz, groups, seq // tseq),
        in_specs=[
            col(0), col(1), col(2), col(3),
            pl.BlockSpec((2, width), lambda b, h, t: (0, h)),
            pl.BlockSpec((1, d), lambda b, h, t: (0, 0)),
            pl.BlockSpec(wst.shape, lambda b, h, t: (0, 0)),
            pl.BlockSpec(gsum.shape, lambda b, h, t: (0, 0)),
        ],
        out_specs=pl.BlockSpec((1, tseq, width), lambda b, h, t: (b, t, h)),
        out_shape=jax.ShapeDtypeStruct((bsz, seq, HG_QK), BF16),
        scratch_shapes=[s for s in per_head for _ in range(HG_PAR)],
        compiler_params=_params(("parallel", "parallel", "arbitrary")),
        name="hgrn2",
    )(proj3, proj3, proj3, proj3, lb_logits, out_gain.reshape(1, d), wst, gsum)


def _rope(x, cs, sn):
    lane = lax.broadcasted_iota(jnp.int32, x.shape, 1)
    partner = jnp.where(lane % NSA_DIM < NSA_DIM // 2, pltpu.roll(x, LANES - NSA_DIM // 2, 1),
                        pltpu.roll(x, NSA_DIM // 2, 1))
    return x * cs + partner * sn


def _nsa_prep_kernel(q_ref, ksl_ref, vsl_ref, kwn_ref, vwn_ref, cs_ref, sn_ref,
                     qrot_ref, kslo_ref, vslo_ref, kwno_ref, vwno_ref):
    cs = cs_ref[...]
    sn = sn_ref[...]
    for c in range(NSA_WIDTH // LANES):
        cols = slice(c * LANES, (c + 1) * LANES)
        qrot_ref[0, :, cols] = (_rope(q_ref[0, :, cols], cs, sn) * (NSA_DIM ** -0.5 * LOG2E)).astype(BF16)
    tseq = q_ref.shape[1]
    lane = lax.broadcasted_iota(jnp.int32, (tseq, LANES), 1)
    block = (pl.program_id(1) * tseq + lax.broadcasted_iota(jnp.int32, (tseq, LANES), 0)) // SLC_BLOCK
    block_onehot = jnp.where(lane - NSA_DIM == block, 1.0, 0.0)
    ones_rows = jnp.where(lax.broadcasted_iota(jnp.int32, (NSA_VPAD, tseq), 0) == 0, 1.0, 0.0).astype(BF16)
    for c in range(NSA_KV // LANES):
        cols = slice(c * LANES, (c + 1) * LANES)
        ks = _rope(ksl_ref[0, :, cols], cs, sn)
        kw = _rope(kwn_ref[0, :, cols], cs, sn).astype(BF16)
        vs = vsl_ref[0, :, cols].T.astype(BF16)
        vw = vwn_ref[0, :, cols].T.astype(BF16)
        for half in range(LANES // NSA_DIM):
            g = c * (LANES // NSA_DIM) + half
            hs = slice(half * NSA_DIM, (half + 1) * NSA_DIM)
            ks_g = ks if half == 0 else pltpu.roll(ks, NSA_DIM, 1)
            kslo_ref[0, g] = jnp.where(lane < NSA_DIM, ks_g, block_onehot).astype(BF16)
            kwno_ref[0, g] = kw[:, hs]
            vslo_ref[0, g, :NSA_DIM] = vs[hs, :]
            vslo_ref[0, g, NSA_DIM:] = ones_rows
            vwno_ref[0, g, :NSA_DIM] = vw[hs, :]
            vwno_ref[0, g, NSA_DIM:] = ones_rows


def _nsa_prep(prec3, rest3, kv_off, tseq):
    bsz, seq, _ = prec3.shape
    half = NSA_DIM // 2
    inv = 1.0 / (ROPE_THETA ** (jnp.arange(0, NSA_DIM, 2, dtype=F32) / NSA_DIM))
    ang = jnp.arange(seq, dtype=F32)[:, None] * inv[None, :]
    cs = jnp.tile(jnp.cos(ang), (1, LANES // half))
    sn = jnp.tile(jnp.concatenate([-jnp.sin(ang), jnp.sin(ang)], axis=1), (1, LANES // NSA_DIM))
    kvb = kv_off // NSA_KV

    def kv_in(i):
        return pl.BlockSpec((1, tseq, NSA_KV), lambda b, t: (b, t, kvb + i))

    assert NSA_DIM + seq // SLC_BLOCK <= LANES

    def k_out(width):
        return (pl.BlockSpec((1, NSA_GROUPS, tseq, width), lambda b, t: (b, 0, t, 0)),
                jax.ShapeDtypeStruct((bsz, NSA_GROUPS, seq, width), BF16))

    (ksl_out, ksl_shape), (kwn_out, kwn_shape) = k_out(LANES), k_out(NSA_DIM)
    v_out = pl.BlockSpec((1, NSA_GROUPS, NSA_DIM + NSA_VPAD, tseq), lambda b, t: (b, 0, 0, t))
    v_shape = jax.ShapeDtypeStruct((bsz, NSA_GROUPS, NSA_DIM + NSA_VPAD, seq), BF16)
    tab = pl.BlockSpec((tseq, LANES), lambda b, t: (t, 0))
    return pl.pallas_call(
        _nsa_prep_kernel,
        grid=(bsz, seq // tseq),
        in_specs=[pl.BlockSpec((1, tseq, NSA_WIDTH), lambda b, t: (b, t, 0)), kv_in(0), kv_in(1), kv_in(2), kv_in(3),
                  tab, tab],
        out_specs=[pl.BlockSpec((1, tseq, NSA_WIDTH), lambda b, t: (b, t, 0)), ksl_out, v_out, kwn_out, v_out],
        out_shape=[jax.ShapeDtypeStruct((bsz, seq, NSA_WIDTH), BF16), ksl_shape, v_shape, kwn_shape, v_shape],
        compiler_params=_params(("parallel", "parallel")),
        name="nsa_prep",
    )(prec3, rest3, rest3, rest3, rest3, cs, sn)


def _compress_kernel(u_ref, pos_ref, w1_ref, w2_ref, o_ref, *, precise):
    mm = _dot3 if precise else (lambda a, b: _dot(a.astype(BF16), b.astype(BF16)))
    u = u_ref[0]
    nu = u.shape[0]
    ya = mm(u + pos_ref[0:1, :], w1_ref[0])
    yb = mm(u + pos_ref[1:2, :], w1_ref[1])
    hid = ya + pltpu.roll(yb, nu - 1, 0)
    o_ref[0] = mm(jax.nn.gelu(hid), w2_ref[...])


def _compress(kv, pos, w1, w2, precise):
    bsz, seq, _ = kv.shape
    nu = seq // CMP_STRIDE
    width = CMP_STRIDE * NSA_DIM
    u = kv.reshape(bsz, nu, CMP_STRIDE, NSA_GROUPS, NSA_DIM).transpose(0, 3, 1, 2, 4).reshape(bsz * NSA_GROUPS, nu, width)
    return pl.pallas_call(
        functools.partial(_compress_kernel, precise=precise),
        grid=(bsz * NSA_GROUPS,),
        in_specs=[
            pl.BlockSpec((1, nu, width), lambda i: (i, 0, 0)),
            pl.BlockSpec((2, width), lambda i: (0, 0)),
            pl.BlockSpec((2, width, CMP_HIDDEN), lambda i: (0, 0, 0)),
            pl.BlockSpec((CMP_HIDDEN, NSA_DIM), lambda i: (0, 0)),
        ],
        out_specs=pl.BlockSpec((1, nu, NSA_DIM), lambda i: (i, 0, 0)),
        out_shape=jax.ShapeDtypeStruct((bsz * NSA_GROUPS, nu, NSA_DIM), F32),
        compiler_params=_params(("parallel",)),
        name="nsa_compress",
    )(u, pos.reshape(2, width), w1.reshape(2, width, CMP_HIDDEN), w2)


def _nsa_attn_kernel(qraw_ref, qrot_ref, kc_ref, vct_ref, ksl_ref, vslt_ref, kwn_ref, vwnt_ref, gate_ref, aggt_ref,
                     o_ref, sa_ref, sb_ref, *win_refs, topk, tk):
    tq = qraw_ref.shape[1]
    nu = kc_ref.shape[2]
    ns = aggt_ref.shape[0]
    rep, dk = NSA_REP, NSA_DIM
    qs = pl.program_id(2) * tq
    tpos = qs + lax.broadcasted_iota(jnp.int32, (1, tq), 1)

    qrt = (qraw_ref[0] * dk ** -0.5).T
    kc_hi, kc_lo = _split2(kc_ref[0, 0])
    vct = vct_ref[0, 0].astype(BF16)
    crow = lax.broadcasted_iota(jnp.int32, (nu, tq), 0)
    m_c = (crow * CMP_STRIDE + CMP_BLOCK - 1 <= tpos) & (crow < nu - 1)
    q_hi, q_lo = _split2(jnp.concatenate([qrt[r * dk:(r + 1) * dk] for r in range(rep)], axis=1))
    s_all = _dot(jnp.concatenate([kc_hi, kc_hi, kc_lo], axis=1),
                 jnp.concatenate([q_hi, q_lo, q_hi], axis=0))

    qt = qrot_ref[0].astype(F32).T.astype(BF16)
    qt_all = jnp.concatenate([qt[r * dk:(r + 1) * dk] for r in range(rep)], axis=1)
    hi = (qs + tq) // tk

    def key_tile(ktc):
        return pl.ds(pl.multiple_of(ktc * tk, tk), tk)

    def scores(k_ref, ktc, dst_ref):
        dst_ref[...] = _dot(k_ref[0, 0, key_tile(ktc), :], qt_all)

    win_tiles = [hi - len(win_refs) + j for j in range(len(win_refs))]
    for kt, dst_ref in zip(win_tiles, win_refs):
        scores(kwn_ref, jnp.maximum(kt, 0), dst_ref)
    psum = jnp.zeros((nu, tq), F32)
    p_all = []
    for r in range(rep):
        s = jnp.where(m_c, s_all[:, r * tq:(r + 1) * tq], NEG)
        e = jnp.exp(s - jnp.max(s, axis=0, keepdims=True))
        p = jnp.where(m_c, e * (1.0 / jnp.sum(e, axis=0, keepdims=True)), 0.0)
        psum = psum + p
        p_all.append(p.astype(BF16))
    o_c_all = _dot(vct, jnp.concatenate(p_all, axis=1))
    o_c = [o_c_all[:, r * tq:(r + 1) * tq] for r in range(rep)]

    p_hi, p_lo = _split2(psum)
    imp = _dot(aggt_ref[...], p_hi) + _dot(aggt_ref[...], p_lo)
    jrow = lax.broadcasted_iota(jnp.int32, (ns, tq), 0)
    dj = jnp.right_shift(tpos, SLC_BLOCK.bit_length() - 1) - jrow
    forced = (jrow == 0) | ((dj >= 0) & (dj < SLC_LOCAL))
    imp = jnp.where(forced, BIG, jnp.where(jrow * SLC_BLOCK <= tpos, imp, -BIG))
    sub8 = lax.broadcasted_iota(jnp.int32, (8, tq), 0)
    chunks = [imp[c * 8:(c + 1) * 8] for c in range(ns // 8)]
    ranks = [jnp.zeros((8, tq), F32) for _ in range(ns // 8)]
    for jp in range(ns):
        row = chunks[jp // 8][jp % 8:jp % 8 + 1]
        for c in range(ns // 8):
            if c < jp // 8:
                ahead = jnp.where(row > chunks[c], 1.0, 0.0)
            elif c > jp // 8:
                ahead = jnp.where(row >= chunks[c], 1.0, 0.0)
            else:
                tie = jnp.where(sub8 > jp % 8, 1.0, 0.0)
                ahead = jnp.where(row > chunks[c], 1.0, jnp.where(row == chunks[c], tie, 0.0))
            ranks[c] = ranks[c] + ahead
    selt = [jnp.where(ranks[c] < topk, 0.0, NEG) for c in range(ns // 8)]

    def consume(vt_ref, ktc, src_ref, carry, mask=None):
        vt = vt_ref[0, 0, :, key_tile(ktc)]
        out = []
        for r in range(rep):
            m_old, acc = carry[r]
            s = src_ref[:, r * tq:(r + 1) * tq]
            if mask is not None:
                s = jnp.where(mask, s, NEG)
            m_new = jnp.maximum(m_old, jnp.max(s, axis=0, keepdims=True))
            alpha = jnp.exp2(m_old - m_new)
            p = jnp.exp2(s - m_new).astype(BF16)
            out.append((m_new, acc * alpha + _dot(vt, p)))
        return tuple(out)

    def normalised(carry):
        return [acc[:dk] * (1.0 / acc[dk:dk + 1]) for _, acc in carry]

    init = tuple((jnp.full((1, tq), NEG, F32), jnp.zeros((dk + NSA_VPAD, tq), F32)) for _ in range(rep))
    krow = lax.broadcasted_iota(jnp.int32, (tk, tq), 0)
    unseen = 1 << 30

    carry = init
    for kt, src_ref in zip(win_tiles, win_refs):
        ktc = jnp.maximum(kt, 0)
        dlt = tpos - (jnp.where(kt >= 0, ktc * tk, unseen) + krow)
        carry = consume(vwnt_ref, ktc, src_ref, carry, mask=pltpu.bitcast(dlt, jnp.uint32) < jnp.uint32(WIN))
    o_w = normalised(carry)

    selb = jnp.concatenate([jnp.concatenate(selt, axis=0)] * rep, axis=1).astype(BF16)
    pad = jnp.zeros((LANES - dk - ns, rep * tq), BF16)
    qt_sel = jnp.concatenate([qt_all, selb] + ([pad] if LANES > dk + ns else []), axis=0)

    def sel_scores(ktc, dst_ref):
        dst_ref[...] = _dot(ksl_ref[0, 0, key_tile(ktc), :], qt_sel)

    def pair(i, carry):
        kt = 2 * i
        sel_scores(kt + 1, sb_ref)
        carry = consume(vslt_ref, kt, sa_ref, carry)
        sel_scores(kt + 2, sa_ref)
        return consume(vslt_ref, kt + 1, sb_ref, carry)

    past = hi - 1
    sel_scores(0, sa_ref)
    carry = lax.fori_loop(0, past // 2, pair, init)
    carry = lax.cond(past % 2 == 1, lambda c: consume(vslt_ref, past - 1, sa_ref, c), lambda c: c, carry)
    sel_scores(past, sb_ref)
    carry = consume(vslt_ref, past, sb_ref, carry, mask=past * tk + krow <= tpos)
    o_s = normalised(carry)

    gate = jax.nn.sigmoid(gate_ref[0, 0])
    o_t = [gate[3 * r:3 * r + 1] * o_c[r] + gate[3 * r + 1:3 * r + 2] * o_s[r] + gate[3 * r + 2:3 * r + 3] * o_w[r]
           for r in range(rep)]
    o_ref[0] = jnp.concatenate(o_t, axis=0).T.astype(o_ref.dtype)


def _nsa_attn(prec3, qrot, kc, vct, ksl, vslt, kwn, vwnt, gates_t, tq, tk):
    bsz, seq, _ = qrot.shape
    nu = seq // CMP_STRIDE
    ns = seq // SLC_BLOCK
    ci = np.arange(nu)[None, :]
    sj = np.arange(ns)[:, None]
    overlap = (ci * CMP_STRIDE < (sj + 1) * SLC_BLOCK) & (ci * CMP_STRIDE + CMP_BLOCK > sj * SLC_BLOCK) & (ci < nu - 1)
    aggt = jnp.asarray(overlap, BF16)
    gw = NSA_REP * NSA_DIM
    assert tq % tk == 0 and seq % tq == 0
    win_tiles = -(-(WIN - 1) // tk) + tq // tk

    def q_spec():
        return pl.BlockSpec((1, tq, gw), lambda b, g, t: (b, t, g))

    def per_group(rows, cols):
        return pl.BlockSpec((1, 1, rows, cols), lambda b, g, t: (b, g, 0, 0))

    return pl.pallas_call(
        functools.partial(_nsa_attn_kernel, topk=min(SLC_TOPK, ns), tk=tk),
        grid=(bsz, NSA_GROUPS, seq // tq),
        in_specs=[q_spec(), q_spec(), per_group(nu, NSA_DIM), per_group(NSA_DIM, nu),
                  per_group(seq, LANES), per_group(NSA_DIM + NSA_VPAD, seq),
                  per_group(seq, NSA_DIM), per_group(NSA_DIM + NSA_VPAD, seq),
                  pl.BlockSpec((1, 1, 3 * NSA_REP, tq), lambda b, g, t: (b, g, 0, t)),
                  pl.BlockSpec((ns, nu), lambda b, g, t: (0, 0))],
        out_specs=q_spec(),
        out_shape=jax.ShapeDtypeStruct((bsz, seq, NSA_WIDTH), BF16),
        scratch_shapes=[pltpu.VMEM((tk, NSA_REP * tq), F32)] * (2 + win_tiles),
        compiler_params=_params(("parallel", "parallel", "arbitrary")),
        name="nsa_attention",
    )(prec3, qrot, kc, vct, ksl, vslt, kwn, vwnt, gates_t, aggt)


def _nsa(prec3, rest3, kv_off, gate_off, cmp_k, cmp_v):
    bsz, seq, _ = prec3.shape
    nu = seq // CMP_STRIDE
    qrot, ksl, vslt, kwn, vwnt = _nsa_prep(prec3, rest3, kv_off + NSA_KV, min(seq, 512))
    kc = _compress(prec3[:, :, NSA_WIDTH:NSA_WIDTH + NSA_KV], *cmp_k, precise=True)
    vc = _compress(rest3[:, :, kv_off:kv_off + NSA_KV], *cmp_v, precise=False)
    kc = kc.reshape(bsz, NSA_GROUPS, nu, NSA_DIM)
    vct = vc.reshape(bsz, NSA_GROUPS, nu, NSA_DIM).transpose(0, 1, 3, 2)
    gates_t = rest3[:, :, gate_off:gate_off + 3 * NSA_HEADS].reshape(bsz, seq, NSA_GROUPS, 3 * NSA_REP).transpose(0, 2, 3, 1)
    return _nsa_attn(prec3, qrot, kc, vct, ksl, vslt, kwn, vwnt, gates_t, 256, 256)


def _out_proj_kernel(yh_ref, yn_ref, g_ref, w_ref, x_ref, o_ref, y_ref):
    @pl.when(pl.program_id(1) == 0)
    def _():
        wh = yh_ref.shape[1]
        y_ref[:, :wh] = yh_ref[...]
        y_ref[:, wh:] = _rms(yn_ref[...].astype(F32), g_ref[...]).astype(BF16)

    o_ref[...] = x_ref[...] + _dot(y_ref[...], w_ref[...])


def _out_proj(y_hg, y_nsa, nsa_gain, w_out, x, tm, tn):
    m, dm = x.shape
    wh, wn = y_hg.shape[1], y_nsa.shape[1]
    return pl.pallas_call(
        _out_proj_kernel,
        grid=(m // tm, dm // tn),
        in_specs=[
            pl.BlockSpec((tm, wh), lambda i, j: (i, 0)),
            pl.BlockSpec((tm, wn), lambda i, j: (i, 0)),
            pl.BlockSpec((1, wn), lambda i, j: (0, 0)),
            pl.BlockSpec((wh + wn, tn), lambda i, j: (0, j)),
            pl.BlockSpec((tm, tn), lambda i, j: (i, j)),
        ],
        out_specs=pl.BlockSpec((tm, tn), lambda i, j: (i, j)),
        out_shape=jax.ShapeDtypeStruct((m, dm), F32),
        scratch_shapes=[pltpu.VMEM((tm, wh + wn), BF16)],
        compiler_params=_params(("parallel", "arbitrary")),
        name="out_proj",
    )(y_hg, y_nsa, nsa_gain.reshape(1, wn), w_out.astype(BF16), x)


MOE_ROWS = 256


def _pack_bf16_pairs(hi):
    n = hi.shape[1] // 2
    bits = pltpu.bitcast(hi.astype(F32), jnp.uint32)
    return jnp.right_shift(bits[:, :n], jnp.uint32(16)) | (bits[:, n:] & jnp.uint32(0xFFFF0000))


def _unpack_f32_pairs(words):
    lo = pltpu.bitcast(jnp.left_shift(words, jnp.uint32(16)), F32)
    hi = pltpu.bitcast(words & jnp.uint32(0xFFFF0000), F32)
    return lo, hi


def _unpack_bf16_pairs(words):
    lo, hi = _unpack_f32_pairs(words)
    return lo.astype(BF16), hi.astype(BF16)


def _router_kernel(x_ref, g_ref, w_ref, b_ref, tri_ref, xn_ref, info_ref, infot_ref, cnt_ref, carry_ref):
    @pl.when(pl.program_id(0) == 0)
    def _():
        carry_ref[...] = jnp.zeros_like(carry_ref)

    xn = _rms(x_ref[...], g_ref[...])
    hi, lo = _split2(xn)
    xn_ref[...] = _pack_bf16_pairs(hi)
    logits = _dot(hi, w_ref[0]) + (_dot(hi, w_ref[1]) + _dot(lo, w_ref[0])) + b_ref[...]
    lane = lax.broadcasted_iota(jnp.int32, logits.shape, 1).astype(F32)
    none = float(LANES)

    def first_max(mask):
        top = jnp.max(jnp.where(mask, logits, -jnp.inf), axis=-1, keepdims=True)
        return top, jnp.min(jnp.where(mask & (logits == top), lane, none), axis=-1, keepdims=True)

    is_g = lane < MOE_GROUPS
    gmax, gsel = first_max(is_g)
    gw = 1.0 / jnp.sum(jnp.where(is_g, jnp.exp(logits - gmax), 0.0), axis=-1, keepdims=True)
    lo_lane = MOE_GROUPS + gsel * MOE_EPG
    in_grp = (lane >= lo_lane) & (lane < lo_lane + MOE_EPG)
    v1, i1 = first_max(in_grp)
    v2, i2 = first_max(in_grp & (lane != i1))
    e = jnp.exp(v2 - v1)
    w1 = gw / (1.0 + e)
    w2 = gw * e / (1.0 + e)
    e1 = i1 - MOE_GROUPS
    e2 = i2 - MOE_GROUPS
    onehot = jnp.where((lane == e1) | (lane == e2), 1.0, 0.0)
    before = _dot(tri_ref[...], onehot.astype(BF16)) + carry_ref[...]
    r1 = jnp.sum(jnp.where(lane == e1, before, 0.0), axis=-1, keepdims=True)
    r2 = jnp.sum(jnp.where(lane == e2, before, 0.0), axis=-1, keepdims=True)
    carry_ref[...] = carry_ref[...] + jnp.sum(onehot, axis=0, keepdims=True)
    cnt_ref[...] = carry_ref[...]
    info = jnp.zeros_like(logits)
    for idx, val in enumerate((e1, e2, w1, w2, r1, r2)):
        info = jnp.where(lane == idx, val, info)
    info_ref[...] = info
    infot_ref[...] = info.T[:8]


def _router(x2, gain, w_group, b_group, w_router, b_router, tm):
    m, dm = x2.shape
    wcat = jnp.pad(jnp.concatenate([w_group, w_router], axis=1), ((0, 0), (0, LANES - MOE_GROUPS - N_EXPERTS)))
    bcat = jnp.pad(jnp.concatenate([b_group, b_router]), (0, LANES - MOE_GROUPS - N_EXPERTS)).reshape(1, LANES)
    tri = jnp.asarray(np.tril(np.ones((tm, tm), np.float32), -1), BF16)
    return pl.pallas_call(
        _router_kernel,
        grid=(m // tm,),
        in_specs=[
            pl.BlockSpec((tm, dm), lambda i: (i, 0)),
            pl.BlockSpec((1, dm), lambda i: (0, 0)),
            pl.BlockSpec((2, dm, LANES), lambda i: (0, 0, 0)),
            pl.BlockSpec((1, LANES), lambda i: (0, 0)),
            pl.BlockSpec((tm, tm), lambda i: (0, 0)),
        ],
        out_specs=[
            pl.BlockSpec((tm, dm // 2), lambda i: (i, 0)),
            pl.BlockSpec((tm, LANES), lambda i: (i, 0)),
            pl.BlockSpec((8, tm), lambda i: (0, i)),
            pl.BlockSpec((1, LANES), lambda i: (0, 0)),
        ],
        out_shape=[
            jax.ShapeDtypeStruct((m, dm // 2), jnp.uint32),
            jax.ShapeDtypeStruct((m, LANES), F32),
            jax.ShapeDtypeStruct((8, m), F32),
            jax.ShapeDtypeStruct((1, LANES), F32),
        ],
        scratch_shapes=[pltpu.VMEM((1, LANES), F32)],
        compiler_params=_params(("arbitrary",)),
        name="moe_router",
    )(x2, gain.reshape(1, dm), jnp.stack(_split2(wcat)), bcat, tri)


def _dispatch_kernel(d1_ref, d2_ref, xn_ref, xs_in_ref, xs_ref, sem):
    del xs_in_ref
    tm = xn_ref.shape[0]
    base = pl.program_id(0) * tm

    def row_copy(i, dest):
        return pltpu.make_async_copy(xn_ref.at[pl.ds(i, 1)], xs_ref.at[pl.ds(dest, 1)], sem)

    def issue(i, c):
        row_copy(i, d1_ref[base + i]).start()
        row_copy(i, d2_ref[base + i]).start()
        return c

    lax.fori_loop(0, tm, issue, 0, unroll=8)
    for _ in range(2):
        pltpu.make_async_copy(xn_ref, xs_ref.at[pl.ds(0, tm)], sem).wait()


def _dispatch(xn, d1, d2, rows, tm):
    m, words = xn.shape
    return pl.pallas_call(
        _dispatch_kernel,
        grid_spec=pltpu.PrefetchScalarGridSpec(
            num_scalar_prefetch=2,
            grid=(m // tm,),
            in_specs=[pl.BlockSpec((tm, words), lambda i, *_: (i, 0)), pl.BlockSpec(memory_space=pl.ANY)],
            out_specs=pl.BlockSpec(memory_space=pl.ANY),
            scratch_shapes=[pltpu.SemaphoreType.DMA(())],
        ),
        out_shape=jax.ShapeDtypeStruct((rows, words), jnp.uint32),
        input_output_aliases={3: 0},
        compiler_params=_params(("arbitrary",)),
        name="moe_dispatch",
    )(d1, d2, xn, jnp.zeros((rows, words), jnp.uint32))


def _expert_kernel(be_ref, nused_ref, x_ref, wg_ref, wu_ref, wd_ref, o_ref, wgb_ref, wub_ref, wdb_ref):
    i = pl.program_id(0)

    @pl.when((i == 0) | (be_ref[i] != be_ref[jnp.maximum(i - 1, 0)]))
    def _():
        wgb_ref[...] = wg_ref[0].astype(BF16)
        wub_ref[...] = wu_ref[0].astype(BF16)
        wdb_ref[...] = wd_ref[0].astype(BF16)

    @pl.when(i < nused_ref[0])
    def _():
        half = wgb_ref.shape[0] // 2
        x_lo, x_hi = _unpack_bf16_pairs(x_ref[...])

        def up(w_ref):
            return _dot(x_lo, w_ref[:half]) + _dot(x_hi, w_ref[half:])

        hid = (jax.nn.silu(up(wgb_ref)) * up(wub_ref)).astype(BF16)
        y = _dot(hid, wdb_ref[...])
        o_ref[...] = _pack_bf16_pairs(y.astype(BF16))

    @pl.when(i >= nused_ref[0])
    def _():
        o_ref[...] = jnp.zeros_like(o_ref)


def _experts(xs, block_e, nused, w_gate, w_up, w_down):
    rows, words = xs.shape
    _, dm, ff = w_gate.shape
    return pl.pallas_call(
        _expert_kernel,
        grid_spec=pltpu.PrefetchScalarGridSpec(
            num_scalar_prefetch=2,
            grid=(rows // MOE_ROWS,),
            in_specs=[
                pl.BlockSpec((MOE_ROWS, words), lambda i, be, nu: (i, 0)),
                pl.BlockSpec((1, dm, ff), lambda i, be, nu: (be[i], 0, 0)),
                pl.BlockSpec((1, dm, ff), lambda i, be, nu: (be[i], 0, 0)),
                pl.BlockSpec((1, ff, dm), lambda i, be, nu: (be[i], 0, 0)),
            ],
            out_specs=pl.BlockSpec((MOE_ROWS, words), lambda i, be, nu: (i, 0)),
            scratch_shapes=[pltpu.VMEM((dm, ff), BF16), pltpu.VMEM((dm, ff), BF16), pltpu.VMEM((ff, dm), BF16)],
        ),
        out_shape=jax.ShapeDtypeStruct((rows, words), jnp.uint32),
        compiler_params=_params(("arbitrary",)),
        name="moe_experts",
    )(block_e, nused, xs, w_gate, w_up, w_down)


def _combine_kernel(d1_ref, d2_ref, x_ref, info_ref, g_ref, ys_ref, o_ref, buf_ref, sem):
    tm, dm = x_ref.shape
    base = pl.program_id(0) * tm

    def row_copy(i, slot, src):
        return pltpu.make_async_copy(ys_ref.at[pl.ds(src, 1)], buf_ref.at[slot, pl.ds(i, 1)], sem)

    def issue(i, c):
        row_copy(i, 0, d1_ref[base + i]).start()
        row_copy(i, 1, d2_ref[base + i]).start()
        return c

    lax.fori_loop(0, tm, issue, 0, unroll=8)
    for slot in range(2):
        pltpu.make_async_copy(ys_ref.at[pl.ds(0, tm)], buf_ref.at[slot], sem).wait()
    info = info_ref[...]
    w1, w2 = info[:, 2:3], info[:, 3:4]
    a_lo, a_hi = _unpack_f32_pairs(buf_ref[0])
    b_lo, b_hi = _unpack_f32_pairs(buf_ref[1])
    y_lo = x_ref[:, :dm // 2] + (w1 * a_lo + w2 * b_lo)
    y_hi = x_ref[:, dm // 2:] + (w1 * a_hi + w2 * b_hi)
    ssq = jnp.sum(y_lo * y_lo, axis=-1, keepdims=True) + jnp.sum(y_hi * y_hi, axis=-1, keepdims=True)
    inv = lax.rsqrt(ssq / dm + EPS)
    o_ref[:, :dm // 2] = y_lo * inv * g_ref[:, :dm // 2]
    o_ref[:, dm // 2:] = y_hi * inv * g_ref[:, dm // 2:]


def _combine(x2, info, gain, ys, d1, d2, tm):
    m, dm = x2.shape
    return pl.pallas_call(
        _combine_kernel,
        grid_spec=pltpu.PrefetchScalarGridSpec(
            num_scalar_prefetch=2,
            grid=(m // tm,),
            in_specs=[
                pl.BlockSpec((tm, dm), lambda i, *_: (i, 0)),
                pl.BlockSpec((tm, LANES), lambda i, *_: (i, 0)),
                pl.BlockSpec((1, dm), lambda i, *_: (0, 0)),
                pl.BlockSpec(memory_space=pl.ANY),
            ],
            out_specs=pl.BlockSpec((tm, dm), lambda i, *_: (i, 0)),
            scratch_shapes=[pltpu.VMEM((2, tm, dm // 2), jnp.uint32), pltpu.SemaphoreType.DMA(())],
        ),
        out_shape=jax.ShapeDtypeStruct((m, dm), F32),
        compiler_params=_params(("arbitrary",)),
        name="moe_combine",
    )(d1, d2, x2, info, gain.reshape(1, dm), ys)


def _moe_and_final_norm(x2, ffn_gain, w_group, b_group, w_router, b_router, w_gate, w_up, w_down, final_gain, tm):
    m, _ = x2.shape
    xn, info, infot, cnt = _router(x2, ffn_gain, w_group, b_group, w_router, b_router, tm)
    counts = cnt[0, :N_EXPERTS].astype(jnp.int32)
    padded = (counts + MOE_ROWS - 1) // MOE_ROWS * MOE_ROWS
    pend = jnp.cumsum(padded)
    pstart = pend - padded
    fields = infot.astype(jnp.int32)
    d1 = pstart[fields[0]] + fields[4]
    d2 = pstart[fields[1]] + fields[5]
    nblocks = 2 * m // MOE_ROWS + N_EXPERTS
    first_row = jnp.arange(nblocks, dtype=jnp.int32) * MOE_ROWS
    block_e = jnp.minimum(jnp.sum(pend[None, :] <= first_row[:, None], axis=1), N_EXPERTS - 1).astype(jnp.int32)
    nused = (pend[-1:] // MOE_ROWS).astype(jnp.int32)
    xs = _dispatch(xn, d1, d2, nblocks * MOE_ROWS, tm)
    ys = _experts(xs, block_e, nused, w_gate, w_up, w_down)
    return _combine(x2, info, final_gain, ys, d1, d2, tm)


def kernel(x, attn_norm, w_in, hg_lb_logits, hg_out_norm, cmp_pos_k, cmp_w1_k, cmp_w2_k, cmp_pos_v, cmp_w1_v,
           cmp_w2_v, nsa_out_norm, w_out, ffn_norm, moe_w_group, moe_b_group, moe_w_router, moe_b_router,
           moe_w_gate, moe_w_up, moe_w_down, final_norm):
    bsz, seq, dm = x.shape
    xt = x.reshape(bsz * seq, dm)
    w = w_in[0]
    p0, p1 = 4 * HG_QK, 4 * HG_QK + NSA_WIDTH + NSA_KV
    tn = 256
    assert p0 % tn == 0 and p1 % tn == 0
    w_all = jnp.pad(w.astype(BF16), ((0, 0), (0, (-w.shape[1]) % tn)))[None]
    rest = _normed_matmul(xt, attn_norm[0], w_all, 1024, tn, skip=(p0 // tn, (p1 - p0) // tn))
    prec = _normed_matmul(xt, attn_norm[0], jnp.stack(_split2(w[:, p0:p1])), 1024, tn)
    rest3 = rest.reshape(bsz, seq, -1)
    prec3 = prec.reshape(bsz, seq, -1)
    y_hg = _hgrn(rest3, hg_lb_logits, hg_out_norm[0], 256)
    y_nsa = _nsa(prec3, rest3, 4 * HG_QK, 4 * HG_QK + 5 * NSA_KV, (cmp_pos_k[0], cmp_w1_k[0], cmp_w2_k[0]),
                 (cmp_pos_v[0], cmp_w1_v[0], cmp_w2_v[0]))
    x2 = _out_proj(y_hg.reshape(bsz * seq, -1), y_nsa.reshape(bsz * seq, -1), nsa_out_norm[0], w_out[0], xt, 1024, 512)
    out = _moe_and_final_norm(x2, ffn_norm[0], moe_w_group[0], moe_b_group[0], moe_w_router[0], moe_b_router[0],
                              moe_w_gate[0], moe_w_up[0], moe_w_down[0], final_norm, 256)
    return out.reshape(bsz, seq, dm)
```

```python
import functools

import jax
import jax.numpy as jnp
import numpy as np
from jax import lax
from jax.experimental import pallas as pl
from jax.experimental.pallas import tpu as pltpu

F32 = jnp.float32
BF16 = jnp.bfloat16

EPS = 1e-6
ROPE_THETA = 10000.0
NEG = -1e30
BIG = 1e9
LOG2E = 1.4426950408889634

HG_HEADS = 8
HG_DIM = 128
HG_QK = HG_HEADS * HG_DIM
HG_CHUNK = 64
HG_SUB = 8

NSA_HEADS = 16
NSA_GROUPS = 4
NSA_REP = NSA_HEADS // NSA_GROUPS
NSA_DIM = 64
NSA_WIDTH = NSA_HEADS * NSA_DIM
NSA_KV = NSA_GROUPS * NSA_DIM
NSA_VPAD = 16
CMP_BLOCK = 32
CMP_STRIDE = 16
CMP_HIDDEN = 256
SLC_BLOCK = 64
SLC_TOPK = 16
SLC_LOCAL = 2
WIN = 512

MOE_GROUPS = 4
MOE_EPG = 8
N_EXPERTS = MOE_GROUPS * MOE_EPG
EXPERT_FF = 512

LANES = 128
VMEM_LIMIT = 56 * 1024 * 1024


def _params(semantics, **kw):
    return pltpu.CompilerParams(dimension_semantics=semantics, vmem_limit_bytes=VMEM_LIMIT, **kw)


def _split2(a):
    hi = a.astype(BF16)
    return hi, (a - hi.astype(F32)).astype(BF16)


def _split3(a):
    hi = a.astype(BF16)
    r = a - hi.astype(F32)
    mid = r.astype(BF16)
    return hi, mid, (r - mid.astype(F32)).astype(BF16)


def _dot(a, b):
    return jnp.dot(a, b, preferred_element_type=F32)


def _dot_nt(a, b):
    return lax.dot_general(a, b, (((1,), (1,)), ((), ())), preferred_element_type=F32)


def _dot3(a, b):
    a_hi, a_lo = _split2(a)
    b_hi, b_lo = _split2(b)
    return _dot(a_hi, b_hi) + (_dot(a_hi, b_lo) + _dot(a_lo, b_hi))


def _dot3_nt(a, b):
    a_hi, a_lo = _split2(a)
    b_hi, b_lo = _split2(b)
    return _dot_nt(a_hi, b_hi) + (_dot_nt(a_hi, b_lo) + _dot_nt(a_lo, b_hi))


def _rms(x, gain):
    return x * lax.rsqrt(jnp.mean(x * x, axis=-1, keepdims=True) + EPS) * gain


def _normed_matmul_kernel(x_ref, g_ref, w_ref, o_ref, h_ref):
    parts = w_ref.shape[0]

    @pl.when(pl.program_id(1) == 0)
    def _():
        y = _rms(x_ref[...], g_ref[...])
        hi = y.astype(BF16)
        h_ref[0] = hi
        if parts == 2:
            h_ref[1] = (y - hi.astype(F32)).astype(BF16)

    acc = _dot(h_ref[0], w_ref[0])
    if parts == 2:
        acc = acc + (_dot(h_ref[0], w_ref[1]) + _dot(h_ref[1], w_ref[0]))
    o_ref[...] = acc


def _column_blocks(w, tn):
    *lead, k, n = w.shape
    return jnp.moveaxis(w.reshape(*lead, k, n // tn, tn), -2, -3)


def _normed_matmul(x, gain, w_blocks, tm, skip=None):
    m, k = x.shape
    parts, nb, _, tn = w_blocks.shape
    first, count = skip if skip else (nb, 0)
    return pl.pallas_call(
        _normed_matmul_kernel,
        grid=(m // tm, nb - count),
        in_specs=[
            pl.BlockSpec((tm, k), lambda i, j: (i, 0)),
            pl.BlockSpec((1, k), lambda i, j: (0, 0)),
            pl.BlockSpec((parts, None, k, tn), lambda i, j: (0, jnp.where(j < first, j, j + count), 0, 0)),
        ],
        out_specs=pl.BlockSpec((tm, tn), lambda i, j: (i, j)),
        out_shape=jax.ShapeDtypeStruct((m, (nb - count) * tn), F32),
        scratch_shapes=[pltpu.VMEM((parts, tm, k), BF16)],
        compiler_params=_params(("parallel", "arbitrary")),
        name="normed_matmul",
    )(x, gain.reshape(1, k), w_blocks)


def _hgrn_consts():
    c, sub = HG_CHUNK, HG_SUB
    tri = np.tile(np.tril(np.ones((c, c), np.float32)), (1, 3))
    gsum = (np.arange(c * sub)[None, :] // sub == np.arange(c)[:, None]).astype(np.float32)
    return jnp.asarray(tri, BF16), jnp.asarray(gsum, BF16)


HG_PAR = 2


def _hgrn_kernel(q_ref, f_ref, i_ref, g_ref, lbl_ref, gain_ref, tri_ref, gsum_ref, o_ref, *scratch):
    c, sub, d = HG_CHUNK, HG_SUB, HG_DIM
    nsub = c // sub
    heads = range(HG_PAR)
    st_refs, p_refs, cl_refs, qs_refs, k_refs = (scratch[i * HG_PAR:(i + 1) * HG_PAR] for i in range(5))

    @pl.when(pl.program_id(2) == 0)
    def _():
        for h in heads:
            st_refs[h][...] = jnp.zeros_like(st_refs[h])

    l0 = lbl_ref[0:1, :]
    l1 = lbl_ref[1:2, :]
    lmax = jnp.maximum(l0, l1)
    e0 = jnp.exp(l0 - lmax)
    lb_all = e0 / (e0 + jnp.exp(l1 - lmax))
    srow = lax.broadcasted_iota(jnp.int32, (sub, d), 0)
    ones = jnp.ones((d, d), BF16)

    def rows_at(x, start):
        parts = ([jnp.zeros((start, d), F32)] if start else []) + [x]
        if start + x.shape[0] < c:
            parts.append(jnp.zeros((c - start - x.shape[0], d), F32))
        return jnp.concatenate(parts, axis=0)

    nchunks = q_ref.shape[1] // c
    cols = [slice(h * d, (h + 1) * d) for h in heads]

    def load(ci):
        rows = pl.ds(pl.multiple_of(ci * c, c), c)
        out = []
        for h in heads:
            lb = lb_all[:, cols[h]]
            f = lb + (1.0 - lb) * jax.nn.sigmoid(f_ref[0, rows, cols[h]])
            bcum = _dot(tri_ref[...], jnp.concatenate(_split3(jnp.log(f)), axis=0))
            out.append((q_ref[0, rows, cols[h]] * (d ** -0.5), 1.0 - f, i_ref[0, rows, cols[h]], bcum))
        return tuple(out)

    def chunk(ci, cur):
        nxt = load(jnp.minimum(ci + 1, nchunks - 1))
        rows = pl.ds(pl.multiple_of(ci * c, c), c)
        q, k, v, bcum = ([cur[h][i] for h in heads] for i in range(4))
        a_off, o_inter = [], []
        for h in heads:
            b = bcum[h]
            edge = [b[i * sub - 1:i * sub] for i in range(1, nsub + 1)]
            cl = b - jnp.concatenate([jnp.zeros((sub, d), F32)] + [jnp.broadcast_to(e, (sub, d)) for e in edge[:-1]],
                                     axis=0)
            cl_refs[h][...] = cl
            qs_refs[h][...] = q[h]
            k_refs[h][...] = k[h]
            qe = q[h] * jnp.exp(cl)
            qcat, kcat = [], []
            for i in range(1, nsub):
                qcat.append(rows_at(qe[i * sub:(i + 1) * sub], i * sub))
                kcat.append(rows_at(k[h][:i * sub] * jnp.exp(edge[i - 1] - b[:i * sub]), 0))
            a_off.append(_dot_nt(jnp.concatenate(qcat, axis=1).astype(BF16),
                                 jnp.concatenate(kcat, axis=1).astype(BF16)))
            st = st_refs[h][...]
            o_inter.append(_dot_nt((q[h] * jnp.exp(b)).astype(BF16), st.astype(BF16)))
            kd = k[h] * jnp.exp(edge[-1] - b)
            st_refs[h][...] = st * jnp.exp(edge[-1]) + _dot(v[h].T.astype(BF16), kd.astype(BF16))
        r2 = []
        for h in heads:
            for t in range(c):
                j0 = (t // sub) * sub
                dlt = cl_refs[h][t:t + 1, :] - cl_refs[h][j0:j0 + sub, :]
                e = jnp.where(srow + j0 <= t, jnp.exp(jnp.minimum(dlt, 0.0)), 0.0)
                p_refs[h][t * sub:(t + 1) * sub, :] = (qs_refs[h][t:t + 1, :] * k_refs[h][j0:j0 + sub, :] * e).astype(BF16)
            r2.append(_dot(p_refs[h][...], ones))
        o = []
        for h in heads:
            o.append(o_inter[h] + _dot(a_off[h].astype(BF16), v[h].astype(BF16)))
        for h in heads:
            x = r2[h].reshape(nsub, sub, sub, d) * v[h].reshape(nsub, 1, sub, d)
            o[h] = o[h] + _dot(gsum_ref[...], x.reshape(c * sub, d).astype(BF16))
        for h in heads:
            gate = jax.nn.silu(g_ref[0, rows, cols[h]])
            o_ref[0, rows, cols[h]] = (_rms(o[h], gain_ref[...]) * gate).astype(o_ref.dtype)
        return nxt

    lax.fori_loop(0, nchunks, chunk, load(0), unroll=2)


def _hgrn(proj3, lb_logits, out_gain, tseq):
    bsz, seq, _ = proj3.shape
    d, c, sub = HG_DIM, HG_CHUNK, HG_SUB
    wst, gsum = _hgrn_consts()
    groups = HG_HEADS // HG_PAR
    width = HG_PAR * d

    def col(off):
        return pl.BlockSpec((1, tseq, width), lambda b, h, t: (b, t, off * groups + h))

    per_head = [pltpu.VMEM((d, d), F32), pltpu.VMEM((c * sub, d), BF16), pltpu.VMEM((c, d), F32),
                pltpu.VMEM((c, d), F32), pltpu.VMEM((c, d), F32)]
    return pl.pallas_call(
        _hgrn_kernel,
        grid=(bsz, groups, seq // tseq),
        in_specs=[
            col(0), col(1), col(2), col(3),
            pl.BlockSpec((2, width), lambda b, h, t: (0, h)),
            pl.BlockSpec((1, d), lambda b, h, t: (0, 0)),
            pl.BlockSpec(wst.shape, lambda b, h, t: (0, 0)),
            pl.BlockSpec(gsum.shape, lambda b, h, t: (0, 0)),
        ],
        out_specs=pl.BlockSpec((1, tseq, width), lambda b, h, t: (b, t, h)),
        out_shape=jax.ShapeDtypeStruct((bsz, seq, HG_QK), BF16),
        scratch_shapes=[s for s in per_head for _ in range(HG_PAR)],
        compiler_params=_params(("parallel", "parallel", "arbitrary")),
        name="hgrn2",
    )(proj3, proj3, proj3, proj3, lb_logits, out_gain.reshape(1, d), wst, gsum)


def _rope(x, cs, sn):
    lane = lax.broadcasted_iota(jnp.int32, x.shape, 1)
    partner = jnp.where(lane % NSA_DIM < NSA_DIM // 2, pltpu.roll(x, LANES - NSA_DIM // 2, 1),
                        pltpu.roll(x, NSA_DIM // 2, 1))
    return x * cs + partner * sn


def _nsa_prep_kernel(q_ref, ksl_ref, vsl_ref, kwn_ref, vwn_ref, cs_ref, sn_ref,
                     qrot_ref, kslo_ref, vslo_ref, kwno_ref, vwno_ref):
    cs = cs_ref[...]
    sn = sn_ref[...]
    for c in range(NSA_WIDTH // LANES):
        cols = slice(c * LANES, (c + 1) * LANES)
        qrot_ref[0, :, cols] = (_rope(q_ref[0, :, cols], cs, sn) * (NSA_DIM ** -0.5 * LOG2E)).astype(BF16)
    tseq = q_ref.shape[1]
    lane = lax.broadcasted_iota(jnp.int32, (tseq, LANES), 1)
    block = (pl.program_id(1) * tseq + lax.broadcasted_iota(jnp.int32, (tseq, LANES), 0)) // SLC_BLOCK
    block_onehot = jnp.where(lane - NSA_DIM == block, 1.0, 0.0)
    ones_rows = jnp.where(lax.broadcasted_iota(jnp.int32, (NSA_VPAD, tseq), 0) == 0, 1.0, 0.0).astype(BF16)
    for c in range(NSA_KV // LANES):
        cols = slice(c * LANES, (c + 1) * LANES)
        ks = _rope(ksl_ref[0, :, cols], cs, sn)
        kw = _rope(kwn_ref[0, :, cols], cs, sn).astype(BF16)
        vs = vsl_ref[0, :, cols].T.astype(BF16)
        vw = vwn_ref[0, :, cols].T.astype(BF16)
        for half in range(LANES // NSA_DIM):
            g = c * (LANES // NSA_DIM) + half
            hs = slice(half * NSA_DIM, (half + 1) * NSA_DIM)
            ks_g = ks if half == 0 else pltpu.roll(ks, NSA_DIM, 1)
            kslo_ref[0, g] = jnp.where(lane < NSA_DIM, ks_g, block_onehot).astype(BF16)
            kwno_ref[0, g] = kw[:, hs]
            vslo_ref[0, g, :NSA_DIM] = vs[hs, :]
            vslo_ref[0, g, NSA_DIM:] = ones_rows
            vwno_ref[0, g, :NSA_DIM] = vw[hs, :]
            vwno_ref[0, g, NSA_DIM:] = ones_rows


def _nsa_prep(prec3, rest3, kv_off, tseq):
    bsz, seq, _ = prec3.shape
    half = NSA_DIM // 2
    inv = 1.0 / (ROPE_THETA ** (jnp.arange(0, NSA_DIM, 2, dtype=F32) / NSA_DIM))
    ang = jnp.arange(seq, dtype=F32)[:, None] * inv[None, :]
    cs = jnp.tile(jnp.cos(ang), (1, LANES // half))
    sn = jnp.tile(jnp.concatenate([-jnp.sin(ang), jnp.sin(ang)], axis=1), (1, LANES // NSA_DIM))
    kvb = kv_off // NSA_KV

    def kv_in(i):
        return pl.BlockSpec((1, tseq, NSA_KV), lambda b, t: (b, t, kvb + i))

    assert NSA_DIM + seq // SLC_BLOCK <= LANES

    def k_out(width):
        return (pl.BlockSpec((1, NSA_GROUPS, tseq, width), lambda b, t: (b, 0, t, 0)),
                jax.ShapeDtypeStruct((bsz, NSA_GROUPS, seq, width), BF16))

    (ksl_out, ksl_shape), (kwn_out, kwn_shape) = k_out(LANES), k_out(NSA_DIM)
    v_out = pl.BlockSpec((1, NSA_GROUPS, NSA_DIM + NSA_VPAD, tseq), lambda b, t: (b, 0, 0, t))
    v_shape = jax.ShapeDtypeStruct((bsz, NSA_GROUPS, NSA_DIM + NSA_VPAD, seq), BF16)
    tab = pl.BlockSpec((tseq, LANES), lambda b, t: (t, 0))
    return pl.pallas_call(
        _nsa_prep_kernel,
        grid=(bsz, seq // tseq),
        in_specs=[pl.BlockSpec((1, tseq, NSA_WIDTH), lambda b, t: (b, t, 0)), kv_in(0), kv_in(1), kv_in(2), kv_in(3),
                  tab, tab],
        out_specs=[pl.BlockSpec((1, tseq, NSA_WIDTH), lambda b, t: (b, t, 0)), ksl_out, v_out, kwn_out, v_out],
        out_shape=[jax.ShapeDtypeStruct((bsz, seq, NSA_WIDTH), BF16), ksl_shape, v_shape, kwn_shape, v_shape],
        compiler_params=_params(("parallel", "parallel")),
        name="nsa_prep",
    )(prec3, rest3, rest3, rest3, rest3, cs, sn)


def _compress_kernel(u_ref, pos_ref, w1_ref, w2_ref, o_ref, *, precise):
    mm = _dot3 if precise else (lambda a, b: _dot(a.astype(BF16), b.astype(BF16)))
    u = u_ref[0]
    nu = u.shape[0]
    ya = mm(u + pos_ref[0:1, :], w1_ref[0])
    yb = mm(u + pos_ref[1:2, :], w1_ref[1])
    hid = ya + pltpu.roll(yb, nu - 1, 0)
    o_ref[0] = mm(jax.nn.gelu(hid), w2_ref[...])


def _compress(kv, pos, w1, w2, precise):
    bsz, seq, _ = kv.shape
    nu = seq // CMP_STRIDE
    width = CMP_STRIDE * NSA_DIM
    u = kv.reshape(bsz, nu, CMP_STRIDE, NSA_GROUPS, NSA_DIM).transpose(0, 3, 1, 2, 4).reshape(bsz * NSA_GROUPS, nu, width)
    return pl.pallas_call(
        functools.partial(_compress_kernel, precise=precise),
        grid=(bsz * NSA_GROUPS,),
        in_specs=[
            pl.BlockSpec((1, nu, width), lambda i: (i, 0, 0)),
            pl.BlockSpec((2, width), lambda i: (0, 0)),
            pl.BlockSpec((2, width, CMP_HIDDEN), lambda i: (0, 0, 0)),
            pl.BlockSpec((CMP_HIDDEN, NSA_DIM), lambda i: (0, 0)),
        ],
        out_specs=pl.BlockSpec((1, nu, NSA_DIM), lambda i: (i, 0, 0)),
        out_shape=jax.ShapeDtypeStruct((bsz * NSA_GROUPS, nu, NSA_DIM), F32),
        compiler_params=_params(("parallel",)),
        name="nsa_compress",
    )(u, pos.reshape(2, width), w1.reshape(2, width, CMP_HIDDEN), w2)


def _nsa_attn_kernel(qraw_ref, qrot_ref, kc_ref, vct_ref, ksl_ref, vslt_ref, kwn_ref, vwnt_ref, gate_ref, aggt_ref,
                     o_ref, sa_ref, sb_ref, *win_refs, topk, tk):
    tq = qraw_ref.shape[1]
    nu = kc_ref.shape[2]
    ns = aggt_ref.shape[0]
    rep, dk = NSA_REP, NSA_DIM
    qs = pl.program_id(2) * tq
    tpos = qs + lax.broadcasted_iota(jnp.int32, (1, tq), 1)

    qrt = (qraw_ref[0] * dk ** -0.5).T
    kc_hi, kc_lo = _split2(kc_ref[0, 0])
    vct = vct_ref[0, 0].astype(BF16)
    crow = lax.broadcasted_iota(jnp.int32, (nu, tq), 0)
    m_c = (crow * CMP_STRIDE + CMP_BLOCK - 1 <= tpos) & (crow < nu - 1)
    q_hi, q_lo = _split2(jnp.concatenate([qrt[r * dk:(r + 1) * dk] for r in range(rep)], axis=1))
    s_all = _dot(jnp.concatenate([kc_hi, kc_hi, kc_lo], axis=1),
                 jnp.concatenate([q_hi, q_lo, q_hi], axis=0))

    qt = qrot_ref[0].astype(F32).T.astype(BF16)
    qt_all = jnp.concatenate([qt[r * dk:(r + 1) * dk] for r in range(rep)], axis=1)
    hi = (qs + tq) // tk

    def key_tile(ktc):
        return pl.ds(pl.multiple_of(ktc * tk, tk), tk)

    def scores(k_ref, ktc, dst_ref):
        dst_ref[...] = _dot(k_ref[0, 0, key_tile(ktc), :], qt_all)

    win_tiles = [hi - len(win_refs) + j for j in range(len(win_refs))]
    for kt, dst_ref in zip(win_tiles, win_refs):
        scores(kwn_ref, jnp.maximum(kt, 0), dst_ref)
    psum = jnp.zeros((nu, tq), F32)
    p_all = []
    for r in range(rep):
        s = jnp.where(m_c, s_all[:, r * tq:(r + 1) * tq], NEG)
        e = jnp.exp(s - jnp.max(s, axis=0, keepdims=True))
        p = jnp.where(m_c, e * (1.0 / jnp.sum(e, axis=0, keepdims=True)), 0.0)
        psum = psum + p
        p_all.append(p.astype(BF16))
    o_c_all = _dot(vct, jnp.concatenate(p_all, axis=1))
    o_c = [o_c_all[:, r * tq:(r + 1) * tq] for r in range(rep)]

    p_hi, p_lo = _split2(psum)
    imp = _dot(aggt_ref[...], p_hi) + _dot(aggt_ref[...], p_lo)
    jrow = lax.broadcasted_iota(jnp.int32, (ns, tq), 0)
    dj = jnp.right_shift(tpos, SLC_BLOCK.bit_length() - 1) - jrow
    forced = (jrow == 0) | ((dj >= 0) & (dj < SLC_LOCAL))
    imp = jnp.where(forced, BIG, jnp.where(jrow * SLC_BLOCK <= tpos, imp, -BIG))
    sub8 = lax.broadcasted_iota(jnp.int32, (8, tq), 0)
    chunks = [imp[c * 8:(c + 1) * 8] for c in range(ns // 8)]
    ranks = [jnp.zeros((8, tq), F32) for _ in range(ns // 8)]
    for jp in range(ns):
        row = chunks[jp // 8][jp % 8:jp % 8 + 1]
        for c in range(ns // 8):
            if c < jp // 8:
                ahead = jnp.where(row > chunks[c], 1.0, 0.0)
            elif c > jp // 8:
                ahead = jnp.where(row >= chunks[c], 1.0, 0.0)
            else:
                tie = jnp.where(sub8 > jp % 8, 1.0, 0.0)
                ahead = jnp.where(row > chunks[c], 1.0, jnp.where(row == chunks[c], tie, 0.0))
            ranks[c] = ranks[c] + ahead
    selt = [jnp.where(ranks[c] < topk, 0.0, NEG) for c in range(ns // 8)]

    def consume(vt_ref, ktc, src_ref, carry, mask=None):
        vt = vt_ref[0, 0, :, key_tile(ktc)]
        out = []
        for r in range(rep):
            m_old, acc = carry[r]
            s = src_ref[:, r * tq:(r + 1) * tq]
            if mask is not None:
                s = jnp.where(mask, s, NEG)
            m_new = jnp.maximum(m_old, jnp.max(s, axis=0, keepdims=True))
            alpha = jnp.exp2(m_old - m_new)
            p = jnp.exp2(s - m_new).astype(BF16)
            out.append((m_new, acc * alpha + _dot(vt, p)))
        return tuple(out)

    def normalised(carry):
        return [acc[:dk] * (1.0 / acc[dk:dk + 1]) for _, acc in carry]

    init = tuple((jnp.full((1, tq), NEG, F32), jnp.zeros((dk + NSA_VPAD, tq), F32)) for _ in range(rep))
    krow = lax.broadcasted_iota(jnp.int32, (tk, tq), 0)
    unseen = 1 << 30

    carry = init
    for kt, src_ref in zip(win_tiles, win_refs):
        ktc = jnp.maximum(kt, 0)
        dlt = tpos - (jnp.where(kt >= 0, ktc * tk, unseen) + krow)
        carry = consume(vwnt_ref, ktc, src_ref, carry, mask=pltpu.bitcast(dlt, jnp.uint32) < jnp.uint32(WIN))
    o_w = normalised(carry)

    selb = jnp.concatenate([jnp.concatenate(selt, axis=0)] * rep, axis=1).astype(BF16)
    pad = jnp.zeros((LANES - dk - ns, rep * tq), BF16)
    qt_sel = jnp.concatenate([qt_all, selb] + ([pad] if LANES > dk + ns else []), axis=0)

    def sel_scores(ktc, dst_ref):
        dst_ref[...] = _dot(ksl_ref[0, 0, key_tile(ktc), :], qt_sel)

    def pair(i, carry):
        kt = 2 * i
        sel_scores(kt + 1, sb_ref)
        carry = consume(vslt_ref, kt, sa_ref, carry)
        sel_scores(kt + 2, sa_ref)
        return consume(vslt_ref, kt + 1, sb_ref, carry)

    past = hi - 1
    sel_scores(0, sa_ref)
    carry = lax.fori_loop(0, past // 2, pair, init)
    carry = lax.cond(past % 2 == 1, lambda c: consume(vslt_ref, past - 1, sa_ref, c), lambda c: c, carry)
    sel_scores(past, sb_ref)
    carry = consume(vslt_ref, past, sb_ref, carry, mask=past * tk + krow <= tpos)
    o_s = normalised(carry)

    gate = jax.nn.sigmoid(gate_ref[0, 0])
    o_t = [gate[3 * r:3 * r + 1] * o_c[r] + gate[3 * r + 1:3 * r + 2] * o_s[r] + gate[3 * r + 2:3 * r + 3] * o_w[r]
           for r in range(rep)]
    o_ref[0] = jnp.concatenate(o_t, axis=0).T.astype(o_ref.dtype)


def _nsa_attn(prec3, qrot, kc, vct, ksl, vslt, kwn, vwnt, gates_t, tq, tk):
    bsz, seq, _ = qrot.shape
    nu = seq // CMP_STRIDE
    ns = seq // SLC_BLOCK
    ci = np.arange(nu)[None, :]
    sj = np.arange(ns)[:, None]
    overlap = (ci * CMP_STRIDE < (sj + 1) * SLC_BLOCK) & (ci * CMP_STRIDE + CMP_BLOCK > sj * SLC_BLOCK) & (ci < nu - 1)
    aggt = jnp.asarray(overlap, BF16)
    gw = NSA_REP * NSA_DIM
    assert tq % tk == 0 and seq % tq == 0
    win_tiles = -(-(WIN - 1) // tk) + tq // tk

    def q_spec():
        return pl.BlockSpec((1, tq, gw), lambda b, g, t: (b, t, g))

    def per_group(rows, cols):
        return pl.BlockSpec((1, 1, rows, cols), lambda b, g, t: (b, g, 0, 0))

    return pl.pallas_call(
        functools.partial(_nsa_attn_kernel, topk=min(SLC_TOPK, ns), tk=tk),
        grid=(bsz, NSA_GROUPS, seq // tq),
        in_specs=[q_spec(), q_spec(), per_group(nu, NSA_DIM), per_group(NSA_DIM, nu),
                  per_group(seq, LANES), per_group(NSA_DIM + NSA_VPAD, seq),
                  per_group(seq, NSA_DIM), per_group(NSA_DIM + NSA_VPAD, seq),
                  pl.BlockSpec((1, 1, 3 * NSA_REP, tq), lambda b, g, t: (b, g, 0, t)),
                  pl.BlockSpec((ns, nu), lambda b, g, t: (0, 0))],
        out_specs=q_spec(),
        out_shape=jax.ShapeDtypeStruct((bsz, seq, NSA_WIDTH), BF16),
        scratch_shapes=[pltpu.VMEM((tk, NSA_REP * tq), F32)] * (2 + win_tiles),
        compiler_params=_params(("parallel", "parallel", "arbitrary")),
        name="nsa_attention",
    )(prec3, qrot, kc, vct, ksl, vslt, kwn, vwnt, gates_t, aggt)


def _nsa(prec3, rest3, kv_off, gate_off, cmp_k, cmp_v):
    bsz, seq, _ = prec3.shape
    nu = seq // CMP_STRIDE
    qrot, ksl, vslt, kwn, vwnt = _nsa_prep(prec3, rest3, kv_off + NSA_KV, min(seq, 512))
    kc = _compress(prec3[:, :, NSA_WIDTH:NSA_WIDTH + NSA_KV], *cmp_k, precise=True)
    vc = _compress(rest3[:, :, kv_off:kv_off + NSA_KV], *cmp_v, precise=False)
    kc = kc.reshape(bsz, NSA_GROUPS, nu, NSA_DIM)
    vct = vc.reshape(bsz, NSA_GROUPS, nu, NSA_DIM).transpose(0, 1, 3, 2)
    gates_t = rest3[:, :, gate_off:gate_off + 3 * NSA_HEADS].reshape(bsz, seq, NSA_GROUPS, 3 * NSA_REP).transpose(0, 2, 3, 1)
    return _nsa_attn(prec3, qrot, kc, vct, ksl, vslt, kwn, vwnt, gates_t, 256, 256)


def _out_proj_kernel(yh_ref, yn_ref, g_ref, w_ref, x_ref, o_ref, y_ref):
    @pl.when(pl.program_id(1) == 0)
    def _():
        wh = yh_ref.shape[1]
        y_ref[:, :wh] = yh_ref[...]
        y_ref[:, wh:] = _rms(yn_ref[...].astype(F32), g_ref[...]).astype(BF16)

    o_ref[...] = x_ref[...] + _dot(y_ref[...], w_ref[...])


def _out_proj(y_hg, y_nsa, nsa_gain, w_out, x, tm, tn):
    m, dm = x.shape
    wh, wn = y_hg.shape[1], y_nsa.shape[1]
    return pl.pallas_call(
        _out_proj_kernel,
        grid=(m // tm, dm // tn),
        in_specs=[
            pl.BlockSpec((tm, wh), lambda i, j: (i, 0)),
            pl.BlockSpec((tm, wn), lambda i, j: (i, 0)),
            pl.BlockSpec((1, wn), lambda i, j: (0, 0)),
            pl.BlockSpec((None, wh + wn, tn), lambda i, j: (j, 0, 0)),
            pl.BlockSpec((tm, tn), lambda i, j: (i, j)),
        ],
        out_specs=pl.BlockSpec((tm, tn), lambda i, j: (i, j)),
        out_shape=jax.ShapeDtypeStruct((m, dm), F32),
        scratch_shapes=[pltpu.VMEM((tm, wh + wn), BF16)],
        compiler_params=_params(("parallel", "arbitrary")),
        name="out_proj",
    )(y_hg, y_nsa, nsa_gain.reshape(1, wn), _column_blocks(w_out.astype(BF16), tn), x)


MOE_ROWS = 256


def _pack_bf16_pairs(hi):
    n = hi.shape[1] // 2
    bits = pltpu.bitcast(hi.astype(F32), jnp.uint32)
    return jnp.right_shift(bits[:, :n], jnp.uint32(16)) | (bits[:, n:] & jnp.uint32(0xFFFF0000))


def _unpack_f32_pairs(words):
    lo = pltpu.bitcast(jnp.left_shift(words, jnp.uint32(16)), F32)
    hi = pltpu.bitcast(words & jnp.uint32(0xFFFF0000), F32)
    return lo, hi


def _unpack_bf16_pairs(words):
    lo, hi = _unpack_f32_pairs(words)
    return lo.astype(BF16), hi.astype(BF16)


def _router_kernel(x_ref, g_ref, w_ref, b_ref, tri_ref, xn_ref, info_ref, infot_ref, cnt_ref, carry_ref):
    @pl.when(pl.program_id(0) == 0)
    def _():
        carry_ref[...] = jnp.zeros_like(carry_ref)

    xn = _rms(x_ref[...], g_ref[...])
    hi, lo = _split2(xn)
    xn_ref[...] = _pack_bf16_pairs(hi)
    logits = _dot(hi, w_ref[0]) + (_dot(hi, w_ref[1]) + _dot(lo, w_ref[0])) + b_ref[...]
    lane = lax.broadcasted_iota(jnp.int32, logits.shape, 1).astype(F32)
    none = float(LANES)

    def first_max(mask):
        top = jnp.max(jnp.where(mask, logits, -jnp.inf), axis=-1, keepdims=True)
        return top, jnp.min(jnp.where(mask & (logits == top), lane, none), axis=-1, keepdims=True)

    is_g = lane < MOE_GROUPS
    gmax, gsel = first_max(is_g)
    gw = 1.0 / jnp.sum(jnp.where(is_g, jnp.exp(logits - gmax), 0.0), axis=-1, keepdims=True)
    lo_lane = MOE_GROUPS + gsel * MOE_EPG
    in_grp = (lane >= lo_lane) & (lane < lo_lane + MOE_EPG)
    v1, i1 = first_max(in_grp)
    v2, i2 = first_max(in_grp & (lane != i1))
    e = jnp.exp(v2 - v1)
    w1 = gw / (1.0 + e)
    w2 = gw * e / (1.0 + e)
    e1 = i1 - MOE_GROUPS
    e2 = i2 - MOE_GROUPS
    onehot = jnp.where((lane == e1) | (lane == e2), 1.0, 0.0)
    before = _dot(tri_ref[...], onehot.astype(BF16)) + carry_ref[...]
    r1 = jnp.sum(jnp.where(lane == e1, before, 0.0), axis=-1, keepdims=True)
    r2 = jnp.sum(jnp.where(lane == e2, before, 0.0), axis=-1, keepdims=True)
    carry_ref[...] = carry_ref[...] + jnp.sum(onehot, axis=0, keepdims=True)
    cnt_ref[...] = carry_ref[...]
    info = jnp.zeros_like(logits)
    for idx, val in enumerate((e1, e2, w1, w2, r1, r2)):
        info = jnp.where(lane == idx, val, info)
    info_ref[...] = info
    infot_ref[...] = info.T[:8]


def _router(x2, gain, w_group, b_group, w_router, b_router, tm):
    m, dm = x2.shape
    wcat = jnp.pad(jnp.concatenate([w_group, w_router], axis=1), ((0, 0), (0, LANES - MOE_GROUPS - N_EXPERTS)))
    bcat = jnp.pad(jnp.concatenate([b_group, b_router]), (0, LANES - MOE_GROUPS - N_EXPERTS)).reshape(1, LANES)
    tri = jnp.asarray(np.tril(np.ones((tm, tm), np.float32), -1), BF16)
    return pl.pallas_call(
        _router_kernel,
        grid=(m // tm,),
        in_specs=[
            pl.BlockSpec((tm, dm), lambda i: (i, 0)),
            pl.BlockSpec((1, dm), lambda i: (0, 0)),
            pl.BlockSpec((2, dm, LANES), lambda i: (0, 0, 0)),
            pl.BlockSpec((1, LANES), lambda i: (0, 0)),
            pl.BlockSpec((tm, tm), lambda i: (0, 0)),
        ],
        out_specs=[
            pl.BlockSpec((tm, dm // 2), lambda i: (i, 0)),
            pl.BlockSpec((tm, LANES), lambda i: (i, 0)),
            pl.BlockSpec((8, tm), lambda i: (0, i)),
            pl.BlockSpec((1, LANES), lambda i: (0, 0)),
        ],
        out_shape=[
            jax.ShapeDtypeStruct((m, dm // 2), jnp.uint32),
            jax.ShapeDtypeStruct((m, LANES), F32),
            jax.ShapeDtypeStruct((8, m), F32),
            jax.ShapeDtypeStruct((1, LANES), F32),
        ],
        scratch_shapes=[pltpu.VMEM((1, LANES), F32)],
        compiler_params=_params(("arbitrary",)),
        name="moe_router",
    )(x2, gain.reshape(1, dm), jnp.stack(_split2(wcat)), bcat, tri)


def _dispatch_kernel(d1_ref, d2_ref, xn_ref, xs_in_ref, xs_ref, sem):
    del xs_in_ref
    tm = xn_ref.shape[0]
    base = pl.program_id(0) * tm

    def row_copy(i, dest):
        return pltpu.make_async_copy(xn_ref.at[pl.ds(i, 1)], xs_ref.at[pl.ds(dest, 1)], sem)

    def issue(i, c):
        row_copy(i, d1_ref[base + i]).start()
        row_copy(i, d2_ref[base + i]).start()
        return c

    lax.fori_loop(0, tm, issue, 0, unroll=8)
    for _ in range(2):
        pltpu.make_async_copy(xn_ref, xs_ref.at[pl.ds(0, tm)], sem).wait()


def _dispatch(xn, d1, d2, rows, tm):
    m, words = xn.shape
    return pl.pallas_call(
        _dispatch_kernel,
        grid_spec=pltpu.PrefetchScalarGridSpec(
            num_scalar_prefetch=2,
            grid=(m // tm,),
            in_specs=[pl.BlockSpec((tm, words), lambda i, *_: (i, 0)), pl.BlockSpec(memory_space=pl.ANY)],
            out_specs=pl.BlockSpec(memory_space=pl.ANY),
            scratch_shapes=[pltpu.SemaphoreType.DMA(())],
        ),
        out_shape=jax.ShapeDtypeStruct((rows, words), jnp.uint32),
        input_output_aliases={3: 0},
        compiler_params=_params(("arbitrary",)),
        name="moe_dispatch",
    )(d1, d2, xn, jnp.zeros((rows, words), jnp.uint32))


def _expert_kernel(be_ref, nused_ref, x_ref, wg_ref, wu_ref, wd_ref, o_ref, wgb_ref, wub_ref, wdb_ref):
    i = pl.program_id(0)

    @pl.when((i == 0) | (be_ref[i] != be_ref[jnp.maximum(i - 1, 0)]))
    def _():
        wgb_ref[...] = wg_ref[0].astype(BF16)
        wub_ref[...] = wu_ref[0].astype(BF16)
        wdb_ref[...] = wd_ref[0].astype(BF16)

    @pl.when(i < nused_ref[0])
    def _():
        half = wgb_ref.shape[0] // 2
        x_lo, x_hi = _unpack_bf16_pairs(x_ref[...])

        def up(w_ref):
            return _dot(x_lo, w_ref[:half]) + _dot(x_hi, w_ref[half:])

        hid = (jax.nn.silu(up(wgb_ref)) * up(wub_ref)).astype(BF16)
        for g in range(o_ref.shape[1] // LANES):
            y = _dot(hid, wdb_ref[:, g * 2 * LANES:(g + 1) * 2 * LANES])
            o_ref[:, g * LANES:(g + 1) * LANES] = _pack_bf16_pairs(y.astype(BF16))

    @pl.when(i >= nused_ref[0])
    def _():
        o_ref[...] = jnp.zeros_like(o_ref)


def _experts(xs, block_e, nused, w_gate, w_up, w_down):
    rows, words = xs.shape
    _, dm, ff = w_gate.shape
    return pl.pallas_call(
        _expert_kernel,
        grid_spec=pltpu.PrefetchScalarGridSpec(
            num_scalar_prefetch=2,
            grid=(rows // MOE_ROWS,),
            in_specs=[
                pl.BlockSpec((MOE_ROWS, words), lambda i, be, nu: (i, 0)),
                pl.BlockSpec((1, dm, ff), lambda i, be, nu: (be[i], 0, 0)),
                pl.BlockSpec((1, dm, ff), lambda i, be, nu: (be[i], 0, 0)),
                pl.BlockSpec((1, ff, dm), lambda i, be, nu: (be[i], 0, 0)),
            ],
            out_specs=pl.BlockSpec((MOE_ROWS, words), lambda i, be, nu: (i, 0)),
            scratch_shapes=[pltpu.VMEM((dm, ff), BF16), pltpu.VMEM((dm, ff), BF16), pltpu.VMEM((ff, dm), BF16)],
        ),
        out_shape=jax.ShapeDtypeStruct((rows, words), jnp.uint32),
        compiler_params=_params(("arbitrary",)),
        name="moe_experts",
    )(block_e, nused, xs, w_gate, w_up, w_down)


def _combine_kernel(d1_ref, d2_ref, x_ref, info_ref, g_ref, ys_ref, o_ref, buf_ref, sem):
    tm, dm = x_ref.shape
    base = pl.program_id(0) * tm

    def row_copy(i, slot, src):
        return pltpu.make_async_copy(ys_ref.at[pl.ds(src, 1)], buf_ref.at[slot, pl.ds(i, 1)], sem)

    def issue(i, c):
        row_copy(i, 0, d1_ref[base + i]).start()
        row_copy(i, 1, d2_ref[base + i]).start()
        return c

    lax.fori_loop(0, tm, issue, 0, unroll=8)
    for slot in range(2):
        pltpu.make_async_copy(ys_ref.at[pl.ds(0, tm)], buf_ref.at[slot], sem).wait()
    info = info_ref[...]
    w1, w2 = info[:, 2:3], info[:, 3:4]
    ssq = jnp.zeros((tm, 1), F32)
    for g in range(dm // (2 * LANES)):
        a = _unpack_f32_pairs(buf_ref[0, :, g * LANES:(g + 1) * LANES])
        b = _unpack_f32_pairs(buf_ref[1, :, g * LANES:(g + 1) * LANES])
        for half in range(2):
            cols = slice((2 * g + half) * LANES, (2 * g + half + 1) * LANES)
            y = x_ref[:, cols] + (w1 * a[half] + w2 * b[half])
            ssq = ssq + jnp.sum(y * y, axis=-1, keepdims=True)
            o_ref[:, cols] = y
    o_ref[...] = o_ref[...] * lax.rsqrt(ssq / dm + EPS) * g_ref[...]


def _combine(x2, info, gain, ys, d1, d2, tm):
    m, dm = x2.shape
    return pl.pallas_call(
        _combine_kernel,
        grid_spec=pltpu.PrefetchScalarGridSpec(
            num_scalar_prefetch=2,
            grid=(m // tm,),
            in_specs=[
                pl.BlockSpec((tm, dm), lambda i, *_: (i, 0)),
                pl.BlockSpec((tm, LANES), lambda i, *_: (i, 0)),
                pl.BlockSpec((1, dm), lambda i, *_: (0, 0)),
                pl.BlockSpec(memory_space=pl.ANY),
            ],
            out_specs=pl.BlockSpec((tm, dm), lambda i, *_: (i, 0)),
            scratch_shapes=[pltpu.VMEM((2, tm, dm // 2), jnp.uint32), pltpu.SemaphoreType.DMA(())],
        ),
        out_shape=jax.ShapeDtypeStruct((m, dm), F32),
        compiler_params=_params(("arbitrary",)),
        name="moe_combine",
    )(d1, d2, x2, info, gain.reshape(1, dm), ys)


def _moe_and_final_norm(x2, ffn_gain, w_group, b_group, w_router, b_router, w_gate, w_up, w_down, final_gain, tm):
    m, _ = x2.shape
    xn, info, infot, cnt = _router(x2, ffn_gain, w_group, b_group, w_router, b_router, tm)
    counts = cnt[0, :N_EXPERTS].astype(jnp.int32)
    padded = (counts + MOE_ROWS - 1) // MOE_ROWS * MOE_ROWS
    pend = jnp.cumsum(padded)
    pstart = pend - padded
    fields = infot.astype(jnp.int32)
    d1 = pstart[fields[0]] + fields[4]
    d2 = pstart[fields[1]] + fields[5]
    nblocks = 2 * m // MOE_ROWS + N_EXPERTS
    first_row = jnp.arange(nblocks, dtype=jnp.int32) * MOE_ROWS
    block_e = jnp.minimum(jnp.sum(pend[None, :] <= first_row[:, None], axis=1), N_EXPERTS - 1).astype(jnp.int32)
    nused = (pend[-1:] // MOE_ROWS).astype(jnp.int32)
    xs = _dispatch(xn, d1, d2, nblocks * MOE_ROWS, tm)
    ys = _experts(xs, block_e, nused, w_gate, w_up, w_down)
    return _combine(x2, info, final_gain, ys, d1, d2, tm)


def kernel(x, attn_norm, w_in, hg_lb_logits, hg_out_norm, cmp_pos_k, cmp_w1_k, cmp_w2_k, cmp_pos_v, cmp_w1_v,
           cmp_w2_v, nsa_out_norm, w_out, ffn_norm, moe_w_group, moe_b_group, moe_w_router, moe_b_router,
           moe_w_gate, moe_w_up, moe_w_down, final_norm):
    bsz, seq, dm = x.shape
    xt = x.reshape(bsz * seq, dm)
    w = w_in[0]
    p0, p1 = 4 * HG_QK, 4 * HG_QK + NSA_WIDTH + NSA_KV
    tn = 256
    assert p0 % tn == 0 and p1 % tn == 0
    w_all = _column_blocks(jnp.pad(w.astype(BF16), ((0, 0), (0, (-w.shape[1]) % tn)))[None], tn)
    rest = _normed_matmul(xt, attn_norm[0], w_all, 1024, skip=(p0 // tn, (p1 - p0) // tn))
    prec = _normed_matmul(xt, attn_norm[0], _column_blocks(jnp.stack(_split2(w[:, p0:p1])), tn), 1024)
    rest3 = rest.reshape(bsz, seq, -1)
    prec3 = prec.reshape(bsz, seq, -1)
    y_hg = _hgrn(rest3, hg_lb_logits, hg_out_norm[0], 256)
    y_nsa = _nsa(prec3, rest3, 4 * HG_QK, 4 * HG_QK + 5 * NSA_KV, (cmp_pos_k[0], cmp_w1_k[0], cmp_w2_k[0]),
                 (cmp_pos_v[0], cmp_w1_v[0], cmp_w2_v[0]))
    x2 = _out_proj(y_hg.reshape(bsz * seq, -1), y_nsa.reshape(bsz * seq, -1), nsa_out_norm[0], w_out[0], xt, 1024, 512)
    out = _moe_and_final_norm(x2, ffn_norm[0], moe_w_group[0], moe_b_group[0], moe_w_router[0], moe_b_router[0],
                              moe_w_gate[0], moe_w_up[0], moe_w_down[0], final_norm, 256)
    return out.reshape(bsz, seq, dm)
```

```python
import functools

import jax
import jax.numpy as jnp
import numpy as np
from jax import lax
from jax.experimental import pallas as pl
from jax.experimental.pallas import tpu as pltpu

F32 = jnp.float32
BF16 = jnp.bfloat16

EPS = 1e-6
ROPE_THETA = 10000.0
NEG = -1e30
BIG = 1e9
LOG2E = 1.4426950408889634

HG_HEADS = 8
HG_DIM = 128
HG_QK = HG_HEADS * HG_DIM
HG_CHUNK = 64
HG_SUB = 8

NSA_HEADS = 16
NSA_GROUPS = 4
NSA_REP = NSA_HEADS // NSA_GROUPS
NSA_DIM = 64
NSA_WIDTH = NSA_HEADS * NSA_DIM
NSA_KV = NSA_GROUPS * NSA_DIM
NSA_VPAD = 16
CMP_BLOCK = 32
CMP_STRIDE = 16
CMP_HIDDEN = 256
SLC_BLOCK = 64
SLC_TOPK = 16
SLC_LOCAL = 2
WIN = 512

MOE_GROUPS = 4
MOE_EPG = 8
N_EXPERTS = MOE_GROUPS * MOE_EPG
EXPERT_FF = 512

LANES = 128
VMEM_LIMIT = 56 * 1024 * 1024


def _params(semantics, **kw):
    return pltpu.CompilerParams(dimension_semantics=semantics, vmem_limit_bytes=VMEM_LIMIT, **kw)


def _split2(a):
    hi = a.astype(BF16)
    return hi, (a - hi.astype(F32)).astype(BF16)


def _split3(a):
    hi = a.astype(BF16)
    r = a - hi.astype(F32)
    mid = r.astype(BF16)
    return hi, mid, (r - mid.astype(F32)).astype(BF16)


def _dot(a, b):
    return jnp.dot(a, b, preferred_element_type=F32)


def _dot_nt(a, b):
    return lax.dot_general(a, b, (((1,), (1,)), ((), ())), preferred_element_type=F32)


def _dot3(a, b):
    a_hi, a_lo = _split2(a)
    b_hi, b_lo = _split2(b)
    return _dot(a_hi, b_hi) + (_dot(a_hi, b_lo) + _dot(a_lo, b_hi))


def _dot3_nt(a, b):
    a_hi, a_lo = _split2(a)
    b_hi, b_lo = _split2(b)
    return _dot_nt(a_hi, b_hi) + (_dot_nt(a_hi, b_lo) + _dot_nt(a_lo, b_hi))


def _rms(x, gain):
    return x * lax.rsqrt(jnp.mean(x * x, axis=-1, keepdims=True) + EPS) * gain


def _normed_matmul_kernel(x_ref, g_ref, w_ref, o_ref, h_ref):
    parts = w_ref.shape[0]

    @pl.when(pl.program_id(1) == 0)
    def _():
        y = _rms(x_ref[...], g_ref[...])
        hi = y.astype(BF16)
        h_ref[0] = hi
        if parts == 2:
            h_ref[1] = (y - hi.astype(F32)).astype(BF16)

    acc = _dot(h_ref[0], w_ref[0])
    if parts == 2:
        acc = acc + (_dot(h_ref[0], w_ref[1]) + _dot(h_ref[1], w_ref[0]))
    o_ref[...] = acc


def _column_blocks(w, tn):
    *lead, k, n = w.shape
    return jnp.moveaxis(w.reshape(*lead, k, n // tn, tn), -2, -3)


def _normed_matmul(x, gain, w_blocks, tm, skip=None):
    m, k = x.shape
    parts, nb, _, tn = w_blocks.shape
    first, count = skip if skip else (nb, 0)
    return pl.pallas_call(
        _normed_matmul_kernel,
        grid=(m // tm, nb - count),
        in_specs=[
            pl.BlockSpec((tm, k), lambda i, j: (i, 0)),
            pl.BlockSpec((1, k), lambda i, j: (0, 0)),
            pl.BlockSpec((parts, None, k, tn), lambda i, j: (0, jnp.where(j < first, j, j + count), 0, 0)),
        ],
        out_specs=pl.BlockSpec((tm, tn), lambda i, j: (i, j)),
        out_shape=jax.ShapeDtypeStruct((m, (nb - count) * tn), F32),
        scratch_shapes=[pltpu.VMEM((parts, tm, k), BF16)],
        compiler_params=_params(("parallel", "arbitrary")),
        name="normed_matmul",
    )(x, gain.reshape(1, k), w_blocks)


def _hgrn_consts():
    c, sub = HG_CHUNK, HG_SUB
    tri = np.tile(np.tril(np.ones((c, c), np.float32)), (1, 3))
    gsum = (np.arange(c * sub)[None, :] // sub == np.arange(c)[:, None]).astype(np.float32)
    return jnp.asarray(tri, BF16), jnp.asarray(gsum, BF16)


HG_PAR = 2


def _hgrn_kernel(q_ref, f_ref, i_ref, g_ref, lbl_ref, gain_ref, tri_ref, gsum_ref, o_ref, *scratch):
    c, sub, d = HG_CHUNK, HG_SUB, HG_DIM
    nsub = c // sub
    heads = range(HG_PAR)
    st_refs, p_refs, cl_refs, qs_refs, k_refs = (scratch[i * HG_PAR:(i + 1) * HG_PAR] for i in range(5))

    @pl.when(pl.program_id(2) == 0)
    def _():
        for h in heads:
            st_refs[h][...] = jnp.zeros_like(st_refs[h])

    l0 = lbl_ref[0:1, :]
    l1 = lbl_ref[1:2, :]
    lmax = jnp.maximum(l0, l1)
    e0 = jnp.exp(l0 - lmax)
    lb_all = e0 / (e0 + jnp.exp(l1 - lmax))
    srow = lax.broadcasted_iota(jnp.int32, (sub, d), 0)
    ones = jnp.ones((d, d), BF16)

    def rows_at(x, start):
        parts = ([jnp.zeros((start, d), F32)] if start else []) + [x]
        if start + x.shape[0] < c:
            parts.append(jnp.zeros((c - start - x.shape[0], d), F32))
        return jnp.concatenate(parts, axis=0)

    nchunks = q_ref.shape[1] // c
    cols = [slice(h * d, (h + 1) * d) for h in heads]

    def load(ci):
        rows = pl.ds(pl.multiple_of(ci * c, c), c)
        out = []
        for h in heads:
            lb = lb_all[:, cols[h]]
            f = lb + (1.0 - lb) * jax.nn.sigmoid(f_ref[0, rows, cols[h]])
            bcum = _dot(tri_ref[...], jnp.concatenate(_split3(jnp.log(f)), axis=0))
            out.append((q_ref[0, rows, cols[h]] * (d ** -0.5), 1.0 - f, i_ref[0, rows, cols[h]], bcum))
        return tuple(out)

    def chunk(ci, cur):
        nxt = load(jnp.minimum(ci + 1, nchunks - 1))
        rows = pl.ds(pl.multiple_of(ci * c, c), c)
        q, k, v, bcum = ([cur[h][i] for h in heads] for i in range(4))
        a_off, o_inter = [], []
        for h in heads:
            b = bcum[h]
            edge = [b[i * sub - 1:i * sub] for i in range(1, nsub + 1)]
            cl = b - jnp.concatenate([jnp.zeros((sub, d), F32)] + [jnp.broadcast_to(e, (sub, d)) for e in edge[:-1]],
                                     axis=0)
            cl_refs[h][...] = cl
            qs_refs[h][...] = q[h]
            k_refs[h][...] = k[h]
            qe = q[h] * jnp.exp(cl)
            qcat, kcat = [], []
            for i in range(1, nsub):
                qcat.append(rows_at(qe[i * sub:(i + 1) * sub], i * sub))
                kcat.append(rows_at(k[h][:i * sub] * jnp.exp(edge[i - 1] - b[:i * sub]), 0))
            a_off.append(_dot_nt(jnp.concatenate(qcat, axis=1).astype(BF16),
                                 jnp.concatenate(kcat, axis=1).astype(BF16)))
            st = st_refs[h][...]
            o_inter.append(_dot_nt((q[h] * jnp.exp(b)).astype(BF16), st.astype(BF16)))
            kd = k[h] * jnp.exp(edge[-1] - b)
            st_refs[h][...] = st * jnp.exp(edge[-1]) + _dot(v[h].T.astype(BF16), kd.astype(BF16))
        r2 = []
        for h in heads:
            for t in range(c):
                j0 = (t // sub) * sub
                dlt = cl_refs[h][t:t + 1, :] - cl_refs[h][j0:j0 + sub, :]
                e = jnp.where(srow + j0 <= t, jnp.exp(jnp.minimum(dlt, 0.0)), 0.0)
                p_refs[h][t * sub:(t + 1) * sub, :] = (qs_refs[h][t:t + 1, :] * k_refs[h][j0:j0 + sub, :] * e).astype(BF16)
            r2.append(_dot(p_refs[h][...], ones))
        o = []
        for h in heads:
            o.append(o_inter[h] + _dot(a_off[h].astype(BF16), v[h].astype(BF16)))
        for h in heads:
            x = r2[h].reshape(nsub, sub, sub, d) * v[h].reshape(nsub, 1, sub, d)
            o[h] = o[h] + _dot(gsum_ref[...], x.reshape(c * sub, d).astype(BF16))
        for h in heads:
            gate = jax.nn.silu(g_ref[0, rows, cols[h]])
            o_ref[0, rows, cols[h]] = (_rms(o[h], gain_ref[...]) * gate).astype(o_ref.dtype)
        return nxt

    lax.fori_loop(0, nchunks, chunk, load(0), unroll=2)


def _hgrn(proj3, lb_logits, out_gain, tseq):
    bsz, seq, _ = proj3.shape
    d, c, sub = HG_DIM, HG_CHUNK, HG_SUB
    wst, gsum = _hgrn_consts()
    groups = HG_HEADS // HG_PAR
    width = HG_PAR * d

    def col(off):
        return pl.BlockSpec((1, tseq, width), lambda b, h, t: (b, t, off * groups + h))

    per_head = [pltpu.VMEM((d, d), F32), pltpu.VMEM((c * sub, d), BF16), pltpu.VMEM((c, d), F32),
                pltpu.VMEM((c, d), F32), pltpu.VMEM((c, d), F32)]
    return pl.pallas_call(
        _hgrn_kernel,
        grid=(bsz, groups, seq // tseq),
        in_specs=[
            col(0), col(1), col(2), col(3),
            pl.BlockSpec((2, width), lambda b, h, t: (0, h)),
            pl.BlockSpec((1, d), lambda b, h, t: (0, 0)),
            pl.BlockSpec(wst.shape, lambda b, h, t: (0, 0)),
            pl.BlockSpec(gsum.shape, lambda b, h, t: (0, 0)),
        ],
        out_specs=pl.BlockSpec((1, tseq, width), lambda b, h, t: (b, t, h)),
        out_shape=jax.ShapeDtypeStruct((bsz, seq, HG_QK), BF16),
        scratch_shapes=[s for s in per_head for _ in range(HG_PAR)],
        compiler_params=_params(("parallel", "parallel", "arbitrary")),
        name="hgrn2",
    )(proj3, proj3, proj3, proj3, lb_logits, out_gain.reshape(1, d), wst, gsum)


def _rope(x, cs, sn):
    lane = lax.broadcasted_iota(jnp.int32, x.shape, 1)
    partner = jnp.where(lane % NSA_DIM < NSA_DIM // 2, pltpu.roll(x, LANES - NSA_DIM // 2, 1),
                        pltpu.roll(x, NSA_DIM // 2, 1))
    return x * cs + partner * sn


def _nsa_prep_kernel(q_ref, ksl_ref, vsl_ref, kwn_ref, vwn_ref, cs_ref, sn_ref,
                     qrot_ref, kslo_ref, vslo_ref, kwno_ref, vwno_ref):
    cs = cs_ref[...]
    sn = sn_ref[...]
    for c in range(NSA_WIDTH // LANES):
        cols = slice(c * LANES, (c + 1) * LANES)
        qrot_ref[0, :, cols] = (_rope(q_ref[0, :, cols], cs, sn) * (NSA_DIM ** -0.5 * LOG2E)).astype(BF16)
    tseq = q_ref.shape[1]
    lane = lax.broadcasted_iota(jnp.int32, (tseq, LANES), 1)
    block = (pl.program_id(1) * tseq + lax.broadcasted_iota(jnp.int32, (tseq, LANES), 0)) // SLC_BLOCK
    block_onehot = jnp.where(lane - NSA_DIM == block, 1.0, 0.0)
    ones_rows = jnp.where(lax.broadcasted_iota(jnp.int32, (NSA_VPAD, tseq), 0) == 0, 1.0, 0.0).astype(BF16)
    for c in range(NSA_KV // LANES):
        cols = slice(c * LANES, (c + 1) * LANES)
        ks = _rope(ksl_ref[0, :, cols], cs, sn)
        kw = _rope(kwn_ref[0, :, cols], cs, sn).astype(BF16)
        vs = vsl_ref[0, :, cols].T.astype(BF16)
        vw = vwn_ref[0, :, cols].T.astype(BF16)
        for half in range(LANES // NSA_DIM):
            g = c * (LANES // NSA_DIM) + half
            hs = slice(half * NSA_DIM, (half + 1) * NSA_DIM)
            ks_g = ks if half == 0 else pltpu.roll(ks, NSA_DIM, 1)
            kslo_ref[0, g] = jnp.where(lane < NSA_DIM, ks_g, block_onehot).astype(BF16)
            kwno_ref[0, g] = kw[:, hs]
            vslo_ref[0, g, :NSA_DIM] = vs[hs, :]
            vslo_ref[0, g, NSA_DIM:] = ones_rows
            vwno_ref[0, g, :NSA_DIM] = vw[hs, :]
            vwno_ref[0, g, NSA_DIM:] = ones_rows


def _nsa_prep(prec3, rest3, kv_off, tseq):
    bsz, seq, _ = prec3.shape
    half = NSA_DIM // 2
    inv = 1.0 / (ROPE_THETA ** (jnp.arange(0, NSA_DIM, 2, dtype=F32) / NSA_DIM))
    ang = jnp.arange(seq, dtype=F32)[:, None] * inv[None, :]
    cs = jnp.tile(jnp.cos(ang), (1, LANES // half))
    sn = jnp.tile(jnp.concatenate([-jnp.sin(ang), jnp.sin(ang)], axis=1), (1, LANES // NSA_DIM))
    kvb = kv_off // NSA_KV

    def kv_in(i):
        return pl.BlockSpec((1, tseq, NSA_KV), lambda b, t: (b, t, kvb + i))

    assert NSA_DIM + seq // SLC_BLOCK <= LANES

    def k_out(width):
        return (pl.BlockSpec((1, NSA_GROUPS, tseq, width), lambda b, t: (b, 0, t, 0)),
                jax.ShapeDtypeStruct((bsz, NSA_GROUPS, seq, width), BF16))

    (ksl_out, ksl_shape), (kwn_out, kwn_shape) = k_out(LANES), k_out(NSA_DIM)
    v_out = pl.BlockSpec((1, NSA_GROUPS, NSA_DIM + NSA_VPAD, tseq), lambda b, t: (b, 0, 0, t))
    v_shape = jax.ShapeDtypeStruct((bsz, NSA_GROUPS, NSA_DIM + NSA_VPAD, seq), BF16)
    tab = pl.BlockSpec((tseq, LANES), lambda b, t: (t, 0))
    return pl.pallas_call(
        _nsa_prep_kernel,
        grid=(bsz, seq // tseq),
        in_specs=[pl.BlockSpec((1, tseq, NSA_WIDTH), lambda b, t: (b, t, 0)), kv_in(0), kv_in(1), kv_in(2), kv_in(3),
                  tab, tab],
        out_specs=[pl.BlockSpec((1, tseq, NSA_WIDTH), lambda b, t: (b, t, 0)), ksl_out, v_out, kwn_out, v_out],
        out_shape=[jax.ShapeDtypeStruct((bsz, seq, NSA_WIDTH), BF16), ksl_shape, v_shape, kwn_shape, v_shape],
        compiler_params=_params(("parallel", "parallel")),
        name="nsa_prep",
    )(prec3, rest3, rest3, rest3, rest3, cs, sn)


def _compress_kernel(u_ref, pos_ref, w1_ref, w2_ref, o_ref, *, precise):
    mm = _dot3 if precise else (lambda a, b: _dot(a.astype(BF16), b.astype(BF16)))
    u = u_ref[0]
    nu = u.shape[0]
    ya = mm(u + pos_ref[0:1, :], w1_ref[0])
    yb = mm(u + pos_ref[1:2, :], w1_ref[1])
    hid = ya + pltpu.roll(yb, nu - 1, 0)
    o_ref[0] = mm(jax.nn.gelu(hid), w2_ref[...])


def _compress(kv, pos, w1, w2, precise):
    bsz, seq, _ = kv.shape
    nu = seq // CMP_STRIDE
    width = CMP_STRIDE * NSA_DIM
    u = kv.reshape(bsz, nu, CMP_STRIDE, NSA_GROUPS, NSA_DIM).transpose(0, 3, 1, 2, 4).reshape(bsz * NSA_GROUPS, nu, width)
    return pl.pallas_call(
        functools.partial(_compress_kernel, precise=precise),
        grid=(bsz * NSA_GROUPS,),
        in_specs=[
            pl.BlockSpec((1, nu, width), lambda i: (i, 0, 0)),
            pl.BlockSpec((2, width), lambda i: (0, 0)),
            pl.BlockSpec((2, width, CMP_HIDDEN), lambda i: (0, 0, 0)),
            pl.BlockSpec((CMP_HIDDEN, NSA_DIM), lambda i: (0, 0)),
        ],
        out_specs=pl.BlockSpec((1, nu, NSA_DIM), lambda i: (i, 0, 0)),
        out_shape=jax.ShapeDtypeStruct((bsz * NSA_GROUPS, nu, NSA_DIM), F32),
        compiler_params=_params(("parallel",)),
        name="nsa_compress",
    )(u, pos.reshape(2, width), w1.reshape(2, width, CMP_HIDDEN), w2)


def _nsa_attn_kernel(qraw_ref, qrot_ref, kc_ref, vct_ref, ksl_ref, vslt_ref, kwn_ref, vwnt_ref, gate_ref, aggt_ref,
                     o_ref, sa_ref, sb_ref, *win_refs, topk, tk):
    tq = qraw_ref.shape[1]
    nu = kc_ref.shape[2]
    ns = aggt_ref.shape[0]
    rep, dk = NSA_REP, NSA_DIM
    qs = pl.program_id(2) * tq
    tpos = qs + lax.broadcasted_iota(jnp.int32, (1, tq), 1)

    qrt = (qraw_ref[0] * dk ** -0.5).T
    kc_hi, kc_lo = _split2(kc_ref[0, 0])
    vct = vct_ref[0, 0].astype(BF16)
    crow = lax.broadcasted_iota(jnp.int32, (nu, tq), 0)
    m_c = (crow * CMP_STRIDE + CMP_BLOCK - 1 <= tpos) & (crow < nu - 1)
    q_hi, q_lo = _split2(jnp.concatenate([qrt[r * dk:(r + 1) * dk] for r in range(rep)], axis=1))
    s_all = _dot(jnp.concatenate([kc_hi, kc_hi, kc_lo], axis=1),
                 jnp.concatenate([q_hi, q_lo, q_hi], axis=0))

    qt = qrot_ref[0].astype(F32).T.astype(BF16)
    qt_all = jnp.concatenate([qt[r * dk:(r + 1) * dk] for r in range(rep)], axis=1)
    hi = (qs + tq) // tk

    def key_tile(ktc):
        return pl.ds(pl.multiple_of(ktc * tk, tk), tk)

    def scores(k_ref, ktc, dst_ref):
        dst_ref[...] = _dot(k_ref[0, 0, key_tile(ktc), :], qt_all)

    win_tiles = [hi - len(win_refs) + j for j in range(len(win_refs))]
    for kt, dst_ref in zip(win_tiles, win_refs):
        scores(kwn_ref, jnp.maximum(kt, 0), dst_ref)
    psum = jnp.zeros((nu, tq), F32)
    p_all = []
    for r in range(rep):
        s = jnp.where(m_c, s_all[:, r * tq:(r + 1) * tq], NEG)
        e = jnp.exp(s - jnp.max(s, axis=0, keepdims=True))
        p = jnp.where(m_c, e * (1.0 / jnp.sum(e, axis=0, keepdims=True)), 0.0)
        psum = psum + p
        p_all.append(p.astype(BF16))
    o_c_all = _dot(vct, jnp.concatenate(p_all, axis=1))
    o_c = [o_c_all[:, r * tq:(r + 1) * tq] for r in range(rep)]

    p_hi, p_lo = _split2(psum)
    imp = _dot(aggt_ref[...], p_hi) + _dot(aggt_ref[...], p_lo)
    jrow = lax.broadcasted_iota(jnp.int32, (ns, tq), 0)
    dj = jnp.right_shift(tpos, SLC_BLOCK.bit_length() - 1) - jrow
    forced = (jrow == 0) | ((dj >= 0) & (dj < SLC_LOCAL))
    imp = jnp.where(forced, BIG, jnp.where(jrow * SLC_BLOCK <= tpos, imp, -BIG))
    sub8 = lax.broadcasted_iota(jnp.int32, (8, tq), 0)
    chunks = [imp[c * 8:(c + 1) * 8] for c in range(ns // 8)]
    ranks = [jnp.zeros((8, tq), F32) for _ in range(ns // 8)]
    for jp in range(ns):
        row = chunks[jp // 8][jp % 8:jp % 8 + 1]
        for c in range(ns // 8):
            if c < jp // 8:
                ahead = jnp.where(row > chunks[c], 1.0, 0.0)
            elif c > jp // 8:
                ahead = jnp.where(row >= chunks[c], 1.0, 0.0)
            else:
                tie = jnp.where(sub8 > jp % 8, 1.0, 0.0)
                ahead = jnp.where(row > chunks[c], 1.0, jnp.where(row == chunks[c], tie, 0.0))
            ranks[c] = ranks[c] + ahead
    selt = [jnp.where(ranks[c] < topk, 0.0, NEG) for c in range(ns // 8)]

    def consume(vt_ref, ktc, src_ref, carry, mask=None):
        vt = vt_ref[0, 0, :, key_tile(ktc)]
        out = []
        for r in range(rep):
            m_old, acc = carry[r]
            s = src_ref[:, r * tq:(r + 1) * tq]
            if mask is not None:
                s = jnp.where(mask, s, NEG)
            m_new = jnp.maximum(m_old, jnp.max(s, axis=0, keepdims=True))
            alpha = jnp.exp2(m_old - m_new)
            p = jnp.exp2(s - m_new).astype(BF16)
            out.append((m_new, acc * alpha + _dot(vt, p)))
        return tuple(out)

    def normalised(carry):
        return [acc[:dk] * (1.0 / acc[dk:dk + 1]) for _, acc in carry]

    init = tuple((jnp.full((1, tq), NEG, F32), jnp.zeros((dk + NSA_VPAD, tq), F32)) for _ in range(rep))
    krow = lax.broadcasted_iota(jnp.int32, (tk, tq), 0)
    unseen = 1 << 30

    carry = init
    for kt, src_ref in zip(win_tiles, win_refs):
        ktc = jnp.maximum(kt, 0)
        dlt = tpos - (jnp.where(kt >= 0, ktc * tk, unseen) + krow)
        carry = consume(vwnt_ref, ktc, src_ref, carry, mask=pltpu.bitcast(dlt, jnp.uint32) < jnp.uint32(WIN))
    o_w = normalised(carry)

    selb = jnp.concatenate([jnp.concatenate(selt, axis=0)] * rep, axis=1).astype(BF16)
    pad = jnp.zeros((LANES - dk - ns, rep * tq), BF16)
    qt_sel = jnp.concatenate([qt_all, selb] + ([pad] if LANES > dk + ns else []), axis=0)

    def sel_scores(ktc, dst_ref):
        dst_ref[...] = _dot(ksl_ref[0, 0, key_tile(ktc), :], qt_sel)

    def pair(i, carry):
        kt = 2 * i
        sel_scores(kt + 1, sb_ref)
        carry = consume(vslt_ref, kt, sa_ref, carry)
        sel_scores(kt + 2, sa_ref)
        return consume(vslt_ref, kt + 1, sb_ref, carry)

    past = hi - 1
    sel_scores(0, sa_ref)
    carry = lax.fori_loop(0, past // 2, pair, init)
    carry = lax.cond(past % 2 == 1, lambda c: consume(vslt_ref, past - 1, sa_ref, c), lambda c: c, carry)
    sel_scores(past, sb_ref)
    carry = consume(vslt_ref, past, sb_ref, carry, mask=past * tk + krow <= tpos)
    o_s = normalised(carry)

    gate = jax.nn.sigmoid(gate_ref[0, 0])
    o_t = [gate[3 * r:3 * r + 1] * o_c[r] + gate[3 * r + 1:3 * r + 2] * o_s[r] + gate[3 * r + 2:3 * r + 3] * o_w[r]
           for r in range(rep)]
    o_ref[0] = jnp.concatenate(o_t, axis=0).T.astype(o_ref.dtype)


def _nsa_attn(prec3, qrot, kc, vct, ksl, vslt, kwn, vwnt, gates_t, tq, tk):
    bsz, seq, _ = qrot.shape
    nu = seq // CMP_STRIDE
    ns = seq // SLC_BLOCK
    ci = np.arange(nu)[None, :]
    sj = np.arange(ns)[:, None]
    overlap = (ci * CMP_STRIDE < (sj + 1) * SLC_BLOCK) & (ci * CMP_STRIDE + CMP_BLOCK > sj * SLC_BLOCK) & (ci < nu - 1)
    aggt = jnp.asarray(overlap, BF16)
    gw = NSA_REP * NSA_DIM
    assert tq % tk == 0 and seq % tq == 0
    win_tiles = -(-(WIN - 1) // tk) + tq // tk

    def q_spec():
        return pl.BlockSpec((1, tq, gw), lambda b, g, t: (b, t, g))

    def per_group(rows, cols):
        return pl.BlockSpec((1, 1, rows, cols), lambda b, g, t: (b, g, 0, 0))

    return pl.pallas_call(
        functools.partial(_nsa_attn_kernel, topk=min(SLC_TOPK, ns), tk=tk),
        grid=(bsz, NSA_GROUPS, seq // tq),
        in_specs=[q_spec(), q_spec(), per_group(nu, NSA_DIM), per_group(NSA_DIM, nu),
                  per_group(seq, LANES), per_group(NSA_DIM + NSA_VPAD, seq),
                  per_group(seq, NSA_DIM), per_group(NSA_DIM + NSA_VPAD, seq),
                  pl.BlockSpec((1, 1, 3 * NSA_REP, tq), lambda b, g, t: (b, g, 0, t)),
                  pl.BlockSpec((ns, nu), lambda b, g, t: (0, 0))],
        out_specs=q_spec(),
        out_shape=jax.ShapeDtypeStruct((bsz, seq, NSA_WIDTH), BF16),
        scratch_shapes=[pltpu.VMEM((tk, NSA_REP * tq), F32)] * (2 + win_tiles),
        compiler_params=_params(("parallel", "parallel", "arbitrary")),
        name="nsa_attention",
    )(prec3, qrot, kc, vct, ksl, vslt, kwn, vwnt, gates_t, aggt)


def _nsa(prec3, rest3, kv_off, gate_off, cmp_k, cmp_v):
    bsz, seq, _ = prec3.shape
    nu = seq // CMP_STRIDE
    qrot, ksl, vslt, kwn, vwnt = _nsa_prep(prec3, rest3, kv_off + NSA_KV, min(seq, 512))
    kc = _compress(prec3[:, :, NSA_WIDTH:NSA_WIDTH + NSA_KV], *cmp_k, precise=True)
    vc = _compress(rest3[:, :, kv_off:kv_off + NSA_KV], *cmp_v, precise=False)
    kc = kc.reshape(bsz, NSA_GROUPS, nu, NSA_DIM)
    vct = vc.reshape(bsz, NSA_GROUPS, nu, NSA_DIM).transpose(0, 1, 3, 2)
    gates_t = rest3[:, :, gate_off:gate_off + 3 * NSA_HEADS].reshape(bsz, seq, NSA_GROUPS, 3 * NSA_REP).transpose(0, 2, 3, 1)
    return _nsa_attn(prec3, qrot, kc, vct, ksl, vslt, kwn, vwnt, gates_t, 256, 256)


def _out_proj_kernel(yh_ref, yn_ref, g_ref, w_ref, x_ref, o_ref, y_ref):
    @pl.when(pl.program_id(1) == 0)
    def _():
        wh = yh_ref.shape[1]
        y_ref[:, :wh] = yh_ref[...]
        y_ref[:, wh:] = _rms(yn_ref[...].astype(F32), g_ref[...]).astype(BF16)

    o_ref[...] = x_ref[...] + _dot(y_ref[...], w_ref[...])


def _out_proj(y_hg, y_nsa, nsa_gain, w_out, x, tm, tn):
    m, dm = x.shape
    wh, wn = y_hg.shape[1], y_nsa.shape[1]
    return pl.pallas_call(
        _out_proj_kernel,
        grid=(m // tm, dm // tn),
        in_specs=[
            pl.BlockSpec((tm, wh), lambda i, j: (i, 0)),
            pl.BlockSpec((tm, wn), lambda i, j: (i, 0)),
            pl.BlockSpec((1, wn), lambda i, j: (0, 0)),
            pl.BlockSpec((None, wh + wn, tn), lambda i, j: (j, 0, 0)),
            pl.BlockSpec((tm, tn), lambda i, j: (i, j)),
        ],
        out_specs=pl.BlockSpec((tm, tn), lambda i, j: (i, j)),
        out_shape=jax.ShapeDtypeStruct((m, dm), F32),
        scratch_shapes=[pltpu.VMEM((tm, wh + wn), BF16)],
        compiler_params=_params(("parallel", "arbitrary")),
        name="out_proj",
    )(y_hg, y_nsa, nsa_gain.reshape(1, wn), _column_blocks(w_out.astype(BF16), tn), x)


MOE_ROWS = 256


def _pack_bf16_pairs(hi):
    n = hi.shape[1] // 2
    bits = pltpu.bitcast(hi.astype(F32), jnp.uint32)
    return jnp.right_shift(bits[:, :n], jnp.uint32(16)) | (bits[:, n:] & jnp.uint32(0xFFFF0000))


def _unpack_f32_pairs(words):
    lo = pltpu.bitcast(jnp.left_shift(words, jnp.uint32(16)), F32)
    hi = pltpu.bitcast(words & jnp.uint32(0xFFFF0000), F32)
    return lo, hi


def _unpack_bf16_pairs(words):
    lo, hi = _unpack_f32_pairs(words)
    return lo.astype(BF16), hi.astype(BF16)


def _router_kernel(x_ref, g_ref, w_ref, b_ref, tri_ref, xn_ref, info_ref, infot_ref, cnt_ref, carry_ref):
    @pl.when(pl.program_id(0) == 0)
    def _():
        carry_ref[...] = jnp.zeros_like(carry_ref)

    xn = _rms(x_ref[...], g_ref[...])
    hi, lo = _split2(xn)
    xn_ref[...] = _pack_bf16_pairs(hi)
    logits = _dot(hi, w_ref[0]) + (_dot(hi, w_ref[1]) + _dot(lo, w_ref[0])) + b_ref[...]
    lane = lax.broadcasted_iota(jnp.int32, logits.shape, 1).astype(F32)
    none = float(LANES)

    def first_max(mask):
        top = jnp.max(jnp.where(mask, logits, -jnp.inf), axis=-1, keepdims=True)
        return top, jnp.min(jnp.where(mask & (logits == top), lane, none), axis=-1, keepdims=True)

    is_g = lane < MOE_GROUPS
    gmax, gsel = first_max(is_g)
    gw = 1.0 / jnp.sum(jnp.where(is_g, jnp.exp(logits - gmax), 0.0), axis=-1, keepdims=True)
    lo_lane = MOE_GROUPS + gsel * MOE_EPG
    in_grp = (lane >= lo_lane) & (lane < lo_lane + MOE_EPG)
    v1, i1 = first_max(in_grp)
    v2, i2 = first_max(in_grp & (lane != i1))
    e = jnp.exp(v2 - v1)
    w1 = gw / (1.0 + e)
    w2 = gw * e / (1.0 + e)
    e1 = i1 - MOE_GROUPS
    e2 = i2 - MOE_GROUPS
    onehot = jnp.where((lane == e1) | (lane == e2), 1.0, 0.0)
    before = _dot(tri_ref[...], onehot.astype(BF16)) + carry_ref[...]
    r1 = jnp.sum(jnp.where(lane == e1, before, 0.0), axis=-1, keepdims=True)
    r2 = jnp.sum(jnp.where(lane == e2, before, 0.0), axis=-1, keepdims=True)
    carry_ref[...] = carry_ref[...] + jnp.sum(onehot, axis=0, keepdims=True)
    cnt_ref[...] = carry_ref[...]
    info = jnp.zeros_like(logits)
    for idx, val in enumerate((e1, e2, w1, w2, r1, r2)):
        info = jnp.where(lane == idx, val, info)
    info_ref[...] = info
    infot_ref[...] = info.T[:8]


def _router(x2, gain, w_group, b_group, w_router, b_router, tm):
    m, dm = x2.shape
    wcat = jnp.pad(jnp.concatenate([w_group, w_router], axis=1), ((0, 0), (0, LANES - MOE_GROUPS - N_EXPERTS)))
    bcat = jnp.pad(jnp.concatenate([b_group, b_router]), (0, LANES - MOE_GROUPS - N_EXPERTS)).reshape(1, LANES)
    tri = jnp.asarray(np.tril(np.ones((tm, tm), np.float32), -1), BF16)
    return pl.pallas_call(
        _router_kernel,
        grid=(m // tm,),
        in_specs=[
            pl.BlockSpec((tm, dm), lambda i: (i, 0)),
            pl.BlockSpec((1, dm), lambda i: (0, 0)),
            pl.BlockSpec((2, dm, LANES), lambda i: (0, 0, 0)),
            pl.BlockSpec((1, LANES), lambda i: (0, 0)),
            pl.BlockSpec((tm, tm), lambda i: (0, 0)),
        ],
        out_specs=[
            pl.BlockSpec((tm, dm // 2), lambda i: (i, 0)),
            pl.BlockSpec((tm, LANES), lambda i: (i, 0)),
            pl.BlockSpec((8, tm), lambda i: (0, i)),
            pl.BlockSpec((1, LANES), lambda i: (0, 0)),
        ],
        out_shape=[
            jax.ShapeDtypeStruct((m, dm // 2), jnp.uint32),
            jax.ShapeDtypeStruct((m, LANES), F32),
            jax.ShapeDtypeStruct((8, m), F32),
            jax.ShapeDtypeStruct((1, LANES), F32),
        ],
        scratch_shapes=[pltpu.VMEM((1, LANES), F32)],
        compiler_params=_params(("arbitrary",)),
        name="moe_router",
    )(x2, gain.reshape(1, dm), jnp.stack(_split2(wcat)), bcat, tri)


def _dispatch_kernel(d1_ref, d2_ref, xn_ref, xs_in_ref, xs_ref, sem):
    del xs_in_ref
    tm = xn_ref.shape[0]
    base = pl.program_id(0) * tm

    def row_copy(i, dest):
        return pltpu.make_async_copy(xn_ref.at[pl.ds(i, 1)], xs_ref.at[pl.ds(dest, 1)], sem)

    def issue(i, c):
        row_copy(i, d1_ref[base + i]).start()
        row_copy(i, d2_ref[base + i]).start()
        return c

    lax.fori_loop(0, tm, issue, 0, unroll=8)
    for _ in range(2):
        pltpu.make_async_copy(xn_ref, xs_ref.at[pl.ds(0, tm)], sem).wait()


def _dispatch(xn, d1, d2, rows, tm):
    m, words = xn.shape
    return pl.pallas_call(
        _dispatch_kernel,
        grid_spec=pltpu.PrefetchScalarGridSpec(
            num_scalar_prefetch=2,
            grid=(m // tm,),
            in_specs=[pl.BlockSpec((tm, words), lambda i, *_: (i, 0)), pl.BlockSpec(memory_space=pl.ANY)],
            out_specs=pl.BlockSpec(memory_space=pl.ANY),
            scratch_shapes=[pltpu.SemaphoreType.DMA(())],
        ),
        out_shape=jax.ShapeDtypeStruct((rows, words), jnp.uint32),
        input_output_aliases={3: 0},
        compiler_params=_params(("arbitrary",)),
        name="moe_dispatch",
    )(d1, d2, xn, jnp.zeros((rows, words), jnp.uint32))


def _expert_kernel(be_ref, nused_ref, x_ref, wg_ref, wu_ref, wd_ref, o_ref, wgb_ref, wub_ref, wdb_ref):
    i = pl.program_id(0)

    @pl.when((i == 0) | (be_ref[i] != be_ref[jnp.maximum(i - 1, 0)]))
    def _():
        wgb_ref[...] = wg_ref[0].astype(BF16)
        wub_ref[...] = wu_ref[0].astype(BF16)
        wdb_ref[...] = wd_ref[0].astype(BF16)

    @pl.when(i < nused_ref[0])
    def _():
        half = wgb_ref.shape[0] // 2
        x_lo, x_hi = _unpack_bf16_pairs(x_ref[...])

        def up(w_ref):
            return _dot(x_lo, w_ref[:half]) + _dot(x_hi, w_ref[half:])

        hid = (jax.nn.silu(up(wgb_ref)) * up(wub_ref)).astype(BF16)
        for g in range(o_ref.shape[1] // LANES):
            y = _dot(hid, wdb_ref[:, g * 2 * LANES:(g + 1) * 2 * LANES])
            o_ref[:, g * LANES:(g + 1) * LANES] = _pack_bf16_pairs(y.astype(BF16))

    @pl.when(i >= nused_ref[0])
    def _():
        o_ref[...] = jnp.zeros_like(o_ref)


def _experts(xs, block_e, nused, w_gate, w_up, w_down):
    rows, words = xs.shape
    _, dm, ff = w_gate.shape
    return pl.pallas_call(
        _expert_kernel,
        grid_spec=pltpu.PrefetchScalarGridSpec(
            num_scalar_prefetch=2,
            grid=(rows // MOE_ROWS,),
            in_specs=[
                pl.BlockSpec((MOE_ROWS, words), lambda i, be, nu: (i, 0)),
                pl.BlockSpec((1, dm, ff), lambda i, be, nu: (be[i], 0, 0)),
                pl.BlockSpec((1, dm, ff), lambda i, be, nu: (be[i], 0, 0)),
                pl.BlockSpec((1, ff, dm), lambda i, be, nu: (be[i], 0, 0)),
            ],
            out_specs=pl.BlockSpec((MOE_ROWS, words), lambda i, be, nu: (i, 0)),
            scratch_shapes=[pltpu.VMEM((dm, ff), BF16), pltpu.VMEM((dm, ff), BF16), pltpu.VMEM((ff, dm), BF16)],
        ),
        out_shape=jax.ShapeDtypeStruct((rows, words), jnp.uint32),
        compiler_params=_params(("arbitrary",)),
        name="moe_experts",
    )(block_e, nused, xs, w_gate, w_up, w_down)


def _combine_kernel(d1_ref, d2_ref, x_ref, info_ref, g_ref, ys_ref, o_ref, buf_a, buf_b, sem_a, sem_b):
    ts = buf_a.shape[1]
    dm = x_ref.shape[1]
    step, nsteps = pl.program_id(0), pl.num_programs(0)
    base = step * 2 * ts

    def row_copy(buf_ref, sem, r, slot, src):
        return pltpu.make_async_copy(ys_ref.at[pl.ds(src, 1)], buf_ref.at[slot, pl.ds(r, 1)], sem)

    def issue(buf_ref, sem, tok0):
        for r in range(ts):
            row_copy(buf_ref, sem, r, 0, d1_ref[tok0 + r]).start()
            row_copy(buf_ref, sem, r, 1, d2_ref[tok0 + r]).start()

    def wait(buf_ref, sem):
        for slot in range(2):
            pltpu.make_async_copy(ys_ref.at[pl.ds(0, ts)], buf_ref.at[slot], sem).wait()

    def finish(buf_ref, rows):
        info = info_ref[rows, :]
        w1, w2 = info[:, 2:3], info[:, 3:4]
        ssq = jnp.zeros((ts, 1), F32)
        for g in range(dm // (2 * LANES)):
            a = _unpack_f32_pairs(buf_ref[0, :, g * LANES:(g + 1) * LANES])
            b = _unpack_f32_pairs(buf_ref[1, :, g * LANES:(g + 1) * LANES])
            for half in range(2):
                cols = slice((2 * g + half) * LANES, (2 * g + half + 1) * LANES)
                y = x_ref[rows, cols] + (w1 * a[half] + w2 * b[half])
                ssq = ssq + jnp.sum(y * y, axis=-1, keepdims=True)
                o_ref[rows, cols] = y
        o_ref[rows, :] = o_ref[rows, :] * lax.rsqrt(ssq / dm + EPS) * g_ref[...]

    @pl.when(step == 0)
    def _():
        issue(buf_a, sem_a, base)

    wait(buf_a, sem_a)
    issue(buf_b, sem_b, base + ts)
    finish(buf_a, slice(0, ts))
    wait(buf_b, sem_b)
    issue(buf_a, sem_a, jnp.where(step + 1 < nsteps, base + 2 * ts, base))
    finish(buf_b, slice(ts, 2 * ts))

    @pl.when(step + 1 == nsteps)
    def _():
        wait(buf_a, sem_a)


def _combine(x2, info, gain, ys, d1, d2, ts):
    m, dm = x2.shape
    tm = 2 * ts
    gather_buf = pltpu.VMEM((2, ts, dm // 2), jnp.uint32)
    return pl.pallas_call(
        _combine_kernel,
        grid_spec=pltpu.PrefetchScalarGridSpec(
            num_scalar_prefetch=2,
            grid=(m // tm,),
            in_specs=[
                pl.BlockSpec((tm, dm), lambda i, *_: (i, 0)),
                pl.BlockSpec((tm, LANES), lambda i, *_: (i, 0)),
                pl.BlockSpec((1, dm), lambda i, *_: (0, 0)),
                pl.BlockSpec(memory_space=pl.ANY),
            ],
            out_specs=pl.BlockSpec((tm, dm), lambda i, *_: (i, 0)),
            scratch_shapes=[gather_buf, gather_buf, pltpu.SemaphoreType.DMA(()), pltpu.SemaphoreType.DMA(())],
        ),
        out_shape=jax.ShapeDtypeStruct((m, dm), F32),
        compiler_params=_params(("arbitrary",)),
        name="moe_combine",
    )(d1, d2, x2, info, gain.reshape(1, dm), ys)


def _moe_and_final_norm(x2, ffn_gain, w_group, b_group, w_router, b_router, w_gate, w_up, w_down, final_gain, tm):
    m, _ = x2.shape
    xn, info, infot, cnt = _router(x2, ffn_gain, w_group, b_group, w_router, b_router, tm)
    counts = cnt[0, :N_EXPERTS].astype(jnp.int32)
    padded = (counts + MOE_ROWS - 1) // MOE_ROWS * MOE_ROWS
    pend = jnp.cumsum(padded)
    pstart = pend - padded
    fields = infot.astype(jnp.int32)
    d1 = pstart[fields[0]] + fields[4]
    d2 = pstart[fields[1]] + fields[5]
    nblocks = 2 * m // MOE_ROWS + N_EXPERTS
    first_row = jnp.arange(nblocks, dtype=jnp.int32) * MOE_ROWS
    block_e = jnp.minimum(jnp.sum(pend[None, :] <= first_row[:, None], axis=1), N_EXPERTS - 1).astype(jnp.int32)
    nused = (pend[-1:] // MOE_ROWS).astype(jnp.int32)
    xs = _dispatch(xn, d1, d2, nblocks * MOE_ROWS, tm)
    ys = _experts(xs, block_e, nused, w_gate, w_up, w_down)
    return _combine(x2, info, final_gain, ys, d1, d2, tm)


def kernel(x, attn_norm, w_in, hg_lb_logits, hg_out_norm, cmp_pos_k, cmp_w1_k, cmp_w2_k, cmp_pos_v, cmp_w1_v,
           cmp_w2_v, nsa_out_norm, w_out, ffn_norm, moe_w_group, moe_b_group, moe_w_router, moe_b_router,
           moe_w_gate, moe_w_up, moe_w_down, final_norm):
    bsz, seq, dm = x.shape
    xt = x.reshape(bsz * seq, dm)
    w = w_in[0]
    p0, p1 = 4 * HG_QK, 4 * HG_QK + NSA_WIDTH + NSA_KV
    tn = 256
    assert p0 % tn == 0 and p1 % tn == 0
    w_all = _column_blocks(jnp.pad(w.astype(BF16), ((0, 0), (0, (-w.shape[1]) % tn)))[None], tn)
    rest = _normed_matmul(xt, attn_norm[0], w_all, 1024, skip=(p0 // tn, (p1 - p0) // tn))
    prec = _normed_matmul(xt, attn_norm[0], _column_blocks(jnp.stack(_split2(w[:, p0:p1])), tn), 1024)
    rest3 = rest.reshape(bsz, seq, -1)
    prec3 = prec.reshape(bsz, seq, -1)
    y_hg = _hgrn(rest3, hg_lb_logits, hg_out_norm[0], 256)
    y_nsa = _nsa(prec3, rest3, 4 * HG_QK, 4 * HG_QK + 5 * NSA_KV, (cmp_pos_k[0], cmp_w1_k[0], cmp_w2_k[0]),
                 (cmp_pos_v[0], cmp_w1_v[0], cmp_w2_v[0]))
    x2 = _out_proj(y_hg.reshape(bsz * seq, -1), y_nsa.reshape(bsz * seq, -1), nsa_out_norm[0], w_out[0], xt, 1024, 512)
    out = _moe_and_final_norm(x2, ffn_norm[0], moe_w_group[0], moe_b_group[0], moe_w_router[0], moe_b_router[0],
                              moe_w_gate[0], moe_w_up[0], moe_w_down[0], final_norm, 256)
    return out.reshape(bsz, seq, dm)
```

```python
import functools

import jax
import jax.numpy as jnp
import numpy as np
from jax import lax
from jax.experimental import pallas as pl
from jax.experimental.pallas import tpu as pltpu

F32 = jnp.float32
BF16 = jnp.bfloat16

EPS = 1e-6
ROPE_THETA = 10000.0
NEG = -1e30
BIG = 1e9
LOG2E = 1.4426950408889634

HG_HEADS = 8
HG_DIM = 128
HG_QK = HG_HEADS * HG_DIM
HG_CHUNK = 64
HG_SUB = 8

NSA_HEADS = 16
NSA_GROUPS = 4
NSA_REP = NSA_HEADS // NSA_GROUPS
NSA_DIM = 64
NSA_WIDTH = NSA_HEADS * NSA_DIM
NSA_KV = NSA_GROUPS * NSA_DIM
NSA_VPAD = 16
CMP_BLOCK = 32
CMP_STRIDE = 16
CMP_HIDDEN = 256
SLC_BLOCK = 64
SLC_TOPK = 16
SLC_LOCAL = 2
WIN = 512

MOE_GROUPS = 4
MOE_EPG = 8
N_EXPERTS = MOE_GROUPS * MOE_EPG
EXPERT_FF = 512

LANES = 128
VMEM_LIMIT = 56 * 1024 * 1024


def _params(semantics, **kw):
    return pltpu.CompilerParams(dimension_semantics=semantics, vmem_limit_bytes=VMEM_LIMIT, **kw)


def _split2(a):
    hi = a.astype(BF16)
    return hi, (a - hi.astype(F32)).astype(BF16)


def _split3(a):
    hi = a.astype(BF16)
    r = a - hi.astype(F32)
    mid = r.astype(BF16)
    return hi, mid, (r - mid.astype(F32)).astype(BF16)


def _dot(a, b):
    return jnp.dot(a, b, preferred_element_type=F32)


def _dot_nt(a, b):
    return lax.dot_general(a, b, (((1,), (1,)), ((), ())), preferred_element_type=F32)


def _dot3(a, b):
    a_hi, a_lo = _split2(a)
    b_hi, b_lo = _split2(b)
    return _dot(a_hi, b_hi) + (_dot(a_hi, b_lo) + _dot(a_lo, b_hi))


def _dot3_nt(a, b):
    a_hi, a_lo = _split2(a)
    b_hi, b_lo = _split2(b)
    return _dot_nt(a_hi, b_hi) + (_dot_nt(a_hi, b_lo) + _dot_nt(a_lo, b_hi))


def _rms(x, gain):
    return x * lax.rsqrt(jnp.mean(x * x, axis=-1, keepdims=True) + EPS) * gain


def _normed_matmul_kernel(x_ref, g_ref, wp_ref, w_ref, o_ref, h_ref, *, nprec):
    j = pl.program_id(1)

    @pl.when(j == 0)
    def _():
        y = _rms(x_ref[...], g_ref[...])
        hi = y.astype(BF16)
        h_ref[0] = hi
        h_ref[1] = (y - hi.astype(F32)).astype(BF16)

    @pl.when(j < nprec)
    def _():
        o_ref[...] = _dot(h_ref[0], wp_ref[0]) + (_dot(h_ref[0], wp_ref[1]) + _dot(h_ref[1], wp_ref[0]))

    @pl.when(j >= nprec)
    def _():
        o_ref[...] = _dot(h_ref[0], w_ref[...])


def _column_blocks(w, tn):
    *lead, k, n = w.shape
    return jnp.moveaxis(w.reshape(*lead, k, n // tn, tn), -2, -3)


def _normed_matmul(x, gain, wp_blocks, w_blocks, first, tm):
    m, k = x.shape
    nb, _, tn = w_blocks.shape
    nprec = wp_blocks.shape[1]

    def single_pass_block(j):
        r = jnp.maximum(j - nprec, 0)
        return jnp.where(r < first, r, r + nprec)

    return pl.pallas_call(
        functools.partial(_normed_matmul_kernel, nprec=nprec),
        grid=(m // tm, nb),
        in_specs=[
            pl.BlockSpec((tm, k), lambda i, j: (i, 0)),
            pl.BlockSpec((1, k), lambda i, j: (0, 0)),
            pl.BlockSpec((2, None, k, tn), lambda i, j: (0, jnp.minimum(j, nprec - 1), 0, 0)),
            pl.BlockSpec((None, k, tn), lambda i, j: (single_pass_block(j), 0, 0)),
        ],
        out_specs=pl.BlockSpec((tm, tn), lambda i, j: (i, j)),
        out_shape=jax.ShapeDtypeStruct((m, nb * tn), F32),
        scratch_shapes=[pltpu.VMEM((2, tm, k), BF16)],
        compiler_params=_params(("parallel", "arbitrary")),
        name="normed_matmul",
    )(x, gain.reshape(1, k), wp_blocks, w_blocks)


def _hgrn_consts():
    c, sub = HG_CHUNK, HG_SUB
    tri = np.tile(np.tril(np.ones((c, c), np.float32)), (1, 3))
    gsum = (np.arange(c * sub)[None, :] // sub == np.arange(c)[:, None]).astype(np.float32)
    return jnp.asarray(tri, BF16), jnp.asarray(gsum, BF16)


HG_PAR = 2


def _hgrn_kernel(q_ref, f_ref, i_ref, g_ref, lbl_ref, gain_ref, tri_ref, gsum_ref, o_ref, *scratch):
    c, sub, d = HG_CHUNK, HG_SUB, HG_DIM
    nsub = c // sub
    heads = range(HG_PAR)
    st_refs, p_refs, cl_refs, qs_refs, k_refs = (scratch[i * HG_PAR:(i + 1) * HG_PAR] for i in range(5))

    @pl.when(pl.program_id(2) == 0)
    def _():
        for h in heads:
            st_refs[h][...] = jnp.zeros_like(st_refs[h])

    l0 = lbl_ref[0:1, :]
    l1 = lbl_ref[1:2, :]
    lmax = jnp.maximum(l0, l1)
    e0 = jnp.exp(l0 - lmax)
    lb_all = e0 / (e0 + jnp.exp(l1 - lmax))
    srow = lax.broadcasted_iota(jnp.int32, (sub, d), 0)
    ones = jnp.ones((d, d), BF16)

    def rows_at(x, start):
        parts = ([jnp.zeros((start, d), F32)] if start else []) + [x]
        if start + x.shape[0] < c:
            parts.append(jnp.zeros((c - start - x.shape[0], d), F32))
        return jnp.concatenate(parts, axis=0)

    nchunks = q_ref.shape[1] // c
    cols = [slice(h * d, (h + 1) * d) for h in heads]

    def load(ci):
        rows = pl.ds(pl.multiple_of(ci * c, c), c)
        out = []
        for h in heads:
            lb = lb_all[:, cols[h]]
            f = lb + (1.0 - lb) * jax.nn.sigmoid(f_ref[0, rows, cols[h]])
            bcum = _dot(tri_ref[...], jnp.concatenate(_split3(jnp.log(f)), axis=0))
            out.append((q_ref[0, rows, cols[h]] * (d ** -0.5), 1.0 - f, i_ref[0, rows, cols[h]], bcum))
        return tuple(out)

    def chunk(ci, cur):
        nxt = load(jnp.minimum(ci + 1, nchunks - 1))
        rows = pl.ds(pl.multiple_of(ci * c, c), c)
        q, k, v, bcum = ([cur[h][i] for h in heads] for i in range(4))
        a_off, o_inter = [], []
        for h in heads:
            b = bcum[h]
            edge = [b[i * sub - 1:i * sub] for i in range(1, nsub + 1)]
            cl = b - jnp.concatenate([jnp.zeros((sub, d), F32)] + [jnp.broadcast_to(e, (sub, d)) for e in edge[:-1]],
                                     axis=0)
            cl_refs[h][...] = cl
            qs_refs[h][...] = q[h]
            k_refs[h][...] = k[h]
            qe = q[h] * jnp.exp(cl)
            qcat, kcat = [], []
            for i in range(1, nsub):
                qcat.append(rows_at(qe[i * sub:(i + 1) * sub], i * sub))
                kcat.append(rows_at(k[h][:i * sub] * jnp.exp(edge[i - 1] - b[:i * sub]), 0))
            a_off.append(_dot_nt(jnp.concatenate(qcat, axis=1).astype(BF16),
                                 jnp.concatenate(kcat, axis=1).astype(BF16)))
            st = st_refs[h][...]
            o_inter.append(_dot_nt((q[h] * jnp.exp(b)).astype(BF16), st.astype(BF16)))
            kd = k[h] * jnp.exp(edge[-1] - b)
            st_refs[h][...] = st * jnp.exp(edge[-1]) + _dot(v[h].T.astype(BF16), kd.astype(BF16))
        r2 = []
        for h in heads:
            for t in range(c):
                j0 = (t // sub) * sub
                dlt = cl_refs[h][t:t + 1, :] - cl_refs[h][j0:j0 + sub, :]
                e = jnp.where(srow + j0 <= t, jnp.exp(jnp.minimum(dlt, 0.0)), 0.0)
                p_refs[h][t * sub:(t + 1) * sub, :] = (qs_refs[h][t:t + 1, :] * k_refs[h][j0:j0 + sub, :] * e).astype(BF16)
            r2.append(_dot(p_refs[h][...], ones))
        o = []
        for h in heads:
            o.append(o_inter[h] + _dot(a_off[h].astype(BF16), v[h].astype(BF16)))
        for h in heads:
            x = r2[h].reshape(nsub, sub, sub, d) * v[h].reshape(nsub, 1, sub, d)
            o[h] = o[h] + _dot(gsum_ref[...], x.reshape(c * sub, d).astype(BF16))
        for h in heads:
            gate = jax.nn.silu(g_ref[0, rows, cols[h]])
            o_ref[0, rows, cols[h]] = (_rms(o[h], gain_ref[...]) * gate).astype(o_ref.dtype)
        return nxt

    lax.fori_loop(0, nchunks, chunk, load(0), unroll=2)


def _hgrn(proj3, col0, lb_logits, out_gain, tseq):
    bsz, seq, _ = proj3.shape
    d, c, sub = HG_DIM, HG_CHUNK, HG_SUB
    wst, gsum = _hgrn_consts()
    groups = HG_HEADS // HG_PAR
    width = HG_PAR * d
    assert col0 % width == 0

    def col(off):
        return pl.BlockSpec((1, tseq, width), lambda b, h, t: (b, t, col0 // width + off * groups + h))

    per_head = [pltpu.VMEM((d, d), F32), pltpu.VMEM((c * sub, d), BF16), pltpu.VMEM((c, d), F32),
                pltpu.VMEM((c, d), F32), pltpu.VMEM((c, d), F32)]
    return pl.pallas_call(
        _hgrn_kernel,
        grid=(bsz, groups, seq // tseq),
        in_specs=[
            col(0), col(1), col(2), col(3),
            pl.BlockSpec((2, width), lambda b, h, t: (0, h)),
            pl.BlockSpec((1, d), lambda b, h, t: (0, 0)),
            pl.BlockSpec(wst.shape, lambda b, h, t: (0, 0)),
            pl.BlockSpec(gsum.shape, lambda b, h, t: (0, 0)),
        ],
        out_specs=pl.BlockSpec((1, tseq, width), lambda b, h, t: (b, t, h)),
        out_shape=jax.ShapeDtypeStruct((bsz, seq, HG_QK), BF16),
        scratch_shapes=[s for s in per_head for _ in range(HG_PAR)],
        compiler_params=_params(("parallel", "parallel", "arbitrary")),
        name="hgrn2",
    )(proj3, proj3, proj3, proj3, lb_logits, out_gain.reshape(1, d), wst, gsum)


def _rope(x, cs, sn):
    lane = lax.broadcasted_iota(jnp.int32, x.shape, 1)
    partner = jnp.where(lane % NSA_DIM < NSA_DIM // 2, pltpu.roll(x, LANES - NSA_DIM // 2, 1),
                        pltpu.roll(x, NSA_DIM // 2, 1))
    return x * cs + partner * sn


def _nsa_prep_kernel(q_ref, ksl_ref, vsl_ref, kwn_ref, vwn_ref, cs_ref, sn_ref,
                     qrot_ref, kslo_ref, vslo_ref, kwno_ref, vwno_ref):
    cs = cs_ref[...]
    sn = sn_ref[...]
    for c in range(NSA_WIDTH // LANES):
        cols = slice(c * LANES, (c + 1) * LANES)
        qrot_ref[0, :, cols] = (_rope(q_ref[0, :, cols], cs, sn) * (NSA_DIM ** -0.5 * LOG2E)).astype(BF16)
    tseq = q_ref.shape[1]
    lane = lax.broadcasted_iota(jnp.int32, (tseq, LANES), 1)
    block = (pl.program_id(1) * tseq + lax.broadcasted_iota(jnp.int32, (tseq, LANES), 0)) // SLC_BLOCK
    block_onehot = jnp.where(lane - NSA_DIM == block, 1.0, 0.0)
    ones_rows = jnp.where(lax.broadcasted_iota(jnp.int32, (NSA_VPAD, tseq), 0) == 0, 1.0, 0.0).astype(BF16)
    for c in range(NSA_KV // LANES):
        cols = slice(c * LANES, (c + 1) * LANES)
        ks = _rope(ksl_ref[0, :, cols], cs, sn)
        kw = _rope(kwn_ref[0, :, cols], cs, sn).astype(BF16)
        vs = vsl_ref[0, :, cols].T.astype(BF16)
        vw = vwn_ref[0, :, cols].T.astype(BF16)
        for half in range(LANES // NSA_DIM):
            g = c * (LANES // NSA_DIM) + half
            hs = slice(half * NSA_DIM, (half + 1) * NSA_DIM)
            ks_g = ks if half == 0 else pltpu.roll(ks, NSA_DIM, 1)
            kslo_ref[0, g] = jnp.where(lane < NSA_DIM, ks_g, block_onehot).astype(BF16)
            kwno_ref[0, g] = kw[:, hs]
            vslo_ref[0, g, :NSA_DIM] = vs[hs, :]
            vslo_ref[0, g, NSA_DIM:] = ones_rows
            vwno_ref[0, g, :NSA_DIM] = vw[hs, :]
            vwno_ref[0, g, NSA_DIM:] = ones_rows


def _nsa_prep(prec3, rest3, kv_off, tseq):
    bsz, seq, _ = prec3.shape
    half = NSA_DIM // 2
    inv = 1.0 / (ROPE_THETA ** (jnp.arange(0, NSA_DIM, 2, dtype=F32) / NSA_DIM))
    ang = jnp.arange(seq, dtype=F32)[:, None] * inv[None, :]
    cs = jnp.tile(jnp.cos(ang), (1, LANES // half))
    sn = jnp.tile(jnp.concatenate([-jnp.sin(ang), jnp.sin(ang)], axis=1), (1, LANES // NSA_DIM))
    kvb = kv_off // NSA_KV

    def kv_in(i):
        return pl.BlockSpec((1, tseq, NSA_KV), lambda b, t: (b, t, kvb + i))

    assert NSA_DIM + seq // SLC_BLOCK <= LANES

    def k_out(width):
        return (pl.BlockSpec((1, NSA_GROUPS, tseq, width), lambda b, t: (b, 0, t, 0)),
                jax.ShapeDtypeStruct((bsz, NSA_GROUPS, seq, width), BF16))

    (ksl_out, ksl_shape), (kwn_out, kwn_shape) = k_out(LANES), k_out(NSA_DIM)
    v_out = pl.BlockSpec((1, NSA_GROUPS, NSA_DIM + NSA_VPAD, tseq), lambda b, t: (b, 0, 0, t))
    v_shape = jax.ShapeDtypeStruct((bsz, NSA_GROUPS, NSA_DIM + NSA_VPAD, seq), BF16)
    tab = pl.BlockSpec((tseq, LANES), lambda b, t: (t, 0))
    return pl.pallas_call(
        _nsa_prep_kernel,
        grid=(bsz, seq // tseq),
        in_specs=[pl.BlockSpec((1, tseq, NSA_WIDTH), lambda b, t: (b, t, 0)), kv_in(0), kv_in(1), kv_in(2), kv_in(3),
                  tab, tab],
        out_specs=[pl.BlockSpec((1, tseq, NSA_WIDTH), lambda b, t: (b, t, 0)), ksl_out, v_out, kwn_out, v_out],
        out_shape=[jax.ShapeDtypeStruct((bsz, seq, NSA_WIDTH), BF16), ksl_shape, v_shape, kwn_shape, v_shape],
        compiler_params=_params(("parallel", "parallel")),
        name="nsa_prep",
    )(prec3, rest3, rest3, rest3, rest3, cs, sn)


def _compress_kernel(u_ref, pos_ref, w1_ref, w2_ref, o_ref, *, precise):
    mm = _dot3 if precise else (lambda a, b: _dot(a.astype(BF16), b.astype(BF16)))
    u = u_ref[0]
    nu = u.shape[0]
    ya = mm(u + pos_ref[0:1, :], w1_ref[0])
    yb = mm(u + pos_ref[1:2, :], w1_ref[1])
    hid = ya + pltpu.roll(yb, nu - 1, 0)
    o_ref[0] = mm(jax.nn.gelu(hid), w2_ref[...])


def _compress(kv, pos, w1, w2, precise):
    bsz, seq, _ = kv.shape
    nu = seq // CMP_STRIDE
    width = CMP_STRIDE * NSA_DIM
    u = kv.reshape(bsz, nu, CMP_STRIDE, NSA_GROUPS, NSA_DIM).transpose(0, 3, 1, 2, 4).reshape(bsz * NSA_GROUPS, nu, width)
    return pl.pallas_call(
        functools.partial(_compress_kernel, precise=precise),
        grid=(bsz * NSA_GROUPS,),
        in_specs=[
            pl.BlockSpec((1, nu, width), lambda i: (i, 0, 0)),
            pl.BlockSpec((2, width), lambda i: (0, 0)),
            pl.BlockSpec((2, width, CMP_HIDDEN), lambda i: (0, 0, 0)),
            pl.BlockSpec((CMP_HIDDEN, NSA_DIM), lambda i: (0, 0)),
        ],
        out_specs=pl.BlockSpec((1, nu, NSA_DIM), lambda i: (i, 0, 0)),
        out_shape=jax.ShapeDtypeStruct((bsz * NSA_GROUPS, nu, NSA_DIM), F32),
        compiler_params=_params(("parallel",)),
        name="nsa_compress",
    )(u, pos.reshape(2, width), w1.reshape(2, width, CMP_HIDDEN), w2)


def _nsa_attn_kernel(qraw_ref, qrot_ref, kc_ref, vct_ref, ksl_ref, vslt_ref, kwn_ref, vwnt_ref, gate_ref, aggt_ref,
                     o_ref, sa_ref, sb_ref, *win_refs, topk, tk):
    tq = qraw_ref.shape[1]
    nu = kc_ref.shape[2]
    ns = aggt_ref.shape[0]
    rep, dk = NSA_REP, NSA_DIM
    qs = pl.program_id(2) * tq
    tpos = qs + lax.broadcasted_iota(jnp.int32, (1, tq), 1)

    qrt = (qraw_ref[0] * (dk ** -0.5 * LOG2E)).T
    kc_hi, kc_lo = _split2(kc_ref[0, 0])
    vct = vct_ref[0, 0].astype(BF16)
    crow = lax.broadcasted_iota(jnp.int32, (nu, tq), 0)
    m_c = (crow * CMP_STRIDE + CMP_BLOCK - 1 <= tpos) & (crow < nu - 1)
    q_hi, q_lo = _split2(jnp.concatenate([qrt[r * dk:(r + 1) * dk] for r in range(rep)], axis=1))
    s_all = _dot(jnp.concatenate([kc_hi, kc_hi, kc_lo], axis=1),
                 jnp.concatenate([q_hi, q_lo, q_hi], axis=0))

    qt = qrot_ref[0].astype(F32).T.astype(BF16)
    qt_all = jnp.concatenate([qt[r * dk:(r + 1) * dk] for r in range(rep)], axis=1)
    hi = (qs + tq) // tk

    def key_tile(ktc):
        return pl.ds(pl.multiple_of(ktc * tk, tk), tk)

    def scores(k_ref, ktc, dst_ref):
        dst_ref[...] = _dot(k_ref[0, 0, key_tile(ktc), :], qt_all)

    win_tiles = [hi - len(win_refs) + j for j in range(len(win_refs))]
    for kt, dst_ref in zip(win_tiles, win_refs):
        scores(kwn_ref, jnp.maximum(kt, 0), dst_ref)
    psum = jnp.zeros((nu, tq), F32)
    p_all = []
    has_block = tpos >= CMP_BLOCK - 1
    for r in range(rep):
        s = jnp.where(m_c, s_all[:, r * tq:(r + 1) * tq], NEG)
        e = jnp.exp2(s - jnp.max(s, axis=0, keepdims=True))
        p = e * jnp.where(has_block, 1.0 / jnp.sum(e, axis=0, keepdims=True), 0.0)
        psum = psum + p
        p_all.append(p.astype(BF16))
    o_c_all = _dot(vct, jnp.concatenate(p_all, axis=1))
    o_c = [o_c_all[:, r * tq:(r + 1) * tq] for r in range(rep)]

    p_hi, p_lo = _split2(psum)
    imp = _dot(aggt_ref[...], p_hi) + _dot(aggt_ref[...], p_lo)
    jrow = lax.broadcasted_iota(jnp.int32, (ns, tq), 0)
    dj = jnp.right_shift(tpos, SLC_BLOCK.bit_length() - 1) - jrow
    forced = (jrow == 0) | ((dj >= 0) & (dj < SLC_LOCAL))
    imp = jnp.where(forced, BIG, jnp.where(jrow * SLC_BLOCK <= tpos, imp, -BIG))
    sub8 = lax.broadcasted_iota(jnp.int32, (8, tq), 0)
    chunks = [imp[c * 8:(c + 1) * 8] for c in range(ns // 8)]
    ranks = [jnp.zeros((8, tq), F32) for _ in range(ns // 8)]
    for jp in range(ns):
        row = chunks[jp // 8][jp % 8:jp % 8 + 1]
        for c in range(ns // 8):
            if c < jp // 8:
                ahead = jnp.where(row > chunks[c], 1.0, 0.0)
            elif c > jp // 8:
                ahead = jnp.where(row >= chunks[c], 1.0, 0.0)
            else:
                tie = jnp.where(sub8 > jp % 8, 1.0, 0.0)
                ahead = jnp.where(row > chunks[c], 1.0, jnp.where(row == chunks[c], tie, 0.0))
            ranks[c] = ranks[c] + ahead
    selt = [jnp.where(ranks[c] < topk, 0.0, NEG) for c in range(ns // 8)]

    def consume(vt_ref, ktc, src_ref, carry, mask=None):
        vt = vt_ref[0, 0, :, key_tile(ktc)]
        out = []
        for r in range(rep):
            m_old, acc = carry[r]
            s = src_ref[:, r * tq:(r + 1) * tq]
            if mask is not None:
                s = jnp.where(mask, s, NEG)
            m_new = jnp.maximum(m_old, jnp.max(s, axis=0, keepdims=True))
            alpha = jnp.exp2(m_old - m_new)
            p = jnp.exp2(s - m_new).astype(BF16)
            out.append((m_new, acc * alpha + _dot(vt, p)))
        return tuple(out)

    def normalised(carry):
        return [acc[:dk] * (1.0 / acc[dk:dk + 1]) for _, acc in carry]

    init = tuple((jnp.full((1, tq), NEG, F32), jnp.zeros((dk + NSA_VPAD, tq), F32)) for _ in range(rep))
    krow = lax.broadcasted_iota(jnp.int32, (tk, tq), 0)
    unseen = 1 << 30

    carry = init
    for kt, src_ref in zip(win_tiles, win_refs):
        ktc = jnp.maximum(kt, 0)
        dlt = tpos - (jnp.where(kt >= 0, ktc * tk, unseen) + krow)
        carry = consume(vwnt_ref, ktc, src_ref, carry, mask=pltpu.bitcast(dlt, jnp.uint32) < jnp.uint32(WIN))
    o_w = normalised(carry)

    selb = jnp.concatenate([jnp.concatenate(selt, axis=0)] * rep, axis=1).astype(BF16)
    pad = jnp.zeros((LANES - dk - ns, rep * tq), BF16)
    qt_sel = jnp.concatenate([qt_all, selb] + ([pad] if LANES > dk + ns else []), axis=0)

    def sel_scores(ktc, dst_ref):
        dst_ref[...] = _dot(ksl_ref[0, 0, key_tile(ktc), :], qt_sel)

    def pair(i, carry):
        kt = 2 * i
        sel_scores(kt + 1, sb_ref)
        carry = consume(vslt_ref, kt, sa_ref, carry)
        sel_scores(kt + 2, sa_ref)
        return consume(vslt_ref, kt + 1, sb_ref, carry)

    past = hi - 1
    sel_scores(0, sa_ref)
    carry = lax.fori_loop(0, past // 2, pair, init)
    carry = lax.cond(past % 2 == 1, lambda c: consume(vslt_ref, past - 1, sa_ref, c), lambda c: c, carry)
    sel_scores(past, sb_ref)
    carry = consume(vslt_ref, past, sb_ref, carry, mask=past * tk + krow <= tpos)
    o_s = normalised(carry)

    gate = jax.nn.sigmoid(gate_ref[0, 0])
    o_t = [gate[3 * r:3 * r + 1] * o_c[r] + gate[3 * r + 1:3 * r + 2] * o_s[r] + gate[3 * r + 2:3 * r + 3] * o_w[r]
           for r in range(rep)]
    o_ref[0] = jnp.concatenate(o_t, axis=0).T.astype(o_ref.dtype)


def _nsa_attn(prec3, qrot, kc, vct, ksl, vslt, kwn, vwnt, gates_t, tq, tk):
    bsz, seq, _ = qrot.shape
    nu = seq // CMP_STRIDE
    ns = seq // SLC_BLOCK
    ci = np.arange(nu)[None, :]
    sj = np.arange(ns)[:, None]
    overlap = (ci * CMP_STRIDE < (sj + 1) * SLC_BLOCK) & (ci * CMP_STRIDE + CMP_BLOCK > sj * SLC_BLOCK) & (ci < nu - 1)
    aggt = jnp.asarray(overlap, BF16)
    gw = NSA_REP * NSA_DIM
    assert tq % tk == 0 and seq % tq == 0
    win_tiles = -(-(WIN - 1) // tk) + tq // tk

    def q_spec():
        return pl.BlockSpec((1, tq, gw), lambda b, g, t: (b, t, g))

    def per_group(rows, cols):
        return pl.BlockSpec((1, 1, rows, cols), lambda b, g, t: (b, g, 0, 0))

    return pl.pallas_call(
        functools.partial(_nsa_attn_kernel, topk=min(SLC_TOPK, ns), tk=tk),
        grid=(bsz, NSA_GROUPS, seq // tq),
        in_specs=[q_spec(), q_spec(), per_group(nu, NSA_DIM), per_group(NSA_DIM, nu),
                  per_group(seq, LANES), per_group(NSA_DIM + NSA_VPAD, seq),
                  per_group(seq, NSA_DIM), per_group(NSA_DIM + NSA_VPAD, seq),
                  pl.BlockSpec((1, 1, 3 * NSA_REP, tq), lambda b, g, t: (b, g, 0, t)),
                  pl.BlockSpec((ns, nu), lambda b, g, t: (0, 0))],
        out_specs=q_spec(),
        out_shape=jax.ShapeDtypeStruct((bsz, seq, NSA_WIDTH), BF16),
        scratch_shapes=[pltpu.VMEM((tk, NSA_REP * tq), F32)] * (2 + win_tiles),
        compiler_params=_params(("parallel", "parallel", "arbitrary")),
        name="nsa_attention",
    )(prec3, qrot, kc, vct, ksl, vslt, kwn, vwnt, gates_t, aggt)


def _nsa(prec3, rest3, kv_off, gate_off, cmp_k, cmp_v):
    bsz, seq, _ = prec3.shape
    nu = seq // CMP_STRIDE
    qrot, ksl, vslt, kwn, vwnt = _nsa_prep(prec3, rest3, kv_off + NSA_KV, min(seq, 512))
    kc = _compress(prec3[:, :, NSA_WIDTH:NSA_WIDTH + NSA_KV], *cmp_k, precise=True)
    vc = _compress(rest3[:, :, kv_off:kv_off + NSA_KV], *cmp_v, precise=False)
    kc = kc.reshape(bsz, NSA_GROUPS, nu, NSA_DIM)
    vct = vc.reshape(bsz, NSA_GROUPS, nu, NSA_DIM).transpose(0, 1, 3, 2)
    gates_t = rest3[:, :, gate_off:gate_off + 3 * NSA_HEADS].reshape(bsz, seq, NSA_GROUPS, 3 * NSA_REP).transpose(0, 2, 3, 1)
    return _nsa_attn(prec3, qrot, kc, vct, ksl, vslt, kwn, vwnt, gates_t, 256, 256)


def _out_proj_kernel(yh_ref, yn_ref, g_ref, w_ref, x_ref, o_ref, y_ref):
    @pl.when(pl.program_id(1) == 0)
    def _():
        wh = yh_ref.shape[1]
        y_ref[:, :wh] = yh_ref[...]
        y_ref[:, wh:] = _rms(yn_ref[...].astype(F32), g_ref[...]).astype(BF16)

    o_ref[...] = x_ref[...] + _dot(y_ref[...], w_ref[...])


def _out_proj(y_hg, y_nsa, nsa_gain, w_out, x, tm, tn):
    m, dm = x.shape
    wh, wn = y_hg.shape[1], y_nsa.shape[1]
    return pl.pallas_call(
        _out_proj_kernel,
        grid=(m // tm, dm // tn),
        in_specs=[
            pl.BlockSpec((tm, wh), lambda i, j: (i, 0)),
            pl.BlockSpec((tm, wn), lambda i, j: (i, 0)),
            pl.BlockSpec((1, wn), lambda i, j: (0, 0)),
            pl.BlockSpec((None, wh + wn, tn), lambda i, j: (j, 0, 0)),
            pl.BlockSpec((tm, tn), lambda i, j: (i, j)),
        ],
        out_specs=pl.BlockSpec((tm, tn), lambda i, j: (i, j)),
        out_shape=jax.ShapeDtypeStruct((m, dm), F32),
        scratch_shapes=[pltpu.VMEM((tm, wh + wn), BF16)],
        compiler_params=_params(("parallel", "arbitrary")),
        name="out_proj",
    )(y_hg, y_nsa, nsa_gain.reshape(1, wn), _column_blocks(w_out.astype(BF16), tn), x)


MOE_ROWS = 256


def _pack_bf16_pairs(hi):
    n = hi.shape[1] // 2
    bits = pltpu.bitcast(hi.astype(F32), jnp.uint32)
    return jnp.right_shift(bits[:, :n], jnp.uint32(16)) | (bits[:, n:] & jnp.uint32(0xFFFF0000))


def _unpack_f32_pairs(words):
    lo = pltpu.bitcast(jnp.left_shift(words, jnp.uint32(16)), F32)
    hi = pltpu.bitcast(words & jnp.uint32(0xFFFF0000), F32)
    return lo, hi


def _unpack_bf16_pairs(words):
    lo, hi = _unpack_f32_pairs(words)
    return lo.astype(BF16), hi.astype(BF16)


def _router_kernel(x_ref, g_ref, w_ref, b_ref, tri_ref, xn_ref, info_ref, infot_ref, cnt_ref, carry_ref):
    @pl.when(pl.program_id(0) == 0)
    def _():
        carry_ref[...] = jnp.zeros_like(carry_ref)

    xn = _rms(x_ref[...], g_ref[...])
    hi, lo = _split2(xn)
    xn_ref[...] = _pack_bf16_pairs(hi)
    logits = _dot(hi, w_ref[0]) + (_dot(hi, w_ref[1]) + _dot(lo, w_ref[0])) + b_ref[...]
    lane = lax.broadcasted_iota(jnp.int32, logits.shape, 1).astype(F32)
    none = float(LANES)

    def first_max(mask):
        top = jnp.max(jnp.where(mask, logits, -jnp.inf), axis=-1, keepdims=True)
        return top, jnp.min(jnp.where(mask & (logits == top), lane, none), axis=-1, keepdims=True)

    is_g = lane < MOE_GROUPS
    gmax, gsel = first_max(is_g)
    gw = 1.0 / jnp.sum(jnp.where(is_g, jnp.exp(logits - gmax), 0.0), axis=-1, keepdims=True)
    lo_lane = MOE_GROUPS + gsel * MOE_EPG
    in_grp = (lane >= lo_lane) & (lane < lo_lane + MOE_EPG)
    v1, i1 = first_max(in_grp)
    v2, i2 = first_max(in_grp & (lane != i1))
    e = jnp.exp(v2 - v1)
    w1 = gw / (1.0 + e)
    w2 = gw * e / (1.0 + e)
    e1 = i1 - MOE_GROUPS
    e2 = i2 - MOE_GROUPS
    onehot = jnp.where((lane == e1) | (lane == e2), 1.0, 0.0)
    before = _dot(tri_ref[...], onehot.astype(BF16)) + carry_ref[...]
    r1 = jnp.sum(jnp.where(lane == e1, before, 0.0), axis=-1, keepdims=True)
    r2 = jnp.sum(jnp.where(lane == e2, before, 0.0), axis=-1, keepdims=True)
    carry_ref[...] = carry_ref[...] + jnp.sum(onehot, axis=0, keepdims=True)
    cnt_ref[...] = carry_ref[...]
    info = jnp.zeros_like(logits)
    for idx, val in enumerate((e1, e2, w1, w2, r1, r2)):
        info = jnp.where(lane == idx, val, info)
    info_ref[...] = info
    infot_ref[...] = info.T[:8]


def _router(x2, gain, w_group, b_group, w_router, b_router, tm):
    m, dm = x2.shape
    wcat = jnp.pad(jnp.concatenate([w_group, w_router], axis=1), ((0, 0), (0, LANES - MOE_GROUPS - N_EXPERTS)))
    bcat = jnp.pad(jnp.concatenate([b_group, b_router]), (0, LANES - MOE_GROUPS - N_EXPERTS)).reshape(1, LANES)
    tri = jnp.asarray(np.tril(np.ones((tm, tm), np.float32), -1), BF16)
    return pl.pallas_call(
        _router_kernel,
        grid=(m // tm,),
        in_specs=[
            pl.BlockSpec((tm, dm), lambda i: (i, 0)),
            pl.BlockSpec((1, dm), lambda i: (0, 0)),
            pl.BlockSpec((2, dm, LANES), lambda i: (0, 0, 0)),
            pl.BlockSpec((1, LANES), lambda i: (0, 0)),
            pl.BlockSpec((tm, tm), lambda i: (0, 0)),
        ],
        out_specs=[
            pl.BlockSpec((tm, dm // 2), lambda i: (i, 0)),
            pl.BlockSpec((tm, LANES), lambda i: (i, 0)),
            pl.BlockSpec((8, tm), lambda i: (0, i)),
            pl.BlockSpec((1, LANES), lambda i: (0, 0)),
        ],
        out_shape=[
            jax.ShapeDtypeStruct((m, dm // 2), jnp.uint32),
            jax.ShapeDtypeStruct((m, LANES), F32),
            jax.ShapeDtypeStruct((8, m), F32),
            jax.ShapeDtypeStruct((1, LANES), F32),
        ],
        scratch_shapes=[pltpu.VMEM((1, LANES), F32)],
        compiler_params=_params(("arbitrary",)),
        name="moe_router",
    )(x2, gain.reshape(1, dm), jnp.stack(_split2(wcat)), bcat, tri)


def _dispatch_kernel(d1_ref, d2_ref, xn_ref, xs_in_ref, xs_ref, sem):
    del xs_in_ref
    tm = xn_ref.shape[0]
    base = pl.program_id(0) * tm

    def row_copy(i, dest):
        return pltpu.make_async_copy(xn_ref.at[pl.ds(i, 1)], xs_ref.at[pl.ds(dest, 1)], sem)

    def issue(i, c):
        row_copy(i, d1_ref[base + i]).start()
        row_copy(i, d2_ref[base + i]).start()
        return c

    lax.fori_loop(0, tm, issue, 0, unroll=8)
    for _ in range(2):
        pltpu.make_async_copy(xn_ref, xs_ref.at[pl.ds(0, tm)], sem).wait()


def _dispatch(xn, d1, d2, rows, tm):
    m, words = xn.shape
    return pl.pallas_call(
        _dispatch_kernel,
        grid_spec=pltpu.PrefetchScalarGridSpec(
            num_scalar_prefetch=2,
            grid=(m // tm,),
            in_specs=[pl.BlockSpec((tm, words), lambda i, *_: (i, 0)), pl.BlockSpec(memory_space=pl.ANY)],
            out_specs=pl.BlockSpec(memory_space=pl.ANY),
            scratch_shapes=[pltpu.SemaphoreType.DMA(())],
        ),
        out_shape=jax.ShapeDtypeStruct((rows, words), jnp.uint32),
        input_output_aliases={3: 0},
        compiler_params=_params(("arbitrary",)),
        name="moe_dispatch",
    )(d1, d2, xn, jnp.zeros((rows, words), jnp.uint32))


def _expert_kernel(be_ref, nused_ref, x_ref, wg_ref, wu_ref, wd_ref, o_ref, wgb_ref, wub_ref, wdb_ref):
    i = pl.program_id(0)

    @pl.when((i == 0) | (be_ref[i] != be_ref[jnp.maximum(i - 1, 0)]))
    def _():
        wgb_ref[...] = wg_ref[0].astype(BF16)
        wub_ref[...] = wu_ref[0].astype(BF16)
        wdb_ref[...] = wd_ref[0].astype(BF16)

    @pl.when(i < nused_ref[0])
    def _():
        half = wgb_ref.shape[0] // 2
        x_lo, x_hi = _unpack_bf16_pairs(x_ref[...])

        def up(w_ref):
            return _dot(x_lo, w_ref[:half]) + _dot(x_hi, w_ref[half:])

        hid = (jax.nn.silu(up(wgb_ref)) * up(wub_ref)).astype(BF16)
        for g in range(o_ref.shape[1] // LANES):
            y = _dot(hid, wdb_ref[:, g * 2 * LANES:(g + 1) * 2 * LANES])
            o_ref[:, g * LANES:(g + 1) * LANES] = _pack_bf16_pairs(y.astype(BF16))

    @pl.when(i >= nused_ref[0])
    def _():
        o_ref[...] = jnp.zeros_like(o_ref)


def _experts(xs, block_e, nused, w_gate, w_up, w_down):
    rows, words = xs.shape
    _, dm, ff = w_gate.shape
    return pl.pallas_call(
        _expert_kernel,
        grid_spec=pltpu.PrefetchScalarGridSpec(
            num_scalar_prefetch=2,
            grid=(rows // MOE_ROWS,),
            in_specs=[
                pl.BlockSpec((MOE_ROWS, words), lambda i, be, nu: (i, 0)),
                pl.BlockSpec((1, dm, ff), lambda i, be, nu: (be[i], 0, 0)),
                pl.BlockSpec((1, dm, ff), lambda i, be, nu: (be[i], 0, 0)),
                pl.BlockSpec((1, ff, dm), lambda i, be, nu: (be[i], 0, 0)),
            ],
            out_specs=pl.BlockSpec((MOE_ROWS, words), lambda i, be, nu: (i, 0)),
            scratch_shapes=[pltpu.VMEM((dm, ff), BF16), pltpu.VMEM((dm, ff), BF16), pltpu.VMEM((ff, dm), BF16)],
        ),
        out_shape=jax.ShapeDtypeStruct((rows, words), jnp.uint32),
        compiler_params=_params(("arbitrary",)),
        name="moe_experts",
    )(block_e, nused, xs, w_gate, w_up, w_down)


def _combine_kernel(d1_ref, d2_ref, x_ref, info_ref, g_ref, ys_ref, o_ref, buf_a, buf_b, sem_a, sem_b):
    ts = buf_a.shape[1]
    dm = x_ref.shape[1]
    step, nsteps = pl.program_id(0), pl.num_programs(0)
    base = step * 2 * ts

    def row_copy(buf_ref, sem, r, slot, src):
        return pltpu.make_async_copy(ys_ref.at[pl.ds(src, 1)], buf_ref.at[slot, pl.ds(r, 1)], sem)

    def issue(buf_ref, sem, tok0):
        for r in range(ts):
            row_copy(buf_ref, sem, r, 0, d1_ref[tok0 + r]).start()
            row_copy(buf_ref, sem, r, 1, d2_ref[tok0 + r]).start()

    def wait(buf_ref, sem):
        for slot in range(2):
            pltpu.make_async_copy(ys_ref.at[pl.ds(0, ts)], buf_ref.at[slot], sem).wait()

    def finish(buf_ref, rows):
        info = info_ref[rows, :]
        w1, w2 = info[:, 2:3], info[:, 3:4]
        ssq = jnp.zeros((ts, 1), F32)
        for g in range(dm // (2 * LANES)):
            a = _unpack_f32_pairs(buf_ref[0, :, g * LANES:(g + 1) * LANES])
            b = _unpack_f32_pairs(buf_ref[1, :, g * LANES:(g + 1) * LANES])
            for half in range(2):
                cols = slice((2 * g + half) * LANES, (2 * g + half + 1) * LANES)
                y = x_ref[rows, cols] + (w1 * a[half] + w2 * b[half])
                ssq = ssq + jnp.sum(y * y, axis=-1, keepdims=True)
                o_ref[rows, cols] = y
        o_ref[rows, :] = o_ref[rows, :] * lax.rsqrt(ssq / dm + EPS) * g_ref[...]

    @pl.when(step == 0)
    def _():
        issue(buf_a, sem_a, base)

    wait(buf_a, sem_a)
    issue(buf_b, sem_b, base + ts)
    finish(buf_a, slice(0, ts))
    wait(buf_b, sem_b)
    issue(buf_a, sem_a, jnp.where(step + 1 < nsteps, base + 2 * ts, base))
    finish(buf_b, slice(ts, 2 * ts))

    @pl.when(step + 1 == nsteps)
    def _():
        wait(buf_a, sem_a)


def _combine(x2, info, gain, ys, d1, d2, ts):
    m, dm = x2.shape
    tm = 2 * ts
    gather_buf = pltpu.VMEM((2, ts, dm // 2), jnp.uint32)
    return pl.pallas_call(
        _combine_kernel,
        grid_spec=pltpu.PrefetchScalarGridSpec(
            num_scalar_prefetch=2,
            grid=(m // tm,),
            in_specs=[
                pl.BlockSpec((tm, dm), lambda i, *_: (i, 0)),
                pl.BlockSpec((tm, LANES), lambda i, *_: (i, 0)),
                pl.BlockSpec((1, dm), lambda i, *_: (0, 0)),
                pl.BlockSpec(memory_space=pl.ANY),
            ],
            out_specs=pl.BlockSpec((tm, dm), lambda i, *_: (i, 0)),
            scratch_shapes=[gather_buf, gather_buf, pltpu.SemaphoreType.DMA(()), pltpu.SemaphoreType.DMA(())],
        ),
        out_shape=jax.ShapeDtypeStruct((m, dm), F32),
        compiler_params=_params(("arbitrary",)),
        name="moe_combine",
    )(d1, d2, x2, info, gain.reshape(1, dm), ys)


def _moe_and_final_norm(x2, ffn_gain, w_group, b_group, w_router, b_router, w_gate, w_up, w_down, final_gain, tm):
    m, _ = x2.shape
    xn, info, infot, cnt = _router(x2, ffn_gain, w_group, b_group, w_router, b_router, tm)
    counts = cnt[0, :N_EXPERTS].astype(jnp.int32)
    padded = (counts + MOE_ROWS - 1) // MOE_ROWS * MOE_ROWS
    pend = jnp.cumsum(padded)
    pstart = pend - padded
    fields = infot.astype(jnp.int32)
    d1 = pstart[fields[0]] + fields[4]
    d2 = pstart[fields[1]] + fields[5]
    nblocks = 2 * m // MOE_ROWS + N_EXPERTS
    first_row = jnp.arange(nblocks, dtype=jnp.int32) * MOE_ROWS
    block_e = jnp.minimum(jnp.sum(pend[None, :] <= first_row[:, None], axis=1), N_EXPERTS - 1).astype(jnp.int32)
    nused = (pend[-1:] // MOE_ROWS).astype(jnp.int32)
    xs = _dispatch(xn, d1, d2, nblocks * MOE_ROWS, tm)
    ys = _experts(xs, block_e, nused, w_gate, w_up, w_down)
    return _combine(x2, info, final_gain, ys, d1, d2, tm)


def kernel(x, attn_norm, w_in, hg_lb_logits, hg_out_norm, cmp_pos_k, cmp_w1_k, cmp_w2_k, cmp_pos_v, cmp_w1_v,
           cmp_w2_v, nsa_out_norm, w_out, ffn_norm, moe_w_group, moe_b_group, moe_w_router, moe_b_router,
           moe_w_gate, moe_w_up, moe_w_down, final_norm):
    bsz, seq, dm = x.shape
    xt = x.reshape(bsz * seq, dm)
    w = w_in[0]
    p0, p1 = 4 * HG_QK, 4 * HG_QK + NSA_WIDTH + NSA_KV
    tn = 256
    assert p0 % tn == 0 and p1 % tn == 0
    w_all = _column_blocks(jnp.pad(w.astype(BF16), ((0, 0), (0, (-w.shape[1]) % tn))), tn)
    w_prec = _column_blocks(jnp.stack(_split2(w[:, p0:p1])), tn)
    proj3 = _normed_matmul(xt, attn_norm[0], w_prec, w_all, p0 // tn, 1024).reshape(bsz, seq, -1)
    hg0 = p1 - p0
    y_hg = _hgrn(proj3, hg0, hg_lb_logits, hg_out_norm[0], 1024)
    y_nsa = _nsa(proj3, proj3, hg0 + 4 * HG_QK, hg0 + 4 * HG_QK + 5 * NSA_KV,
                 (cmp_pos_k[0], cmp_w1_k[0], cmp_w2_k[0]), (cmp_pos_v[0], cmp_w1_v[0], cmp_w2_v[0]))
    x2 = _out_proj(y_hg.reshape(bsz * seq, -1), y_nsa.reshape(bsz * seq, -1), nsa_out_norm[0], w_out[0], xt, 1024, 512)
    out = _moe_and_final_norm(x2, ffn_norm[0], moe_w_group[0], moe_b_group[0], moe_w_router[0], moe_b_router[0],
                              moe_w_gate[0], moe_w_up[0], moe_w_down[0], final_norm, 256)
    return out.reshape(bsz, seq, dm)
```

```python
import functools

import jax
import jax.numpy as jnp
import numpy as np
from jax import lax
from jax.experimental import pallas as pl
from jax.experimental.pallas import tpu as pltpu

F32 = jnp.float32
BF16 = jnp.bfloat16

EPS = 1e-6
ROPE_THETA = 10000.0
NEG = -1e30
BIG = 1e9
LOG2E = 1.4426950408889634

HG_HEADS = 8
HG_DIM = 128
HG_QK = HG_HEADS * HG_DIM
HG_CHUNK = 64
HG_SUB = 8

NSA_HEADS = 16
NSA_GROUPS = 4
NSA_REP = NSA_HEADS // NSA_GROUPS
NSA_DIM = 64
NSA_WIDTH = NSA_HEADS * NSA_DIM
NSA_KV = NSA_GROUPS * NSA_DIM
NSA_VPAD = 16
CMP_BLOCK = 32
CMP_STRIDE = 16
CMP_HIDDEN = 256
SLC_BLOCK = 64
SLC_TOPK = 16
SLC_LOCAL = 2
WIN = 512

MOE_GROUPS = 4
MOE_EPG = 8
N_EXPERTS = MOE_GROUPS * MOE_EPG
EXPERT_FF = 512

LANES = 128
VMEM_LIMIT = 56 * 1024 * 1024


def _params(semantics, **kw):
    return pltpu.CompilerParams(dimension_semantics=semantics, vmem_limit_bytes=VMEM_LIMIT, **kw)


def _split2(a):
    hi = a.astype(BF16)
    return hi, (a - hi.astype(F32)).astype(BF16)


def _split3(a):
    hi = a.astype(BF16)
    r = a - hi.astype(F32)
    mid = r.astype(BF16)
    return hi, mid, (r - mid.astype(F32)).astype(BF16)


def _dot(a, b):
    return jnp.dot(a, b, preferred_element_type=F32)


def _dot_nt(a, b):
    return lax.dot_general(a, b, (((1,), (1,)), ((), ())), preferred_element_type=F32)


def _dot3(a, b):
    a_hi, a_lo = _split2(a)
    b_hi, b_lo = _split2(b)
    return _dot(a_hi, b_hi) + (_dot(a_hi, b_lo) + _dot(a_lo, b_hi))


def _dot3_nt(a, b):
    a_hi, a_lo = _split2(a)
    b_hi, b_lo = _split2(b)
    return _dot_nt(a_hi, b_hi) + (_dot_nt(a_hi, b_lo) + _dot_nt(a_lo, b_hi))


def _rms(x, gain):
    return x * lax.rsqrt(jnp.mean(x * x, axis=-1, keepdims=True) + EPS) * gain


def _normed_matmul_kernel(x_ref, g_ref, wp_ref, w_ref, o_ref, h_ref, *, nprec):
    j = pl.program_id(1)

    @pl.when(j == 0)
    def _():
        y = _rms(x_ref[...], g_ref[...])
        hi = y.astype(BF16)
        h_ref[0] = hi
        h_ref[1] = (y - hi.astype(F32)).astype(BF16)

    @pl.when(j < nprec)
    def _():
        o_ref[...] = _dot(h_ref[0], wp_ref[0]) + (_dot(h_ref[0], wp_ref[1]) + _dot(h_ref[1], wp_ref[0]))

    @pl.when(j >= nprec)
    def _():
        o_ref[...] = _dot(h_ref[0], w_ref[...])


def _column_blocks(w, tn):
    *lead, k, n = w.shape
    return jnp.moveaxis(w.reshape(*lead, k, n // tn, tn), -2, -3)


def _normed_matmul(x, gain, wp_blocks, w_blocks, first, tm):
    m, k = x.shape
    nb, _, tn = w_blocks.shape
    nprec = wp_blocks.shape[1]

    def single_pass_block(j):
        r = jnp.maximum(j - nprec, 0)
        return jnp.where(r < first, r, r + nprec)

    return pl.pallas_call(
        functools.partial(_normed_matmul_kernel, nprec=nprec),
        grid=(m // tm, nb),
        in_specs=[
            pl.BlockSpec((tm, k), lambda i, j: (i, 0)),
            pl.BlockSpec((1, k), lambda i, j: (0, 0)),
            pl.BlockSpec((2, None, k, tn), lambda i, j: (0, jnp.minimum(j, nprec - 1), 0, 0)),
            pl.BlockSpec((None, k, tn), lambda i, j: (single_pass_block(j), 0, 0)),
        ],
        out_specs=pl.BlockSpec((tm, tn), lambda i, j: (i, j)),
        out_shape=jax.ShapeDtypeStruct((m, nb * tn), F32),
        scratch_shapes=[pltpu.VMEM((2, tm, k), BF16)],
        compiler_params=_params(("parallel", "arbitrary")),
        name="normed_matmul",
    )(x, gain.reshape(1, k), wp_blocks, w_blocks)


def _hgrn_consts():
    c, sub = HG_CHUNK, HG_SUB
    tri = np.tile(np.tril(np.ones((c, c), np.float32)), (1, 3))
    gsum = (np.arange(c * sub)[None, :] // sub == np.arange(c)[:, None]).astype(np.float32)
    return jnp.asarray(tri, BF16), jnp.asarray(gsum, BF16)


HG_PAR = 2


def _hgrn_kernel(q_ref, f_ref, i_ref, g_ref, lbl_ref, gain_ref, tri_ref, gsum_ref, o_ref, *scratch):
    c, sub, d = HG_CHUNK, HG_SUB, HG_DIM
    nsub = c // sub
    heads = range(HG_PAR)
    st_refs, p_refs, cl_refs, qs_refs, k_refs = (scratch[i * HG_PAR:(i + 1) * HG_PAR] for i in range(5))

    @pl.when(pl.program_id(2) == 0)
    def _():
        for h in heads:
            st_refs[h][...] = jnp.zeros_like(st_refs[h])

    l0 = lbl_ref[0:1, :]
    l1 = lbl_ref[1:2, :]
    lmax = jnp.maximum(l0, l1)
    e0 = jnp.exp(l0 - lmax)
    lb_all = e0 / (e0 + jnp.exp(l1 - lmax))
    srow = lax.broadcasted_iota(jnp.int32, (sub, d), 0)
    ones = jnp.ones((d, d), BF16)

    def rows_at(x, start):
        parts = ([jnp.zeros((start, d), F32)] if start else []) + [x]
        if start + x.shape[0] < c:
            parts.append(jnp.zeros((c - start - x.shape[0], d), F32))
        return jnp.concatenate(parts, axis=0)

    nchunks = q_ref.shape[1] // c
    cols = [slice(h * d, (h + 1) * d) for h in heads]

    def load(ci):
        rows = pl.ds(pl.multiple_of(ci * c, c), c)
        out = []
        for h in heads:
            lb = lb_all[:, cols[h]]
            f = lb + (1.0 - lb) * jax.nn.sigmoid(f_ref[0, rows, cols[h]])
            bcum = _dot(tri_ref[...], jnp.concatenate(_split3(jnp.log(f)), axis=0))
            out.append((q_ref[0, rows, cols[h]] * (d ** -0.5), 1.0 - f, i_ref[0, rows, cols[h]], bcum))
        return tuple(out)

    def chunk(ci, cur):
        nxt = load(jnp.minimum(ci + 1, nchunks - 1))
        rows = pl.ds(pl.multiple_of(ci * c, c), c)
        q, k, v, bcum = ([cur[h][i] for h in heads] for i in range(4))
        a_off, o_inter = [], []
        for h in heads:
            b = bcum[h]
            edge = [b[i * sub - 1:i * sub] for i in range(1, nsub + 1)]
            cl = b - jnp.concatenate([jnp.zeros((sub, d), F32)] + [jnp.broadcast_to(e, (sub, d)) for e in edge[:-1]],
                                     axis=0)
            cl_refs[h][...] = cl
            qs_refs[h][...] = q[h]
            k_refs[h][...] = k[h]
            qe = q[h] * jnp.exp(cl)
            qcat, kcat = [], []
            for i in range(1, nsub):
                qcat.append(rows_at(qe[i * sub:(i + 1) * sub], i * sub))
                kcat.append(rows_at(k[h][:i * sub] * jnp.exp(edge[i - 1] - b[:i * sub]), 0))
            a_off.append(_dot_nt(jnp.concatenate(qcat, axis=1).astype(BF16),
                                 jnp.concatenate(kcat, axis=1).astype(BF16)))
            st = st_refs[h][...]
            o_inter.append(_dot_nt((q[h] * jnp.exp(b)).astype(BF16), st.astype(BF16)))
            kd = k[h] * jnp.exp(edge[-1] - b)
            st_refs[h][...] = st * jnp.exp(edge[-1]) + _dot(v[h].T.astype(BF16), kd.astype(BF16))
        r2 = []
        for h in heads:
            for t in range(c):
                j0 = (t // sub) * sub
                dlt = cl_refs[h][t:t + 1, :] - cl_refs[h][j0:j0 + sub, :]
                e = jnp.where(srow + j0 <= t, jnp.exp(jnp.minimum(dlt, 0.0)), 0.0)
                p_refs[h][t * sub:(t + 1) * sub, :] = (qs_refs[h][t:t + 1, :] * k_refs[h][j0:j0 + sub, :] * e).astype(BF16)
            r2.append(_dot(p_refs[h][...], ones))
        o = []
        for h in heads:
            o.append(o_inter[h] + _dot(a_off[h].astype(BF16), v[h].astype(BF16)))
        for h in heads:
            x = r2[h].reshape(nsub, sub, sub, d) * v[h].reshape(nsub, 1, sub, d)
            o[h] = o[h] + _dot(gsum_ref[...], x.reshape(c * sub, d).astype(BF16))
        for h in heads:
            gate = jax.nn.silu(g_ref[0, rows, cols[h]])
            o_ref[0, rows, cols[h]] = (_rms(o[h], gain_ref[...]) * gate).astype(o_ref.dtype)
        return nxt

    lax.fori_loop(0, nchunks, chunk, load(0), unroll=2)


def _hgrn(proj3, col0, lb_logits, out_gain, tseq):
    bsz, seq, _ = proj3.shape
    d, c, sub = HG_DIM, HG_CHUNK, HG_SUB
    wst, gsum = _hgrn_consts()
    groups = HG_HEADS // HG_PAR
    width = HG_PAR * d
    assert col0 % width == 0

    def col(off):
        return pl.BlockSpec((1, tseq, width), lambda b, h, t: (b, t, col0 // width + off * groups + h))

    per_head = [pltpu.VMEM((d, d), F32), pltpu.VMEM((c * sub, d), BF16), pltpu.VMEM((c, d), F32),
                pltpu.VMEM((c, d), F32), pltpu.VMEM((c, d), F32)]
    return pl.pallas_call(
        _hgrn_kernel,
        grid=(bsz, groups, seq // tseq),
        in_specs=[
            col(0), col(1), col(2), col(3),
            pl.BlockSpec((2, width), lambda b, h, t: (0, h)),
            pl.BlockSpec((1, d), lambda b, h, t: (0, 0)),
            pl.BlockSpec(wst.shape, lambda b, h, t: (0, 0)),
            pl.BlockSpec(gsum.shape, lambda b, h, t: (0, 0)),
        ],
        out_specs=pl.BlockSpec((1, tseq, width), lambda b, h, t: (b, t, h)),
        out_shape=jax.ShapeDtypeStruct((bsz, seq, HG_QK), BF16),
        scratch_shapes=[s for s in per_head for _ in range(HG_PAR)],
        compiler_params=_params(("parallel", "parallel", "arbitrary")),
        name="hgrn2",
    )(proj3, proj3, proj3, proj3, lb_logits, out_gain.reshape(1, d), wst, gsum)


def _rope(x, cs, sn):
    lane = lax.broadcasted_iota(jnp.int32, x.shape, 1)
    partner = jnp.where(lane % NSA_DIM < NSA_DIM // 2, pltpu.roll(x, LANES - NSA_DIM // 2, 1),
                        pltpu.roll(x, NSA_DIM // 2, 1))
    return x * cs + partner * sn


def _nsa_prep_kernel(q_ref, ksl_ref, vsl_ref, kwn_ref, vwn_ref, cs_ref, sn_ref,
                     qrot_ref, kslo_ref, vslo_ref, kwno_ref, vwno_ref):
    cs = cs_ref[...]
    sn = sn_ref[...]
    for c in range(NSA_WIDTH // LANES):
        cols = slice(c * LANES, (c + 1) * LANES)
        qrot_ref[0, :, cols] = (_rope(q_ref[0, :, cols], cs, sn) * (NSA_DIM ** -0.5 * LOG2E)).astype(BF16)
    tseq = q_ref.shape[1]
    lane = lax.broadcasted_iota(jnp.int32, (tseq, LANES), 1)
    block = (pl.program_id(1) * tseq + lax.broadcasted_iota(jnp.int32, (tseq, LANES), 0)) // SLC_BLOCK
    block_onehot = jnp.where(lane - NSA_DIM == block, 1.0, 0.0)
    ones_rows = jnp.where(lax.broadcasted_iota(jnp.int32, (NSA_VPAD, tseq), 0) == 0, 1.0, 0.0).astype(BF16)
    for c in range(NSA_KV // LANES):
        cols = slice(c * LANES, (c + 1) * LANES)
        ks = _rope(ksl_ref[0, :, cols], cs, sn)
        kw = _rope(kwn_ref[0, :, cols], cs, sn).astype(BF16)
        vs = vsl_ref[0, :, cols].T.astype(BF16)
        vw = vwn_ref[0, :, cols].T.astype(BF16)
        for half in range(LANES // NSA_DIM):
            g = c * (LANES // NSA_DIM) + half
            hs = slice(half * NSA_DIM, (half + 1) * NSA_DIM)
            ks_g = ks if half == 0 else pltpu.roll(ks, NSA_DIM, 1)
            kslo_ref[0, g] = jnp.where(lane < NSA_DIM, ks_g, block_onehot).astype(BF16)
            kwno_ref[0, g] = kw[:, hs]
            vslo_ref[0, g, :NSA_DIM] = vs[hs, :]
            vslo_ref[0, g, NSA_DIM:] = ones_rows
            vwno_ref[0, g, :NSA_DIM] = vw[hs, :]
            vwno_ref[0, g, NSA_DIM:] = ones_rows


def _nsa_prep(prec3, rest3, kv_off, tseq):
    bsz, seq, _ = prec3.shape
    half = NSA_DIM // 2
    inv = 1.0 / (ROPE_THETA ** (jnp.arange(0, NSA_DIM, 2, dtype=F32) / NSA_DIM))
    ang = jnp.arange(seq, dtype=F32)[:, None] * inv[None, :]
    cs = jnp.tile(jnp.cos(ang), (1, LANES // half))
    sn = jnp.tile(jnp.concatenate([-jnp.sin(ang), jnp.sin(ang)], axis=1), (1, LANES // NSA_DIM))
    kvb = kv_off // NSA_KV

    def kv_in(i):
        return pl.BlockSpec((1, tseq, NSA_KV), lambda b, t: (b, t, kvb + i))

    assert NSA_DIM + seq // SLC_BLOCK <= LANES

    def k_out(width):
        return (pl.BlockSpec((1, NSA_GROUPS, tseq, width), lambda b, t: (b, 0, t, 0)),
                jax.ShapeDtypeStruct((bsz, NSA_GROUPS, seq, width), BF16))

    (ksl_out, ksl_shape), (kwn_out, kwn_shape) = k_out(LANES), k_out(NSA_DIM)
    v_out = pl.BlockSpec((1, NSA_GROUPS, NSA_DIM + NSA_VPAD, tseq), lambda b, t: (b, 0, 0, t))
    v_shape = jax.ShapeDtypeStruct((bsz, NSA_GROUPS, NSA_DIM + NSA_VPAD, seq), BF16)
    tab = pl.BlockSpec((tseq, LANES), lambda b, t: (t, 0))
    return pl.pallas_call(
        _nsa_prep_kernel,
        grid=(bsz, seq // tseq),
        in_specs=[pl.BlockSpec((1, tseq, NSA_WIDTH), lambda b, t: (b, t, 0)), kv_in(0), kv_in(1), kv_in(2), kv_in(3),
                  tab, tab],
        out_specs=[pl.BlockSpec((1, tseq, NSA_WIDTH), lambda b, t: (b, t, 0)), ksl_out, v_out, kwn_out, v_out],
        out_shape=[jax.ShapeDtypeStruct((bsz, seq, NSA_WIDTH), BF16), ksl_shape, v_shape, kwn_shape, v_shape],
        compiler_params=_params(("parallel", "parallel")),
        name="nsa_prep",
    )(prec3, rest3, rest3, rest3, rest3, cs, sn)


def _compress_kernel(u_ref, pos_ref, w1_ref, w2_ref, o_ref, *, precise):
    mm = _dot3 if precise else (lambda a, b: _dot(a.astype(BF16), b.astype(BF16)))
    u = u_ref[0]
    nu = u.shape[0]
    ya = mm(u + pos_ref[0:1, :], w1_ref[0])
    yb = mm(u + pos_ref[1:2, :], w1_ref[1])
    hid = ya + pltpu.roll(yb, nu - 1, 0)
    o_ref[0] = mm(jax.nn.gelu(hid), w2_ref[...])


def _compress(kv, pos, w1, w2, precise):
    bsz, seq, _ = kv.shape
    nu = seq // CMP_STRIDE
    width = CMP_STRIDE * NSA_DIM
    u = kv.reshape(bsz, nu, CMP_STRIDE, NSA_GROUPS, NSA_DIM).transpose(0, 3, 1, 2, 4).reshape(bsz * NSA_GROUPS, nu, width)
    return pl.pallas_call(
        functools.partial(_compress_kernel, precise=precise),
        grid=(bsz * NSA_GROUPS,),
        in_specs=[
            pl.BlockSpec((1, nu, width), lambda i: (i, 0, 0)),
            pl.BlockSpec((2, width), lambda i: (0, 0)),
            pl.BlockSpec((2, width, CMP_HIDDEN), lambda i: (0, 0, 0)),
            pl.BlockSpec((CMP_HIDDEN, NSA_DIM), lambda i: (0, 0)),
        ],
        out_specs=pl.BlockSpec((1, nu, NSA_DIM), lambda i: (i, 0, 0)),
        out_shape=jax.ShapeDtypeStruct((bsz * NSA_GROUPS, nu, NSA_DIM), F32),
        compiler_params=_params(("parallel",)),
        name="nsa_compress",
    )(u, pos.reshape(2, width), w1.reshape(2, width, CMP_HIDDEN), w2)


def _nsa_attn_kernel(qraw_ref, qrot_ref, kc_ref, vct_ref, ksl_ref, vslt_ref, kwn_ref, vwnt_ref, gate_ref, aggt_ref,
                     o_ref, sa_ref, sb_ref, *win_refs, topk, tk):
    tq = qraw_ref.shape[1]
    nu = kc_ref.shape[2]
    ns = aggt_ref.shape[0]
    rep, dk = NSA_REP, NSA_DIM
    qs = pl.program_id(2) * tq
    tpos = qs + lax.broadcasted_iota(jnp.int32, (1, tq), 1)

    qrt = (qraw_ref[0] * (dk ** -0.5 * LOG2E)).T
    kc_hi, kc_lo = _split2(kc_ref[0, 0])
    vct = vct_ref[0, 0].astype(BF16)
    crow = lax.broadcasted_iota(jnp.int32, (nu, tq), 0)
    m_c = (crow * CMP_STRIDE + CMP_BLOCK - 1 <= tpos) & (crow < nu - 1)
    q_hi, q_lo = _split2(jnp.concatenate([qrt[r * dk:(r + 1) * dk] for r in range(rep)], axis=1))
    s_all = _dot(jnp.concatenate([kc_hi, kc_hi, kc_lo], axis=1),
                 jnp.concatenate([q_hi, q_lo, q_hi], axis=0))

    qt = qrot_ref[0].astype(F32).T.astype(BF16)
    qt_all = jnp.concatenate([qt[r * dk:(r + 1) * dk] for r in range(rep)], axis=1)
    hi = (qs + tq) // tk

    def key_tile(ktc):
        return pl.ds(pl.multiple_of(ktc * tk, tk), tk)

    def scores(k_ref, ktc, dst_ref):
        dst_ref[...] = _dot(k_ref[0, 0, key_tile(ktc), :], qt_all)

    win_tiles = [hi - len(win_refs) + j for j in range(len(win_refs))]
    for kt, dst_ref in zip(win_tiles, win_refs):
        scores(kwn_ref, jnp.maximum(kt, 0), dst_ref)
    psum = jnp.zeros((nu, tq), F32)
    p_all = []
    has_block = tpos >= CMP_BLOCK - 1
    for r in range(rep):
        s = jnp.where(m_c, s_all[:, r * tq:(r + 1) * tq], NEG)
        e = jnp.exp2(s - jnp.max(s, axis=0, keepdims=True))
        p = e * jnp.where(has_block, 1.0 / jnp.sum(e, axis=0, keepdims=True), 0.0)
        psum = psum + p
        p_all.append(p.astype(BF16))
    o_c_all = _dot(vct, jnp.concatenate(p_all, axis=1))
    o_c = [o_c_all[:, r * tq:(r + 1) * tq] for r in range(rep)]

    p_hi, p_lo = _split2(psum)
    imp = _dot(aggt_ref[...], p_hi) + _dot(aggt_ref[...], p_lo)
    jrow = lax.broadcasted_iota(jnp.int32, (ns, tq), 0)
    dj = jnp.right_shift(tpos, SLC_BLOCK.bit_length() - 1) - jrow
    forced = (jrow == 0) | ((dj >= 0) & (dj < SLC_LOCAL))
    imp = jnp.where(forced, BIG, jnp.where(jrow * SLC_BLOCK <= tpos, imp, -BIG))
    sub8 = lax.broadcasted_iota(jnp.int32, (8, tq), 0)
    chunks = [imp[c * 8:(c + 1) * 8] for c in range(ns // 8)]
    ranks = [jnp.zeros((8, tq), F32) for _ in range(ns // 8)]
    for jp in range(ns):
        row = chunks[jp // 8][jp % 8:jp % 8 + 1]
        for c in range(ns // 8):
            if c < jp // 8:
                ahead = jnp.where(row > chunks[c], 1.0, 0.0)
            elif c > jp // 8:
                ahead = jnp.where(row >= chunks[c], 1.0, 0.0)
            else:
                tie = jnp.where(sub8 > jp % 8, 1.0, 0.0)
                ahead = jnp.where(row > chunks[c], 1.0, jnp.where(row == chunks[c], tie, 0.0))
            ranks[c] = ranks[c] + ahead
    selt = [jnp.where(ranks[c] < topk, 0.0, NEG) for c in range(ns // 8)]

    def consume(vt_ref, ktc, src_ref, carry, mask=None):
        vt = vt_ref[0, 0, :, key_tile(ktc)]
        out = []
        for r in range(rep):
            m_old, acc = carry[r]
            s = src_ref[:, r * tq:(r + 1) * tq]
            if mask is not None:
                s = jnp.where(mask, s, NEG)
            m_new = jnp.maximum(m_old, jnp.max(s, axis=0, keepdims=True))
            alpha = jnp.exp2(m_old - m_new)
            p = jnp.exp2(s - m_new).astype(BF16)
            out.append((m_new, acc * alpha + _dot(vt, p)))
        return tuple(out)

    def normalised(carry):
        return [acc[:dk] * (1.0 / acc[dk:dk + 1]) for _, acc in carry]

    init = tuple((jnp.full((1, tq), NEG, F32), jnp.zeros((dk + NSA_VPAD, tq), F32)) for _ in range(rep))
    krow = lax.broadcasted_iota(jnp.int32, (tk, tq), 0)
    unseen = 1 << 30

    carry = init
    for kt, src_ref in zip(win_tiles, win_refs):
        ktc = jnp.maximum(kt, 0)
        dlt = tpos - (jnp.where(kt >= 0, ktc * tk, unseen) + krow)
        carry = consume(vwnt_ref, ktc, src_ref, carry, mask=pltpu.bitcast(dlt, jnp.uint32) < jnp.uint32(WIN))
    o_w = normalised(carry)

    selb = jnp.concatenate([jnp.concatenate(selt, axis=0)] * rep, axis=1).astype(BF16)
    pad = jnp.zeros((LANES - dk - ns, rep * tq), BF16)
    qt_sel = jnp.concatenate([qt_all, selb] + ([pad] if LANES > dk + ns else []), axis=0)

    def sel_scores(ktc, dst_ref):
        dst_ref[...] = _dot(ksl_ref[0, 0, key_tile(ktc), :], qt_sel)

    def pair(i, carry):
        kt = 2 * i
        sel_scores(kt + 1, sb_ref)
        carry = consume(vslt_ref, kt, sa_ref, carry)
        sel_scores(kt + 2, sa_ref)
        return consume(vslt_ref, kt + 1, sb_ref, carry)

    past = hi - 1
    sel_scores(0, sa_ref)
    carry = lax.fori_loop(0, past // 2, pair, init)
    carry = lax.cond(past % 2 == 1, lambda c: consume(vslt_ref, past - 1, sa_ref, c), lambda c: c, carry)
    sel_scores(past, sb_ref)
    carry = consume(vslt_ref, past, sb_ref, carry, mask=past * tk + krow <= tpos)
    o_s = normalised(carry)

    gate = jax.nn.sigmoid(gate_ref[0, 0])
    o_t = [gate[3 * r:3 * r + 1] * o_c[r] + gate[3 * r + 1:3 * r + 2] * o_s[r] + gate[3 * r + 2:3 * r + 3] * o_w[r]
           for r in range(rep)]
    o_ref[0] = jnp.concatenate(o_t, axis=0).T.astype(o_ref.dtype)


def _nsa_attn(prec3, qrot, kc, vct, ksl, vslt, kwn, vwnt, gates_t, tq, tk):
    bsz, seq, _ = qrot.shape
    nu = seq // CMP_STRIDE
    ns = seq // SLC_BLOCK
    ci = np.arange(nu)[None, :]
    sj = np.arange(ns)[:, None]
    overlap = (ci * CMP_STRIDE < (sj + 1) * SLC_BLOCK) & (ci * CMP_STRIDE + CMP_BLOCK > sj * SLC_BLOCK) & (ci < nu - 1)
    aggt = jnp.asarray(overlap, BF16)
    gw = NSA_REP * NSA_DIM
    assert tq % tk == 0 and seq % tq == 0
    win_tiles = -(-(WIN - 1) // tk) + tq // tk

    def q_spec():
        return pl.BlockSpec((1, tq, gw), lambda b, g, t: (b, t, g))

    def per_group(rows, cols):
        return pl.BlockSpec((1, 1, rows, cols), lambda b, g, t: (b, g, 0, 0))

    return pl.pallas_call(
        functools.partial(_nsa_attn_kernel, topk=min(SLC_TOPK, ns), tk=tk),
        grid=(bsz, NSA_GROUPS, seq // tq),
        in_specs=[q_spec(), q_spec(), per_group(nu, NSA_DIM), per_group(NSA_DIM, nu),
                  per_group(seq, LANES), per_group(NSA_DIM + NSA_VPAD, seq),
                  per_group(seq, NSA_DIM), per_group(NSA_DIM + NSA_VPAD, seq),
                  pl.BlockSpec((1, 1, 3 * NSA_REP, tq), lambda b, g, t: (b, g, 0, t)),
                  pl.BlockSpec((ns, nu), lambda b, g, t: (0, 0))],
        out_specs=q_spec(),
        out_shape=jax.ShapeDtypeStruct((bsz, seq, NSA_WIDTH), BF16),
        scratch_shapes=[pltpu.VMEM((tk, NSA_REP * tq), F32)] * (2 + win_tiles),
        compiler_params=_params(("parallel", "parallel", "arbitrary")),
        name="nsa_attention",
    )(prec3, qrot, kc, vct, ksl, vslt, kwn, vwnt, gates_t, aggt)


def _nsa(prec3, rest3, kv_off, gate_off, cmp_k, cmp_v):
    bsz, seq, _ = prec3.shape
    nu = seq // CMP_STRIDE
    qrot, ksl, vslt, kwn, vwnt = _nsa_prep(prec3, rest3, kv_off + NSA_KV, min(seq, 512))
    kc = _compress(prec3[:, :, NSA_WIDTH:NSA_WIDTH + NSA_KV], *cmp_k, precise=True)
    vc = _compress(rest3[:, :, kv_off:kv_off + NSA_KV], *cmp_v, precise=False)
    kc = kc.reshape(bsz, NSA_GROUPS, nu, NSA_DIM)
    vct = vc.reshape(bsz, NSA_GROUPS, nu, NSA_DIM).transpose(0, 1, 3, 2)
    gates_t = rest3[:, :, gate_off:gate_off + 3 * NSA_HEADS].reshape(bsz, seq, NSA_GROUPS, 3 * NSA_REP).transpose(0, 2, 3, 1)
    return _nsa_attn(prec3, qrot, kc, vct, ksl, vslt, kwn, vwnt, gates_t, 256, 256)


def _out_proj_kernel(yh_ref, yn_ref, g_ref, w_ref, x_ref, o_ref, y_ref):
    @pl.when(pl.program_id(1) == 0)
    def _():
        wh = yh_ref.shape[1]
        y_ref[:, :wh] = yh_ref[...]
        y_ref[:, wh:] = _rms(yn_ref[...].astype(F32), g_ref[...]).astype(BF16)

    o_ref[...] = x_ref[...] + _dot(y_ref[...], w_ref[...])


def _out_proj(y_hg, y_nsa, nsa_gain, w_out, x, tm, tn):
    m, dm = x.shape
    wh, wn = y_hg.shape[1], y_nsa.shape[1]
    return pl.pallas_call(
        _out_proj_kernel,
        grid=(m // tm, dm // tn),
        in_specs=[
            pl.BlockSpec((tm, wh), lambda i, j: (i, 0)),
            pl.BlockSpec((tm, wn), lambda i, j: (i, 0)),
            pl.BlockSpec((1, wn), lambda i, j: (0, 0)),
            pl.BlockSpec((None, wh + wn, tn), lambda i, j: (j, 0, 0)),
            pl.BlockSpec((tm, tn), lambda i, j: (i, j)),
        ],
        out_specs=pl.BlockSpec((tm, tn), lambda i, j: (i, j)),
        out_shape=jax.ShapeDtypeStruct((m, dm), F32),
        scratch_shapes=[pltpu.VMEM((tm, wh + wn), BF16)],
        compiler_params=_params(("parallel", "arbitrary")),
        name="out_proj",
    )(y_hg, y_nsa, nsa_gain.reshape(1, wn), _column_blocks(w_out.astype(BF16), tn), x)


MOE_ROWS = 256


def _pack_bf16_pairs(hi):
    n = hi.shape[1] // 2
    bits = pltpu.bitcast(hi.astype(F32), jnp.uint32)
    return jnp.right_shift(bits[:, :n], jnp.uint32(16)) | (bits[:, n:] & jnp.uint32(0xFFFF0000))


def _unpack_f32_pairs(words):
    lo = pltpu.bitcast(jnp.left_shift(words, jnp.uint32(16)), F32)
    hi = pltpu.bitcast(words & jnp.uint32(0xFFFF0000), F32)
    return lo, hi


def _unpack_bf16_pairs(words):
    lo, hi = _unpack_f32_pairs(words)
    return lo.astype(BF16), hi.astype(BF16)


def _router_kernel(x_ref, g_ref, w_ref, b_ref, tri_ref, xn_ref, info_ref, infot_ref, cnt_ref, carry_ref):
    @pl.when(pl.program_id(0) == 0)
    def _():
        carry_ref[...] = jnp.zeros_like(carry_ref)

    xn = _rms(x_ref[...], g_ref[...])
    hi, lo = _split2(xn)
    xn_ref[...] = _pack_bf16_pairs(hi)
    logits = _dot(hi, w_ref[0]) + (_dot(hi, w_ref[1]) + _dot(lo, w_ref[0])) + b_ref[...]
    lane = lax.broadcasted_iota(jnp.int32, logits.shape, 1).astype(F32)
    none = float(LANES)

    def first_max(mask):
        top = jnp.max(jnp.where(mask, logits, -jnp.inf), axis=-1, keepdims=True)
        return top, jnp.min(jnp.where(mask & (logits == top), lane, none), axis=-1, keepdims=True)

    is_g = lane < MOE_GROUPS
    gmax, gsel = first_max(is_g)
    gw = 1.0 / jnp.sum(jnp.where(is_g, jnp.exp(logits - gmax), 0.0), axis=-1, keepdims=True)
    lo_lane = MOE_GROUPS + gsel * MOE_EPG
    in_grp = (lane >= lo_lane) & (lane < lo_lane + MOE_EPG)
    v1, i1 = first_max(in_grp)
    v2, i2 = first_max(in_grp & (lane != i1))
    e = jnp.exp(v2 - v1)
    w1 = gw / (1.0 + e)
    w2 = gw * e / (1.0 + e)
    e1 = i1 - MOE_GROUPS
    e2 = i2 - MOE_GROUPS
    onehot = jnp.where((lane == e1) | (lane == e2), 1.0, 0.0)
    before = _dot(tri_ref[...], onehot.astype(BF16)) + carry_ref[...]
    r1 = jnp.sum(jnp.where(lane == e1, before, 0.0), axis=-1, keepdims=True)
    r2 = jnp.sum(jnp.where(lane == e2, before, 0.0), axis=-1, keepdims=True)
    carry_ref[...] = carry_ref[...] + jnp.sum(onehot, axis=0, keepdims=True)
    cnt_ref[...] = carry_ref[...]
    info = jnp.zeros_like(logits)
    for idx, val in enumerate((e1, e2, w1, w2, r1, r2)):
        info = jnp.where(lane == idx, val, info)
    info_ref[...] = info
    infot_ref[...] = info.T[:8]


def _router(x2, gain, w_group, b_group, w_router, b_router, tm):
    m, dm = x2.shape
    wcat = jnp.pad(jnp.concatenate([w_group, w_router], axis=1), ((0, 0), (0, LANES - MOE_GROUPS - N_EXPERTS)))
    bcat = jnp.pad(jnp.concatenate([b_group, b_router]), (0, LANES - MOE_GROUPS - N_EXPERTS)).reshape(1, LANES)
    tri = jnp.asarray(np.tril(np.ones((tm, tm), np.float32), -1), BF16)
    return pl.pallas_call(
        _router_kernel,
        grid=(m // tm,),
        in_specs=[
            pl.BlockSpec((tm, dm), lambda i: (i, 0)),
            pl.BlockSpec((1, dm), lambda i: (0, 0)),
            pl.BlockSpec((2, dm, LANES), lambda i: (0, 0, 0)),
            pl.BlockSpec((1, LANES), lambda i: (0, 0)),
            pl.BlockSpec((tm, tm), lambda i: (0, 0)),
        ],
        out_specs=[
            pl.BlockSpec((tm, dm // 2), lambda i: (i, 0)),
            pl.BlockSpec((tm, LANES), lambda i: (i, 0)),
            pl.BlockSpec((8, tm), lambda i: (0, i)),
            pl.BlockSpec((1, LANES), lambda i: (0, 0)),
        ],
        out_shape=[
            jax.ShapeDtypeStruct((m, dm // 2), jnp.uint32),
            jax.ShapeDtypeStruct((m, LANES), F32),
            jax.ShapeDtypeStruct((8, m), F32),
            jax.ShapeDtypeStruct((1, LANES), F32),
        ],
        scratch_shapes=[pltpu.VMEM((1, LANES), F32)],
        compiler_params=_params(("arbitrary",)),
        name="moe_router",
    )(x2, gain.reshape(1, dm), jnp.stack(_split2(wcat)), bcat, tri)


def _expert_kernel(be_ref, nused_ref, rowtok_ref, xn_ref, wg_ref, wu_ref, wd_ref, o_ref,
                   wgb_ref, wub_ref, wdb_ref, buf_a, buf_b, sem_a, sem_b):
    i = pl.program_id(0)
    nused = nused_ref[0]
    rows = o_ref.shape[0]

    def row_copy(buf_ref, sem, r, tok):
        return pltpu.make_async_copy(xn_ref.at[pl.ds(tok, 1)], buf_ref.at[pl.ds(r, 1)], sem)

    def wait(buf_ref, sem):
        pltpu.make_async_copy(xn_ref.at[pl.ds(0, rows)], buf_ref, sem).wait()

    @pl.when(i == 0)
    def _():
        def first(r, c):
            row_copy(buf_a, sem_a, r, rowtok_ref[r]).start()
            return c

        lax.fori_loop(0, rows, first, 0, unroll=8)

    @pl.when((i == 0) | (be_ref[i] != be_ref[jnp.maximum(i - 1, 0)]))
    def _():
        wgb_ref[...] = wg_ref[0].astype(BF16)
        wub_ref[...] = wu_ref[0].astype(BF16)
        wdb_ref[...] = wd_ref[0].astype(BF16)

    def used_block(cur_ref, cur_sem, nxt_ref, nxt_sem):
        wait(cur_ref, cur_sem)
        nxt0 = jnp.minimum(i + 1, nused - 1) * rows
        for r in range(rows):
            row_copy(nxt_ref, nxt_sem, r, rowtok_ref[nxt0 + r]).start()
        half = wgb_ref.shape[0] // 2
        x_lo, x_hi = _unpack_bf16_pairs(cur_ref[...])

        def up(w_ref):
            return _dot(x_lo, w_ref[:half]) + _dot(x_hi, w_ref[half:])

        hid = (jax.nn.silu(up(wgb_ref)) * up(wub_ref)).astype(BF16)
        for g in range(o_ref.shape[1] // LANES):
            y = _dot(hid, wdb_ref[:, g * 2 * LANES:(g + 1) * 2 * LANES])
            o_ref[:, g * LANES:(g + 1) * LANES] = _pack_bf16_pairs(y.astype(BF16))

        @pl.when(i == nused - 1)
        def _():
            wait(nxt_ref, nxt_sem)

    @pl.when((i < nused) & (i % 2 == 0))
    def _():
        used_block(buf_a, sem_a, buf_b, sem_b)

    @pl.when((i < nused) & (i % 2 == 1))
    def _():
        used_block(buf_b, sem_b, buf_a, sem_a)

    @pl.when(i >= nused)
    def _():
        o_ref[...] = jnp.zeros_like(o_ref)


def _experts(xn, row_tok, block_e, nused, w_gate, w_up, w_down):
    _, words = xn.shape
    rows = row_tok.shape[0]
    _, dm, ff = w_gate.shape
    gather_buf = pltpu.VMEM((MOE_ROWS, words), jnp.uint32)
    return pl.pallas_call(
        _expert_kernel,
        grid_spec=pltpu.PrefetchScalarGridSpec(
            num_scalar_prefetch=3,
            grid=(rows // MOE_ROWS,),
            in_specs=[
                pl.BlockSpec(memory_space=pl.ANY),
                pl.BlockSpec((1, dm, ff), lambda i, be, *_: (be[i], 0, 0)),
                pl.BlockSpec((1, dm, ff), lambda i, be, *_: (be[i], 0, 0)),
                pl.BlockSpec((1, ff, dm), lambda i, be, *_: (be[i], 0, 0)),
            ],
            out_specs=pl.BlockSpec((MOE_ROWS, words), lambda i, *_: (i, 0)),
            scratch_shapes=[pltpu.VMEM((dm, ff), BF16), pltpu.VMEM((dm, ff), BF16), pltpu.VMEM((ff, dm), BF16),
                            gather_buf, gather_buf, pltpu.SemaphoreType.DMA(()), pltpu.SemaphoreType.DMA(())],
        ),
        out_shape=jax.ShapeDtypeStruct((rows, words), jnp.uint32),
        compiler_params=_params(("arbitrary",)),
        name="moe_experts",
    )(block_e, nused, row_tok, xn, w_gate, w_up, w_down)


def _combine_kernel(d1_ref, d2_ref, x_ref, info_ref, g_ref, ys_ref, o_ref, buf_a, buf_b, sem_a, sem_b):
    ts = buf_a.shape[1]
    dm = x_ref.shape[1]
    step, nsteps = pl.program_id(0), pl.num_programs(0)
    base = step * 2 * ts

    def row_copy(buf_ref, sem, r, slot, src):
        return pltpu.make_async_copy(ys_ref.at[pl.ds(src, 1)], buf_ref.at[slot, pl.ds(r, 1)], sem)

    def issue(buf_ref, sem, tok0):
        for r in range(ts):
            row_copy(buf_ref, sem, r, 0, d1_ref[tok0 + r]).start()
            row_copy(buf_ref, sem, r, 1, d2_ref[tok0 + r]).start()

    def wait(buf_ref, sem):
        for slot in range(2):
            pltpu.make_async_copy(ys_ref.at[pl.ds(0, ts)], buf_ref.at[slot], sem).wait()

    def finish(buf_ref, rows):
        info = info_ref[rows, :]
        w1, w2 = info[:, 2:3], info[:, 3:4]
        ssq = jnp.zeros((ts, 1), F32)
        for g in range(dm // (2 * LANES)):
            a = _unpack_f32_pairs(buf_ref[0, :, g * LANES:(g + 1) * LANES])
            b = _unpack_f32_pairs(buf_ref[1, :, g * LANES:(g + 1) * LANES])
            for half in range(2):
                cols = slice((2 * g + half) * LANES, (2 * g + half + 1) * LANES)
                y = x_ref[rows, cols] + (w1 * a[half] + w2 * b[half])
                ssq = ssq + jnp.sum(y * y, axis=-1, keepdims=True)
                o_ref[rows, cols] = y
        o_ref[rows, :] = o_ref[rows, :] * lax.rsqrt(ssq / dm + EPS) * g_ref[...]

    @pl.when(step == 0)
    def _():
        issue(buf_a, sem_a, base)

    wait(buf_a, sem_a)
    issue(buf_b, sem_b, base + ts)
    finish(buf_a, slice(0, ts))
    wait(buf_b, sem_b)
    issue(buf_a, sem_a, jnp.where(step + 1 < nsteps, base + 2 * ts, base))
    finish(buf_b, slice(ts, 2 * ts))

    @pl.when(step + 1 == nsteps)
    def _():
        wait(buf_a, sem_a)


def _combine(x2, info, gain, ys, d1, d2, ts):
    m, dm = x2.shape
    tm = 2 * ts
    gather_buf = pltpu.VMEM((2, ts, dm // 2), jnp.uint32)
    return pl.pallas_call(
        _combine_kernel,
        grid_spec=pltpu.PrefetchScalarGridSpec(
            num_scalar_prefetch=2,
            grid=(m // tm,),
            in_specs=[
                pl.BlockSpec((tm, dm), lambda i, *_: (i, 0)),
                pl.BlockSpec((tm, LANES), lambda i, *_: (i, 0)),
                pl.BlockSpec((1, dm), lambda i, *_: (0, 0)),
                pl.BlockSpec(memory_space=pl.ANY),
            ],
            out_specs=pl.BlockSpec((tm, dm), lambda i, *_: (i, 0)),
            scratch_shapes=[gather_buf, gather_buf, pltpu.SemaphoreType.DMA(()), pltpu.SemaphoreType.DMA(())],
        ),
        out_shape=jax.ShapeDtypeStruct((m, dm), F32),
        compiler_params=_params(("arbitrary",)),
        name="moe_combine",
    )(d1, d2, x2, info, gain.reshape(1, dm), ys)


def _moe_and_final_norm(x2, ffn_gain, w_group, b_group, w_router, b_router, w_gate, w_up, w_down, final_gain, tm):
    m, _ = x2.shape
    xn, info, infot, cnt = _router(x2, ffn_gain, w_group, b_group, w_router, b_router, tm)
    counts = cnt[0, :N_EXPERTS].astype(jnp.int32)
    padded = (counts + MOE_ROWS - 1) // MOE_ROWS * MOE_ROWS
    pend = jnp.cumsum(padded)
    pstart = pend - padded
    fields = infot.astype(jnp.int32)
    d1 = pstart[fields[0]] + fields[4]
    d2 = pstart[fields[1]] + fields[5]
    nblocks = 2 * m // MOE_ROWS + N_EXPERTS
    first_row = jnp.arange(nblocks, dtype=jnp.int32) * MOE_ROWS
    block_e = jnp.minimum(jnp.sum(pend[None, :] <= first_row[:, None], axis=1), N_EXPERTS - 1).astype(jnp.int32)
    nused = (pend[-1:] // MOE_ROWS).astype(jnp.int32)
    tok = jnp.arange(m, dtype=jnp.int32)
    row_tok = jnp.zeros((nblocks * MOE_ROWS,), jnp.int32).at[jnp.concatenate([d1, d2])].set(
        jnp.concatenate([tok, tok]), unique_indices=True)
    ys = _experts(xn, row_tok, block_e, nused, w_gate, w_up, w_down)
    return _combine(x2, info, final_gain, ys, d1, d2, tm)


def kernel(x, attn_norm, w_in, hg_lb_logits, hg_out_norm, cmp_pos_k, cmp_w1_k, cmp_w2_k, cmp_pos_v, cmp_w1_v,
           cmp_w2_v, nsa_out_norm, w_out, ffn_norm, moe_w_group, moe_b_group, moe_w_router, moe_b_router,
           moe_w_gate, moe_w_up, moe_w_down, final_norm):
    bsz, seq, dm = x.shape
    xt = x.reshape(bsz * seq, dm)
    w = w_in[0]
    p0, p1 = 4 * HG_QK, 4 * HG_QK + NSA_WIDTH + NSA_KV
    tn = 256
    assert p0 % tn == 0 and p1 % tn == 0
    w_all = _column_blocks(jnp.pad(w.astype(BF16), ((0, 0), (0, (-w.shape[1]) % tn))), tn)
    w_prec = _column_blocks(jnp.stack(_split2(w[:, p0:p1])), tn)
    proj3 = _normed_matmul(xt, attn_norm[0], w_prec, w_all, p0 // tn, 1024).reshape(bsz, seq, -1)
    hg0 = p1 - p0
    y_hg = _hgrn(proj3, hg0, hg_lb_logits, hg_out_norm[0], 1024)
    y_nsa = _nsa(proj3, proj3, hg0 + 4 * HG_QK, hg0 + 4 * HG_QK + 5 * NSA_KV,
                 (cmp_pos_k[0], cmp_w1_k[0], cmp_w2_k[0]), (cmp_pos_v[0], cmp_w1_v[0], cmp_w2_v[0]))
    x2 = _out_proj(y_hg.reshape(bsz * seq, -1), y_nsa.reshape(bsz * seq, -1), nsa_out_norm[0], w_out[0], xt, 1024, 512)
    out = _moe_and_final_norm(x2, ffn_norm[0], moe_w_group[0], moe_b_group[0], moe_w_router[0], moe_b_router[0],
                              moe_w_gate[0], moe_w_up[0], moe_w_down[0], final_norm, 256)
    return out.reshape(bsz, seq, dm)
```

```python
import functools

import jax
import jax.numpy as jnp
import numpy as np
from jax import lax
from jax.experimental import pallas as pl
from jax.experimental.pallas import tpu as pltpu

F32 = jnp.float32
BF16 = jnp.bfloat16

EPS = 1e-6
ROPE_THETA = 10000.0
NEG = -1e30
BIG = 1e9
LOG2E = 1.4426950408889634

HG_HEADS = 8
HG_DIM = 128
HG_QK = HG_HEADS * HG_DIM
HG_CHUNK = 64
HG_SUB = 8

NSA_HEADS = 16
NSA_GROUPS = 4
NSA_REP = NSA_HEADS // NSA_GROUPS
NSA_DIM = 64
NSA_WIDTH = NSA_HEADS * NSA_DIM
NSA_KV = NSA_GROUPS * NSA_DIM
NSA_VPAD = 16
CMP_BLOCK = 32
CMP_STRIDE = 16
CMP_HIDDEN = 256
SLC_BLOCK = 64
SLC_TOPK = 16
SLC_LOCAL = 2
WIN = 512

MOE_GROUPS = 4
MOE_EPG = 8
N_EXPERTS = MOE_GROUPS * MOE_EPG
EXPERT_FF = 512

LANES = 128
VMEM_LIMIT = 56 * 1024 * 1024


def _params(semantics, **kw):
    return pltpu.CompilerParams(dimension_semantics=semantics, vmem_limit_bytes=VMEM_LIMIT, **kw)


def _split2(a):
    hi = a.astype(BF16)
    return hi, (a - hi.astype(F32)).astype(BF16)


def _split3(a):
    hi = a.astype(BF16)
    r = a - hi.astype(F32)
    mid = r.astype(BF16)
    return hi, mid, (r - mid.astype(F32)).astype(BF16)


def _dot(a, b):
    return jnp.dot(a, b, preferred_element_type=F32)


def _dot_nt(a, b):
    return lax.dot_general(a, b, (((1,), (1,)), ((), ())), preferred_element_type=F32)


def _dot3(a, b):
    a_hi, a_lo = _split2(a)
    b_hi, b_lo = _split2(b)
    return _dot(a_hi, b_hi) + (_dot(a_hi, b_lo) + _dot(a_lo, b_hi))


def _dot3_nt(a, b):
    a_hi, a_lo = _split2(a)
    b_hi, b_lo = _split2(b)
    return _dot_nt(a_hi, b_hi) + (_dot_nt(a_hi, b_lo) + _dot_nt(a_lo, b_hi))


def _rms(x, gain):
    return x * lax.rsqrt(jnp.mean(x * x, axis=-1, keepdims=True) + EPS) * gain


def _normed_matmul_kernel(x_ref, g_ref, wp_ref, w_ref, o_ref, h_ref, *, nprec):
    j = pl.program_id(1)

    @pl.when(j == 0)
    def _():
        y = _rms(x_ref[...], g_ref[...])
        hi = y.astype(BF16)
        h_ref[0] = hi
        h_ref[1] = (y - hi.astype(F32)).astype(BF16)

    @pl.when(j < nprec)
    def _():
        o_ref[...] = _dot(h_ref[0], wp_ref[0]) + (_dot(h_ref[0], wp_ref[1]) + _dot(h_ref[1], wp_ref[0]))

    @pl.when(j >= nprec)
    def _():
        o_ref[...] = _dot(h_ref[0], w_ref[...])


def _column_blocks(w, tn):
    *lead, k, n = w.shape
    return jnp.moveaxis(w.reshape(*lead, k, n // tn, tn), -2, -3)


def _normed_matmul(x, gain, wp_blocks, w_blocks, first, tm):
    m, k = x.shape
    nb, _, tn = w_blocks.shape
    nprec = wp_blocks.shape[1]

    def single_pass_block(j):
        r = jnp.maximum(j - nprec, 0)
        return jnp.where(r < first, r, r + nprec)

    return pl.pallas_call(
        functools.partial(_normed_matmul_kernel, nprec=nprec),
        grid=(m // tm, nb),
        in_specs=[
            pl.BlockSpec((tm, k), lambda i, j: (i, 0)),
            pl.BlockSpec((1, k), lambda i, j: (0, 0)),
            pl.BlockSpec((2, None, k, tn), lambda i, j: (0, jnp.minimum(j, nprec - 1), 0, 0)),
            pl.BlockSpec((None, k, tn), lambda i, j: (single_pass_block(j), 0, 0)),
        ],
        out_specs=pl.BlockSpec((tm, tn), lambda i, j: (i, j)),
        out_shape=jax.ShapeDtypeStruct((m, nb * tn), F32),
        scratch_shapes=[pltpu.VMEM((2, tm, k), BF16)],
        compiler_params=_params(("parallel", "arbitrary")),
        name="normed_matmul",
    )(x, gain.reshape(1, k), wp_blocks, w_blocks)


def _hgrn_consts():
    c, sub = HG_CHUNK, HG_SUB
    tri = np.tile(np.tril(np.ones((c, c), np.float32)), (1, 3))
    gsum = (np.arange(c * sub)[None, :] // sub == np.arange(c)[:, None]).astype(np.float32)
    return jnp.asarray(tri, BF16), jnp.asarray(gsum, BF16)


HG_PAR = 2


def _hgrn_kernel(q_ref, f_ref, i_ref, g_ref, lbl_ref, gain_ref, tri_ref, gsum_ref, o_ref, *scratch):
    c, sub, d = HG_CHUNK, HG_SUB, HG_DIM
    nsub = c // sub
    heads = range(HG_PAR)
    st_refs, p_refs, cl_refs, qs_refs, k_refs = (scratch[i * HG_PAR:(i + 1) * HG_PAR] for i in range(5))

    @pl.when(pl.program_id(2) == 0)
    def _():
        for h in heads:
            st_refs[h][...] = jnp.zeros_like(st_refs[h])

    l0 = lbl_ref[0:1, :]
    l1 = lbl_ref[1:2, :]
    lmax = jnp.maximum(l0, l1)
    e0 = jnp.exp(l0 - lmax)
    lb_all = e0 / (e0 + jnp.exp(l1 - lmax))
    srow = lax.broadcasted_iota(jnp.int32, (sub, d), 0)
    ones = jnp.ones((d, d), BF16)

    def rows_at(x, start):
        parts = ([jnp.zeros((start, d), F32)] if start else []) + [x]
        if start + x.shape[0] < c:
            parts.append(jnp.zeros((c - start - x.shape[0], d), F32))
        return jnp.concatenate(parts, axis=0)

    nchunks = q_ref.shape[1] // c
    cols = [slice(h * d, (h + 1) * d) for h in heads]

    def load(ci):
        rows = pl.ds(pl.multiple_of(ci * c, c), c)
        out = []
        for h in heads:
            lb = lb_all[:, cols[h]]
            f = lb + (1.0 - lb) * jax.nn.sigmoid(f_ref[0, rows, cols[h]])
            bcum = _dot(tri_ref[...], jnp.concatenate(_split3(jnp.log(f)), axis=0))
            out.append((q_ref[0, rows, cols[h]] * (d ** -0.5), 1.0 - f, i_ref[0, rows, cols[h]], bcum))
        return tuple(out)

    def chunk(ci, cur):
        nxt = load(jnp.minimum(ci + 1, nchunks - 1))
        rows = pl.ds(pl.multiple_of(ci * c, c), c)
        q, k, v, bcum = ([cur[h][i] for h in heads] for i in range(4))
        a_off, o_inter = [], []
        for h in heads:
            b = bcum[h]
            edge = [b[i * sub - 1:i * sub] for i in range(1, nsub + 1)]
            cl = b - jnp.concatenate([jnp.zeros((sub, d), F32)] + [jnp.broadcast_to(e, (sub, d)) for e in edge[:-1]],
                                     axis=0)
            cl_refs[h][...] = cl * LOG2E
            qs_refs[h][...] = q[h]
            k_refs[h][...] = k[h]
            qe = q[h] * jnp.exp(cl)
            qcat, kcat = [], []
            for i in range(1, nsub):
                qcat.append(rows_at(qe[i * sub:(i + 1) * sub], i * sub))
                kcat.append(rows_at(k[h][:i * sub] * jnp.exp(edge[i - 1] - b[:i * sub]), 0))
            a_off.append(_dot_nt(jnp.concatenate(qcat, axis=1).astype(BF16),
                                 jnp.concatenate(kcat, axis=1).astype(BF16)))
            st = st_refs[h][...]
            o_inter.append(_dot_nt((q[h] * jnp.exp(b)).astype(BF16), st.astype(BF16)))
            kd = k[h] * jnp.exp(edge[-1] - b)
            st_refs[h][...] = st * jnp.exp(edge[-1]) + _dot(v[h].T.astype(BF16), kd.astype(BF16))
        r2 = []
        for h in heads:
            for t in range(c):
                j0 = (t // sub) * sub
                dlt = cl_refs[h][t:t + 1, :] - cl_refs[h][j0:j0 + sub, :]
                e = jnp.exp2(jnp.where(srow <= t - j0, dlt, NEG))
                p_refs[h][t * sub:(t + 1) * sub, :] = (qs_refs[h][t:t + 1, :] * k_refs[h][j0:j0 + sub, :] * e).astype(BF16)
            r2.append(_dot(p_refs[h][...], ones))
        o = []
        for h in heads:
            o.append(o_inter[h] + _dot(a_off[h].astype(BF16), v[h].astype(BF16)))
        for h in heads:
            x = r2[h].reshape(nsub, sub, sub, d) * v[h].reshape(nsub, 1, sub, d)
            o[h] = o[h] + _dot(gsum_ref[...], x.reshape(c * sub, d).astype(BF16))
        for h in heads:
            gate = jax.nn.silu(g_ref[0, rows, cols[h]])
            o_ref[0, rows, cols[h]] = (_rms(o[h], gain_ref[...]) * gate).astype(o_ref.dtype)
        return nxt

    lax.fori_loop(0, nchunks, chunk, load(0), unroll=8)


def _hgrn(proj3, col0, lb_logits, out_gain, tseq):
    bsz, seq, _ = proj3.shape
    d, c, sub = HG_DIM, HG_CHUNK, HG_SUB
    wst, gsum = _hgrn_consts()
    groups = HG_HEADS // HG_PAR
    width = HG_PAR * d
    assert col0 % width == 0

    def col(off):
        return pl.BlockSpec((1, tseq, width), lambda b, h, t: (b, t, col0 // width + off * groups + h))

    per_head = [pltpu.VMEM((d, d), F32), pltpu.VMEM((c * sub, d), BF16), pltpu.VMEM((c, d), F32),
                pltpu.VMEM((c, d), F32), pltpu.VMEM((c, d), F32)]
    return pl.pallas_call(
        _hgrn_kernel,
        grid=(bsz, groups, seq // tseq),
        in_specs=[
            col(0), col(1), col(2), col(3),
            pl.BlockSpec((2, width), lambda b, h, t: (0, h)),
            pl.BlockSpec((1, d), lambda b, h, t: (0, 0)),
            pl.BlockSpec(wst.shape, lambda b, h, t: (0, 0)),
            pl.BlockSpec(gsum.shape, lambda b, h, t: (0, 0)),
        ],
        out_specs=pl.BlockSpec((1, tseq, width), lambda b, h, t: (b, t, h)),
        out_shape=jax.ShapeDtypeStruct((bsz, seq, HG_QK), BF16),
        scratch_shapes=[s for s in per_head for _ in range(HG_PAR)],
        compiler_params=_params(("parallel", "parallel", "arbitrary")),
        name="hgrn2",
    )(proj3, proj3, proj3, proj3, lb_logits, out_gain.reshape(1, d), wst, gsum)


def _rope(x, cs, sn):
    lane = lax.broadcasted_iota(jnp.int32, x.shape, 1)
    partner = jnp.where(lane % NSA_DIM < NSA_DIM // 2, pltpu.roll(x, LANES - NSA_DIM // 2, 1),
                        pltpu.roll(x, NSA_DIM // 2, 1))
    return x * cs + partner * sn


def _nsa_prep_kernel(q_ref, ksl_ref, vsl_ref, kwn_ref, vwn_ref, cs_ref, sn_ref,
                     qrot_ref, kslo_ref, vslo_ref, kwno_ref, vwno_ref):
    cs = cs_ref[...]
    sn = sn_ref[...]
    for c in range(NSA_WIDTH // LANES):
        cols = slice(c * LANES, (c + 1) * LANES)
        qrot_ref[0, :, cols] = (_rope(q_ref[0, :, cols], cs, sn) * (NSA_DIM ** -0.5 * LOG2E)).astype(BF16)
    tseq = q_ref.shape[1]
    lane = lax.broadcasted_iota(jnp.int32, (tseq, LANES), 1)
    block = (pl.program_id(1) * tseq + lax.broadcasted_iota(jnp.int32, (tseq, LANES), 0)) // SLC_BLOCK
    block_onehot = jnp.where(lane - NSA_DIM == block, 1.0, 0.0)
    ones_rows = jnp.where(lax.broadcasted_iota(jnp.int32, (NSA_VPAD, tseq), 0) == 0, 1.0, 0.0).astype(BF16)
    for c in range(NSA_KV // LANES):
        cols = slice(c * LANES, (c + 1) * LANES)
        ks = _rope(ksl_ref[0, :, cols], cs, sn)
        kw = _rope(kwn_ref[0, :, cols], cs, sn).astype(BF16)
        vs = vsl_ref[0, :, cols].T.astype(BF16)
        vw = vwn_ref[0, :, cols].T.astype(BF16)
        for half in range(LANES // NSA_DIM):
            g = c * (LANES // NSA_DIM) + half
            hs = slice(half * NSA_DIM, (half + 1) * NSA_DIM)
            ks_g = ks if half == 0 else pltpu.roll(ks, NSA_DIM, 1)
            kslo_ref[0, g] = jnp.where(lane < NSA_DIM, ks_g, block_onehot).astype(BF16)
            kwno_ref[0, g] = kw[:, hs]
            vslo_ref[0, g, :NSA_DIM] = vs[hs, :]
            vslo_ref[0, g, NSA_DIM:] = ones_rows
            vwno_ref[0, g, :NSA_DIM] = vw[hs, :]
            vwno_ref[0, g, NSA_DIM:] = ones_rows


def _nsa_prep(prec3, rest3, kv_off, tseq):
    bsz, seq, _ = prec3.shape
    half = NSA_DIM // 2
    inv = 1.0 / (ROPE_THETA ** (jnp.arange(0, NSA_DIM, 2, dtype=F32) / NSA_DIM))
    ang = jnp.arange(seq, dtype=F32)[:, None] * inv[None, :]
    cs = jnp.tile(jnp.cos(ang), (1, LANES // half))
    sn = jnp.tile(jnp.concatenate([-jnp.sin(ang), jnp.sin(ang)], axis=1), (1, LANES // NSA_DIM))
    kvb = kv_off // NSA_KV

    def kv_in(i):
        return pl.BlockSpec((1, tseq, NSA_KV), lambda b, t: (b, t, kvb + i))

    assert NSA_DIM + seq // SLC_BLOCK <= LANES

    def k_out(width):
        return (pl.BlockSpec((1, NSA_GROUPS, tseq, width), lambda b, t: (b, 0, t, 0)),
                jax.ShapeDtypeStruct((bsz, NSA_GROUPS, seq, width), BF16))

    (ksl_out, ksl_shape), (kwn_out, kwn_shape) = k_out(LANES), k_out(NSA_DIM)
    v_out = pl.BlockSpec((1, NSA_GROUPS, NSA_DIM + NSA_VPAD, tseq), lambda b, t: (b, 0, 0, t))
    v_shape = jax.ShapeDtypeStruct((bsz, NSA_GROUPS, NSA_DIM + NSA_VPAD, seq), BF16)
    tab = pl.BlockSpec((tseq, LANES), lambda b, t: (t, 0))
    return pl.pallas_call(
        _nsa_prep_kernel,
        grid=(bsz, seq // tseq),
        in_specs=[pl.BlockSpec((1, tseq, NSA_WIDTH), lambda b, t: (b, t, 0)), kv_in(0), kv_in(1), kv_in(2), kv_in(3),
                  tab, tab],
        out_specs=[pl.BlockSpec((1, tseq, NSA_WIDTH), lambda b, t: (b, t, 0)), ksl_out, v_out, kwn_out, v_out],
        out_shape=[jax.ShapeDtypeStruct((bsz, seq, NSA_WIDTH), BF16), ksl_shape, v_shape, kwn_shape, v_shape],
        compiler_params=_params(("parallel", "parallel")),
        name="nsa_prep",
    )(prec3, rest3, rest3, rest3, rest3, cs, sn)


def _compress_kernel(u_ref, pos_ref, w1_ref, w2_ref, o_ref, *, precise):
    mm = _dot3 if precise else (lambda a, b: _dot(a.astype(BF16), b.astype(BF16)))
    u = u_ref[0]
    nu = u.shape[0]
    ya = mm(u + pos_ref[0:1, :], w1_ref[0])
    yb = mm(u + pos_ref[1:2, :], w1_ref[1])
    hid = ya + pltpu.roll(yb, nu - 1, 0)
    o_ref[0] = mm(jax.nn.gelu(hid), w2_ref[...])


def _compress(kv, pos, w1, w2, precise):
    bsz, seq, _ = kv.shape
    nu = seq // CMP_STRIDE
    width = CMP_STRIDE * NSA_DIM
    u = kv.reshape(bsz, nu, CMP_STRIDE, NSA_GROUPS, NSA_DIM).transpose(0, 3, 1, 2, 4).reshape(bsz * NSA_GROUPS, nu, width)
    return pl.pallas_call(
        functools.partial(_compress_kernel, precise=precise),
        grid=(bsz * NSA_GROUPS,),
        in_specs=[
            pl.BlockSpec((1, nu, width), lambda i: (i, 0, 0)),
            pl.BlockSpec((2, width), lambda i: (0, 0)),
            pl.BlockSpec((2, width, CMP_HIDDEN), lambda i: (0, 0, 0)),
            pl.BlockSpec((CMP_HIDDEN, NSA_DIM), lambda i: (0, 0)),
        ],
        out_specs=pl.BlockSpec((1, nu, NSA_DIM), lambda i: (i, 0, 0)),
        out_shape=jax.ShapeDtypeStruct((bsz * NSA_GROUPS, nu, NSA_DIM), F32),
        compiler_params=_params(("parallel",)),
        name="nsa_compress",
    )(u, pos.reshape(2, width), w1.reshape(2, width, CMP_HIDDEN), w2)


def _nsa_attn_kernel(qraw_ref, qrot_ref, kc_ref, vct_ref, ksl_ref, vslt_ref, kwn_ref, vwnt_ref, gate_ref, aggt_ref,
                     o_ref, sa_ref, sb_ref, *win_refs, topk, tk):
    tq = qraw_ref.shape[1]
    nu = kc_ref.shape[2]
    ns = aggt_ref.shape[0]
    rep, dk = NSA_REP, NSA_DIM
    qs = pl.program_id(2) * tq
    tpos = qs + lax.broadcasted_iota(jnp.int32, (1, tq), 1)

    qrt = (qraw_ref[0] * (dk ** -0.5 * LOG2E)).T
    kc_hi, kc_lo = _split2(kc_ref[0, 0])
    vct = vct_ref[0, 0].astype(BF16)
    crow = lax.broadcasted_iota(jnp.int32, (nu, tq), 0)
    m_c = (crow * CMP_STRIDE + CMP_BLOCK - 1 <= tpos) & (crow < nu - 1)
    q_hi, q_lo = _split2(jnp.concatenate([qrt[r * dk:(r + 1) * dk] for r in range(rep)], axis=1))
    s_all = _dot(jnp.concatenate([kc_hi, kc_hi, kc_lo], axis=1),
                 jnp.concatenate([q_hi, q_lo, q_hi], axis=0))

    qt = qrot_ref[0].astype(F32).T.astype(BF16)
    qt_all = jnp.concatenate([qt[r * dk:(r + 1) * dk] for r in range(rep)], axis=1)
    hi = (qs + tq) // tk

    def key_tile(ktc):
        return pl.ds(pl.multiple_of(ktc * tk, tk), tk)

    def scores(k_ref, ktc, dst_ref):
        dst_ref[...] = _dot(k_ref[0, 0, key_tile(ktc), :], qt_all)

    win_tiles = [hi - len(win_refs) + j for j in range(len(win_refs))]
    for kt, dst_ref in zip(win_tiles, win_refs):
        scores(kwn_ref, jnp.maximum(kt, 0), dst_ref)
    psum = jnp.zeros((nu, tq), F32)
    p_all = []
    has_block = tpos >= CMP_BLOCK - 1
    for r in range(rep):
        s = jnp.where(m_c, s_all[:, r * tq:(r + 1) * tq], NEG)
        e = jnp.exp2(s - jnp.max(s, axis=0, keepdims=True))
        p = e * jnp.where(has_block, 1.0 / jnp.sum(e, axis=0, keepdims=True), 0.0)
        psum = psum + p
        p_all.append(p.astype(BF16))
    o_c_all = _dot(vct, jnp.concatenate(p_all, axis=1))
    o_c = [o_c_all[:, r * tq:(r + 1) * tq] for r in range(rep)]

    p_hi, p_lo = _split2(psum)
    imp = _dot(aggt_ref[...], p_hi) + _dot(aggt_ref[...], p_lo)
    jrow = lax.broadcasted_iota(jnp.int32, (ns, tq), 0)
    dj = jnp.right_shift(tpos, SLC_BLOCK.bit_length() - 1) - jrow
    forced = (jrow == 0) | ((dj >= 0) & (dj < SLC_LOCAL))
    imp = jnp.where(forced, BIG, jnp.where(jrow * SLC_BLOCK <= tpos, imp, -BIG))
    sub8 = lax.broadcasted_iota(jnp.int32, (8, tq), 0)
    chunks = [imp[c * 8:(c + 1) * 8] for c in range(ns // 8)]
    ranks = [jnp.zeros((8, tq), F32) for _ in range(ns // 8)]
    for jp in range(ns):
        row = chunks[jp // 8][jp % 8:jp % 8 + 1]
        for c in range(ns // 8):
            if c < jp // 8:
                ahead = jnp.where(row > chunks[c], 1.0, 0.0)
            elif c > jp // 8:
                ahead = jnp.where(row >= chunks[c], 1.0, 0.0)
            else:
                tie = jnp.where(sub8 > jp % 8, 1.0, 0.0)
                ahead = jnp.where(row > chunks[c], 1.0, jnp.where(row == chunks[c], tie, 0.0))
            ranks[c] = ranks[c] + ahead
    selt = [jnp.where(ranks[c] < topk, 0.0, NEG) for c in range(ns // 8)]

    def consume(vt_ref, ktc, src_ref, carry, mask=None):
        vt = vt_ref[0, 0, :, key_tile(ktc)]
        out = []
        for r in range(rep):
            m_old, acc = carry[r]
            s = src_ref[:, r * tq:(r + 1) * tq]
            if mask is not None:
                s = jnp.where(mask, s, NEG)
            m_new = jnp.maximum(m_old, jnp.max(s, axis=0, keepdims=True))
            alpha = jnp.exp2(m_old - m_new)
            p = jnp.exp2(s - m_new).astype(BF16)
            out.append((m_new, acc * alpha + _dot(vt, p)))
        return tuple(out)

    def normalised(carry):
        return [acc[:dk] * (1.0 / acc[dk:dk + 1]) for _, acc in carry]

    init = tuple((jnp.full((1, tq), NEG, F32), jnp.zeros((dk + NSA_VPAD, tq), F32)) for _ in range(rep))
    krow = lax.broadcasted_iota(jnp.int32, (tk, tq), 0)
    unseen = 1 << 30

    carry = init
    for kt, src_ref in zip(win_tiles, win_refs):
        ktc = jnp.maximum(kt, 0)
        dlt = tpos - (jnp.where(kt >= 0, ktc * tk, unseen) + krow)
        carry = consume(vwnt_ref, ktc, src_ref, carry, mask=pltpu.bitcast(dlt, jnp.uint32) < jnp.uint32(WIN))
    o_w = normalised(carry)

    selb = jnp.concatenate([jnp.concatenate(selt, axis=0)] * rep, axis=1).astype(BF16)
    pad = jnp.zeros((LANES - dk - ns, rep * tq), BF16)
    qt_sel = jnp.concatenate([qt_all, selb] + ([pad] if LANES > dk + ns else []), axis=0)

    def sel_scores(ktc, dst_ref):
        dst_ref[...] = _dot(ksl_ref[0, 0, key_tile(ktc), :], qt_sel)

    def pair(i, carry):
        kt = 2 * i
        sel_scores(kt + 1, sb_ref)
        carry = consume(vslt_ref, kt, sa_ref, carry)
        sel_scores(kt + 2, sa_ref)
        return consume(vslt_ref, kt + 1, sb_ref, carry)

    past = hi - 1
    sel_scores(0, sa_ref)
    carry = lax.fori_loop(0, past // 2, pair, init)
    carry = lax.cond(past % 2 == 1, lambda c: consume(vslt_ref, past - 1, sa_ref, c), lambda c: c, carry)
    sel_scores(past, sb_ref)
    carry = consume(vslt_ref, past, sb_ref, carry, mask=past * tk + krow <= tpos)
    o_s = normalised(carry)

    gate = jax.nn.sigmoid(gate_ref[0, 0])
    o_t = [gate[3 * r:3 * r + 1] * o_c[r] + gate[3 * r + 1:3 * r + 2] * o_s[r] + gate[3 * r + 2:3 * r + 3] * o_w[r]
           for r in range(rep)]
    o_ref[0] = jnp.concatenate(o_t, axis=0).T.astype(o_ref.dtype)


def _nsa_attn(prec3, qrot, kc, vct, ksl, vslt, kwn, vwnt, gates_t, tq, tk):
    bsz, seq, _ = qrot.shape
    nu = seq // CMP_STRIDE
    ns = seq // SLC_BLOCK
    ci = np.arange(nu)[None, :]
    sj = np.arange(ns)[:, None]
    overlap = (ci * CMP_STRIDE < (sj + 1) * SLC_BLOCK) & (ci * CMP_STRIDE + CMP_BLOCK > sj * SLC_BLOCK) & (ci < nu - 1)
    aggt = jnp.asarray(overlap, BF16)
    gw = NSA_REP * NSA_DIM
    assert tq % tk == 0 and seq % tq == 0
    win_tiles = -(-(WIN - 1) // tk) + tq // tk

    def q_spec():
        return pl.BlockSpec((1, tq, gw), lambda b, g, t: (b, t, g))

    def per_group(rows, cols):
        return pl.BlockSpec((1, 1, rows, cols), lambda b, g, t: (b, g, 0, 0))

    return pl.pallas_call(
        functools.partial(_nsa_attn_kernel, topk=min(SLC_TOPK, ns), tk=tk),
        grid=(bsz, NSA_GROUPS, seq // tq),
        in_specs=[q_spec(), q_spec(), per_group(nu, NSA_DIM), per_group(NSA_DIM, nu),
                  per_group(seq, LANES), per_group(NSA_DIM + NSA_VPAD, seq),
                  per_group(seq, NSA_DIM), per_group(NSA_DIM + NSA_VPAD, seq),
                  pl.BlockSpec((1, 1, 3 * NSA_REP, tq), lambda b, g, t: (b, g, 0, t)),
                  pl.BlockSpec((ns, nu), lambda b, g, t: (0, 0))],
        out_specs=q_spec(),
        out_shape=jax.ShapeDtypeStruct((bsz, seq, NSA_WIDTH), BF16),
        scratch_shapes=[pltpu.VMEM((tk, NSA_REP * tq), F32)] * (2 + win_tiles),
        compiler_params=_params(("parallel", "parallel", "arbitrary")),
        name="nsa_attention",
    )(prec3, qrot, kc, vct, ksl, vslt, kwn, vwnt, gates_t, aggt)


def _nsa(prec3, rest3, kv_off, gate_off, cmp_k, cmp_v):
    bsz, seq, _ = prec3.shape
    nu = seq // CMP_STRIDE
    qrot, ksl, vslt, kwn, vwnt = _nsa_prep(prec3, rest3, kv_off + NSA_KV, min(seq, 512))
    kc = _compress(prec3[:, :, NSA_WIDTH:NSA_WIDTH + NSA_KV], *cmp_k, precise=True)
    vc = _compress(rest3[:, :, kv_off:kv_off + NSA_KV], *cmp_v, precise=False)
    kc = kc.reshape(bsz, NSA_GROUPS, nu, NSA_DIM)
    vct = vc.reshape(bsz, NSA_GROUPS, nu, NSA_DIM).transpose(0, 1, 3, 2)
    gates_t = rest3[:, :, gate_off:gate_off + 3 * NSA_HEADS].reshape(bsz, seq, NSA_GROUPS, 3 * NSA_REP).transpose(0, 2, 3, 1)
    return _nsa_attn(prec3, qrot, kc, vct, ksl, vslt, kwn, vwnt, gates_t, 256, 256)


def _out_proj_kernel(yh_ref, yn_ref, g_ref, w_ref, x_ref, o_ref, y_ref):
    @pl.when(pl.program_id(1) == 0)
    def _():
        wh = yh_ref.shape[1]
        y_ref[:, :wh] = yh_ref[...]
        y_ref[:, wh:] = _rms(yn_ref[...].astype(F32), g_ref[...]).astype(BF16)

    o_ref[...] = x_ref[...] + _dot(y_ref[...], w_ref[...])


def _out_proj(y_hg, y_nsa, nsa_gain, w_out, x, tm, tn):
    m, dm = x.shape
    wh, wn = y_hg.shape[1], y_nsa.shape[1]
    return pl.pallas_call(
        _out_proj_kernel,
        grid=(m // tm, dm // tn),
        in_specs=[
            pl.BlockSpec((tm, wh), lambda i, j: (i, 0)),
            pl.BlockSpec((tm, wn), lambda i, j: (i, 0)),
            pl.BlockSpec((1, wn), lambda i, j: (0, 0)),
            pl.BlockSpec((None, wh + wn, tn), lambda i, j: (j, 0, 0)),
            pl.BlockSpec((tm, tn), lambda i, j: (i, j)),
        ],
        out_specs=pl.BlockSpec((tm, tn), lambda i, j: (i, j)),
        out_shape=jax.ShapeDtypeStruct((m, dm), F32),
        scratch_shapes=[pltpu.VMEM((tm, wh + wn), BF16)],
        compiler_params=_params(("parallel", "arbitrary")),
        name="out_proj",
    )(y_hg, y_nsa, nsa_gain.reshape(1, wn), _column_blocks(w_out.astype(BF16), tn), x)


MOE_ROWS = 256


def _pack_bf16_pairs(hi):
    n = hi.shape[1] // 2
    bits = pltpu.bitcast(hi.astype(F32), jnp.uint32)
    return jnp.right_shift(bits[:, :n], jnp.uint32(16)) | (bits[:, n:] & jnp.uint32(0xFFFF0000))


def _unpack_f32_pairs(words):
    lo = pltpu.bitcast(jnp.left_shift(words, jnp.uint32(16)), F32)
    hi = pltpu.bitcast(words & jnp.uint32(0xFFFF0000), F32)
    return lo, hi


def _unpack_bf16_pairs(words):
    lo, hi = _unpack_f32_pairs(words)
    return lo.astype(BF16), hi.astype(BF16)


def _router_kernel(x_ref, g_ref, w_ref, b_ref, tri_ref, xn_ref, info_ref, infot_ref, cnt_ref, carry_ref):
    @pl.when(pl.program_id(0) == 0)
    def _():
        carry_ref[...] = jnp.zeros_like(carry_ref)

    xn = _rms(x_ref[...], g_ref[...])
    hi, lo = _split2(xn)
    xn_ref[...] = _pack_bf16_pairs(hi)
    logits = _dot(hi, w_ref[0]) + (_dot(hi, w_ref[1]) + _dot(lo, w_ref[0])) + b_ref[...]
    lane = lax.broadcasted_iota(jnp.int32, logits.shape, 1).astype(F32)
    none = float(LANES)

    def first_max(mask):
        top = jnp.max(jnp.where(mask, logits, -jnp.inf), axis=-1, keepdims=True)
        return top, jnp.min(jnp.where(mask & (logits == top), lane, none), axis=-1, keepdims=True)

    is_g = lane < MOE_GROUPS
    gmax, gsel = first_max(is_g)
    gw = 1.0 / jnp.sum(jnp.where(is_g, jnp.exp(logits - gmax), 0.0), axis=-1, keepdims=True)
    lo_lane = MOE_GROUPS + gsel * MOE_EPG
    in_grp = (lane >= lo_lane) & (lane < lo_lane + MOE_EPG)
    v1, i1 = first_max(in_grp)
    v2, i2 = first_max(in_grp & (lane != i1))
    e = jnp.exp(v2 - v1)
    w1 = gw / (1.0 + e)
    w2 = gw * e / (1.0 + e)
    e1 = i1 - MOE_GROUPS
    e2 = i2 - MOE_GROUPS
    onehot = jnp.where((lane == e1) | (lane == e2), 1.0, 0.0)
    before = _dot(tri_ref[...], onehot.astype(BF16)) + carry_ref[...]
    r1 = jnp.sum(jnp.where(lane == e1, before, 0.0), axis=-1, keepdims=True)
    r2 = jnp.sum(jnp.where(lane == e2, before, 0.0), axis=-1, keepdims=True)
    carry_ref[...] = carry_ref[...] + jnp.sum(onehot, axis=0, keepdims=True)
    cnt_ref[...] = carry_ref[...]
    info = jnp.zeros_like(logits)
    for idx, val in enumerate((e1, e2, w1, w2, r1, r2)):
        info = jnp.where(lane == idx, val, info)
    info_ref[...] = info
    infot_ref[...] = info.T[:8]


def _router(x2, gain, w_group, b_group, w_router, b_router, tm):
    m, dm = x2.shape
    wcat = jnp.pad(jnp.concatenate([w_group, w_router], axis=1), ((0, 0), (0, LANES - MOE_GROUPS - N_EXPERTS)))
    bcat = jnp.pad(jnp.concatenate([b_group, b_router]), (0, LANES - MOE_GROUPS - N_EXPERTS)).reshape(1, LANES)
    tri = jnp.asarray(np.tril(np.ones((tm, tm), np.float32), -1), BF16)
    return pl.pallas_call(
        _router_kernel,
        grid=(m // tm,),
        in_specs=[
            pl.BlockSpec((tm, dm), lambda i: (i, 0)),
            pl.BlockSpec((1, dm), lambda i: (0, 0)),
            pl.BlockSpec((2, dm, LANES), lambda i: (0, 0, 0)),
            pl.BlockSpec((1, LANES), lambda i: (0, 0)),
            pl.BlockSpec((tm, tm), lambda i: (0, 0)),
        ],
        out_specs=[
            pl.BlockSpec((tm, dm // 2), lambda i: (i, 0)),
            pl.BlockSpec((tm, LANES), lambda i: (i, 0)),
            pl.BlockSpec((8, tm), lambda i: (0, i)),
            pl.BlockSpec((1, LANES), lambda i: (0, 0)),
        ],
        out_shape=[
            jax.ShapeDtypeStruct((m, dm // 2), jnp.uint32),
            jax.ShapeDtypeStruct((m, LANES), F32),
            jax.ShapeDtypeStruct((8, m), F32),
            jax.ShapeDtypeStruct((1, LANES), F32),
        ],
        scratch_shapes=[pltpu.VMEM((1, LANES), F32)],
        compiler_params=_params(("arbitrary",)),
        name="moe_router",
    )(x2, gain.reshape(1, dm), jnp.stack(_split2(wcat)), bcat, tri)


def _dispatch_kernel(d1_ref, d2_ref, xn_ref, xs_in_ref, xs_ref, sem):
    del xs_in_ref
    tm = xn_ref.shape[0]
    base = pl.program_id(0) * tm

    def row_copy(i, dest):
        return pltpu.make_async_copy(xn_ref.at[pl.ds(i, 1)], xs_ref.at[pl.ds(dest, 1)], sem)

    def issue(i, c):
        row_copy(i, d1_ref[base + i]).start()
        row_copy(i, d2_ref[base + i]).start()
        return c

    lax.fori_loop(0, tm, issue, 0, unroll=8)
    for _ in range(2):
        pltpu.make_async_copy(xn_ref, xs_ref.at[pl.ds(0, tm)], sem).wait()


def _dispatch(xn, d1, d2, rows, tm):
    m, words = xn.shape
    return pl.pallas_call(
        _dispatch_kernel,
        grid_spec=pltpu.PrefetchScalarGridSpec(
            num_scalar_prefetch=2,
            grid=(m // tm,),
            in_specs=[pl.BlockSpec((tm, words), lambda i, *_: (i, 0)), pl.BlockSpec(memory_space=pl.ANY)],
            out_specs=pl.BlockSpec(memory_space=pl.ANY),
            scratch_shapes=[pltpu.SemaphoreType.DMA(())],
        ),
        out_shape=jax.ShapeDtypeStruct((rows, words), jnp.uint32),
        input_output_aliases={3: 0},
        compiler_params=_params(("arbitrary",)),
        name="moe_dispatch",
    )(d1, d2, xn, jnp.zeros((rows, words), jnp.uint32))


def _expert_kernel(be_ref, nused_ref, x_ref, wg_ref, wu_ref, wd_ref, o_ref, wgb_ref, wub_ref, wdb_ref):
    i = pl.program_id(0)

    @pl.when((i == 0) | (be_ref[i] != be_ref[jnp.maximum(i - 1, 0)]))
    def _():
        wgb_ref[...] = wg_ref[0].astype(BF16)
        wub_ref[...] = wu_ref[0].astype(BF16)
        wdb_ref[...] = wd_ref[0].astype(BF16)

    @pl.when(i < nused_ref[0])
    def _():
        half = wgb_ref.shape[0] // 2
        x_lo, x_hi = _unpack_bf16_pairs(x_ref[...])

        def up(w_ref):
            return _dot(x_lo, w_ref[:half]) + _dot(x_hi, w_ref[half:])

        hid = (jax.nn.silu(up(wgb_ref)) * up(wub_ref)).astype(BF16)
        for g in range(o_ref.shape[1] // LANES):
            y = _dot(hid, wdb_ref[:, g * 2 * LANES:(g + 1) * 2 * LANES])
            o_ref[:, g * LANES:(g + 1) * LANES] = _pack_bf16_pairs(y.astype(BF16))

    @pl.when(i >= nused_ref[0])
    def _():
        o_ref[...] = jnp.zeros_like(o_ref)


def _experts(xs, block_e, nused, w_gate, w_up, w_down):
    rows, words = xs.shape
    _, dm, ff = w_gate.shape
    return pl.pallas_call(
        _expert_kernel,
        grid_spec=pltpu.PrefetchScalarGridSpec(
            num_scalar_prefetch=2,
            grid=(rows // MOE_ROWS,),
            in_specs=[
                pl.BlockSpec((MOE_ROWS, words), lambda i, be, nu: (i, 0)),
                pl.BlockSpec((1, dm, ff), lambda i, be, nu: (be[i], 0, 0)),
                pl.BlockSpec((1, dm, ff), lambda i, be, nu: (be[i], 0, 0)),
                pl.BlockSpec((1, ff, dm), lambda i, be, nu: (be[i], 0, 0)),
            ],
            out_specs=pl.BlockSpec((MOE_ROWS, words), lambda i, be, nu: (i, 0)),
            scratch_shapes=[pltpu.VMEM((dm, ff), BF16), pltpu.VMEM((dm, ff), BF16), pltpu.VMEM((ff, dm), BF16)],
        ),
        out_shape=jax.ShapeDtypeStruct((rows, words), jnp.uint32),
        compiler_params=_params(("arbitrary",)),
        name="moe_experts",
    )(block_e, nused, xs, w_gate, w_up, w_down)


def _combine_kernel(d1_ref, d2_ref, x_ref, info_ref, g_ref, ys_ref, o_ref, buf_a, buf_b, sem_a, sem_b):
    ts = buf_a.shape[1]
    dm = x_ref.shape[1]
    step, nsteps = pl.program_id(0), pl.num_programs(0)
    base = step * 2 * ts

    def row_copy(buf_ref, sem, r, slot, src):
        return pltpu.make_async_copy(ys_ref.at[pl.ds(src, 1)], buf_ref.at[slot, pl.ds(r, 1)], sem)

    def issue(buf_ref, sem, tok0):
        for r in range(ts):
            row_copy(buf_ref, sem, r, 0, d1_ref[tok0 + r]).start()
            row_copy(buf_ref, sem, r, 1, d2_ref[tok0 + r]).start()

    def wait(buf_ref, sem):
        for slot in range(2):
            pltpu.make_async_copy(ys_ref.at[pl.ds(0, ts)], buf_ref.at[slot], sem).wait()

    def finish(buf_ref, rows):
        info = info_ref[rows, :]
        w1, w2 = info[:, 2:3], info[:, 3:4]
        ssq = jnp.zeros((ts, 1), F32)
        for g in range(dm // (2 * LANES)):
            a = _unpack_f32_pairs(buf_ref[0, :, g * LANES:(g + 1) * LANES])
            b = _unpack_f32_pairs(buf_ref[1, :, g * LANES:(g + 1) * LANES])
            for half in range(2):
                cols = slice((2 * g + half) * LANES, (2 * g + half + 1) * LANES)
                y = x_ref[rows, cols] + (w1 * a[half] + w2 * b[half])
                ssq = ssq + jnp.sum(y * y, axis=-1, keepdims=True)
                o_ref[rows, cols] = y
        o_ref[rows, :] = o_ref[rows, :] * lax.rsqrt(ssq / dm + EPS) * g_ref[...]

    @pl.when(step == 0)
    def _():
        issue(buf_a, sem_a, base)

    wait(buf_a, sem_a)
    issue(buf_b, sem_b, base + ts)
    finish(buf_a, slice(0, ts))
    wait(buf_b, sem_b)
    issue(buf_a, sem_a, jnp.where(step + 1 < nsteps, base + 2 * ts, base))
    finish(buf_b, slice(ts, 2 * ts))

    @pl.when(step + 1 == nsteps)
    def _():
        wait(buf_a, sem_a)


def _combine(x2, info, gain, ys, d1, d2, ts):
    m, dm = x2.shape
    tm = 2 * ts
    gather_buf = pltpu.VMEM((2, ts, dm // 2), jnp.uint32)
    return pl.pallas_call(
        _combine_kernel,
        grid_spec=pltpu.PrefetchScalarGridSpec(
            num_scalar_prefetch=2,
            grid=(m // tm,),
            in_specs=[
                pl.BlockSpec((tm, dm), lambda i, *_: (i, 0)),
                pl.BlockSpec((tm, LANES), lambda i, *_: (i, 0)),
                pl.BlockSpec((1, dm), lambda i, *_: (0, 0)),
                pl.BlockSpec(memory_space=pl.ANY),
            ],
            out_specs=pl.BlockSpec((tm, dm), lambda i, *_: (i, 0)),
            scratch_shapes=[gather_buf, gather_buf, pltpu.SemaphoreType.DMA(()), pltpu.SemaphoreType.DMA(())],
        ),
        out_shape=jax.ShapeDtypeStruct((m, dm), F32),
        compiler_params=_params(("arbitrary",)),
        name="moe_combine",
    )(d1, d2, x2, info, gain.reshape(1, dm), ys)


def _moe_and_final_norm(x2, ffn_gain, w_group, b_group, w_router, b_router, w_gate, w_up, w_down, final_gain, tm):
    m, _ = x2.shape
    xn, info, infot, cnt = _router(x2, ffn_gain, w_group, b_group, w_router, b_router, tm)
    counts = cnt[0, :N_EXPERTS].astype(jnp.int32)
    padded = (counts + MOE_ROWS - 1) // MOE_ROWS * MOE_ROWS
    pend = jnp.cumsum(padded)
    pstart = pend - padded
    fields = infot.astype(jnp.int32)
    d1 = pstart[fields[0]] + fields[4]
    d2 = pstart[fields[1]] + fields[5]
    nblocks = 2 * m // MOE_ROWS + N_EXPERTS
    first_row = jnp.arange(nblocks, dtype=jnp.int32) * MOE_ROWS
    block_e = jnp.minimum(jnp.sum(pend[None, :] <= first_row[:, None], axis=1), N_EXPERTS - 1).astype(jnp.int32)
    nused = (pend[-1:] // MOE_ROWS).astype(jnp.int32)
    xs = _dispatch(xn, d1, d2, nblocks * MOE_ROWS, tm)
    ys = _experts(xs, block_e, nused, w_gate, w_up, w_down)
    return _combine(x2, info, final_gain, ys, d1, d2, tm)


def kernel(x, attn_norm, w_in, hg_lb_logits, hg_out_norm, cmp_pos_k, cmp_w1_k, cmp_w2_k, cmp_pos_v, cmp_w1_v,
           cmp_w2_v, nsa_out_norm, w_out, ffn_norm, moe_w_group, moe_b_group, moe_w_router, moe_b_router,
           moe_w_gate, moe_w_up, moe_w_down, final_norm):
    bsz, seq, dm = x.shape
    xt = x.reshape(bsz * seq, dm)
    w = w_in[0]
    p0, p1 = 4 * HG_QK, 4 * HG_QK + NSA_WIDTH + NSA_KV
    tn = 256
    assert p0 % tn == 0 and p1 % tn == 0
    w_all = _column_blocks(jnp.pad(w.astype(BF16), ((0, 0), (0, (-w.shape[1]) % tn))), tn)
    w_prec = _column_blocks(jnp.stack(_split2(w[:, p0:p1])), tn)
    proj3 = _normed_matmul(xt, attn_norm[0], w_prec, w_all, p0 // tn, 1024).reshape(bsz, seq, -1)
    hg0 = p1 - p0
    y_hg = _hgrn(proj3, hg0, hg_lb_logits, hg_out_norm[0], 1024)
    y_nsa = _nsa(proj3, proj3, hg0 + 4 * HG_QK, hg0 + 4 * HG_QK + 5 * NSA_KV,
                 (cmp_pos_k[0], cmp_w1_k[0], cmp_w2_k[0]), (cmp_pos_v[0], cmp_w1_v[0], cmp_w2_v[0]))
    x2 = _out_proj(y_hg.reshape(bsz * seq, -1), y_nsa.reshape(bsz * seq, -1), nsa_out_norm[0], w_out[0], xt, 1024, 512)
    out = _moe_and_final_norm(x2, ffn_norm[0], moe_w_group[0], moe_b_group[0], moe_w_router[0], moe_b_router[0],
                              moe_w_gate[0], moe_w_up[0], moe_w_down[0], final_norm, 256)
    return out.reshape(bsz, seq, dm)
```

```python
import functools

import jax
import jax.numpy as jnp
import numpy as np
from jax import lax
from jax.experimental import pallas as pl
from jax.experimental.pallas import tpu as pltpu

F32 = jnp.float32
BF16 = jnp.bfloat16

EPS = 1e-6
ROPE_THETA = 10000.0
NEG = -1e30
BIG = 1e9
LOG2E = 1.4426950408889634

HG_HEADS = 8
HG_DIM = 128
HG_QK = HG_HEADS * HG_DIM
HG_CHUNK = 64
HG_SUB = 8

NSA_HEADS = 16
NSA_GROUPS = 4
NSA_REP = NSA_HEADS // NSA_GROUPS
NSA_DIM = 64
NSA_WIDTH = NSA_HEADS * NSA_DIM
NSA_KV = NSA_GROUPS * NSA_DIM
NSA_VPAD = 16
NSA_GATE_ROWS = 16
CMP_BLOCK = 32
CMP_STRIDE = 16
CMP_HIDDEN = 256
SLC_BLOCK = 64
SLC_TOPK = 16
SLC_LOCAL = 2
WIN = 512

MOE_GROUPS = 4
MOE_EPG = 8
N_EXPERTS = MOE_GROUPS * MOE_EPG
EXPERT_FF = 512

LANES = 128
VMEM_LIMIT = 56 * 1024 * 1024


def _params(semantics, **kw):
    return pltpu.CompilerParams(dimension_semantics=semantics, vmem_limit_bytes=VMEM_LIMIT, **kw)


def _split2(a):
    hi = a.astype(BF16)
    return hi, (a - hi.astype(F32)).astype(BF16)


def _split3(a):
    hi = a.astype(BF16)
    r = a - hi.astype(F32)
    mid = r.astype(BF16)
    return hi, mid, (r - mid.astype(F32)).astype(BF16)


def _dot(a, b):
    return jnp.dot(a, b, preferred_element_type=F32)


def _dot_nt(a, b):
    return lax.dot_general(a, b, (((1,), (1,)), ((), ())), preferred_element_type=F32)


def _dot3(a, b):
    a_hi, a_lo = _split2(a)
    b_hi, b_lo = _split2(b)
    return _dot(a_hi, b_hi) + (_dot(a_hi, b_lo) + _dot(a_lo, b_hi))


def _dot3_nt(a, b):
    a_hi, a_lo = _split2(a)
    b_hi, b_lo = _split2(b)
    return _dot_nt(a_hi, b_hi) + (_dot_nt(a_hi, b_lo) + _dot_nt(a_lo, b_hi))


def _rms(x, gain):
    return x * lax.rsqrt(jnp.mean(x * x, axis=-1, keepdims=True) + EPS) * gain


def _normed_matmul_kernel(x_ref, g_ref, wp_ref, w_ref, o_ref, h_ref, *, nprec):
    j = pl.program_id(1)

    @pl.when(j == 0)
    def _():
        y = _rms(x_ref[...], g_ref[...])
        hi = y.astype(BF16)
        h_ref[0] = hi
        h_ref[1] = (y - hi.astype(F32)).astype(BF16)

    @pl.when(j < nprec)
    def _():
        o_ref[...] = _dot(h_ref[0], wp_ref[0]) + (_dot(h_ref[0], wp_ref[1]) + _dot(h_ref[1], wp_ref[0]))

    @pl.when(j >= nprec)
    def _():
        o_ref[...] = _dot(h_ref[0], w_ref[...])


def _column_blocks(w, tn):
    *lead, k, n = w.shape
    return jnp.moveaxis(w.reshape(*lead, k, n // tn, tn), -2, -3)


def _normed_matmul(x, gain, wp_blocks, w_blocks, first, tm):
    m, k = x.shape
    nb, _, tn = w_blocks.shape
    nprec = wp_blocks.shape[1]

    def single_pass_block(j):
        r = jnp.maximum(j - nprec, 0)
        return jnp.where(r < first, r, r + nprec)

    return pl.pallas_call(
        functools.partial(_normed_matmul_kernel, nprec=nprec),
        grid=(m // tm, nb),
        in_specs=[
            pl.BlockSpec((tm, k), lambda i, j: (i, 0)),
            pl.BlockSpec((1, k), lambda i, j: (0, 0)),
            pl.BlockSpec((2, None, k, tn), lambda i, j: (0, jnp.minimum(j, nprec - 1), 0, 0)),
            pl.BlockSpec((None, k, tn), lambda i, j: (single_pass_block(j), 0, 0)),
        ],
        out_specs=pl.BlockSpec((tm, tn), lambda i, j: (i, j)),
        out_shape=jax.ShapeDtypeStruct((m, nb * tn), F32),
        scratch_shapes=[pltpu.VMEM((2, tm, k), BF16)],
        compiler_params=_params(("parallel", "arbitrary")),
        name="normed_matmul",
    )(x, gain.reshape(1, k), wp_blocks, w_blocks)


def _hgrn_consts():
    c, sub = HG_CHUNK, HG_SUB
    tri = np.tile(np.tril(np.ones((c, c), np.float32)), (1, 3))
    gsum = (np.arange(c * sub)[None, :] // sub == np.arange(c)[:, None]).astype(np.float32)
    return jnp.asarray(tri, BF16), jnp.asarray(gsum, BF16)


HG_PAR = 2


def _hgrn_kernel(q_ref, f_ref, i_ref, g_ref, lbl_ref, gain_ref, tri_ref, gsum_ref, o_ref, *scratch):
    c, sub, d = HG_CHUNK, HG_SUB, HG_DIM
    nsub = c // sub
    heads = range(HG_PAR)
    st_refs, p_refs, cl_refs, qs_refs, k_refs = (scratch[i * HG_PAR:(i + 1) * HG_PAR] for i in range(5))

    @pl.when(pl.program_id(2) == 0)
    def _():
        for h in heads:
            st_refs[h][...] = jnp.zeros_like(st_refs[h])

    l0 = lbl_ref[0:1, :]
    l1 = lbl_ref[1:2, :]
    lmax = jnp.maximum(l0, l1)
    e0 = jnp.exp(l0 - lmax)
    lb_all = e0 / (e0 + jnp.exp(l1 - lmax))
    srow = lax.broadcasted_iota(jnp.int32, (sub, d), 0)
    ones = jnp.ones((d, d), BF16)

    def rows_at(x, start):
        parts = ([jnp.zeros((start, d), F32)] if start else []) + [x]
        if start + x.shape[0] < c:
            parts.append(jnp.zeros((c - start - x.shape[0], d), F32))
        return jnp.concatenate(parts, axis=0)

    nchunks = q_ref.shape[1] // c
    cols = [slice(h * d, (h + 1) * d) for h in heads]

    def load(ci):
        rows = pl.ds(pl.multiple_of(ci * c, c), c)
        out = []
        for h in heads:
            lb = lb_all[:, cols[h]]
            f = lb + (1.0 - lb) * jax.nn.sigmoid(f_ref[0, rows, cols[h]])
            bcum = _dot(tri_ref[...], jnp.concatenate(_split3(jnp.log(f)), axis=0))
            out.append((q_ref[0, rows, cols[h]] * (d ** -0.5), 1.0 - f, i_ref[0, rows, cols[h]], bcum))
        return tuple(out)

    def chunk(ci, cur):
        nxt = load(jnp.minimum(ci + 1, nchunks - 1))
        rows = pl.ds(pl.multiple_of(ci * c, c), c)
        q, k, v, bcum = ([cur[h][i] for h in heads] for i in range(4))
        a_off, o_inter = [], []
        for h in heads:
            b = bcum[h]
            edge = [b[i * sub - 1:i * sub] for i in range(1, nsub + 1)]
            cl = b - jnp.concatenate([jnp.zeros((sub, d), F32)] + [jnp.broadcast_to(e, (sub, d)) for e in edge[:-1]],
                                     axis=0)
            cl_refs[h][...] = cl * LOG2E
            qs_refs[h][...] = q[h]
            k_refs[h][...] = k[h]
            qe = q[h] * jnp.exp(cl)
            qcat, kcat = [], []
            for i in range(1, nsub):
                qcat.append(rows_at(qe[i * sub:(i + 1) * sub], i * sub))
                kcat.append(rows_at(k[h][:i * sub] * jnp.exp(edge[i - 1] - b[:i * sub]), 0))
            a_off.append(_dot_nt(jnp.concatenate(qcat, axis=1).astype(BF16),
                                 jnp.concatenate(kcat, axis=1).astype(BF16)))
            st = st_refs[h][...]
            o_inter.append(_dot_nt((q[h] * jnp.exp(b)).astype(BF16), st.astype(BF16)))
            kd = k[h] * jnp.exp(edge[-1] - b)
            st_refs[h][...] = st * jnp.exp(edge[-1]) + _dot(v[h].T.astype(BF16), kd.astype(BF16))
        r2 = []
        for h in heads:
            for t in range(c):
                j0 = (t // sub) * sub
                dlt = cl_refs[h][t:t + 1, :] - cl_refs[h][j0:j0 + sub, :]
                e = jnp.exp2(jnp.where(srow <= t - j0, dlt, NEG))
                p_refs[h][t * sub:(t + 1) * sub, :] = (qs_refs[h][t:t + 1, :] * k_refs[h][j0:j0 + sub, :] * e).astype(BF16)
            r2.append(_dot(p_refs[h][...], ones))
        o = []
        for h in heads:
            o.append(o_inter[h] + _dot(a_off[h].astype(BF16), v[h].astype(BF16)))
        for h in heads:
            x = r2[h].reshape(nsub, sub, sub, d) * v[h].reshape(nsub, 1, sub, d)
            o[h] = o[h] + _dot(gsum_ref[...], x.reshape(c * sub, d).astype(BF16))
        for h in heads:
            gate = jax.nn.silu(g_ref[0, rows, cols[h]])
            o_ref[0, rows, cols[h]] = (_rms(o[h], gain_ref[...]) * gate).astype(o_ref.dtype)
        return nxt

    lax.fori_loop(0, nchunks, chunk, load(0), unroll=8)


def _hgrn(proj3, col0, lb_logits, out_gain, tseq):
    bsz, seq, _ = proj3.shape
    d, c, sub = HG_DIM, HG_CHUNK, HG_SUB
    wst, gsum = _hgrn_consts()
    groups = HG_HEADS // HG_PAR
    width = HG_PAR * d
    assert col0 % width == 0

    def col(off):
        return pl.BlockSpec((1, tseq, width), lambda b, h, t: (b, t, col0 // width + off * groups + h))

    per_head = [pltpu.VMEM((d, d), F32), pltpu.VMEM((c * sub, d), BF16), pltpu.VMEM((c, d), F32),
                pltpu.VMEM((c, d), F32), pltpu.VMEM((c, d), F32)]
    return pl.pallas_call(
        _hgrn_kernel,
        grid=(bsz, groups, seq // tseq),
        in_specs=[
            col(0), col(1), col(2), col(3),
            pl.BlockSpec((2, width), lambda b, h, t: (0, h)),
            pl.BlockSpec((1, d), lambda b, h, t: (0, 0)),
            pl.BlockSpec(wst.shape, lambda b, h, t: (0, 0)),
            pl.BlockSpec(gsum.shape, lambda b, h, t: (0, 0)),
        ],
        out_specs=pl.BlockSpec((1, tseq, width), lambda b, h, t: (b, t, h)),
        out_shape=jax.ShapeDtypeStruct((bsz, seq, HG_QK), BF16),
        scratch_shapes=[s for s in per_head for _ in range(HG_PAR)],
        compiler_params=_params(("parallel", "parallel", "arbitrary")),
        name="hgrn2",
    )(proj3, proj3, proj3, proj3, lb_logits, out_gain.reshape(1, d), wst, gsum)


def _rope(x, cs, sn):
    lane = lax.broadcasted_iota(jnp.int32, x.shape, 1)
    partner = jnp.where(lane % NSA_DIM < NSA_DIM // 2, pltpu.roll(x, LANES - NSA_DIM // 2, 1),
                        pltpu.roll(x, NSA_DIM // 2, 1))
    return x * cs + partner * sn


def _nsa_prep_kernel(q_ref, ksl_ref, vsl_ref, kwn_ref, vwn_ref, gate_ref, cs_ref, sn_ref,
                     qrot_ref, kslo_ref, vslo_ref, kwno_ref, vwno_ref, gateo_ref):
    cs = cs_ref[...]
    sn = sn_ref[...]
    gate_t = gate_ref[0].T
    per_group = 3 * NSA_REP
    gateo_ref[...] = jnp.zeros_like(gateo_ref)
    for g in range(NSA_GROUPS):
        gateo_ref[0, g, :per_group] = gate_t[g * per_group:(g + 1) * per_group]
    for c in range(NSA_WIDTH // LANES):
        cols = slice(c * LANES, (c + 1) * LANES)
        qrot_ref[0, :, cols] = (_rope(q_ref[0, :, cols], cs, sn) * (NSA_DIM ** -0.5 * LOG2E)).astype(BF16)
    tseq = q_ref.shape[1]
    lane = lax.broadcasted_iota(jnp.int32, (tseq, LANES), 1)
    block = (pl.program_id(1) * tseq + lax.broadcasted_iota(jnp.int32, (tseq, LANES), 0)) // SLC_BLOCK
    block_onehot = jnp.where(lane - NSA_DIM == block, 1.0, 0.0)
    ones_rows = jnp.where(lax.broadcasted_iota(jnp.int32, (NSA_VPAD, tseq), 0) == 0, 1.0, 0.0).astype(BF16)
    for c in range(NSA_KV // LANES):
        cols = slice(c * LANES, (c + 1) * LANES)
        ks = _rope(ksl_ref[0, :, cols], cs, sn)
        kw = _rope(kwn_ref[0, :, cols], cs, sn).astype(BF16)
        vs = vsl_ref[0, :, cols].T.astype(BF16)
        vw = vwn_ref[0, :, cols].T.astype(BF16)
        for half in range(LANES // NSA_DIM):
            g = c * (LANES // NSA_DIM) + half
            hs = slice(half * NSA_DIM, (half + 1) * NSA_DIM)
            ks_g = ks if half == 0 else pltpu.roll(ks, NSA_DIM, 1)
            kslo_ref[0, g] = jnp.where(lane < NSA_DIM, ks_g, block_onehot).astype(BF16)
            kwno_ref[0, g] = kw[:, hs]
            vslo_ref[0, g, :NSA_DIM] = vs[hs, :]
            vslo_ref[0, g, NSA_DIM:] = ones_rows
            vwno_ref[0, g, :NSA_DIM] = vw[hs, :]
            vwno_ref[0, g, NSA_DIM:] = ones_rows


def _nsa_prep(prec3, rest3, kv_off, gate_off, tseq):
    assert gate_off % LANES == 0 and 3 * NSA_HEADS <= LANES
    bsz, seq, _ = prec3.shape
    half = NSA_DIM // 2
    inv = 1.0 / (ROPE_THETA ** (jnp.arange(0, NSA_DIM, 2, dtype=F32) / NSA_DIM))
    ang = jnp.arange(seq, dtype=F32)[:, None] * inv[None, :]
    cs = jnp.tile(jnp.cos(ang), (1, LANES // half))
    sn = jnp.tile(jnp.concatenate([-jnp.sin(ang), jnp.sin(ang)], axis=1), (1, LANES // NSA_DIM))
    kvb = kv_off // NSA_KV

    def kv_in(i):
        return pl.BlockSpec((1, tseq, NSA_KV), lambda b, t: (b, t, kvb + i))

    assert NSA_DIM + seq // SLC_BLOCK <= LANES

    def k_out(width):
        return (pl.BlockSpec((1, NSA_GROUPS, tseq, width), lambda b, t: (b, 0, t, 0)),
                jax.ShapeDtypeStruct((bsz, NSA_GROUPS, seq, width), BF16))

    (ksl_out, ksl_shape), (kwn_out, kwn_shape) = k_out(LANES), k_out(NSA_DIM)
    v_out = pl.BlockSpec((1, NSA_GROUPS, NSA_DIM + NSA_VPAD, tseq), lambda b, t: (b, 0, 0, t))
    v_shape = jax.ShapeDtypeStruct((bsz, NSA_GROUPS, NSA_DIM + NSA_VPAD, seq), BF16)
    tab = pl.BlockSpec((tseq, LANES), lambda b, t: (t, 0))
    return pl.pallas_call(
        _nsa_prep_kernel,
        grid=(bsz, seq // tseq),
        in_specs=[pl.BlockSpec((1, tseq, NSA_WIDTH), lambda b, t: (b, t, 0)), kv_in(0), kv_in(1), kv_in(2), kv_in(3),
                  pl.BlockSpec((1, tseq, LANES), lambda b, t: (b, t, gate_off // LANES)), tab, tab],
        out_specs=[pl.BlockSpec((1, tseq, NSA_WIDTH), lambda b, t: (b, t, 0)), ksl_out, v_out, kwn_out, v_out,
                   pl.BlockSpec((1, NSA_GROUPS, NSA_GATE_ROWS, tseq), lambda b, t: (b, 0, 0, t))],
        out_shape=[jax.ShapeDtypeStruct((bsz, seq, NSA_WIDTH), BF16), ksl_shape, v_shape, kwn_shape, v_shape,
                   jax.ShapeDtypeStruct((bsz, NSA_GROUPS, NSA_GATE_ROWS, seq), F32)],
        compiler_params=_params(("parallel", "parallel")),
        name="nsa_prep",
    )(prec3, rest3, rest3, rest3, rest3, rest3, cs, sn)


def _compress_kernel(*refs, precise):
    *kv_refs, pos_ref, w1_ref, w2_ref, o_ref = refs
    mm = _dot3 if precise else (lambda a, b: _dot(a.astype(BF16), b.astype(BF16)))
    nu = kv_refs[0].shape[1] // CMP_STRIDE
    per_tile = LANES // NSA_DIM
    for g in range(NSA_GROUPS):
        kv_ref, lanes = kv_refs[g // per_tile], slice((g % per_tile) * NSA_DIM, (g % per_tile + 1) * NSA_DIM)
        u = jnp.concatenate([kv_ref[0, pl.ds(l, nu, stride=CMP_STRIDE), :][:, lanes] for l in range(CMP_STRIDE)],
                            axis=1)
        ya = mm(u + pos_ref[0:1, :], w1_ref[0])
        yb = mm(u + pos_ref[1:2, :], w1_ref[1])
        hid = ya + pltpu.roll(yb, nu - 1, 0)
        o_ref[0, g] = mm(jax.nn.gelu(hid), w2_ref[...])


def _compress(proj3, col, pos, w1, w2, precise):
    bsz, seq, _ = proj3.shape
    nu = seq // CMP_STRIDE
    width = CMP_STRIDE * NSA_DIM
    assert col % LANES == 0
    tiles = NSA_KV // LANES
    return pl.pallas_call(
        functools.partial(_compress_kernel, precise=precise),
        grid=(bsz,),
        in_specs=[pl.BlockSpec((1, seq, LANES), lambda b, t=t: (b, 0, col // LANES + t)) for t in range(tiles)] + [
            pl.BlockSpec((2, width), lambda b: (0, 0)),
            pl.BlockSpec((2, width, CMP_HIDDEN), lambda b: (0, 0, 0)),
            pl.BlockSpec((CMP_HIDDEN, NSA_DIM), lambda b: (0, 0)),
        ],
        out_specs=pl.BlockSpec((1, NSA_GROUPS, nu, NSA_DIM), lambda b: (b, 0, 0, 0)),
        out_shape=jax.ShapeDtypeStruct((bsz, NSA_GROUPS, nu, NSA_DIM), F32),
        compiler_params=_params(("parallel",)),
        name="nsa_compress",
    )(*([proj3] * tiles), pos.reshape(2, width), w1.reshape(2, width, CMP_HIDDEN), w2)


def _nsa_attn_kernel(qraw_ref, qrot_ref, kc_ref, vct_ref, ksl_ref, vslt_ref, kwn_ref, vwnt_ref, gate_ref, aggt_ref,
                     o_ref, sa_ref, sb_ref, *win_refs, topk, tk):
    tq = qraw_ref.shape[1]
    nu = kc_ref.shape[2]
    ns = aggt_ref.shape[0]
    rep, dk = NSA_REP, NSA_DIM
    qs = pl.program_id(2) * tq
    tpos = qs + lax.broadcasted_iota(jnp.int32, (1, tq), 1)

    qrt = (qraw_ref[0] * (dk ** -0.5 * LOG2E)).T
    kc_hi, kc_lo = _split2(kc_ref[0, 0])
    vct = vct_ref[0, 0].astype(BF16)
    crow = lax.broadcasted_iota(jnp.int32, (nu, tq), 0)
    m_c = (crow * CMP_STRIDE + CMP_BLOCK - 1 <= tpos) & (crow < nu - 1)
    q_hi, q_lo = _split2(jnp.concatenate([qrt[r * dk:(r + 1) * dk] for r in range(rep)], axis=1))
    s_all = _dot(jnp.concatenate([kc_hi, kc_hi, kc_lo], axis=1),
                 jnp.concatenate([q_hi, q_lo, q_hi], axis=0))

    qt = qrot_ref[0].astype(F32).T.astype(BF16)
    qt_all = jnp.concatenate([qt[r * dk:(r + 1) * dk] for r in range(rep)], axis=1)
    hi = (qs + tq) // tk

    def key_tile(ktc):
        return pl.ds(pl.multiple_of(ktc * tk, tk), tk)

    def scores(k_ref, ktc, dst_ref):
        dst_ref[...] = _dot(k_ref[0, 0, key_tile(ktc), :], qt_all)

    win_tiles = [hi - len(win_refs) + j for j in range(len(win_refs))]
    for kt, dst_ref in zip(win_tiles, win_refs):
        scores(kwn_ref, jnp.maximum(kt, 0), dst_ref)
    psum = jnp.zeros((nu, tq), F32)
    p_all = []
    has_block = tpos >= CMP_BLOCK - 1
    for r in range(rep):
        s = jnp.where(m_c, s_all[:, r * tq:(r + 1) * tq], NEG)
        e = jnp.exp2(s - jnp.max(s, axis=0, keepdims=True))
        p = e * jnp.where(has_block, 1.0 / jnp.sum(e, axis=0, keepdims=True), 0.0)
        psum = psum + p
        p_all.append(p.astype(BF16))
    o_c_all = _dot(vct, jnp.concatenate(p_all, axis=1))
    o_c = [o_c_all[:, r * tq:(r + 1) * tq] for r in range(rep)]

    p_hi, p_lo = _split2(psum)
    imp = _dot(aggt_ref[...], p_hi) + _dot(aggt_ref[...], p_lo)
    jrow = lax.broadcasted_iota(jnp.int32, (ns, tq), 0)
    dj = jnp.right_shift(tpos, SLC_BLOCK.bit_length() - 1) - jrow
    forced = (jrow == 0) | ((dj >= 0) & (dj < SLC_LOCAL))
    imp = jnp.where(forced, BIG, jnp.where(jrow * SLC_BLOCK <= tpos, imp, -BIG))
    sub8 = lax.broadcasted_iota(jnp.int32, (8, tq), 0)
    chunks = [imp[c * 8:(c + 1) * 8] for c in range(ns // 8)]
    ranks = [jnp.zeros((8, tq), F32) for _ in range(ns // 8)]
    for jp in range(ns):
        row = chunks[jp // 8][jp % 8:jp % 8 + 1]
        for c in range(ns // 8):
            if c < jp // 8:
                ahead = jnp.where(row > chunks[c], 1.0, 0.0)
            elif c > jp // 8:
                ahead = jnp.where(row >= chunks[c], 1.0, 0.0)
            else:
                tie = jnp.where(sub8 > jp % 8, 1.0, 0.0)
                ahead = jnp.where(row > chunks[c], 1.0, jnp.where(row == chunks[c], tie, 0.0))
            ranks[c] = ranks[c] + ahead
    selt = [jnp.where(ranks[c] < topk, 0.0, NEG) for c in range(ns // 8)]

    def consume(vt_ref, ktc, src_ref, carry, mask=None):
        vt = vt_ref[0, 0, :, key_tile(ktc)]
        out = []
        for r in range(rep):
            m_old, acc = carry[r]
            s = src_ref[:, r * tq:(r + 1) * tq]
            if mask is not None:
                s = jnp.where(mask, s, NEG)
            m_new = jnp.maximum(m_old, jnp.max(s, axis=0, keepdims=True))
            alpha = jnp.exp2(m_old - m_new)
            p = jnp.exp2(s - m_new).astype(BF16)
            out.append((m_new, acc * alpha + _dot(vt, p)))
        return tuple(out)

    def normalised(carry):
        return [acc[:dk] * (1.0 / acc[dk:dk + 1]) for _, acc in carry]

    init = tuple((jnp.full((1, tq), NEG, F32), jnp.zeros((dk + NSA_VPAD, tq), F32)) for _ in range(rep))
    krow = lax.broadcasted_iota(jnp.int32, (tk, tq), 0)
    unseen = 1 << 30

    carry = init
    for kt, src_ref in zip(win_tiles, win_refs):
        ktc = jnp.maximum(kt, 0)
        dlt = tpos - (jnp.where(kt >= 0, ktc * tk, unseen) + krow)
        carry = consume(vwnt_ref, ktc, src_ref, carry, mask=pltpu.bitcast(dlt, jnp.uint32) < jnp.uint32(WIN))
    o_w = normalised(carry)

    selb = jnp.concatenate([jnp.concatenate(selt, axis=0)] * rep, axis=1).astype(BF16)
    pad = jnp.zeros((LANES - dk - ns, rep * tq), BF16)
    qt_sel = jnp.concatenate([qt_all, selb] + ([pad] if LANES > dk + ns else []), axis=0)

    def sel_scores(ktc, dst_ref):
        dst_ref[...] = _dot(ksl_ref[0, 0, key_tile(ktc), :], qt_sel)

    def pair(i, carry):
        kt = 2 * i
        sel_scores(kt + 1, sb_ref)
        carry = consume(vslt_ref, kt, sa_ref, carry)
        sel_scores(kt + 2, sa_ref)
        return consume(vslt_ref, kt + 1, sb_ref, carry)

    past = hi - 1
    sel_scores(0, sa_ref)
    carry = lax.fori_loop(0, past // 2, pair, init)
    carry = lax.cond(past % 2 == 1, lambda c: consume(vslt_ref, past - 1, sa_ref, c), lambda c: c, carry)
    sel_scores(past, sb_ref)
    carry = consume(vslt_ref, past, sb_ref, carry, mask=past * tk + krow <= tpos)
    o_s = normalised(carry)

    gate = jax.nn.sigmoid(gate_ref[0, 0])
    o_t = [gate[3 * r:3 * r + 1] * o_c[r] + gate[3 * r + 1:3 * r + 2] * o_s[r] + gate[3 * r + 2:3 * r + 3] * o_w[r]
           for r in range(rep)]
    o_ref[0] = jnp.concatenate(o_t, axis=0).T.astype(o_ref.dtype)


def _nsa_attn(prec3, qrot, kc, vct, ksl, vslt, kwn, vwnt, gates_t, tq, tk):
    bsz, seq, _ = qrot.shape
    nu = seq // CMP_STRIDE
    ns = seq // SLC_BLOCK
    ci = np.arange(nu)[None, :]
    sj = np.arange(ns)[:, None]
    overlap = (ci * CMP_STRIDE < (sj + 1) * SLC_BLOCK) & (ci * CMP_STRIDE + CMP_BLOCK > sj * SLC_BLOCK) & (ci < nu - 1)
    aggt = jnp.asarray(overlap, BF16)
    gw = NSA_REP * NSA_DIM
    assert tq % tk == 0 and seq % tq == 0
    win_tiles = -(-(WIN - 1) // tk) + tq // tk

    def q_spec():
        return pl.BlockSpec((1, tq, gw), lambda b, g, t: (b, t, g))

    def per_group(rows, cols):
        return pl.BlockSpec((1, 1, rows, cols), lambda b, g, t: (b, g, 0, 0))

    return pl.pallas_call(
        functools.partial(_nsa_attn_kernel, topk=min(SLC_TOPK, ns), tk=tk),
        grid=(bsz, NSA_GROUPS, seq // tq),
        in_specs=[q_spec(), q_spec(), per_group(nu, NSA_DIM), per_group(NSA_DIM, nu),
                  per_group(seq, LANES), per_group(NSA_DIM + NSA_VPAD, seq),
                  per_group(seq, NSA_DIM), per_group(NSA_DIM + NSA_VPAD, seq),
                  pl.BlockSpec((1, 1, NSA_GATE_ROWS, tq), lambda b, g, t: (b, g, 0, t)),
                  pl.BlockSpec((ns, nu), lambda b, g, t: (0, 0))],
        out_specs=q_spec(),
        out_shape=jax.ShapeDtypeStruct((bsz, seq, NSA_WIDTH), BF16),
        scratch_shapes=[pltpu.VMEM((tk, NSA_REP * tq), F32)] * (2 + win_tiles),
        compiler_params=_params(("parallel", "parallel", "arbitrary")),
        name="nsa_attention",
    )(prec3, qrot, kc, vct, ksl, vslt, kwn, vwnt, gates_t, aggt)


def _nsa(prec3, rest3, kv_off, gate_off, cmp_k, cmp_v):
    bsz, seq, _ = prec3.shape
    nu = seq // CMP_STRIDE
    qrot, ksl, vslt, kwn, vwnt, gates_t = _nsa_prep(prec3, rest3, kv_off + NSA_KV, gate_off, min(seq, 512))
    kc = _compress(prec3, NSA_WIDTH, *cmp_k, precise=True)
    vct = _compress(rest3, kv_off, *cmp_v, precise=False).transpose(0, 1, 3, 2)
    return _nsa_attn(prec3, qrot, kc, vct, ksl, vslt, kwn, vwnt, gates_t, 256, 256)


def _out_proj_kernel(yh_ref, yn_ref, g_ref, w_ref, x_ref, o_ref, y_ref):
    @pl.when(pl.program_id(1) == 0)
    def _():
        wh = yh_ref.shape[1]
        y_ref[:, :wh] = yh_ref[...]
        y_ref[:, wh:] = _rms(yn_ref[...].astype(F32), g_ref[...]).astype(BF16)

    o_ref[...] = x_ref[...] + _dot(y_ref[...], w_ref[...])


def _out_proj(y_hg, y_nsa, nsa_gain, w_out, x, tm, tn):
    m, dm = x.shape
    wh, wn = y_hg.shape[1], y_nsa.shape[1]
    return pl.pallas_call(
        _out_proj_kernel,
        grid=(m // tm, dm // tn),
        in_specs=[
            pl.BlockSpec((tm, wh), lambda i, j: (i, 0)),
            pl.BlockSpec((tm, wn), lambda i, j: (i, 0)),
            pl.BlockSpec((1, wn), lambda i, j: (0, 0)),
            pl.BlockSpec((None, wh + wn, tn), lambda i, j: (j, 0, 0)),
            pl.BlockSpec((tm, tn), lambda i, j: (i, j)),
        ],
        out_specs=pl.BlockSpec((tm, tn), lambda i, j: (i, j)),
        out_shape=jax.ShapeDtypeStruct((m, dm), F32),
        scratch_shapes=[pltpu.VMEM((tm, wh + wn), BF16)],
        compiler_params=_params(("parallel", "arbitrary")),
        name="out_proj",
    )(y_hg, y_nsa, nsa_gain.reshape(1, wn), _column_blocks(w_out.astype(BF16), tn), x)


MOE_ROWS = 256


def _pack_bf16_pairs(hi):
    n = hi.shape[1] // 2
    bits = pltpu.bitcast(hi.astype(F32), jnp.uint32)
    return jnp.right_shift(bits[:, :n], jnp.uint32(16)) | (bits[:, n:] & jnp.uint32(0xFFFF0000))


def _unpack_f32_pairs(words):
    lo = pltpu.bitcast(jnp.left_shift(words, jnp.uint32(16)), F32)
    hi = pltpu.bitcast(words & jnp.uint32(0xFFFF0000), F32)
    return lo, hi


def _unpack_bf16_pairs(words):
    lo, hi = _unpack_f32_pairs(words)
    return lo.astype(BF16), hi.astype(BF16)


def _router_kernel(x_ref, g_ref, w_ref, b_ref, tri_ref, xn_ref, info_ref, infot_ref, cnt_ref, carry_ref):
    @pl.when(pl.program_id(0) == 0)
    def _():
        carry_ref[...] = jnp.zeros_like(carry_ref)

    xn = _rms(x_ref[...], g_ref[...])
    hi, lo = _split2(xn)
    xn_ref[...] = _pack_bf16_pairs(hi)
    logits = _dot(hi, w_ref[0]) + (_dot(hi, w_ref[1]) + _dot(lo, w_ref[0])) + b_ref[...]
    lane = lax.broadcasted_iota(jnp.int32, logits.shape, 1).astype(F32)
    none = float(LANES)

    def first_max(mask):
        top = jnp.max(jnp.where(mask, logits, -jnp.inf), axis=-1, keepdims=True)
        return top, jnp.min(jnp.where(mask & (logits == top), lane, none), axis=-1, keepdims=True)

    is_g = lane < MOE_GROUPS
    gmax, gsel = first_max(is_g)
    gw = 1.0 / jnp.sum(jnp.where(is_g, jnp.exp(logits - gmax), 0.0), axis=-1, keepdims=True)
    lo_lane = MOE_GROUPS + gsel * MOE_EPG
    in_grp = (lane >= lo_lane) & (lane < lo_lane + MOE_EPG)
    v1, i1 = first_max(in_grp)
    v2, i2 = first_max(in_grp & (lane != i1))
    e = jnp.exp(v2 - v1)
    w1 = gw / (1.0 + e)
    w2 = gw * e / (1.0 + e)
    e1 = i1 - MOE_GROUPS
    e2 = i2 - MOE_GROUPS
    onehot = jnp.where((lane == e1) | (lane == e2), 1.0, 0.0)
    before = _dot(tri_ref[...], onehot.astype(BF16)) + carry_ref[...]
    r1 = jnp.sum(jnp.where(lane == e1, before, 0.0), axis=-1, keepdims=True)
    r2 = jnp.sum(jnp.where(lane == e2, before, 0.0), axis=-1, keepdims=True)
    carry_ref[...] = carry_ref[...] + jnp.sum(onehot, axis=0, keepdims=True)
    cnt_ref[...] = carry_ref[...]
    info = jnp.zeros_like(logits)
    for idx, val in enumerate((e1, e2, w1, w2, r1, r2)):
        info = jnp.where(lane == idx, val, info)
    info_ref[...] = info
    infot_ref[...] = info.T[:8]


def _router(x2, gain, w_group, b_group, w_router, b_router, tm):
    m, dm = x2.shape
    wcat = jnp.pad(jnp.concatenate([w_group, w_router], axis=1), ((0, 0), (0, LANES - MOE_GROUPS - N_EXPERTS)))
    bcat = jnp.pad(jnp.concatenate([b_group, b_router]), (0, LANES - MOE_GROUPS - N_EXPERTS)).reshape(1, LANES)
    tri = jnp.asarray(np.tril(np.ones((tm, tm), np.float32), -1), BF16)
    return pl.pallas_call(
        _router_kernel,
        grid=(m // tm,),
        in_specs=[
            pl.BlockSpec((tm, dm), lambda i: (i, 0)),
            pl.BlockSpec((1, dm), lambda i: (0, 0)),
            pl.BlockSpec((2, dm, LANES), lambda i: (0, 0, 0)),
            pl.BlockSpec((1, LANES), lambda i: (0, 0)),
            pl.BlockSpec((tm, tm), lambda i: (0, 0)),
        ],
        out_specs=[
            pl.BlockSpec((tm, dm // 2), lambda i: (i, 0)),
            pl.BlockSpec((tm, LANES), lambda i: (i, 0)),
            pl.BlockSpec((8, tm), lambda i: (0, i)),
            pl.BlockSpec((1, LANES), lambda i: (0, 0)),
        ],
        out_shape=[
            jax.ShapeDtypeStruct((m, dm // 2), jnp.uint32),
            jax.ShapeDtypeStruct((m, LANES), F32),
            jax.ShapeDtypeStruct((8, m), F32),
            jax.ShapeDtypeStruct((1, LANES), F32),
        ],
        scratch_shapes=[pltpu.VMEM((1, LANES), F32)],
        compiler_params=_params(("arbitrary",)),
        name="moe_router",
    )(x2, gain.reshape(1, dm), jnp.stack(_split2(wcat)), bcat, tri)


def _dispatch_kernel(d1_ref, d2_ref, xn_ref, xs_in_ref, xs_ref, sem):
    del xs_in_ref
    tm = xn_ref.shape[0]
    base = pl.program_id(0) * tm

    def row_copy(i, dest):
        return pltpu.make_async_copy(xn_ref.at[pl.ds(i, 1)], xs_ref.at[pl.ds(dest, 1)], sem)

    def issue(i, c):
        row_copy(i, d1_ref[base + i]).start()
        row_copy(i, d2_ref[base + i]).start()
        return c

    lax.fori_loop(0, tm, issue, 0, unroll=8)
    for _ in range(2):
        pltpu.make_async_copy(xn_ref, xs_ref.at[pl.ds(0, tm)], sem).wait()


def _dispatch(xn, d1, d2, rows, tm):
    m, words = xn.shape
    return pl.pallas_call(
        _dispatch_kernel,
        grid_spec=pltpu.PrefetchScalarGridSpec(
            num_scalar_prefetch=2,
            grid=(m // tm,),
            in_specs=[pl.BlockSpec((tm, words), lambda i, *_: (i, 0)), pl.BlockSpec(memory_space=pl.ANY)],
            out_specs=pl.BlockSpec(memory_space=pl.ANY),
            scratch_shapes=[pltpu.SemaphoreType.DMA(())],
        ),
        out_shape=jax.ShapeDtypeStruct((rows, words), jnp.uint32),
        input_output_aliases={3: 0},
        compiler_params=_params(("arbitrary",)),
        name="moe_dispatch",
    )(d1, d2, xn, jnp.zeros((rows, words), jnp.uint32))


def _expert_kernel(be_ref, nused_ref, x_ref, wg_ref, wu_ref, wd_ref, o_ref, wgb_ref, wub_ref, wdb_ref):
    i = pl.program_id(0)

    @pl.when((i == 0) | (be_ref[i] != be_ref[jnp.maximum(i - 1, 0)]))
    def _():
        wgb_ref[...] = wg_ref[0].astype(BF16)
        wub_ref[...] = wu_ref[0].astype(BF16)
        wdb_ref[...] = wd_ref[0].astype(BF16)

    @pl.when(i < nused_ref[0])
    def _():
        half = wgb_ref.shape[0] // 2
        x_lo, x_hi = _unpack_bf16_pairs(x_ref[...])

        def up(w_ref):
            return _dot(x_lo, w_ref[:half]) + _dot(x_hi, w_ref[half:])

        hid = (jax.nn.silu(up(wgb_ref)) * up(wub_ref)).astype(BF16)
        for g in range(o_ref.shape[1] // LANES):
            y = _dot(hid, wdb_ref[:, g * 2 * LANES:(g + 1) * 2 * LANES])
            o_ref[:, g * LANES:(g + 1) * LANES] = _pack_bf16_pairs(y.astype(BF16))

    @pl.when(i >= nused_ref[0])
    def _():
        o_ref[...] = jnp.zeros_like(o_ref)


def _experts(xs, block_e, nused, w_gate, w_up, w_down):
    rows, words = xs.shape
    _, dm, ff = w_gate.shape
    return pl.pallas_call(
        _expert_kernel,
        grid_spec=pltpu.PrefetchScalarGridSpec(
            num_scalar_prefetch=2,
            grid=(rows // MOE_ROWS,),
            in_specs=[
                pl.BlockSpec((MOE_ROWS, words), lambda i, be, nu: (i, 0)),
                pl.BlockSpec((1, dm, ff), lambda i, be, nu: (be[i], 0, 0)),
                pl.BlockSpec((1, dm, ff), lambda i, be, nu: (be[i], 0, 0)),
                pl.BlockSpec((1, ff, dm), lambda i, be, nu: (be[i], 0, 0)),
            ],
            out_specs=pl.BlockSpec((MOE_ROWS, words), lambda i, be, nu: (i, 0)),
            scratch_shapes=[pltpu.VMEM((dm, ff), BF16), pltpu.VMEM((dm, ff), BF16), pltpu.VMEM((ff, dm), BF16)],
        ),
        out_shape=jax.ShapeDtypeStruct((rows, words), jnp.uint32),
        compiler_params=_params(("arbitrary",)),
        name="moe_experts",
    )(block_e, nused, xs, w_gate, w_up, w_down)


def _combine_kernel(d1_ref, d2_ref, x_ref, info_ref, g_ref, ys_ref, o_ref, buf_a, buf_b, sem_a, sem_b):
    ts = buf_a.shape[1]
    dm = x_ref.shape[1]
    step, nsteps = pl.program_id(0), pl.num_programs(0)
    base = step * 2 * ts

    def row_copy(buf_ref, sem, r, slot, src):
        return pltpu.make_async_copy(ys_ref.at[pl.ds(src, 1)], buf_ref.at[slot, pl.ds(r, 1)], sem)

    def issue(buf_ref, sem, tok0):
        for r in range(ts):
            row_copy(buf_ref, sem, r, 0, d1_ref[tok0 + r]).start()
            row_copy(buf_ref, sem, r, 1, d2_ref[tok0 + r]).start()

    def wait(buf_ref, sem):
        for slot in range(2):
            pltpu.make_async_copy(ys_ref.at[pl.ds(0, ts)], buf_ref.at[slot], sem).wait()

    def finish(buf_ref, rows):
        info = info_ref[rows, :]
        w1, w2 = info[:, 2:3], info[:, 3:4]
        ssq = jnp.zeros((ts, 1), F32)
        for g in range(dm // (2 * LANES)):
            a = _unpack_f32_pairs(buf_ref[0, :, g * LANES:(g + 1) * LANES])
            b = _unpack_f32_pairs(buf_ref[1, :, g * LANES:(g + 1) * LANES])
            for half in range(2):
                cols = slice((2 * g + half) * LANES, (2 * g + half + 1) * LANES)
                y = x_ref[rows, cols] + (w1 * a[half] + w2 * b[half])
                ssq = ssq + jnp.sum(y * y, axis=-1, keepdims=True)
                o_ref[rows, cols] = y
        o_ref[rows, :] = o_ref[rows, :] * lax.rsqrt(ssq / dm + EPS) * g_ref[...]

    @pl.when(step == 0)
    def _():
        issue(buf_a, sem_a, base)

    wait(buf_a, sem_a)
    issue(buf_b, sem_b, base + ts)
    finish(buf_a, slice(0, ts))
    wait(buf_b, sem_b)
    issue(buf_a, sem_a, jnp.where(step + 1 < nsteps, base + 2 * ts, base))
    finish(buf_b, slice(ts, 2 * ts))

    @pl.when(step + 1 == nsteps)
    def _():
        wait(buf_a, sem_a)


def _combine(x2, info, gain, ys, d1, d2, ts):
    m, dm = x2.shape
    tm = 2 * ts
    gather_buf = pltpu.VMEM((2, ts, dm // 2), jnp.uint32)
    return pl.pallas_call(
        _combine_kernel,
        grid_spec=pltpu.PrefetchScalarGridSpec(
            num_scalar_prefetch=2,
            grid=(m // tm,),
            in_specs=[
                pl.BlockSpec((tm, dm), lambda i, *_: (i, 0)),
                pl.BlockSpec((tm, LANES), lambda i, *_: (i, 0)),
                pl.BlockSpec((1, dm), lambda i, *_: (0, 0)),
                pl.BlockSpec(memory_space=pl.ANY),
            ],
            out_specs=pl.BlockSpec((tm, dm), lambda i, *_: (i, 0)),
            scratch_shapes=[gather_buf, gather_buf, pltpu.SemaphoreType.DMA(()), pltpu.SemaphoreType.DMA(())],
        ),
        out_shape=jax.ShapeDtypeStruct((m, dm), F32),
        compiler_params=_params(("arbitrary",)),
        name="moe_combine",
    )(d1, d2, x2, info, gain.reshape(1, dm), ys)


def _moe_and_final_norm(x2, ffn_gain, w_group, b_group, w_router, b_router, w_gate, w_up, w_down, final_gain, tm):
    m, _ = x2.shape
    xn, info, infot, cnt = _router(x2, ffn_gain, w_group, b_group, w_router, b_router, tm)
    counts = cnt[0, :N_EXPERTS].astype(jnp.int32)
    padded = (counts + MOE_ROWS - 1) // MOE_ROWS * MOE_ROWS
    pend = jnp.cumsum(padded)
    pstart = pend - padded
    fields = infot.astype(jnp.int32)
    d1 = pstart[fields[0]] + fields[4]
    d2 = pstart[fields[1]] + fields[5]
    nblocks = 2 * m // MOE_ROWS + N_EXPERTS
    first_row = jnp.arange(nblocks, dtype=jnp.int32) * MOE_ROWS
    block_e = jnp.minimum(jnp.sum(pend[None, :] <= first_row[:, None], axis=1), N_EXPERTS - 1).astype(jnp.int32)
    nused = (pend[-1:] // MOE_ROWS).astype(jnp.int32)
    xs = _dispatch(xn, d1, d2, nblocks * MOE_ROWS, tm)
    ys = _experts(xs, block_e, nused, w_gate, w_up, w_down)
    return _combine(x2, info, final_gain, ys, d1, d2, tm)


def kernel(x, attn_norm, w_in, hg_lb_logits, hg_out_norm, cmp_pos_k, cmp_w1_k, cmp_w2_k, cmp_pos_v, cmp_w1_v,
           cmp_w2_v, nsa_out_norm, w_out, ffn_norm, moe_w_group, moe_b_group, moe_w_router, moe_b_router,
           moe_w_gate, moe_w_up, moe_w_down, final_norm):
    bsz, seq, dm = x.shape
    xt = x.reshape(bsz * seq, dm)
    w = w_in[0]
    p0, p1 = 4 * HG_QK, 4 * HG_QK + NSA_WIDTH + NSA_KV
    tn = 256
    assert p0 % tn == 0 and p1 % tn == 0
    w_all = _column_blocks(jnp.pad(w.astype(BF16), ((0, 0), (0, (-w.shape[1]) % tn))), tn)
    w_prec = _column_blocks(jnp.stack(_split2(w[:, p0:p1])), tn)
    proj3 = _normed_matmul(xt, attn_norm[0], w_prec, w_all, p0 // tn, 1024).reshape(bsz, seq, -1)
    hg0 = p1 - p0
    y_hg = _hgrn(proj3, hg0, hg_lb_logits, hg_out_norm[0], 1024)
    y_nsa = _nsa(proj3, proj3, hg0 + 4 * HG_QK, hg0 + 4 * HG_QK + 5 * NSA_KV,
                 (cmp_pos_k[0], cmp_w1_k[0], cmp_w2_k[0]), (cmp_pos_v[0], cmp_w1_v[0], cmp_w2_v[0]))
    x2 = _out_proj(y_hg.reshape(bsz * seq, -1), y_nsa.reshape(bsz * seq, -1), nsa_out_norm[0], w_out[0], xt, 1024, 512)
    out = _moe_and_final_norm(x2, ffn_norm[0], moe_w_group[0], moe_b_group[0], moe_w_router[0], moe_b_router[0],
                              moe_w_gate[0], moe_w_up[0], moe_w_down[0], final_norm, 256)
    return out.reshape(bsz, seq, dm)
```

```python
import functools

import jax
import jax.numpy as jnp
import numpy as np
from jax import lax
from jax.experimental import pallas as pl
from jax.experimental.pallas import tpu as pltpu

F32 = jnp.float32
BF16 = jnp.bfloat16

EPS = 1e-6
ROPE_THETA = 10000.0
NEG = -1e30
BIG = 1e9
LOG2E = 1.4426950408889634

HG_HEADS = 8
HG_DIM = 128
HG_QK = HG_HEADS * HG_DIM
HG_CHUNK = 64
HG_SUB = 8

NSA_HEADS = 16
NSA_GROUPS = 4
NSA_REP = NSA_HEADS // NSA_GROUPS
NSA_DIM = 64
NSA_WIDTH = NSA_HEADS * NSA_DIM
NSA_KV = NSA_GROUPS * NSA_DIM
NSA_VPAD = 16
NSA_GATE_ROWS = 16
CMP_BLOCK = 32
CMP_STRIDE = 16
CMP_HIDDEN = 256
SLC_BLOCK = 64
SLC_TOPK = 16
SLC_LOCAL = 2
WIN = 512

MOE_GROUPS = 4
MOE_EPG = 8
N_EXPERTS = MOE_GROUPS * MOE_EPG
EXPERT_FF = 512

LANES = 128
VMEM_LIMIT = 56 * 1024 * 1024


def _params(semantics, **kw):
    return pltpu.CompilerParams(dimension_semantics=semantics, vmem_limit_bytes=VMEM_LIMIT, **kw)


def _split2(a):
    hi = a.astype(BF16)
    return hi, (a - hi.astype(F32)).astype(BF16)


def _split3(a):
    hi = a.astype(BF16)
    r = a - hi.astype(F32)
    mid = r.astype(BF16)
    return hi, mid, (r - mid.astype(F32)).astype(BF16)


def _dot(a, b):
    return jnp.dot(a, b, preferred_element_type=F32)


def _dot_nt(a, b):
    return lax.dot_general(a, b, (((1,), (1,)), ((), ())), preferred_element_type=F32)


def _dot3(a, b):
    a_hi, a_lo = _split2(a)
    b_hi, b_lo = _split2(b)
    return _dot(a_hi, b_hi) + (_dot(a_hi, b_lo) + _dot(a_lo, b_hi))


def _dot3_nt(a, b):
    a_hi, a_lo = _split2(a)
    b_hi, b_lo = _split2(b)
    return _dot_nt(a_hi, b_hi) + (_dot_nt(a_hi, b_lo) + _dot_nt(a_lo, b_hi))


def _rms(x, gain):
    return x * lax.rsqrt(jnp.mean(x * x, axis=-1, keepdims=True) + EPS) * gain


def _normed_matmul_kernel(x_ref, g_ref, wp_ref, w_ref, o_ref, h_ref, *, nprec):
    j = pl.program_id(1)

    @pl.when(j == 0)
    def _():
        y = _rms(x_ref[...], g_ref[...])
        hi = y.astype(BF16)
        h_ref[0] = hi
        h_ref[1] = (y - hi.astype(F32)).astype(BF16)

    @pl.when(j < nprec)
    def _():
        o_ref[...] = _dot(h_ref[0], wp_ref[0]) + (_dot(h_ref[0], wp_ref[1]) + _dot(h_ref[1], wp_ref[0]))

    @pl.when(j >= nprec)
    def _():
        o_ref[...] = _dot(h_ref[0], w_ref[...])


def _column_blocks(w, tn):
    *lead, k, n = w.shape
    return jnp.moveaxis(w.reshape(*lead, k, n // tn, tn), -2, -3)


def _normed_matmul(x, gain, wp_blocks, w_blocks, first, tm):
    m, k = x.shape
    nb, _, tn = w_blocks.shape
    nprec = wp_blocks.shape[1]

    def single_pass_block(j):
        r = jnp.maximum(j - nprec, 0)
        return jnp.where(r < first, r, r + nprec)

    return pl.pallas_call(
        functools.partial(_normed_matmul_kernel, nprec=nprec),
        grid=(m // tm, nb),
        in_specs=[
            pl.BlockSpec((tm, k), lambda i, j: (i, 0)),
            pl.BlockSpec((1, k), lambda i, j: (0, 0)),
            pl.BlockSpec((2, None, k, tn), lambda i, j: (0, jnp.minimum(j, nprec - 1), 0, 0)),
            pl.BlockSpec((None, k, tn), lambda i, j: (single_pass_block(j), 0, 0)),
        ],
        out_specs=pl.BlockSpec((tm, tn), lambda i, j: (i, j)),
        out_shape=jax.ShapeDtypeStruct((m, nb * tn), F32),
        scratch_shapes=[pltpu.VMEM((2, tm, k), BF16)],
        compiler_params=_params(("parallel", "arbitrary")),
        name="normed_matmul",
    )(x, gain.reshape(1, k), wp_blocks, w_blocks)


def _hgrn_consts():
    c, sub = HG_CHUNK, HG_SUB
    tri = np.tile(np.tril(np.ones((c, c), np.float32)), (1, 3))
    gsum = (np.arange(c * sub)[None, :] // sub == np.arange(c)[:, None]).astype(np.float32)
    return jnp.asarray(tri, BF16), jnp.asarray(gsum, BF16)


HG_PAR = 2


def _hgrn_kernel(q_ref, f_ref, i_ref, g_ref, lbl_ref, gain_ref, tri_ref, gsum_ref, o_ref, *scratch):
    c, sub, d = HG_CHUNK, HG_SUB, HG_DIM
    nsub = c // sub
    heads = range(HG_PAR)
    st_refs, p_refs, cl_refs, qs_refs, k_refs = (scratch[i * HG_PAR:(i + 1) * HG_PAR] for i in range(5))

    @pl.when(pl.program_id(2) == 0)
    def _():
        for h in heads:
            st_refs[h][...] = jnp.zeros_like(st_refs[h])

    l0 = lbl_ref[0:1, :]
    l1 = lbl_ref[1:2, :]
    lmax = jnp.maximum(l0, l1)
    e0 = jnp.exp(l0 - lmax)
    lb_all = e0 / (e0 + jnp.exp(l1 - lmax))
    srow = lax.broadcasted_iota(jnp.int32, (sub, d), 0)
    ones = jnp.ones((d, d), BF16)

    def rows_at(x, start):
        parts = ([jnp.zeros((start, d), F32)] if start else []) + [x]
        if start + x.shape[0] < c:
            parts.append(jnp.zeros((c - start - x.shape[0], d), F32))
        return jnp.concatenate(parts, axis=0)

    nchunks = q_ref.shape[1] // c
    cols = [slice(h * d, (h + 1) * d) for h in heads]

    def load(ci):
        rows = pl.ds(pl.multiple_of(ci * c, c), c)
        out = []
        for h in heads:
            lb = lb_all[:, cols[h]]
            f = lb + (1.0 - lb) * jax.nn.sigmoid(f_ref[0, rows, cols[h]])
            bcum = _dot(tri_ref[...], jnp.concatenate(_split3(jnp.log(f)), axis=0))
            out.append((q_ref[0, rows, cols[h]] * (d ** -0.5), 1.0 - f, i_ref[0, rows, cols[h]], bcum))
        return tuple(out)

    def chunk(ci, cur):
        nxt = load(jnp.minimum(ci + 1, nchunks - 1))
        rows = pl.ds(pl.multiple_of(ci * c, c), c)
        q, k, v, bcum = ([cur[h][i] for h in heads] for i in range(4))
        a_off, o_inter = [], []
        for h in heads:
            b = bcum[h]
            edge = [b[i * sub - 1:i * sub] for i in range(1, nsub + 1)]
            cl = b - jnp.concatenate([jnp.zeros((sub, d), F32)] + [jnp.broadcast_to(e, (sub, d)) for e in edge[:-1]],
                                     axis=0)
            cl_refs[h][...] = cl * LOG2E
            qs_refs[h][...] = q[h]
            k_refs[h][...] = k[h]
            qe = q[h] * jnp.exp(cl)
            qcat, kcat = [], []
            for i in range(1, nsub):
                qcat.append(rows_at(qe[i * sub:(i + 1) * sub], i * sub))
                kcat.append(rows_at(k[h][:i * sub] * jnp.exp(edge[i - 1] - b[:i * sub]), 0))
            a_off.append(_dot_nt(jnp.concatenate(qcat, axis=1).astype(BF16),
                                 jnp.concatenate(kcat, axis=1).astype(BF16)))
            st = st_refs[h][...]
            o_inter.append(_dot_nt((q[h] * jnp.exp(b)).astype(BF16), st.astype(BF16)))
            kd = k[h] * jnp.exp(edge[-1] - b)
            st_refs[h][...] = st * jnp.exp(edge[-1]) + _dot(v[h].T.astype(BF16), kd.astype(BF16))
        r2 = []
        for h in heads:
            for t in range(c):
                j0 = (t // sub) * sub
                dlt = cl_refs[h][t:t + 1, :] - cl_refs[h][j0:j0 + sub, :]
                e = jnp.exp2(jnp.where(srow <= t - j0, dlt, NEG))
                p_refs[h][t * sub:(t + 1) * sub, :] = (qs_refs[h][t:t + 1, :] * k_refs[h][j0:j0 + sub, :] * e).astype(BF16)
            r2.append(_dot(p_refs[h][...], ones))
        o = []
        for h in heads:
            o.append(o_inter[h] + _dot(a_off[h].astype(BF16), v[h].astype(BF16)))
        for h in heads:
            x = r2[h].reshape(nsub, sub, sub, d) * v[h].reshape(nsub, 1, sub, d)
            o[h] = o[h] + _dot(gsum_ref[...], x.reshape(c * sub, d).astype(BF16))
        for h in heads:
            gate = jax.nn.silu(g_ref[0, rows, cols[h]])
            o_ref[0, rows, cols[h]] = (_rms(o[h], gain_ref[...]) * gate).astype(o_ref.dtype)
        return nxt

    lax.fori_loop(0, nchunks, chunk, load(0), unroll=8)


def _hgrn(proj3, col0, lb_logits, out_gain, tseq):
    bsz, seq, _ = proj3.shape
    d, c, sub = HG_DIM, HG_CHUNK, HG_SUB
    wst, gsum = _hgrn_consts()
    groups = HG_HEADS // HG_PAR
    width = HG_PAR * d
    assert col0 % width == 0

    def col(off):
        return pl.BlockSpec((1, tseq, width), lambda b, h, t: (b, t, col0 // width + off * groups + h))

    per_head = [pltpu.VMEM((d, d), F32), pltpu.VMEM((c * sub, d), BF16), pltpu.VMEM((c, d), F32),
                pltpu.VMEM((c, d), F32), pltpu.VMEM((c, d), F32)]
    return pl.pallas_call(
        _hgrn_kernel,
        grid=(bsz, groups, seq // tseq),
        in_specs=[
            col(0), col(1), col(2), col(3),
            pl.BlockSpec((2, width), lambda b, h, t: (0, h)),
            pl.BlockSpec((1, d), lambda b, h, t: (0, 0)),
            pl.BlockSpec(wst.shape, lambda b, h, t: (0, 0)),
            pl.BlockSpec(gsum.shape, lambda b, h, t: (0, 0)),
        ],
        out_specs=pl.BlockSpec((1, tseq, width), lambda b, h, t: (b, t, h)),
        out_shape=jax.ShapeDtypeStruct((bsz, seq, HG_QK), BF16),
        scratch_shapes=[s for s in per_head for _ in range(HG_PAR)],
        compiler_params=_params(("parallel", "parallel", "arbitrary")),
        name="hgrn2",
    )(proj3, proj3, proj3, proj3, lb_logits, out_gain.reshape(1, d), wst, gsum)


def _rope(x, cs, sn):
    lane = lax.broadcasted_iota(jnp.int32, x.shape, 1)
    partner = jnp.where(lane % NSA_DIM < NSA_DIM // 2, pltpu.roll(x, LANES - NSA_DIM // 2, 1),
                        pltpu.roll(x, NSA_DIM // 2, 1))
    return x * cs + partner * sn


def _nsa_prep_kernel(q_ref, ksl_ref, vsl_ref, kwn_ref, vwn_ref, gate_ref, cs_ref, sn_ref,
                     qrot_ref, kslo_ref, vslo_ref, kwno_ref, vwno_ref, gateo_ref):
    cs = cs_ref[...]
    sn = sn_ref[...]
    gate_t = gate_ref[0].T
    per_group = 3 * NSA_REP
    gateo_ref[...] = jnp.zeros_like(gateo_ref)
    for g in range(NSA_GROUPS):
        gateo_ref[0, g, :per_group] = gate_t[g * per_group:(g + 1) * per_group]
    for c in range(NSA_WIDTH // LANES):
        cols = slice(c * LANES, (c + 1) * LANES)
        qrot_ref[0, :, cols] = (_rope(q_ref[0, :, cols], cs, sn) * (NSA_DIM ** -0.5 * LOG2E)).astype(BF16)
    tseq = q_ref.shape[1]
    lane = lax.broadcasted_iota(jnp.int32, (tseq, LANES), 1)
    block = (pl.program_id(1) * tseq + lax.broadcasted_iota(jnp.int32, (tseq, LANES), 0)) // SLC_BLOCK
    block_onehot = jnp.where(lane - NSA_DIM == block, 1.0, 0.0)
    ones_rows = jnp.where(lax.broadcasted_iota(jnp.int32, (NSA_VPAD, tseq), 0) == 0, 1.0, 0.0).astype(BF16)
    for c in range(NSA_KV // LANES):
        cols = slice(c * LANES, (c + 1) * LANES)
        ks = _rope(ksl_ref[0, :, cols], cs, sn)
        kw = _rope(kwn_ref[0, :, cols], cs, sn).astype(BF16)
        vs = vsl_ref[0, :, cols].T.astype(BF16)
        vw = vwn_ref[0, :, cols].T.astype(BF16)
        for half in range(LANES // NSA_DIM):
            g = c * (LANES // NSA_DIM) + half
            hs = slice(half * NSA_DIM, (half + 1) * NSA_DIM)
            ks_g = ks if half == 0 else pltpu.roll(ks, NSA_DIM, 1)
            kslo_ref[0, g] = jnp.where(lane < NSA_DIM, ks_g, block_onehot).astype(BF16)
            kwno_ref[0, g] = kw[:, hs]
            vslo_ref[0, g, :NSA_DIM] = vs[hs, :]
            vslo_ref[0, g, NSA_DIM:] = ones_rows
            vwno_ref[0, g, :NSA_DIM] = vw[hs, :]
            vwno_ref[0, g, NSA_DIM:] = ones_rows


def _nsa_prep(prec3, rest3, kv_off, gate_off, tseq):
    assert gate_off % LANES == 0 and 3 * NSA_HEADS <= LANES
    bsz, seq, _ = prec3.shape
    half = NSA_DIM // 2
    inv = 1.0 / (ROPE_THETA ** (jnp.arange(0, NSA_DIM, 2, dtype=F32) / NSA_DIM))
    ang = jnp.arange(seq, dtype=F32)[:, None] * inv[None, :]
    cs = jnp.tile(jnp.cos(ang), (1, LANES // half))
    sn = jnp.tile(jnp.concatenate([-jnp.sin(ang), jnp.sin(ang)], axis=1), (1, LANES // NSA_DIM))
    kvb = kv_off // NSA_KV

    def kv_in(i):
        return pl.BlockSpec((1, tseq, NSA_KV), lambda b, t: (b, t, kvb + i))

    assert NSA_DIM + seq // SLC_BLOCK <= LANES

    def k_out(width):
        return (pl.BlockSpec((1, NSA_GROUPS, tseq, width), lambda b, t: (b, 0, t, 0)),
                jax.ShapeDtypeStruct((bsz, NSA_GROUPS, seq, width), BF16))

    (ksl_out, ksl_shape), (kwn_out, kwn_shape) = k_out(LANES), k_out(NSA_DIM)
    v_out = pl.BlockSpec((1, NSA_GROUPS, NSA_DIM + NSA_VPAD, tseq), lambda b, t: (b, 0, 0, t))
    v_shape = jax.ShapeDtypeStruct((bsz, NSA_GROUPS, NSA_DIM + NSA_VPAD, seq), BF16)
    tab = pl.BlockSpec((tseq, LANES), lambda b, t: (t, 0))
    return pl.pallas_call(
        _nsa_prep_kernel,
        grid=(bsz, seq // tseq),
        in_specs=[pl.BlockSpec((1, tseq, NSA_WIDTH), lambda b, t: (b, t, 0)), kv_in(0), kv_in(1), kv_in(2), kv_in(3),
                  pl.BlockSpec((1, tseq, LANES), lambda b, t: (b, t, gate_off // LANES)), tab, tab],
        out_specs=[pl.BlockSpec((1, tseq, NSA_WIDTH), lambda b, t: (b, t, 0)), ksl_out, v_out, kwn_out, v_out,
                   pl.BlockSpec((1, NSA_GROUPS, NSA_GATE_ROWS, tseq), lambda b, t: (b, 0, 0, t))],
        out_shape=[jax.ShapeDtypeStruct((bsz, seq, NSA_WIDTH), BF16), ksl_shape, v_shape, kwn_shape, v_shape,
                   jax.ShapeDtypeStruct((bsz, NSA_GROUPS, NSA_GATE_ROWS, seq), F32)],
        compiler_params=_params(("parallel", "parallel")),
        name="nsa_prep",
    )(prec3, rest3, rest3, rest3, rest3, rest3, cs, sn)


def _compress_kernel(*refs, precise):
    *kv_refs, pos_ref, w1_ref, w2_ref, o_ref = refs
    mm = _dot3 if precise else (lambda a, b: _dot(a.astype(BF16), b.astype(BF16)))
    nu = kv_refs[0].shape[1] // CMP_STRIDE
    per_tile = LANES // NSA_DIM
    for g in range(NSA_GROUPS):
        kv_ref, lanes = kv_refs[g // per_tile], slice((g % per_tile) * NSA_DIM, (g % per_tile + 1) * NSA_DIM)
        u = jnp.concatenate([kv_ref[0, pl.ds(l, nu, stride=CMP_STRIDE), :][:, lanes] for l in range(CMP_STRIDE)],
                            axis=1)
        ya = mm(u + pos_ref[0:1, :], w1_ref[0])
        yb = mm(u + pos_ref[1:2, :], w1_ref[1])
        hid = ya + pltpu.roll(yb, nu - 1, 0)
        o_ref[0, g] = mm(jax.nn.gelu(hid), w2_ref[...])


def _compress(proj3, col, pos, w1, w2, precise):
    bsz, seq, _ = proj3.shape
    nu = seq // CMP_STRIDE
    width = CMP_STRIDE * NSA_DIM
    assert col % LANES == 0
    tiles = NSA_KV // LANES
    return pl.pallas_call(
        functools.partial(_compress_kernel, precise=precise),
        grid=(bsz,),
        in_specs=[pl.BlockSpec((1, seq, LANES), lambda b, t=t: (b, 0, col // LANES + t)) for t in range(tiles)] + [
            pl.BlockSpec((2, width), lambda b: (0, 0)),
            pl.BlockSpec((2, width, CMP_HIDDEN), lambda b: (0, 0, 0)),
            pl.BlockSpec((CMP_HIDDEN, NSA_DIM), lambda b: (0, 0)),
        ],
        out_specs=pl.BlockSpec((1, NSA_GROUPS, nu, NSA_DIM), lambda b: (b, 0, 0, 0)),
        out_shape=jax.ShapeDtypeStruct((bsz, NSA_GROUPS, nu, NSA_DIM), F32),
        compiler_params=_params(("parallel",)),
        name="nsa_compress",
    )(*([proj3] * tiles), pos.reshape(2, width), w1.reshape(2, width, CMP_HIDDEN), w2)


def _nsa_attn_kernel(qraw_ref, qrot_ref, kc_ref, vct_ref, ksl_ref, vslt_ref, kwn_ref, vwnt_ref, gate_ref, aggt_ref,
                     o_ref, sa_ref, sb_ref, *win_refs, topk, tk):
    tq = qraw_ref.shape[1]
    nu = kc_ref.shape[2]
    ns = aggt_ref.shape[0]
    rep, dk = NSA_REP, NSA_DIM
    qs = pl.program_id(2) * tq
    tpos = qs + lax.broadcasted_iota(jnp.int32, (1, tq), 1)

    qrt = (qraw_ref[0] * (dk ** -0.5 * LOG2E)).T
    kc_hi, kc_lo = _split2(kc_ref[0, 0])
    vct = vct_ref[0, 0].astype(BF16)
    crow = lax.broadcasted_iota(jnp.int32, (nu, tq), 0)
    m_c = (crow * CMP_STRIDE + CMP_BLOCK - 1 <= tpos) & (crow < nu - 1)
    q_hi, q_lo = _split2(jnp.concatenate([qrt[r * dk:(r + 1) * dk] for r in range(rep)], axis=1))
    s_all = _dot(jnp.concatenate([kc_hi, kc_hi, kc_lo], axis=1),
                 jnp.concatenate([q_hi, q_lo, q_hi], axis=0))

    qt = qrot_ref[0].astype(F32).T.astype(BF16)
    qt_all = jnp.concatenate([qt[r * dk:(r + 1) * dk] for r in range(rep)], axis=1)
    hi = (qs + tq) // tk

    def key_tile(ktc):
        return pl.ds(pl.multiple_of(ktc * tk, tk), tk)

    def scores(k_ref, ktc, dst_ref):
        dst_ref[...] = _dot(k_ref[0, 0, key_tile(ktc), :], qt_all)

    win_tiles = [hi - len(win_refs) + j for j in range(len(win_refs))]
    for kt, dst_ref in zip(win_tiles, win_refs):
        scores(kwn_ref, jnp.maximum(kt, 0), dst_ref)
    psum = jnp.zeros((nu, tq), F32)
    p_all = []
    has_block = tpos >= CMP_BLOCK - 1
    for r in range(rep):
        s = jnp.where(m_c, s_all[:, r * tq:(r + 1) * tq], NEG)
        e = jnp.exp2(s - jnp.max(s, axis=0, keepdims=True))
        p = e * jnp.where(has_block, 1.0 / jnp.sum(e, axis=0, keepdims=True), 0.0)
        psum = psum + p
        p_all.append(p.astype(BF16))
    o_c_all = _dot(vct, jnp.concatenate(p_all, axis=1))
    o_c = [o_c_all[:, r * tq:(r + 1) * tq] for r in range(rep)]

    p_hi, p_lo = _split2(psum)
    imp = _dot(aggt_ref[...], p_hi) + _dot(aggt_ref[...], p_lo)
    jrow = lax.broadcasted_iota(jnp.int32, (ns, tq), 0)
    dj = jnp.right_shift(tpos, SLC_BLOCK.bit_length() - 1) - jrow
    forced = (jrow == 0) | ((dj >= 0) & (dj < SLC_LOCAL))
    imp = jnp.where(forced, BIG, jnp.where(jrow * SLC_BLOCK <= tpos, imp, -BIG))
    sub8 = lax.broadcasted_iota(jnp.int32, (8, tq), 0)
    chunks = [imp[c * 8:(c + 1) * 8] for c in range(ns // 8)]
    ranks = [jnp.zeros((8, tq), F32) for _ in range(ns // 8)]
    for jp in range(ns):
        row = chunks[jp // 8][jp % 8:jp % 8 + 1]
        for c in range(ns // 8):
            if c < jp // 8:
                ahead = jnp.where(row > chunks[c], 1.0, 0.0)
            elif c > jp // 8:
                ahead = jnp.where(row >= chunks[c], 1.0, 0.0)
            else:
                tie = jnp.where(sub8 > jp % 8, 1.0, 0.0)
                ahead = jnp.where(row > chunks[c], 1.0, jnp.where(row == chunks[c], tie, 0.0))
            ranks[c] = ranks[c] + ahead
    selt = [jnp.where(ranks[c] < topk, 0.0, NEG) for c in range(ns // 8)]

    def consume(vt_ref, ktc, src_ref, carry, mask=None):
        vt = vt_ref[0, 0, :, key_tile(ktc)]
        out = []
        for r in range(rep):
            m_old, acc = carry[r]
            s = src_ref[:, r * tq:(r + 1) * tq]
            if mask is not None:
                s = jnp.where(mask, s, NEG)
            m_new = jnp.maximum(m_old, jnp.max(s, axis=0, keepdims=True))
            alpha = jnp.exp2(m_old - m_new)
            p = jnp.exp2(s - m_new).astype(BF16)
            out.append((m_new, acc * alpha + _dot(vt, p)))
        return tuple(out)

    def normalised(carry):
        return [acc[:dk] * (1.0 / acc[dk:dk + 1]) for _, acc in carry]

    init = tuple((jnp.full((1, tq), NEG, F32), jnp.zeros((dk + NSA_VPAD, tq), F32)) for _ in range(rep))
    krow = lax.broadcasted_iota(jnp.int32, (tk, tq), 0)
    unseen = 1 << 30

    carry = init
    for kt, src_ref in zip(win_tiles, win_refs):
        ktc = jnp.maximum(kt, 0)
        dlt = tpos - (jnp.where(kt >= 0, ktc * tk, unseen) + krow)
        carry = consume(vwnt_ref, ktc, src_ref, carry, mask=pltpu.bitcast(dlt, jnp.uint32) < jnp.uint32(WIN))
    o_w = normalised(carry)

    selb = jnp.concatenate([jnp.concatenate(selt, axis=0)] * rep, axis=1).astype(BF16)
    pad = jnp.zeros((LANES - dk - ns, rep * tq), BF16)
    qt_sel = jnp.concatenate([qt_all, selb] + ([pad] if LANES > dk + ns else []), axis=0)

    def sel_scores(ktc, dst_ref):
        dst_ref[...] = _dot(ksl_ref[0, 0, key_tile(ktc), :], qt_sel)

    def pair(i, carry):
        kt = 2 * i
        sel_scores(kt + 1, sb_ref)
        carry = consume(vslt_ref, kt, sa_ref, carry)
        sel_scores(kt + 2, sa_ref)
        return consume(vslt_ref, kt + 1, sb_ref, carry)

    past = hi - 1
    sel_scores(0, sa_ref)
    carry = lax.fori_loop(0, past // 2, pair, init)
    carry = lax.cond(past % 2 == 1, lambda c: consume(vslt_ref, past - 1, sa_ref, c), lambda c: c, carry)
    sel_scores(past, sb_ref)
    carry = consume(vslt_ref, past, sb_ref, carry, mask=past * tk + krow <= tpos)
    o_s = normalised(carry)

    gate = jax.nn.sigmoid(gate_ref[0, 0])
    o_t = [gate[3 * r:3 * r + 1] * o_c[r] + gate[3 * r + 1:3 * r + 2] * o_s[r] + gate[3 * r + 2:3 * r + 3] * o_w[r]
           for r in range(rep)]
    o_ref[0] = jnp.concatenate(o_t, axis=0).T.astype(o_ref.dtype)


def _nsa_attn(prec3, qrot, kc, vct, ksl, vslt, kwn, vwnt, gates_t, tq, tk):
    bsz, seq, _ = qrot.shape
    nu = seq // CMP_STRIDE
    ns = seq // SLC_BLOCK
    ci = np.arange(nu)[None, :]
    sj = np.arange(ns)[:, None]
    overlap = (ci * CMP_STRIDE < (sj + 1) * SLC_BLOCK) & (ci * CMP_STRIDE + CMP_BLOCK > sj * SLC_BLOCK) & (ci < nu - 1)
    aggt = jnp.asarray(overlap, BF16)
    gw = NSA_REP * NSA_DIM
    assert tq % tk == 0 and seq % tq == 0
    win_tiles = -(-(WIN - 1) // tk) + tq // tk

    def q_spec():
        return pl.BlockSpec((1, tq, gw), lambda b, g, t: (b, t, g))

    def per_group(rows, cols):
        return pl.BlockSpec((1, 1, rows, cols), lambda b, g, t: (b, g, 0, 0))

    return pl.pallas_call(
        functools.partial(_nsa_attn_kernel, topk=min(SLC_TOPK, ns), tk=tk),
        grid=(bsz, NSA_GROUPS, seq // tq),
        in_specs=[q_spec(), q_spec(), per_group(nu, NSA_DIM), per_group(NSA_DIM, nu),
                  per_group(seq, LANES), per_group(NSA_DIM + NSA_VPAD, seq),
                  per_group(seq, NSA_DIM), per_group(NSA_DIM + NSA_VPAD, seq),
                  pl.BlockSpec((1, 1, NSA_GATE_ROWS, tq), lambda b, g, t: (b, g, 0, t)),
                  pl.BlockSpec((ns, nu), lambda b, g, t: (0, 0))],
        out_specs=q_spec(),
        out_shape=jax.ShapeDtypeStruct((bsz, seq, NSA_WIDTH), BF16),
        scratch_shapes=[pltpu.VMEM((tk, NSA_REP * tq), F32)] * (2 + win_tiles),
        compiler_params=_params(("parallel", "parallel", "arbitrary")),
        name="nsa_attention",
    )(prec3, qrot, kc, vct, ksl, vslt, kwn, vwnt, gates_t, aggt)


def _nsa(prec3, rest3, kv_off, gate_off, cmp_k, cmp_v):
    bsz, seq, _ = prec3.shape
    nu = seq // CMP_STRIDE
    qrot, ksl, vslt, kwn, vwnt, gates_t = _nsa_prep(prec3, rest3, kv_off + NSA_KV, gate_off, min(seq, 512))
    kc = _compress(prec3, NSA_WIDTH, *cmp_k, precise=True)
    vct = _compress(rest3, kv_off, *cmp_v, precise=False).transpose(0, 1, 3, 2)
    return _nsa_attn(prec3, qrot, kc, vct, ksl, vslt, kwn, vwnt, gates_t, 256, 256)


def _out_proj_kernel(yh_ref, yn_ref, g_ref, w_ref, x_ref, o_ref, y_ref):
    @pl.when(pl.program_id(1) == 0)
    def _():
        wh = yh_ref.shape[1]
        y_ref[:, :wh] = yh_ref[...]
        y_ref[:, wh:] = _rms(yn_ref[...].astype(F32), g_ref[...]).astype(BF16)

    o_ref[...] = x_ref[...] + _dot(y_ref[...], w_ref[...])


def _out_proj(y_hg, y_nsa, nsa_gain, w_out, x, tm, tn):
    m, dm = x.shape
    wh, wn = y_hg.shape[1], y_nsa.shape[1]
    return pl.pallas_call(
        _out_proj_kernel,
        grid=(m // tm, dm // tn),
        in_specs=[
            pl.BlockSpec((tm, wh), lambda i, j: (i, 0)),
            pl.BlockSpec((tm, wn), lambda i, j: (i, 0)),
            pl.BlockSpec((1, wn), lambda i, j: (0, 0)),
            pl.BlockSpec((None, wh + wn, tn), lambda i, j: (j, 0, 0)),
            pl.BlockSpec((tm, tn), lambda i, j: (i, j)),
        ],
        out_specs=pl.BlockSpec((tm, tn), lambda i, j: (i, j)),
        out_shape=jax.ShapeDtypeStruct((m, dm), F32),
        scratch_shapes=[pltpu.VMEM((tm, wh + wn), BF16)],
        compiler_params=_params(("parallel", "arbitrary")),
        name="out_proj",
    )(y_hg, y_nsa, nsa_gain.reshape(1, wn), _column_blocks(w_out.astype(BF16), tn), x)


MOE_ROWS = 256


def _pack_bf16_pairs(hi):
    n = hi.shape[1] // 2
    bits = pltpu.bitcast(hi.astype(F32), jnp.uint32)
    return jnp.right_shift(bits[:, :n], jnp.uint32(16)) | (bits[:, n:] & jnp.uint32(0xFFFF0000))


def _unpack_f32_pairs(words):
    lo = pltpu.bitcast(jnp.left_shift(words, jnp.uint32(16)), F32)
    hi = pltpu.bitcast(words & jnp.uint32(0xFFFF0000), F32)
    return lo, hi


def _unpack_bf16_pairs(words):
    lo, hi = _unpack_f32_pairs(words)
    return lo.astype(BF16), hi.astype(BF16)


def _router_kernel(x_ref, g_ref, w_ref, b_ref, tri_ref, xn_ref, info_ref, infot_ref, cnt_ref, carry_ref):
    @pl.when(pl.program_id(0) == 0)
    def _():
        carry_ref[...] = jnp.zeros_like(carry_ref)

    xn = _rms(x_ref[...], g_ref[...])
    hi, lo = _split2(xn)
    xn_ref[...] = _pack_bf16_pairs(hi)
    logits = _dot(hi, w_ref[0]) + (_dot(hi, w_ref[1]) + _dot(lo, w_ref[0])) + b_ref[...]
    lane = lax.broadcasted_iota(jnp.int32, logits.shape, 1).astype(F32)
    none = float(LANES)

    def first_max(mask):
        top = jnp.max(jnp.where(mask, logits, -jnp.inf), axis=-1, keepdims=True)
        return top, jnp.min(jnp.where(mask & (logits == top), lane, none), axis=-1, keepdims=True)

    is_g = lane < MOE_GROUPS
    gmax, gsel = first_max(is_g)
    gw = 1.0 / jnp.sum(jnp.where(is_g, jnp.exp(logits - gmax), 0.0), axis=-1, keepdims=True)
    lo_lane = MOE_GROUPS + gsel * MOE_EPG
    in_grp = (lane >= lo_lane) & (lane < lo_lane + MOE_EPG)
    v1, i1 = first_max(in_grp)
    v2, i2 = first_max(in_grp & (lane != i1))
    e = jnp.exp(v2 - v1)
    w1 = gw / (1.0 + e)
    w2 = gw * e / (1.0 + e)
    e1 = i1 - MOE_GROUPS
    e2 = i2 - MOE_GROUPS
    onehot = jnp.where((lane == e1) | (lane == e2), 1.0, 0.0)
    before = _dot(tri_ref[...], onehot.astype(BF16)) + carry_ref[...]
    r1 = jnp.sum(jnp.where(lane == e1, before, 0.0), axis=-1, keepdims=True)
    r2 = jnp.sum(jnp.where(lane == e2, before, 0.0), axis=-1, keepdims=True)
    carry_ref[...] = carry_ref[...] + jnp.sum(onehot, axis=0, keepdims=True)
    cnt_ref[...] = carry_ref[...]
    info = jnp.zeros_like(logits)
    for idx, val in enumerate((e1, e2, w1, w2, r1, r2)):
        info = jnp.where(lane == idx, val, info)
    info_ref[...] = info
    infot_ref[...] = info.T[:8]


def _router(x2, gain, w_group, b_group, w_router, b_router, tm):
    m, dm = x2.shape
    wcat = jnp.pad(jnp.concatenate([w_group, w_router], axis=1), ((0, 0), (0, LANES - MOE_GROUPS - N_EXPERTS)))
    bcat = jnp.pad(jnp.concatenate([b_group, b_router]), (0, LANES - MOE_GROUPS - N_EXPERTS)).reshape(1, LANES)
    tri = jnp.asarray(np.tril(np.ones((tm, tm), np.float32), -1), BF16)
    return pl.pallas_call(
        _router_kernel,
        grid=(m // tm,),
        in_specs=[
            pl.BlockSpec((tm, dm), lambda i: (i, 0)),
            pl.BlockSpec((1, dm), lambda i: (0, 0)),
            pl.BlockSpec((2, dm, LANES), lambda i: (0, 0, 0)),
            pl.BlockSpec((1, LANES), lambda i: (0, 0)),
            pl.BlockSpec((tm, tm), lambda i: (0, 0)),
        ],
        out_specs=[
            pl.BlockSpec((tm, dm // 2), lambda i: (i, 0)),
            pl.BlockSpec((tm, LANES), lambda i: (i, 0)),
            pl.BlockSpec((8, tm), lambda i: (0, i)),
            pl.BlockSpec((1, LANES), lambda i: (0, 0)),
        ],
        out_shape=[
            jax.ShapeDtypeStruct((m, dm // 2), jnp.uint32),
            jax.ShapeDtypeStruct((m, LANES), F32),
            jax.ShapeDtypeStruct((8, m), F32),
            jax.ShapeDtypeStruct((1, LANES), F32),
        ],
        scratch_shapes=[pltpu.VMEM((1, LANES), F32)],
        compiler_params=_params(("arbitrary",)),
        name="moe_router",
    )(x2, gain.reshape(1, dm), jnp.stack(_split2(wcat)), bcat, tri)


def _dispatch_kernel(d1_ref, d2_ref, tail_ref, xn_ref, xs_ref, zero_ref, sem, zero_sem):
    tm = xn_ref.shape[0]
    base = pl.program_id(0) * tm

    @pl.when(pl.program_id(0) == 0)
    def _():
        zero_ref[...] = jnp.zeros_like(zero_ref)

        def fill(e):
            rows = zero_ref.shape[0]
            return pltpu.make_async_copy(zero_ref, xs_ref.at[pl.ds(jnp.maximum(tail_ref[e], 0) * rows, rows)], zero_sem)

        for e in range(N_EXPERTS):
            @pl.when(tail_ref[e] >= 0)
            def _():
                fill(e).start()

        def fill_unused(blk):
            rows = zero_ref.shape[0]
            return pltpu.make_async_copy(zero_ref, xs_ref.at[pl.ds(blk * rows, rows)], zero_sem)

        unused = (tail_ref[N_EXPERTS], xs_ref.shape[0] // zero_ref.shape[0])
        lax.fori_loop(*unused, lambda blk, c: (fill_unused(blk).start(), c)[1], 0)
        lax.fori_loop(*unused, lambda blk, c: (fill_unused(blk).wait(), c)[1], 0)
        for e in range(N_EXPERTS):
            @pl.when(tail_ref[e] >= 0)
            def _():
                fill(e).wait()

    def row_copy(i, dest):
        return pltpu.make_async_copy(xn_ref.at[pl.ds(i, 1)], xs_ref.at[pl.ds(dest, 1)], sem)

    for i in range(tm):
        row_copy(i, d1_ref[base + i]).start()
        row_copy(i, d2_ref[base + i]).start()
    for _ in range(2):
        pltpu.make_async_copy(xn_ref, xs_ref.at[pl.ds(0, tm)], sem).wait()


def _dispatch(xn, d1, d2, tail_block, rows, tm):
    m, words = xn.shape
    return pl.pallas_call(
        _dispatch_kernel,
        grid_spec=pltpu.PrefetchScalarGridSpec(
            num_scalar_prefetch=3,
            grid=(m // tm,),
            in_specs=[pl.BlockSpec((tm, words), lambda i, *_: (i, 0))],
            out_specs=pl.BlockSpec(memory_space=pl.ANY),
            scratch_shapes=[pltpu.VMEM((MOE_ROWS, words), jnp.uint32), pltpu.SemaphoreType.DMA(()),
                            pltpu.SemaphoreType.DMA(())],
        ),
        out_shape=jax.ShapeDtypeStruct((rows, words), jnp.uint32),
        compiler_params=_params(("arbitrary",)),
        name="moe_dispatch",
    )(d1, d2, tail_block, xn)


def _expert_kernel(be_ref, nused_ref, x_ref, wg_ref, wu_ref, wd_ref, o_ref, wgb_ref, wub_ref, wdb_ref):
    i = pl.program_id(0)

    @pl.when((i == 0) | (be_ref[i] != be_ref[jnp.maximum(i - 1, 0)]))
    def _():
        wgb_ref[...] = wg_ref[0].astype(BF16)
        wub_ref[...] = wu_ref[0].astype(BF16)
        wdb_ref[...] = wd_ref[0].astype(BF16)

    @pl.when(i < nused_ref[0])
    def _():
        half = wgb_ref.shape[0] // 2
        x_lo, x_hi = _unpack_bf16_pairs(x_ref[...])

        def up(w_ref):
            return _dot(x_lo, w_ref[:half]) + _dot(x_hi, w_ref[half:])

        hid = (jax.nn.silu(up(wgb_ref)) * up(wub_ref)).astype(BF16)
        for g in range(o_ref.shape[1] // LANES):
            y = _dot(hid, wdb_ref[:, g * 2 * LANES:(g + 1) * 2 * LANES])
            o_ref[:, g * LANES:(g + 1) * LANES] = _pack_bf16_pairs(y.astype(BF16))

    @pl.when(i >= nused_ref[0])
    def _():
        o_ref[...] = jnp.zeros_like(o_ref)


def _experts(xs, block_e, nused, w_gate, w_up, w_down):
    rows, words = xs.shape
    _, dm, ff = w_gate.shape
    return pl.pallas_call(
        _expert_kernel,
        grid_spec=pltpu.PrefetchScalarGridSpec(
            num_scalar_prefetch=2,
            grid=(rows // MOE_ROWS,),
            in_specs=[
                pl.BlockSpec((MOE_ROWS, words), lambda i, be, nu: (i, 0)),
                pl.BlockSpec((1, dm, ff), lambda i, be, nu: (be[i], 0, 0)),
                pl.BlockSpec((1, dm, ff), lambda i, be, nu: (be[i], 0, 0)),
                pl.BlockSpec((1, ff, dm), lambda i, be, nu: (be[i], 0, 0)),
            ],
            out_specs=pl.BlockSpec((MOE_ROWS, words), lambda i, be, nu: (i, 0)),
            scratch_shapes=[pltpu.VMEM((dm, ff), BF16), pltpu.VMEM((dm, ff), BF16), pltpu.VMEM((ff, dm), BF16)],
        ),
        out_shape=jax.ShapeDtypeStruct((rows, words), jnp.uint32),
        compiler_params=_params(("arbitrary",)),
        name="moe_experts",
    )(block_e, nused, xs, w_gate, w_up, w_down)


def _combine_kernel(d1_ref, d2_ref, x_ref, info_ref, g_ref, ys_ref, o_ref, buf_a, buf_b, sem_a, sem_b):
    ts = buf_a.shape[1]
    dm = x_ref.shape[1]
    step, nsteps = pl.program_id(0), pl.num_programs(0)
    base = step * 2 * ts

    def row_copy(buf_ref, sem, r, slot, src):
        return pltpu.make_async_copy(ys_ref.at[pl.ds(src, 1)], buf_ref.at[slot, pl.ds(r, 1)], sem)

    def issue(buf_ref, sem, tok0):
        for r in range(ts):
            row_copy(buf_ref, sem, r, 0, d1_ref[tok0 + r]).start()
            row_copy(buf_ref, sem, r, 1, d2_ref[tok0 + r]).start()

    def wait(buf_ref, sem):
        for slot in range(2):
            pltpu.make_async_copy(ys_ref.at[pl.ds(0, ts)], buf_ref.at[slot], sem).wait()

    def finish(buf_ref, rows):
        info = info_ref[rows, :]
        w1, w2 = info[:, 2:3], info[:, 3:4]
        ssq = jnp.zeros((ts, 1), F32)
        for g in range(dm // (2 * LANES)):
            a = _unpack_f32_pairs(buf_ref[0, :, g * LANES:(g + 1) * LANES])
            b = _unpack_f32_pairs(buf_ref[1, :, g * LANES:(g + 1) * LANES])
            for half in range(2):
                cols = slice((2 * g + half) * LANES, (2 * g + half + 1) * LANES)
                y = x_ref[rows, cols] + (w1 * a[half] + w2 * b[half])
                ssq = ssq + jnp.sum(y * y, axis=-1, keepdims=True)
                o_ref[rows, cols] = y
        o_ref[rows, :] = o_ref[rows, :] * lax.rsqrt(ssq / dm + EPS) * g_ref[...]

    @pl.when(step == 0)
    def _():
        issue(buf_a, sem_a, base)

    wait(buf_a, sem_a)
    issue(buf_b, sem_b, base + ts)
    finish(buf_a, slice(0, ts))
    wait(buf_b, sem_b)
    issue(buf_a, sem_a, jnp.where(step + 1 < nsteps, base + 2 * ts, base))
    finish(buf_b, slice(ts, 2 * ts))

    @pl.when(step + 1 == nsteps)
    def _():
        wait(buf_a, sem_a)


def _combine(x2, info, gain, ys, d1, d2, ts):
    m, dm = x2.shape
    tm = 2 * ts
    gather_buf = pltpu.VMEM((2, ts, dm // 2), jnp.uint32)
    return pl.pallas_call(
        _combine_kernel,
        grid_spec=pltpu.PrefetchScalarGridSpec(
            num_scalar_prefetch=2,
            grid=(m // tm,),
            in_specs=[
                pl.BlockSpec((tm, dm), lambda i, *_: (i, 0)),
                pl.BlockSpec((tm, LANES), lambda i, *_: (i, 0)),
                pl.BlockSpec((1, dm), lambda i, *_: (0, 0)),
                pl.BlockSpec(memory_space=pl.ANY),
            ],
            out_specs=pl.BlockSpec((tm, dm), lambda i, *_: (i, 0)),
            scratch_shapes=[gather_buf, gather_buf, pltpu.SemaphoreType.DMA(()), pltpu.SemaphoreType.DMA(())],
        ),
        out_shape=jax.ShapeDtypeStruct((m, dm), F32),
        compiler_params=_params(("arbitrary",)),
        name="moe_combine",
    )(d1, d2, x2, info, gain.reshape(1, dm), ys)


def _moe_and_final_norm(x2, ffn_gain, w_group, b_group, w_router, b_router, w_gate, w_up, w_down, final_gain, tm):
    m, _ = x2.shape
    xn, info, infot, cnt = _router(x2, ffn_gain, w_group, b_group, w_router, b_router, tm)
    counts = cnt[0, :N_EXPERTS].astype(jnp.int32)
    padded = (counts + MOE_ROWS - 1) // MOE_ROWS * MOE_ROWS
    pend = jnp.cumsum(padded)
    pstart = pend - padded
    fields = infot.astype(jnp.int32)
    d1 = pstart[fields[0]] + fields[4]
    d2 = pstart[fields[1]] + fields[5]
    nblocks = 2 * m // MOE_ROWS + N_EXPERTS
    first_row = jnp.arange(nblocks, dtype=jnp.int32) * MOE_ROWS
    block_e = jnp.minimum(jnp.sum(pend[None, :] <= first_row[:, None], axis=1), N_EXPERTS - 1).astype(jnp.int32)
    nused = (pend[-1:] // MOE_ROWS).astype(jnp.int32)
    tail_block = jnp.concatenate([jnp.where(padded > 0, pend // MOE_ROWS - 1, -1), pend[-1:] // MOE_ROWS]).astype(jnp.int32)
    xs = _dispatch(xn, d1, d2, tail_block, nblocks * MOE_ROWS, tm)
    ys = _experts(xs, block_e, nused, w_gate, w_up, w_down)
    return _combine(x2, info, final_gain, ys, d1, d2, tm)


def kernel(x, attn_norm, w_in, hg_lb_logits, hg_out_norm, cmp_pos_k, cmp_w1_k, cmp_w2_k, cmp_pos_v, cmp_w1_v,
           cmp_w2_v, nsa_out_norm, w_out, ffn_norm, moe_w_group, moe_b_group, moe_w_router, moe_b_router,
           moe_w_gate, moe_w_up, moe_w_down, final_norm):
    bsz, seq, dm = x.shape
    xt = x.reshape(bsz * seq, dm)
    w = w_in[0]
    p0, p1 = 4 * HG_QK, 4 * HG_QK + NSA_WIDTH + NSA_KV
    tn = 256
    assert p0 % tn == 0 and p1 % tn == 0
    w_all = _column_blocks(jnp.pad(w.astype(BF16), ((0, 0), (0, (-w.shape[1]) % tn))), tn)
    w_prec = _column_blocks(jnp.stack(_split2(w[:, p0:p1])), tn)
    proj3 = _normed_matmul(xt, attn_norm[0], w_prec, w_all, p0 // tn, 1024).reshape(bsz, seq, -1)
    hg0 = p1 - p0
    y_hg = _hgrn(proj3, hg0, hg_lb_logits, hg_out_norm[0], 1024)
    y_nsa = _nsa(proj3, proj3, hg0 + 4 * HG_QK, hg0 + 4 * HG_QK + 5 * NSA_KV,
                 (cmp_pos_k[0], cmp_w1_k[0], cmp_w2_k[0]), (cmp_pos_v[0], cmp_w1_v[0], cmp_w2_v[0]))
    x2 = _out_proj(y_hg.reshape(bsz * seq, -1), y_nsa.reshape(bsz * seq, -1), nsa_out_norm[0], w_out[0], xt, 1024, 512)
    out = _moe_and_final_norm(x2, ffn_norm[0], moe_w_group[0], moe_b_group[0], moe_w_router[0], moe_b_router[0],
                              moe_w_gate[0], moe_w_up[0], moe_w_down[0], final_norm, 256)
    return out.reshape(bsz, seq, dm)
```

```python
import functools

import jax
import jax.numpy as jnp
import numpy as np
from jax import lax
from jax.experimental import pallas as pl
from jax.experimental.pallas import tpu as pltpu

F32 = jnp.float32
BF16 = jnp.bfloat16

EPS = 1e-6
ROPE_THETA = 10000.0
NEG = -1e30
BIG = 1e9
LOG2E = 1.4426950408889634

HG_HEADS = 8
HG_DIM = 128
HG_QK = HG_HEADS * HG_DIM
HG_CHUNK = 64
HG_SUB = 8

NSA_HEADS = 16
NSA_GROUPS = 4
NSA_REP = NSA_HEADS // NSA_GROUPS
NSA_DIM = 64
NSA_WIDTH = NSA_HEADS * NSA_DIM
NSA_KV = NSA_GROUPS * NSA_DIM
NSA_VPAD = 16
NSA_GATE_ROWS = 16
CMP_BLOCK = 32
CMP_STRIDE = 16
CMP_HIDDEN = 256
SLC_BLOCK = 64
SLC_TOPK = 16
SLC_LOCAL = 2
WIN = 512

MOE_GROUPS = 4
MOE_EPG = 8
N_EXPERTS = MOE_GROUPS * MOE_EPG

LANES = 128
VMEM_LIMIT = 56 * 1024 * 1024

PROJ_ROWS = 1024
PROJ_COLS = 256
OUT_COLS = 512
HG_SEQ = 1024
HG_UNROLL = 8
NSA_PREP_SEQ = 512
NSA_TQ = 256
NSA_TK = 256
MOE_TOKENS = 256


def _params(semantics, **kw):
    return pltpu.CompilerParams(dimension_semantics=semantics, vmem_limit_bytes=VMEM_LIMIT, **kw)


def _split2(a):
    hi = a.astype(BF16)
    return hi, (a - hi.astype(F32)).astype(BF16)


def _split3(a):
    hi = a.astype(BF16)
    r = a - hi.astype(F32)
    mid = r.astype(BF16)
    return hi, mid, (r - mid.astype(F32)).astype(BF16)


def _dot(a, b):
    return jnp.dot(a, b, preferred_element_type=F32)


def _dot_nt(a, b):
    return lax.dot_general(a, b, (((1,), (1,)), ((), ())), preferred_element_type=F32)


def _dot3(a, b):
    a_hi, a_lo = _split2(a)
    b_hi, b_lo = _split2(b)
    return _dot(a_hi, b_hi) + (_dot(a_hi, b_lo) + _dot(a_lo, b_hi))


def _rms(x, gain):
    return x * lax.rsqrt(jnp.mean(x * x, axis=-1, keepdims=True) + EPS) * gain


def _normed_matmul_kernel(x_ref, g_ref, wp_ref, w_ref, o_ref, h_ref, *, nprec):
    j = pl.program_id(1)

    @pl.when(j == 0)
    def _():
        y = _rms(x_ref[...], g_ref[...])
        hi = y.astype(BF16)
        h_ref[0] = hi
        h_ref[1] = (y - hi.astype(F32)).astype(BF16)

    @pl.when(j < nprec)
    def _():
        o_ref[...] = _dot(h_ref[0], wp_ref[0]) + (_dot(h_ref[0], wp_ref[1]) + _dot(h_ref[1], wp_ref[0]))

    @pl.when(j >= nprec)
    def _():
        o_ref[...] = _dot(h_ref[0], w_ref[...])


def _column_blocks(w, tn):
    *lead, k, n = w.shape
    return jnp.moveaxis(w.reshape(*lead, k, n // tn, tn), -2, -3)


def _normed_matmul(x, gain, wp_blocks, w_blocks, first, tm):
    m, k = x.shape
    nb, _, tn = w_blocks.shape
    nprec = wp_blocks.shape[1]

    def single_pass_block(j):
        r = jnp.maximum(j - nprec, 0)
        return jnp.where(r < first, r, r + nprec)

    return pl.pallas_call(
        functools.partial(_normed_matmul_kernel, nprec=nprec),
        grid=(m // tm, nb),
        in_specs=[
            pl.BlockSpec((tm, k), lambda i, j: (i, 0)),
            pl.BlockSpec((1, k), lambda i, j: (0, 0)),
            pl.BlockSpec((2, None, k, tn), lambda i, j: (0, jnp.minimum(j, nprec - 1), 0, 0)),
            pl.BlockSpec((None, k, tn), lambda i, j: (single_pass_block(j), 0, 0)),
        ],
        out_specs=pl.BlockSpec((tm, tn), lambda i, j: (i, j)),
        out_shape=jax.ShapeDtypeStruct((m, nb * tn), F32),
        scratch_shapes=[pltpu.VMEM((2, tm, k), BF16)],
        compiler_params=_params(("parallel", "arbitrary")),
        name="normed_matmul",
    )(x, gain.reshape(1, k), wp_blocks, w_blocks)


def _hgrn_consts():
    c, sub = HG_CHUNK, HG_SUB
    tri = np.tile(np.tril(np.ones((c, c), np.float32)), (1, 3))
    gsum = (np.arange(c * sub)[None, :] // sub == np.arange(c)[:, None]).astype(np.float32)
    return jnp.asarray(tri, BF16), jnp.asarray(gsum, BF16)


HG_PAR = 2


def _hgrn_kernel(q_ref, f_ref, i_ref, g_ref, lbl_ref, gain_ref, tri_ref, gsum_ref, o_ref, *scratch):
    c, sub, d = HG_CHUNK, HG_SUB, HG_DIM
    nsub = c // sub
    heads = range(HG_PAR)
    st_refs, p_refs, cl_refs, qs_refs, k_refs = (scratch[i * HG_PAR:(i + 1) * HG_PAR] for i in range(5))

    @pl.when(pl.program_id(2) == 0)
    def _():
        for h in heads:
            st_refs[h][...] = jnp.zeros_like(st_refs[h])

    l0 = lbl_ref[0:1, :]
    l1 = lbl_ref[1:2, :]
    lmax = jnp.maximum(l0, l1)
    e0 = jnp.exp(l0 - lmax)
    lb_all = e0 / (e0 + jnp.exp(l1 - lmax))
    srow = lax.broadcasted_iota(jnp.int32, (sub, d), 0)
    ones = jnp.ones((d, d), BF16)

    def rows_at(x, start):
        parts = ([jnp.zeros((start, d), F32)] if start else []) + [x]
        if start + x.shape[0] < c:
            parts.append(jnp.zeros((c - start - x.shape[0], d), F32))
        return jnp.concatenate(parts, axis=0)

    nchunks = q_ref.shape[1] // c
    cols = [slice(h * d, (h + 1) * d) for h in heads]

    def load(ci):
        rows = pl.ds(pl.multiple_of(ci * c, c), c)
        out = []
        for h in heads:
            lb = lb_all[:, cols[h]]
            f = lb + (1.0 - lb) * jax.nn.sigmoid(f_ref[0, rows, cols[h]])
            bcum = _dot(tri_ref[...], jnp.concatenate(_split3(jnp.log(f)), axis=0))
            out.append((q_ref[0, rows, cols[h]] * (d ** -0.5), 1.0 - f, i_ref[0, rows, cols[h]], bcum))
        return tuple(out)

    def chunk(ci, cur):
        nxt = load(jnp.minimum(ci + 1, nchunks - 1))
        rows = pl.ds(pl.multiple_of(ci * c, c), c)
        q, k, v, bcum = ([cur[h][i] for h in heads] for i in range(4))
        a_off, o_inter = [], []
        for h in heads:
            b = bcum[h]
            edge = [b[i * sub - 1:i * sub] for i in range(1, nsub + 1)]
            cl = b - jnp.concatenate([jnp.zeros((sub, d), F32)] + [jnp.broadcast_to(e, (sub, d)) for e in edge[:-1]],
                                     axis=0)
            cl_refs[h][...] = cl * LOG2E
            qs_refs[h][...] = q[h]
            k_refs[h][...] = k[h]
            qe = q[h] * jnp.exp(cl)
            qcat, kcat = [], []
            for i in range(1, nsub):
                qcat.append(rows_at(qe[i * sub:(i + 1) * sub], i * sub))
                kcat.append(rows_at(k[h][:i * sub] * jnp.exp(edge[i - 1] - b[:i * sub]), 0))
            a_off.append(_dot_nt(jnp.concatenate(qcat, axis=1).astype(BF16),
                                 jnp.concatenate(kcat, axis=1).astype(BF16)))
            st = st_refs[h][...]
            o_inter.append(_dot_nt((q[h] * jnp.exp(b)).astype(BF16), st.astype(BF16)))
            kd = k[h] * jnp.exp(edge[-1] - b)
            st_refs[h][...] = st * jnp.exp(edge[-1]) + _dot(v[h].T.astype(BF16), kd.astype(BF16))
        r2 = []
        for h in heads:
            for t in range(c):
                j0 = (t // sub) * sub
                dlt = cl_refs[h][t:t + 1, :] - cl_refs[h][j0:j0 + sub, :]
                e = jnp.exp2(jnp.where(srow <= t - j0, dlt, NEG))
                p_refs[h][t * sub:(t + 1) * sub, :] = (qs_refs[h][t:t + 1, :] * k_refs[h][j0:j0 + sub, :] * e).astype(BF16)
            r2.append(_dot(p_refs[h][...], ones))
        o = []
        for h in heads:
            o.append(o_inter[h] + _dot(a_off[h].astype(BF16), v[h].astype(BF16)))
        for h in heads:
            x = r2[h].reshape(nsub, sub, sub, d) * v[h].reshape(nsub, 1, sub, d)
            o[h] = o[h] + _dot(gsum_ref[...], x.reshape(c * sub, d).astype(BF16))
        for h in heads:
            gate = jax.nn.silu(g_ref[0, rows, cols[h]])
            o_ref[0, rows, cols[h]] = (_rms(o[h], gain_ref[...]) * gate).astype(o_ref.dtype)
        return nxt

    lax.fori_loop(0, nchunks, chunk, load(0), unroll=HG_UNROLL)


def _hgrn(proj3, col0, lb_logits, out_gain, tseq):
    bsz, seq, _ = proj3.shape
    d, c, sub = HG_DIM, HG_CHUNK, HG_SUB
    wst, gsum = _hgrn_consts()
    groups = HG_HEADS // HG_PAR
    width = HG_PAR * d
    assert col0 % width == 0

    def col(off):
        return pl.BlockSpec((1, tseq, width), lambda b, h, t: (b, t, col0 // width + off * groups + h))

    per_head = [pltpu.VMEM((d, d), F32), pltpu.VMEM((c * sub, d), BF16), pltpu.VMEM((c, d), F32),
                pltpu.VMEM((c, d), F32), pltpu.VMEM((c, d), F32)]
    return pl.pallas_call(
        _hgrn_kernel,
        grid=(bsz, groups, seq // tseq),
        in_specs=[
            col(0), col(1), col(2), col(3),
            pl.BlockSpec((2, width), lambda b, h, t: (0, h)),
            pl.BlockSpec((1, d), lambda b, h, t: (0, 0)),
            pl.BlockSpec(wst.shape, lambda b, h, t: (0, 0)),
            pl.BlockSpec(gsum.shape, lambda b, h, t: (0, 0)),
        ],
        out_specs=pl.BlockSpec((1, tseq, width), lambda b, h, t: (b, t, h)),
        out_shape=jax.ShapeDtypeStruct((bsz, seq, HG_QK), BF16),
        scratch_shapes=[s for s in per_head for _ in range(HG_PAR)],
        compiler_params=_params(("parallel", "parallel", "arbitrary")),
        name="hgrn2",
    )(proj3, proj3, proj3, proj3, lb_logits, out_gain.reshape(1, d), wst, gsum)


def _rope(x, cs, sn):
    lane = lax.broadcasted_iota(jnp.int32, x.shape, 1)
    partner = jnp.where(lane % NSA_DIM < NSA_DIM // 2, pltpu.roll(x, LANES - NSA_DIM // 2, 1),
                        pltpu.roll(x, NSA_DIM // 2, 1))
    return x * cs + partner * sn


def _nsa_prep_kernel(q_ref, ksl_ref, vsl_ref, kwn_ref, vwn_ref, gate_ref, cs_ref, sn_ref,
                     qrot_ref, kslo_ref, vslo_ref, kwno_ref, vwno_ref, gateo_ref):
    cs = cs_ref[...]
    sn = sn_ref[...]
    gate_t = gate_ref[0].T
    per_group = 3 * NSA_REP
    gateo_ref[...] = jnp.zeros_like(gateo_ref)
    for g in range(NSA_GROUPS):
        gateo_ref[0, g, :per_group] = gate_t[g * per_group:(g + 1) * per_group]
    for c in range(NSA_WIDTH // LANES):
        cols = slice(c * LANES, (c + 1) * LANES)
        qrot_ref[0, :, cols] = (_rope(q_ref[0, :, cols], cs, sn) * (NSA_DIM ** -0.5 * LOG2E)).astype(BF16)
    tseq = q_ref.shape[1]
    lane = lax.broadcasted_iota(jnp.int32, (tseq, LANES), 1)
    block = (pl.program_id(1) * tseq + lax.broadcasted_iota(jnp.int32, (tseq, LANES), 0)) // SLC_BLOCK
    block_onehot = jnp.where(lane - NSA_DIM == block, 1.0, 0.0)
    ones_rows = jnp.where(lax.broadcasted_iota(jnp.int32, (NSA_VPAD, tseq), 0) == 0, 1.0, 0.0).astype(BF16)
    for c in range(NSA_KV // LANES):
        cols = slice(c * LANES, (c + 1) * LANES)
        ks = _rope(ksl_ref[0, :, cols], cs, sn)
        kw = _rope(kwn_ref[0, :, cols], cs, sn).astype(BF16)
        vs = vsl_ref[0, :, cols].T.astype(BF16)
        vw = vwn_ref[0, :, cols].T.astype(BF16)
        for half in range(LANES // NSA_DIM):
            g = c * (LANES // NSA_DIM) + half
            hs = slice(half * NSA_DIM, (half + 1) * NSA_DIM)
            ks_g = ks if half == 0 else pltpu.roll(ks, NSA_DIM, 1)
            kslo_ref[0, g] = jnp.where(lane < NSA_DIM, ks_g, block_onehot).astype(BF16)
            kwno_ref[0, g] = kw[:, hs]
            vslo_ref[0, g, :NSA_DIM] = vs[hs, :]
            vslo_ref[0, g, NSA_DIM:] = ones_rows
            vwno_ref[0, g, :NSA_DIM] = vw[hs, :]
            vwno_ref[0, g, NSA_DIM:] = ones_rows


def _nsa_prep(prec3, rest3, kv_off, gate_off, tseq):
    assert gate_off % LANES == 0 and 3 * NSA_HEADS <= LANES
    bsz, seq, _ = prec3.shape
    half = NSA_DIM // 2
    inv = 1.0 / (ROPE_THETA ** (jnp.arange(0, NSA_DIM, 2, dtype=F32) / NSA_DIM))
    ang = jnp.arange(seq, dtype=F32)[:, None] * inv[None, :]
    cs = jnp.tile(jnp.cos(ang), (1, LANES // half))
    sn = jnp.tile(jnp.concatenate([-jnp.sin(ang), jnp.sin(ang)], axis=1), (1, LANES // NSA_DIM))
    kvb = kv_off // NSA_KV

    def kv_in(i):
        return pl.BlockSpec((1, tseq, NSA_KV), lambda b, t: (b, t, kvb + i))

    assert NSA_DIM + seq // SLC_BLOCK <= LANES

    def k_out(width):
        return (pl.BlockSpec((1, NSA_GROUPS, tseq, width), lambda b, t: (b, 0, t, 0)),
                jax.ShapeDtypeStruct((bsz, NSA_GROUPS, seq, width), BF16))

    (ksl_out, ksl_shape), (kwn_out, kwn_shape) = k_out(LANES), k_out(NSA_DIM)
    v_out = pl.BlockSpec((1, NSA_GROUPS, NSA_DIM + NSA_VPAD, tseq), lambda b, t: (b, 0, 0, t))
    v_shape = jax.ShapeDtypeStruct((bsz, NSA_GROUPS, NSA_DIM + NSA_VPAD, seq), BF16)
    tab = pl.BlockSpec((tseq, LANES), lambda b, t: (t, 0))
    return pl.pallas_call(
        _nsa_prep_kernel,
        grid=(bsz, seq // tseq),
        in_specs=[pl.BlockSpec((1, tseq, NSA_WIDTH), lambda b, t: (b, t, 0)), kv_in(0), kv_in(1), kv_in(2), kv_in(3),
                  pl.BlockSpec((1, tseq, LANES), lambda b, t: (b, t, gate_off // LANES)), tab, tab],
        out_specs=[pl.BlockSpec((1, tseq, NSA_WIDTH), lambda b, t: (b, t, 0)), ksl_out, v_out, kwn_out, v_out,
                   pl.BlockSpec((1, NSA_GROUPS, NSA_GATE_ROWS, tseq), lambda b, t: (b, 0, 0, t))],
        out_shape=[jax.ShapeDtypeStruct((bsz, seq, NSA_WIDTH), BF16), ksl_shape, v_shape, kwn_shape, v_shape,
                   jax.ShapeDtypeStruct((bsz, NSA_GROUPS, NSA_GATE_ROWS, seq), F32)],
        compiler_params=_params(("parallel", "parallel")),
        name="nsa_prep",
    )(prec3, rest3, rest3, rest3, rest3, rest3, cs, sn)


def _compress_kernel(*refs, precise):
    *kv_refs, pos_ref, w1_ref, w2_ref, o_ref = refs
    mm = _dot3 if precise else (lambda a, b: _dot(a.astype(BF16), b.astype(BF16)))
    nu = kv_refs[0].shape[1] // CMP_STRIDE
    per_tile = LANES // NSA_DIM
    for g in range(NSA_GROUPS):
        kv_ref, lanes = kv_refs[g // per_tile], slice((g % per_tile) * NSA_DIM, (g % per_tile + 1) * NSA_DIM)
        u = jnp.concatenate([kv_ref[0, pl.ds(l, nu, stride=CMP_STRIDE), :][:, lanes] for l in range(CMP_STRIDE)],
                            axis=1)
        ya = mm(u + pos_ref[0:1, :], w1_ref[0])
        yb = mm(u + pos_ref[1:2, :], w1_ref[1])
        hid = ya + pltpu.roll(yb, nu - 1, 0)
        o_ref[0, g] = mm(jax.nn.gelu(hid), w2_ref[...])


def _compress(proj3, col, pos, w1, w2, precise):
    bsz, seq, _ = proj3.shape
    nu = seq // CMP_STRIDE
    width = CMP_STRIDE * NSA_DIM
    assert col % LANES == 0
    tiles = NSA_KV // LANES
    return pl.pallas_call(
        functools.partial(_compress_kernel, precise=precise),
        grid=(bsz,),
        in_specs=[pl.BlockSpec((1, seq, LANES), lambda b, t=t: (b, 0, col // LANES + t)) for t in range(tiles)] + [
            pl.BlockSpec((2, width), lambda b: (0, 0)),
            pl.BlockSpec((2, width, CMP_HIDDEN), lambda b: (0, 0, 0)),
            pl.BlockSpec((CMP_HIDDEN, NSA_DIM), lambda b: (0, 0)),
        ],
        out_specs=pl.BlockSpec((1, NSA_GROUPS, nu, NSA_DIM), lambda b: (b, 0, 0, 0)),
        out_shape=jax.ShapeDtypeStruct((bsz, NSA_GROUPS, nu, NSA_DIM), F32),
        compiler_params=_params(("parallel",)),
        name="nsa_compress",
    )(*([proj3] * tiles), pos.reshape(2, width), w1.reshape(2, width, CMP_HIDDEN), w2)


def _nsa_attn_kernel(qraw_ref, qrot_ref, kc_ref, vct_ref, ksl_ref, vslt_ref, kwn_ref, vwnt_ref, gate_ref, aggt_ref,
                     o_ref, sa_ref, sb_ref, *win_refs, topk, tk):
    tq = qraw_ref.shape[1]
    nu = kc_ref.shape[2]
    ns = aggt_ref.shape[0]
    rep, dk = NSA_REP, NSA_DIM
    qs = pl.program_id(2) * tq
    tpos = qs + lax.broadcasted_iota(jnp.int32, (1, tq), 1)

    qrt = (qraw_ref[0] * (dk ** -0.5 * LOG2E)).T
    kc_hi, kc_lo = _split2(kc_ref[0, 0])
    vct = vct_ref[0, 0].astype(BF16)
    crow = lax.broadcasted_iota(jnp.int32, (nu, tq), 0)
    m_c = (crow * CMP_STRIDE + CMP_BLOCK - 1 <= tpos) & (crow < nu - 1)
    q_hi, q_lo = _split2(jnp.concatenate([qrt[r * dk:(r + 1) * dk] for r in range(rep)], axis=1))
    s_all = _dot(jnp.concatenate([kc_hi, kc_hi, kc_lo], axis=1),
                 jnp.concatenate([q_hi, q_lo, q_hi], axis=0))

    qt = qrot_ref[0].astype(F32).T.astype(BF16)
    qt_all = jnp.concatenate([qt[r * dk:(r + 1) * dk] for r in range(rep)], axis=1)
    hi = (qs + tq) // tk

    def key_tile(ktc):
        return pl.ds(pl.multiple_of(ktc * tk, tk), tk)

    def scores(k_ref, ktc, dst_ref):
        dst_ref[...] = _dot(k_ref[0, 0, key_tile(ktc), :], qt_all)

    win_tiles = [hi - len(win_refs) + j for j in range(len(win_refs))]
    for kt, dst_ref in zip(win_tiles, win_refs):
        scores(kwn_ref, jnp.maximum(kt, 0), dst_ref)
    psum = jnp.zeros((nu, tq), F32)
    p_all = []
    has_block = tpos >= CMP_BLOCK - 1
    for r in range(rep):
        s = jnp.where(m_c, s_all[:, r * tq:(r + 1) * tq], NEG)
        e = jnp.exp2(s - jnp.max(s, axis=0, keepdims=True))
        p = e * jnp.where(has_block, 1.0 / jnp.sum(e, axis=0, keepdims=True), 0.0)
        psum = psum + p
        p_all.append(p.astype(BF16))
    o_c_all = _dot(vct, jnp.concatenate(p_all, axis=1))
    o_c = [o_c_all[:, r * tq:(r + 1) * tq] for r in range(rep)]

    p_hi, p_lo = _split2(psum)
    imp = _dot(aggt_ref[...], p_hi) + _dot(aggt_ref[...], p_lo)
    jrow = lax.broadcasted_iota(jnp.int32, (ns, tq), 0)
    dj = jnp.right_shift(tpos, SLC_BLOCK.bit_length() - 1) - jrow
    forced = (jrow == 0) | ((dj >= 0) & (dj < SLC_LOCAL))
    imp = jnp.where(forced, BIG, jnp.where(jrow * SLC_BLOCK <= tpos, imp, -BIG))
    sub8 = lax.broadcasted_iota(jnp.int32, (8, tq), 0)
    chunks = [imp[c * 8:(c + 1) * 8] for c in range(ns // 8)]
    ranks = [jnp.zeros((8, tq), F32) for _ in range(ns // 8)]
    for jp in range(ns):
        row = chunks[jp // 8][jp % 8:jp % 8 + 1]
        for c in range(ns // 8):
            if c < jp // 8:
                ahead = jnp.where(row > chunks[c], 1.0, 0.0)
            elif c > jp // 8:
                ahead = jnp.where(row >= chunks[c], 1.0, 0.0)
            else:
                tie = jnp.where(sub8 > jp % 8, 1.0, 0.0)
                ahead = jnp.where(row > chunks[c], 1.0, jnp.where(row == chunks[c], tie, 0.0))
            ranks[c] = ranks[c] + ahead
    selt = [jnp.where(ranks[c] < topk, 0.0, NEG) for c in range(ns // 8)]

    def consume(vt_ref, ktc, src_ref, carry, mask=None):
        vt = vt_ref[0, 0, :, key_tile(ktc)]
        out = []
        for r in range(rep):
            m_old, acc = carry[r]
            s = src_ref[:, r * tq:(r + 1) * tq]
            if mask is not None:
                s = jnp.where(mask, s, NEG)
            m_new = jnp.maximum(m_old, jnp.max(s, axis=0, keepdims=True))
            alpha = jnp.exp2(m_old - m_new)
            p = jnp.exp2(s - m_new).astype(BF16)
            out.append((m_new, acc * alpha + _dot(vt, p)))
        return tuple(out)

    def normalised(carry):
        return [acc[:dk] * (1.0 / acc[dk:dk + 1]) for _, acc in carry]

    init = tuple((jnp.full((1, tq), NEG, F32), jnp.zeros((dk + NSA_VPAD, tq), F32)) for _ in range(rep))
    krow = lax.broadcasted_iota(jnp.int32, (tk, tq), 0)
    unseen = 1 << 30

    carry = init
    for kt, src_ref in zip(win_tiles, win_refs):
        ktc = jnp.maximum(kt, 0)
        dlt = tpos - (jnp.where(kt >= 0, ktc * tk, unseen) + krow)
        carry = consume(vwnt_ref, ktc, src_ref, carry, mask=pltpu.bitcast(dlt, jnp.uint32) < jnp.uint32(WIN))
    o_w = normalised(carry)

    selb = jnp.concatenate([jnp.concatenate(selt, axis=0)] * rep, axis=1).astype(BF16)
    pad = jnp.zeros((LANES - dk - ns, rep * tq), BF16)
    qt_sel = jnp.concatenate([qt_all, selb] + ([pad] if LANES > dk + ns else []), axis=0)

    def sel_scores(ktc, dst_ref):
        dst_ref[...] = _dot(ksl_ref[0, 0, key_tile(ktc), :], qt_sel)

    def pair(i, carry):
        kt = 2 * i
        sel_scores(kt + 1, sb_ref)
        carry = consume(vslt_ref, kt, sa_ref, carry)
        sel_scores(kt + 2, sa_ref)
        return consume(vslt_ref, kt + 1, sb_ref, carry)

    past = hi - 1
    sel_scores(0, sa_ref)
    carry = lax.fori_loop(0, past // 2, pair, init)
    carry = lax.cond(past % 2 == 1, lambda c: consume(vslt_ref, past - 1, sa_ref, c), lambda c: c, carry)
    sel_scores(past, sb_ref)
    carry = consume(vslt_ref, past, sb_ref, carry, mask=past * tk + krow <= tpos)
    o_s = normalised(carry)

    gate = jax.nn.sigmoid(gate_ref[0, 0])
    o_t = [gate[3 * r:3 * r + 1] * o_c[r] + gate[3 * r + 1:3 * r + 2] * o_s[r] + gate[3 * r + 2:3 * r + 3] * o_w[r]
           for r in range(rep)]
    o_ref[0] = jnp.concatenate(o_t, axis=0).T.astype(o_ref.dtype)


def _nsa_attn(prec3, qrot, kc, vct, ksl, vslt, kwn, vwnt, gates_t, tq, tk):
    bsz, seq, _ = qrot.shape
    nu = seq // CMP_STRIDE
    ns = seq // SLC_BLOCK
    ci = np.arange(nu)[None, :]
    sj = np.arange(ns)[:, None]
    overlap = (ci * CMP_STRIDE < (sj + 1) * SLC_BLOCK) & (ci * CMP_STRIDE + CMP_BLOCK > sj * SLC_BLOCK) & (ci < nu - 1)
    aggt = jnp.asarray(overlap, BF16)
    gw = NSA_REP * NSA_DIM
    assert tq % tk == 0 and seq % tq == 0
    win_tiles = -(-(WIN - 1) // tk) + tq // tk

    def q_spec():
        return pl.BlockSpec((1, tq, gw), lambda b, g, t: (b, t, g))

    def per_group(rows, cols):
        return pl.BlockSpec((1, 1, rows, cols), lambda b, g, t: (b, g, 0, 0))

    return pl.pallas_call(
        functools.partial(_nsa_attn_kernel, topk=min(SLC_TOPK, ns), tk=tk),
        grid=(bsz, NSA_GROUPS, seq // tq),
        in_specs=[q_spec(), q_spec(), per_group(nu, NSA_DIM), per_group(NSA_DIM, nu),
                  per_group(seq, LANES), per_group(NSA_DIM + NSA_VPAD, seq),
                  per_group(seq, NSA_DIM), per_group(NSA_DIM + NSA_VPAD, seq),
                  pl.BlockSpec((1, 1, NSA_GATE_ROWS, tq), lambda b, g, t: (b, g, 0, t)),
                  pl.BlockSpec((ns, nu), lambda b, g, t: (0, 0))],
        out_specs=q_spec(),
        out_shape=jax.ShapeDtypeStruct((bsz, seq, NSA_WIDTH), BF16),
        scratch_shapes=[pltpu.VMEM((tk, NSA_REP * tq), F32)] * (2 + win_tiles),
        compiler_params=_params(("parallel", "parallel", "arbitrary")),
        name="nsa_attention",
    )(prec3, qrot, kc, vct, ksl, vslt, kwn, vwnt, gates_t, aggt)


def _nsa(prec3, rest3, kv_off, gate_off, cmp_k, cmp_v):
    seq = prec3.shape[1]
    qrot, ksl, vslt, kwn, vwnt, gates_t = _nsa_prep(prec3, rest3, kv_off + NSA_KV, gate_off, min(seq, NSA_PREP_SEQ))
    kc = _compress(prec3, NSA_WIDTH, *cmp_k, precise=True)
    vct = _compress(rest3, kv_off, *cmp_v, precise=False).transpose(0, 1, 3, 2)
    return _nsa_attn(prec3, qrot, kc, vct, ksl, vslt, kwn, vwnt, gates_t, NSA_TQ, NSA_TK)


def _out_proj_kernel(yh_ref, yn_ref, g_ref, w_ref, x_ref, o_ref, y_ref):
    @pl.when(pl.program_id(1) == 0)
    def _():
        wh = yh_ref.shape[1]
        y_ref[:, :wh] = yh_ref[...]
        y_ref[:, wh:] = _rms(yn_ref[...].astype(F32), g_ref[...]).astype(BF16)

    o_ref[...] = x_ref[...] + _dot(y_ref[...], w_ref[...])


def _out_proj(y_hg, y_nsa, nsa_gain, w_out, x, tm, tn):
    m, dm = x.shape
    wh, wn = y_hg.shape[1], y_nsa.shape[1]
    return pl.pallas_call(
        _out_proj_kernel,
        grid=(m // tm, dm // tn),
        in_specs=[
            pl.BlockSpec((tm, wh), lambda i, j: (i, 0)),
            pl.BlockSpec((tm, wn), lambda i, j: (i, 0)),
            pl.BlockSpec((1, wn), lambda i, j: (0, 0)),
            pl.BlockSpec((None, wh + wn, tn), lambda i, j: (j, 0, 0)),
            pl.BlockSpec((tm, tn), lambda i, j: (i, j)),
        ],
        out_specs=pl.BlockSpec((tm, tn), lambda i, j: (i, j)),
        out_shape=jax.ShapeDtypeStruct((m, dm), F32),
        scratch_shapes=[pltpu.VMEM((tm, wh + wn), BF16)],
        compiler_params=_params(("parallel", "arbitrary")),
        name="out_proj",
    )(y_hg, y_nsa, nsa_gain.reshape(1, wn), _column_blocks(w_out.astype(BF16), tn), x)


MOE_ROWS = 256


def _pack_bf16_pairs(hi):
    n = hi.shape[1] // 2
    bits = pltpu.bitcast(hi.astype(F32), jnp.uint32)
    return jnp.right_shift(bits[:, :n], jnp.uint32(16)) | (bits[:, n:] & jnp.uint32(0xFFFF0000))


def _unpack_f32_pairs(words):
    lo = pltpu.bitcast(jnp.left_shift(words, jnp.uint32(16)), F32)
    hi = pltpu.bitcast(words & jnp.uint32(0xFFFF0000), F32)
    return lo, hi


def _unpack_bf16_pairs(words):
    lo, hi = _unpack_f32_pairs(words)
    return lo.astype(BF16), hi.astype(BF16)


def _router_kernel(x_ref, g_ref, w_ref, b_ref, tri_ref, xn_ref, info_ref, infot_ref, cnt_ref, carry_ref):
    @pl.when(pl.program_id(0) == 0)
    def _():
        carry_ref[...] = jnp.zeros_like(carry_ref)

    xn = _rms(x_ref[...], g_ref[...])
    hi, lo = _split2(xn)
    xn_ref[...] = _pack_bf16_pairs(hi)
    logits = _dot(hi, w_ref[0]) + (_dot(hi, w_ref[1]) + _dot(lo, w_ref[0])) + b_ref[...]
    lane = lax.broadcasted_iota(jnp.int32, logits.shape, 1).astype(F32)
    none = float(LANES)

    def first_max(mask):
        top = jnp.max(jnp.where(mask, logits, -jnp.inf), axis=-1, keepdims=True)
        return top, jnp.min(jnp.where(mask & (logits == top), lane, none), axis=-1, keepdims=True)

    is_g = lane < MOE_GROUPS
    gmax, gsel = first_max(is_g)
    gw = 1.0 / jnp.sum(jnp.where(is_g, jnp.exp(logits - gmax), 0.0), axis=-1, keepdims=True)
    lo_lane = MOE_GROUPS + gsel * MOE_EPG
    in_grp = (lane >= lo_lane) & (lane < lo_lane + MOE_EPG)
    v1, i1 = first_max(in_grp)
    v2, i2 = first_max(in_grp & (lane != i1))
    e = jnp.exp(v2 - v1)
    w1 = gw / (1.0 + e)
    w2 = gw * e / (1.0 + e)
    e1 = i1 - MOE_GROUPS
    e2 = i2 - MOE_GROUPS
    onehot = jnp.where((lane == e1) | (lane == e2), 1.0, 0.0)
    before = _dot(tri_ref[...], onehot.astype(BF16)) + carry_ref[...]
    r1 = jnp.sum(jnp.where(lane == e1, before, 0.0), axis=-1, keepdims=True)
    r2 = jnp.sum(jnp.where(lane == e2, before, 0.0), axis=-1, keepdims=True)
    carry_ref[...] = carry_ref[...] + jnp.sum(onehot, axis=0, keepdims=True)
    cnt_ref[...] = carry_ref[...]
    info = jnp.zeros_like(logits)
    for idx, val in enumerate((e1, e2, w1, w2, r1, r2)):
        info = jnp.where(lane == idx, val, info)
    info_ref[...] = info
    infot_ref[...] = info.T[:8]


def _router(x2, gain, w_group, b_group, w_router, b_router, tm):
    m, dm = x2.shape
    wcat = jnp.pad(jnp.concatenate([w_group, w_router], axis=1), ((0, 0), (0, LANES - MOE_GROUPS - N_EXPERTS)))
    bcat = jnp.pad(jnp.concatenate([b_group, b_router]), (0, LANES - MOE_GROUPS - N_EXPERTS)).reshape(1, LANES)
    tri = jnp.asarray(np.tril(np.ones((tm, tm), np.float32), -1), BF16)
    return pl.pallas_call(
        _router_kernel,
        grid=(m // tm,),
        in_specs=[
            pl.BlockSpec((tm, dm), lambda i: (i, 0)),
            pl.BlockSpec((1, dm), lambda i: (0, 0)),
            pl.BlockSpec((2, dm, LANES), lambda i: (0, 0, 0)),
            pl.BlockSpec((1, LANES), lambda i: (0, 0)),
            pl.BlockSpec((tm, tm), lambda i: (0, 0)),
        ],
        out_specs=[
            pl.BlockSpec((tm, dm // 2), lambda i: (i, 0)),
            pl.BlockSpec((tm, LANES), lambda i: (i, 0)),
            pl.BlockSpec((8, tm), lambda i: (0, i)),
            pl.BlockSpec((1, LANES), lambda i: (0, 0)),
        ],
        out_shape=[
            jax.ShapeDtypeStruct((m, dm // 2), jnp.uint32),
            jax.ShapeDtypeStruct((m, LANES), F32),
            jax.ShapeDtypeStruct((8, m), F32),
            jax.ShapeDtypeStruct((1, LANES), F32),
        ],
        scratch_shapes=[pltpu.VMEM((1, LANES), F32)],
        compiler_params=_params(("arbitrary",)),
        name="moe_router",
    )(x2, gain.reshape(1, dm), jnp.stack(_split2(wcat)), bcat, tri)


def _dispatch_kernel(d1_ref, d2_ref, tail_ref, xn_ref, xs_ref, zero_ref, sem, zero_sem):
    tm = xn_ref.shape[0]
    base = pl.program_id(0) * tm

    @pl.when(pl.program_id(0) == 0)
    def _():
        zero_ref[...] = jnp.zeros_like(zero_ref)

        def fill(e):
            rows = zero_ref.shape[0]
            return pltpu.make_async_copy(zero_ref, xs_ref.at[pl.ds(jnp.maximum(tail_ref[e], 0) * rows, rows)], zero_sem)

        for e in range(N_EXPERTS):
            @pl.when(tail_ref[e] >= 0)
            def _():
                fill(e).start()

        def fill_unused(blk):
            rows = zero_ref.shape[0]
            return pltpu.make_async_copy(zero_ref, xs_ref.at[pl.ds(blk * rows, rows)], zero_sem)

        unused = (tail_ref[N_EXPERTS], xs_ref.shape[0] // zero_ref.shape[0])
        lax.fori_loop(*unused, lambda blk, c: (fill_unused(blk).start(), c)[1], 0)
        lax.fori_loop(*unused, lambda blk, c: (fill_unused(blk).wait(), c)[1], 0)
        for e in range(N_EXPERTS):
            @pl.when(tail_ref[e] >= 0)
            def _():
                fill(e).wait()

    def row_copy(i, dest):
        return pltpu.make_async_copy(xn_ref.at[pl.ds(i, 1)], xs_ref.at[pl.ds(dest, 1)], sem)

    for i in range(tm):
        row_copy(i, d1_ref[base + i]).start()
        row_copy(i, d2_ref[base + i]).start()
    for _ in range(2):
        pltpu.make_async_copy(xn_ref, xs_ref.at[pl.ds(0, tm)], sem).wait()


def _dispatch(xn, d1, d2, tail_block, rows, tm):
    m, words = xn.shape
    return pl.pallas_call(
        _dispatch_kernel,
        grid_spec=pltpu.PrefetchScalarGridSpec(
            num_scalar_prefetch=3,
            grid=(m // tm,),
            in_specs=[pl.BlockSpec((tm, words), lambda i, *_: (i, 0))],
            out_specs=pl.BlockSpec(memory_space=pl.ANY),
            scratch_shapes=[pltpu.VMEM((MOE_ROWS, words), jnp.uint32), pltpu.SemaphoreType.DMA(()),
                            pltpu.SemaphoreType.DMA(())],
        ),
        out_shape=jax.ShapeDtypeStruct((rows, words), jnp.uint32),
        compiler_params=_params(("arbitrary",)),
        name="moe_dispatch",
    )(d1, d2, tail_block, xn)


def _expert_kernel(be_ref, nused_ref, x_ref, wg_ref, wu_ref, wd_ref, o_ref, wgb_ref, wub_ref, wdb_ref):
    i = pl.program_id(0)

    @pl.when((i == 0) | (be_ref[i] != be_ref[jnp.maximum(i - 1, 0)]))
    def _():
        wgb_ref[...] = wg_ref[0].astype(BF16)
        wub_ref[...] = wu_ref[0].astype(BF16)
        wdb_ref[...] = wd_ref[0].astype(BF16)

    @pl.when(i < nused_ref[0])
    def _():
        half = wgb_ref.shape[0] // 2
        x_lo, x_hi = _unpack_bf16_pairs(x_ref[...])

        def up(w_ref):
            return _dot(x_lo, w_ref[:half]) + _dot(x_hi, w_ref[half:])

        hid = (jax.nn.silu(up(wgb_ref)) * up(wub_ref)).astype(BF16)
        for g in range(o_ref.shape[1] // LANES):
            y = _dot(hid, wdb_ref[:, g * 2 * LANES:(g + 1) * 2 * LANES])
            o_ref[:, g * LANES:(g + 1) * LANES] = _pack_bf16_pairs(y.astype(BF16))

    @pl.when(i >= nused_ref[0])
    def _():
        o_ref[...] = jnp.zeros_like(o_ref)


def _experts(xs, block_e, nused, w_gate, w_up, w_down):
    rows, words = xs.shape
    _, dm, ff = w_gate.shape
    return pl.pallas_call(
        _expert_kernel,
        grid_spec=pltpu.PrefetchScalarGridSpec(
            num_scalar_prefetch=2,
            grid=(rows // MOE_ROWS,),
            in_specs=[
                pl.BlockSpec((MOE_ROWS, words), lambda i, be, nu: (i, 0)),
                pl.BlockSpec((1, dm, ff), lambda i, be, nu: (be[i], 0, 0)),
                pl.BlockSpec((1, dm, ff), lambda i, be, nu: (be[i], 0, 0)),
                pl.BlockSpec((1, ff, dm), lambda i, be, nu: (be[i], 0, 0)),
            ],
            out_specs=pl.BlockSpec((MOE_ROWS, words), lambda i, be, nu: (i, 0)),
            scratch_shapes=[pltpu.VMEM((dm, ff), BF16), pltpu.VMEM((dm, ff), BF16), pltpu.VMEM((ff, dm), BF16)],
        ),
        out_shape=jax.ShapeDtypeStruct((rows, words), jnp.uint32),
        compiler_params=_params(("arbitrary",)),
        name="moe_experts",
    )(block_e, nused, xs, w_gate, w_up, w_down)


def _combine_kernel(d1_ref, d2_ref, x_ref, info_ref, g_ref, ys_ref, o_ref, buf_a, buf_b, sem_a, sem_b):
    ts = buf_a.shape[1]
    dm = x_ref.shape[1]
    step, nsteps = pl.program_id(0), pl.num_programs(0)
    base = step * 2 * ts

    def row_copy(buf_ref, sem, r, slot, src):
        return pltpu.make_async_copy(ys_ref.at[pl.ds(src, 1)], buf_ref.at[slot, pl.ds(r, 1)], sem)

    def issue(buf_ref, sem, tok0):
        for r in range(ts):
            row_copy(buf_ref, sem, r, 0, d1_ref[tok0 + r]).start()
            row_copy(buf_ref, sem, r, 1, d2_ref[tok0 + r]).start()

    def wait(buf_ref, sem):
        for slot in range(2):
            pltpu.make_async_copy(ys_ref.at[pl.ds(0, ts)], buf_ref.at[slot], sem).wait()

    def finish(buf_ref, rows):
        info = info_ref[rows, :]
        w1, w2 = info[:, 2:3], info[:, 3:4]
        ssq = jnp.zeros((ts, 1), F32)
        for g in range(dm // (2 * LANES)):
            a = _unpack_f32_pairs(buf_ref[0, :, g * LANES:(g + 1) * LANES])
            b = _unpack_f32_pairs(buf_ref[1, :, g * LANES:(g + 1) * LANES])
            for half in range(2):
                cols = slice((2 * g + half) * LANES, (2 * g + half + 1) * LANES)
                y = x_ref[rows, cols] + (w1 * a[half] + w2 * b[half])
                ssq = ssq + jnp.sum(y * y, axis=-1, keepdims=True)
                o_ref[rows, cols] = y
        o_ref[rows, :] = o_ref[rows, :] * lax.rsqrt(ssq / dm + EPS) * g_ref[...]

    @pl.when(step == 0)
    def _():
        issue(buf_a, sem_a, base)

    wait(buf_a, sem_a)
    issue(buf_b, sem_b, base + ts)
    finish(buf_a, slice(0, ts))
    wait(buf_b, sem_b)
    issue(buf_a, sem_a, jnp.where(step + 1 < nsteps, base + 2 * ts, base))
    finish(buf_b, slice(ts, 2 * ts))

    @pl.when(step + 1 == nsteps)
    def _():
        wait(buf_a, sem_a)


def _combine(x2, info, gain, ys, d1, d2, ts):
    m, dm = x2.shape
    tm = 2 * ts
    gather_buf = pltpu.VMEM((2, ts, dm // 2), jnp.uint32)
    return pl.pallas_call(
        _combine_kernel,
        grid_spec=pltpu.PrefetchScalarGridSpec(
            num_scalar_prefetch=2,
            grid=(m // tm,),
            in_specs=[
                pl.BlockSpec((tm, dm), lambda i, *_: (i, 0)),
                pl.BlockSpec((tm, LANES), lambda i, *_: (i, 0)),
                pl.BlockSpec((1, dm), lambda i, *_: (0, 0)),
                pl.BlockSpec(memory_space=pl.ANY),
            ],
            out_specs=pl.BlockSpec((tm, dm), lambda i, *_: (i, 0)),
            scratch_shapes=[gather_buf, gather_buf, pltpu.SemaphoreType.DMA(()), pltpu.SemaphoreType.DMA(())],
        ),
        out_shape=jax.ShapeDtypeStruct((m, dm), F32),
        compiler_params=_params(("arbitrary",)),
        name="moe_combine",
    )(d1, d2, x2, info, gain.reshape(1, dm), ys)


def _moe_and_final_norm(x2, ffn_gain, w_group, b_group, w_router, b_router, w_gate, w_up, w_down, final_gain, tm):
    m, _ = x2.shape
    xn, info, infot, cnt = _router(x2, ffn_gain, w_group, b_group, w_router, b_router, tm)
    counts = cnt[0, :N_EXPERTS].astype(jnp.int32)
    padded = (counts + MOE_ROWS - 1) // MOE_ROWS * MOE_ROWS
    pend = jnp.cumsum(padded)
    pstart = pend - padded
    fields = infot.astype(jnp.int32)
    d1 = pstart[fields[0]] + fields[4]
    d2 = pstart[fields[1]] + fields[5]
    nblocks = 2 * m // MOE_ROWS + N_EXPERTS
    first_row = jnp.arange(nblocks, dtype=jnp.int32) * MOE_ROWS
    block_e = jnp.minimum(jnp.sum(pend[None, :] <= first_row[:, None], axis=1), N_EXPERTS - 1).astype(jnp.int32)
    nused = (pend[-1:] // MOE_ROWS).astype(jnp.int32)
    tail_block = jnp.concatenate([jnp.where(padded > 0, pend // MOE_ROWS - 1, -1), pend[-1:] // MOE_ROWS]).astype(jnp.int32)
    xs = _dispatch(xn, d1, d2, tail_block, nblocks * MOE_ROWS, tm)
    ys = _experts(xs, block_e, nused, w_gate, w_up, w_down)
    return _combine(x2, info, final_gain, ys, d1, d2, tm)


def kernel(x, attn_norm, w_in, hg_lb_logits, hg_out_norm, cmp_pos_k, cmp_w1_k, cmp_w2_k, cmp_pos_v, cmp_w1_v,
           cmp_w2_v, nsa_out_norm, w_out, ffn_norm, moe_w_group, moe_b_group, moe_w_router, moe_b_router,
           moe_w_gate, moe_w_up, moe_w_down, final_norm):
    bsz, seq, dm = x.shape
    xt = x.reshape(bsz * seq, dm)
    w = w_in[0]
    p0, p1 = 4 * HG_QK, 4 * HG_QK + NSA_WIDTH + NSA_KV
    tn = PROJ_COLS
    assert p0 % tn == 0 and p1 % tn == 0
    w_all = _column_blocks(jnp.pad(w.astype(BF16), ((0, 0), (0, (-w.shape[1]) % tn))), tn)
    w_prec = _column_blocks(jnp.stack(_split2(w[:, p0:p1])), tn)
    proj3 = _normed_matmul(xt, attn_norm[0], w_prec, w_all, p0 // tn, PROJ_ROWS).reshape(bsz, seq, -1)
    hg0 = p1 - p0
    y_hg = _hgrn(proj3, hg0, hg_lb_logits, hg_out_norm[0], HG_SEQ)
    y_nsa = _nsa(proj3, proj3, hg0 + 4 * HG_QK, hg0 + 4 * HG_QK + 5 * NSA_KV,
                 (cmp_pos_k[0], cmp_w1_k[0], cmp_w2_k[0]), (cmp_pos_v[0], cmp_w1_v[0], cmp_w2_v[0]))
    x2 = _out_proj(y_hg.reshape(bsz * seq, -1), y_nsa.reshape(bsz * seq, -1), nsa_out_norm[0], w_out[0], xt,
                   PROJ_ROWS, OUT_COLS)
    out = _moe_and_final_norm(x2, ffn_norm[0], moe_w_group[0], moe_b_group[0], moe_w_router[0], moe_b_router[0],
                              moe_w_gate[0], moe_w_up[0], moe_w_down[0], final_norm, MOE_TOKENS)
    return out.reshape(bsz, seq, dm)
```

```python
import functools

import jax
import jax.numpy as jnp
import numpy as np
from jax import lax
from jax.experimental import pallas as pl
from jax.experimental.pallas import tpu as pltpu

F32 = jnp.float32
BF16 = jnp.bfloat16

EPS = 1e-6
ROPE_THETA = 10000.0
NEG = -1e30
BIG = 1e9
LOG2E = 1.4426950408889634

HG_HEADS = 8
HG_DIM = 128
HG_QK = HG_HEADS * HG_DIM
HG_CHUNK = 64
HG_SUB = 8

NSA_HEADS = 16
NSA_GROUPS = 4
NSA_REP = NSA_HEADS // NSA_GROUPS
NSA_DIM = 64
NSA_WIDTH = NSA_HEADS * NSA_DIM
NSA_KV = NSA_GROUPS * NSA_DIM
NSA_VPAD = 16
NSA_GATE_ROWS = 16
CMP_BLOCK = 32
CMP_STRIDE = 16
CMP_HIDDEN = 256
SLC_BLOCK = 64
SLC_TOPK = 16
SLC_LOCAL = 2
WIN = 512

MOE_GROUPS = 4
MOE_EPG = 8
N_EXPERTS = MOE_GROUPS * MOE_EPG

LANES = 128
VMEM_LIMIT = 56 * 1024 * 1024

PROJ_ROWS = 1024
PROJ_COLS = 256
OUT_COLS = 512
HG_SEQ = 1024
HG_UNROLL = 8
NSA_PREP_SEQ = 512
NSA_TQ = 256
NSA_TK = 256
MOE_TOKENS = 256


def _params(semantics, **kw):
    return pltpu.CompilerParams(dimension_semantics=semantics, vmem_limit_bytes=VMEM_LIMIT, **kw)


def _split2(a):
    hi = a.astype(BF16)
    return hi, (a - hi.astype(F32)).astype(BF16)


def _split3(a):
    hi = a.astype(BF16)
    r = a - hi.astype(F32)
    mid = r.astype(BF16)
    return hi, mid, (r - mid.astype(F32)).astype(BF16)


def _dot(a, b):
    return jnp.dot(a, b, preferred_element_type=F32)


def _dot_nt(a, b):
    return lax.dot_general(a, b, (((1,), (1,)), ((), ())), preferred_element_type=F32)


def _dot3(a, b):
    a_hi, a_lo = _split2(a)
    b_hi, b_lo = _split2(b)
    return _dot(a_hi, b_hi) + (_dot(a_hi, b_lo) + _dot(a_lo, b_hi))


def _rms(x, gain):
    return x * lax.rsqrt(jnp.mean(x * x, axis=-1, keepdims=True) + EPS) * gain


def _normed_matmul_kernel(x_ref, g_ref, wp_ref, w_ref, o_ref, h_ref, *, nprec):
    j = pl.program_id(1)

    @pl.when(j == 0)
    def _():
        y = _rms(x_ref[...], g_ref[...])
        hi = y.astype(BF16)
        h_ref[0] = hi
        h_ref[1] = (y - hi.astype(F32)).astype(BF16)

    @pl.when(j < nprec)
    def _():
        o_ref[...] = _dot(h_ref[0], wp_ref[0]) + (_dot(h_ref[0], wp_ref[1]) + _dot(h_ref[1], wp_ref[0]))

    @pl.when(j >= nprec)
    def _():
        o_ref[...] = _dot(h_ref[0], w_ref[...])


def _column_blocks(w, tn):
    *lead, k, n = w.shape
    return jnp.moveaxis(w.reshape(*lead, k, n // tn, tn), -2, -3)


def _normed_matmul(x, gain, wp_blocks, w_blocks, first, tm):
    m, k = x.shape
    nb, _, tn = w_blocks.shape
    nprec = wp_blocks.shape[1]

    def single_pass_block(j):
        r = jnp.maximum(j - nprec, 0)
        return jnp.where(r < first, r, r + nprec)

    return pl.pallas_call(
        functools.partial(_normed_matmul_kernel, nprec=nprec),
        grid=(m // tm, nb),
        in_specs=[
            pl.BlockSpec((tm, k), lambda i, j: (i, 0)),
            pl.BlockSpec((1, k), lambda i, j: (0, 0)),
            pl.BlockSpec((2, None, k, tn), lambda i, j: (0, jnp.minimum(j, nprec - 1), 0, 0)),
            pl.BlockSpec((None, k, tn), lambda i, j: (single_pass_block(j), 0, 0)),
        ],
        out_specs=pl.BlockSpec((tm, tn), lambda i, j: (i, j)),
        out_shape=jax.ShapeDtypeStruct((m, nb * tn), F32),
        scratch_shapes=[pltpu.VMEM((2, tm, k), BF16)],
        compiler_params=_params(("parallel", "arbitrary")),
        name="normed_matmul",
    )(x, gain.reshape(1, k), wp_blocks, w_blocks)


def _hgrn_consts():
    c, sub = HG_CHUNK, HG_SUB
    tri = np.tile(np.tril(np.ones((c, c), np.float32)), (1, 3))
    gsum = (np.arange(c * sub)[None, :] // sub == np.arange(c)[:, None]).astype(np.float32)
    return jnp.asarray(tri, BF16), jnp.asarray(gsum, BF16)


HG_PAR = 2


def _hgrn_kernel(q_ref, f_ref, i_ref, g_ref, lbl_ref, gain_ref, tri_ref, gsum_ref, o_ref, *scratch):
    c, sub, d = HG_CHUNK, HG_SUB, HG_DIM
    nsub = c // sub
    heads = range(HG_PAR)
    st_refs, p_refs, cl_refs, qs_refs, k_refs = (scratch[i * HG_PAR:(i + 1) * HG_PAR] for i in range(5))

    @pl.when(pl.program_id(2) == 0)
    def _():
        for h in heads:
            st_refs[h][...] = jnp.zeros_like(st_refs[h])

    l0 = lbl_ref[0:1, :]
    l1 = lbl_ref[1:2, :]
    lmax = jnp.maximum(l0, l1)
    e0 = jnp.exp(l0 - lmax)
    lb_all = e0 / (e0 + jnp.exp(l1 - lmax))
    srow = lax.broadcasted_iota(jnp.int32, (sub, d), 0)
    ones = jnp.ones((d, d), BF16)

    def rows_at(x, start):
        parts = ([jnp.zeros((start, d), F32)] if start else []) + [x]
        if start + x.shape[0] < c:
            parts.append(jnp.zeros((c - start - x.shape[0], d), F32))
        return jnp.concatenate(parts, axis=0)

    nchunks = q_ref.shape[1] // c
    cols = [slice(h * d, (h + 1) * d) for h in heads]

    def load(ci):
        rows = pl.ds(pl.multiple_of(ci * c, c), c)
        out = []
        for h in heads:
            lb = lb_all[:, cols[h]]
            f = lb + (1.0 - lb) * jax.nn.sigmoid(f_ref[0, rows, cols[h]])
            bcum = _dot(tri_ref[...], jnp.concatenate(_split3(jnp.log(f)), axis=0))
            out.append((q_ref[0, rows, cols[h]] * (d ** -0.5), 1.0 - f, i_ref[0, rows, cols[h]], bcum))
        return tuple(out)

    def chunk(ci, cur):
        nxt = load(jnp.minimum(ci + 1, nchunks - 1))
        rows = pl.ds(pl.multiple_of(ci * c, c), c)
        q, k, v, bcum = ([cur[h][i] for h in heads] for i in range(4))
        a_off, o_inter = [], []
        for h in heads:
            b = bcum[h]
            edge = [b[i * sub - 1:i * sub] for i in range(1, nsub + 1)]
            cl = b - jnp.concatenate([jnp.zeros((sub, d), F32)] + [jnp.broadcast_to(e, (sub, d)) for e in edge[:-1]],
                                     axis=0)
            cl_refs[h][...] = cl * LOG2E
            qs_refs[h][...] = q[h]
            k_refs[h][...] = k[h]
            qe = q[h] * jnp.exp(cl)
            qcat, kcat = [], []
            for i in range(1, nsub):
                qcat.append(rows_at(qe[i * sub:(i + 1) * sub], i * sub))
                kcat.append(rows_at(k[h][:i * sub] * jnp.exp(edge[i - 1] - b[:i * sub]), 0))
            a_off.append(_dot_nt(jnp.concatenate(qcat, axis=1).astype(BF16),
                                 jnp.concatenate(kcat, axis=1).astype(BF16)))
            st = st_refs[h][...]
            o_inter.append(_dot_nt((q[h] * jnp.exp(b)).astype(BF16), st.astype(BF16)))
            kd = k[h] * jnp.exp(edge[-1] - b)
            st_refs[h][...] = st * jnp.exp(edge[-1]) + _dot(v[h].T.astype(BF16), kd.astype(BF16))
        r2 = []
        for h in heads:
            for t in range(c):
                j0 = (t // sub) * sub
                dlt = cl_refs[h][t:t + 1, :] - cl_refs[h][j0:j0 + sub, :]
                e = jnp.exp2(jnp.where(srow <= t - j0, dlt, NEG))
                p_refs[h][t * sub:(t + 1) * sub, :] = (qs_refs[h][t:t + 1, :] * k_refs[h][j0:j0 + sub, :] * e).astype(BF16)
            r2.append(_dot(p_refs[h][...], ones))
        o = []
        for h in heads:
            o.append(o_inter[h] + _dot(a_off[h].astype(BF16), v[h].astype(BF16)))
        for h in heads:
            x = r2[h].reshape(nsub, sub, sub, d) * v[h].reshape(nsub, 1, sub, d)
            o[h] = o[h] + _dot(gsum_ref[...], x.reshape(c * sub, d).astype(BF16))
        for h in heads:
            gate = jax.nn.silu(g_ref[0, rows, cols[h]])
            o_ref[0, rows, cols[h]] = (_rms(o[h], gain_ref[...]) * gate).astype(o_ref.dtype)
        return nxt

    lax.fori_loop(0, nchunks, chunk, load(0), unroll=HG_UNROLL)


def _hgrn(proj3, col0, lb_logits, out_gain, tseq):
    bsz, seq, _ = proj3.shape
    d, c, sub = HG_DIM, HG_CHUNK, HG_SUB
    wst, gsum = _hgrn_consts()
    groups = HG_HEADS // HG_PAR
    width = HG_PAR * d
    assert col0 % width == 0

    def col(off):
        return pl.BlockSpec((1, tseq, width), lambda b, h, t: (b, t, col0 // width + off * groups + h))

    per_head = [pltpu.VMEM((d, d), F32), pltpu.VMEM((c * sub, d), BF16), pltpu.VMEM((c, d), F32),
                pltpu.VMEM((c, d), F32), pltpu.VMEM((c, d), F32)]
    return pl.pallas_call(
        _hgrn_kernel,
        grid=(bsz, groups, seq // tseq),
        in_specs=[
            col(0), col(1), col(2), col(3),
            pl.BlockSpec((2, width), lambda b, h, t: (0, h)),
            pl.BlockSpec((1, d), lambda b, h, t: (0, 0)),
            pl.BlockSpec(wst.shape, lambda b, h, t: (0, 0)),
            pl.BlockSpec(gsum.shape, lambda b, h, t: (0, 0)),
        ],
        out_specs=pl.BlockSpec((1, tseq, width), lambda b, h, t: (b, t, h)),
        out_shape=jax.ShapeDtypeStruct((bsz, seq, HG_QK), BF16),
        scratch_shapes=[s for s in per_head for _ in range(HG_PAR)],
        compiler_params=_params(("parallel", "parallel", "arbitrary")),
        name="hgrn2",
    )(proj3, proj3, proj3, proj3, lb_logits, out_gain.reshape(1, d), wst, gsum)


def _rope(x, cs, sn):
    lane = lax.broadcasted_iota(jnp.int32, x.shape, 1)
    partner = jnp.where(lane % NSA_DIM < NSA_DIM // 2, pltpu.roll(x, LANES - NSA_DIM // 2, 1),
                        pltpu.roll(x, NSA_DIM // 2, 1))
    return x * cs + partner * sn


def _nsa_prep_kernel(q_ref, ksl_ref, vsl_ref, kwn_ref, vwn_ref, gate_ref, cs_ref, sn_ref,
                     qrot_ref, kslo_ref, vslo_ref, kwno_ref, vwno_ref, gateo_ref):
    cs = cs_ref[...]
    sn = sn_ref[...]
    gate_t = gate_ref[0].T
    per_group = 3 * NSA_REP
    gateo_ref[...] = jnp.zeros_like(gateo_ref)
    for g in range(NSA_GROUPS):
        gateo_ref[0, g, :per_group] = gate_t[g * per_group:(g + 1) * per_group]
    for c in range(NSA_WIDTH // LANES):
        cols = slice(c * LANES, (c + 1) * LANES)
        qrot_ref[0, :, cols] = (_rope(q_ref[0, :, cols], cs, sn) * (NSA_DIM ** -0.5 * LOG2E)).astype(BF16)
    tseq = q_ref.shape[1]
    lane = lax.broadcasted_iota(jnp.int32, (tseq, LANES), 1)
    block = (pl.program_id(1) * tseq + lax.broadcasted_iota(jnp.int32, (tseq, LANES), 0)) // SLC_BLOCK
    block_onehot = jnp.where(lane - NSA_DIM == block, 1.0, 0.0)
    ones_rows = jnp.where(lax.broadcasted_iota(jnp.int32, (NSA_VPAD, tseq), 0) == 0, 1.0, 0.0).astype(BF16)
    for c in range(NSA_KV // LANES):
        cols = slice(c * LANES, (c + 1) * LANES)
        ks = _rope(ksl_ref[0, :, cols], cs, sn)
        kw = _rope(kwn_ref[0, :, cols], cs, sn).astype(BF16)
        vs = vsl_ref[0, :, cols].T.astype(BF16)
        vw = vwn_ref[0, :, cols].T.astype(BF16)
        for half in range(LANES // NSA_DIM):
            g = c * (LANES // NSA_DIM) + half
            hs = slice(half * NSA_DIM, (half + 1) * NSA_DIM)
            ks_g = ks if half == 0 else pltpu.roll(ks, NSA_DIM, 1)
            kslo_ref[0, g] = jnp.where(lane < NSA_DIM, ks_g, block_onehot).astype(BF16)
            kwno_ref[0, g] = kw[:, hs]
            vslo_ref[0, g, :NSA_DIM] = vs[hs, :]
            vslo_ref[0, g, NSA_DIM:] = ones_rows
            vwno_ref[0, g, :NSA_DIM] = vw[hs, :]
            vwno_ref[0, g, NSA_DIM:] = ones_rows


def _nsa_prep(prec3, rest3, kv_off, gate_off, tseq):
    assert gate_off % LANES == 0 and 3 * NSA_HEADS <= LANES
    bsz, seq, _ = prec3.shape
    half = NSA_DIM // 2
    inv = 1.0 / (ROPE_THETA ** (jnp.arange(0, NSA_DIM, 2, dtype=F32) / NSA_DIM))
    ang = jnp.arange(seq, dtype=F32)[:, None] * inv[None, :]
    cs = jnp.tile(jnp.cos(ang), (1, LANES // half))
    sn = jnp.tile(jnp.concatenate([-jnp.sin(ang), jnp.sin(ang)], axis=1), (1, LANES // NSA_DIM))
    kvb = kv_off // NSA_KV

    def kv_in(i):
        return pl.BlockSpec((1, tseq, NSA_KV), lambda b, t: (b, t, kvb + i))

    assert NSA_DIM + seq // SLC_BLOCK <= LANES

    def k_out(width):
        return (pl.BlockSpec((1, NSA_GROUPS, tseq, width), lambda b, t: (b, 0, t, 0)),
                jax.ShapeDtypeStruct((bsz, NSA_GROUPS, seq, width), BF16))

    (ksl_out, ksl_shape), (kwn_out, kwn_shape) = k_out(LANES), k_out(NSA_DIM)
    v_out = pl.BlockSpec((1, NSA_GROUPS, NSA_DIM + NSA_VPAD, tseq), lambda b, t: (b, 0, 0, t))
    v_shape = jax.ShapeDtypeStruct((bsz, NSA_GROUPS, NSA_DIM + NSA_VPAD, seq), BF16)
    tab = pl.BlockSpec((tseq, LANES), lambda b, t: (t, 0))
    return pl.pallas_call(
        _nsa_prep_kernel,
        grid=(bsz, seq // tseq),
        in_specs=[pl.BlockSpec((1, tseq, NSA_WIDTH), lambda b, t: (b, t, 0)), kv_in(0), kv_in(1), kv_in(2), kv_in(3),
                  pl.BlockSpec((1, tseq, LANES), lambda b, t: (b, t, gate_off // LANES)), tab, tab],
        out_specs=[pl.BlockSpec((1, tseq, NSA_WIDTH), lambda b, t: (b, t, 0)), ksl_out, v_out, kwn_out, v_out,
                   pl.BlockSpec((1, NSA_GROUPS, NSA_GATE_ROWS, tseq), lambda b, t: (b, 0, 0, t))],
        out_shape=[jax.ShapeDtypeStruct((bsz, seq, NSA_WIDTH), BF16), ksl_shape, v_shape, kwn_shape, v_shape,
                   jax.ShapeDtypeStruct((bsz, NSA_GROUPS, NSA_GATE_ROWS, seq), F32)],
        compiler_params=_params(("parallel", "parallel")),
        name="nsa_prep",
    )(prec3, rest3, rest3, rest3, rest3, rest3, cs, sn)


def _compress_kernel(*refs, precise):
    *kv_refs, pos_ref, w1_ref, w2_ref, o_ref = refs
    mm = _dot3 if precise else (lambda a, b: _dot(a.astype(BF16), b.astype(BF16)))
    nu = kv_refs[0].shape[1] // CMP_STRIDE
    per_tile = LANES // NSA_DIM
    for g in range(NSA_GROUPS):
        kv_ref, lanes = kv_refs[g // per_tile], slice((g % per_tile) * NSA_DIM, (g % per_tile + 1) * NSA_DIM)
        u = jnp.concatenate([kv_ref[0, pl.ds(l, nu, stride=CMP_STRIDE), :][:, lanes] for l in range(CMP_STRIDE)],
                            axis=1)
        ya = mm(u + pos_ref[0:1, :], w1_ref[0])
        yb = mm(u + pos_ref[1:2, :], w1_ref[1])
        hid = ya + pltpu.roll(yb, nu - 1, 0)
        o_ref[0, g] = mm(jax.nn.gelu(hid), w2_ref[...])


def _compress(proj3, col, pos, w1, w2, precise):
    bsz, seq, _ = proj3.shape
    nu = seq // CMP_STRIDE
    width = CMP_STRIDE * NSA_DIM
    assert col % LANES == 0
    tiles = NSA_KV // LANES
    return pl.pallas_call(
        functools.partial(_compress_kernel, precise=precise),
        grid=(bsz,),
        in_specs=[pl.BlockSpec((1, seq, LANES), lambda b, t=t: (b, 0, col // LANES + t)) for t in range(tiles)] + [
            pl.BlockSpec((2, width), lambda b: (0, 0)),
            pl.BlockSpec((2, width, CMP_HIDDEN), lambda b: (0, 0, 0)),
            pl.BlockSpec((CMP_HIDDEN, NSA_DIM), lambda b: (0, 0)),
        ],
        out_specs=pl.BlockSpec((1, NSA_GROUPS, nu, NSA_DIM), lambda b: (b, 0, 0, 0)),
        out_shape=jax.ShapeDtypeStruct((bsz, NSA_GROUPS, nu, NSA_DIM), F32),
        compiler_params=_params(("parallel",)),
        name="nsa_compress",
    )(*([proj3] * tiles), pos.reshape(2, width), w1.reshape(2, width, CMP_HIDDEN), w2)


def _nsa_attn_kernel(qraw_ref, qrot_ref, kc_ref, vct_ref, ksl_ref, vslt_ref, kwn_ref, vwnt_ref, gate_ref, aggt_ref,
                     o_ref, sa_ref, sb_ref, *win_refs, topk, tk):
    tq = qraw_ref.shape[1]
    nu = kc_ref.shape[2]
    ns = aggt_ref.shape[0]
    rep, dk = NSA_REP, NSA_DIM
    qs = pl.program_id(2) * tq
    tpos = qs + lax.broadcasted_iota(jnp.int32, (1, tq), 1)

    qrt = (qraw_ref[0] * (dk ** -0.5 * LOG2E)).T
    kc_hi, kc_lo = _split2(kc_ref[0, 0])
    vct = vct_ref[0, 0].astype(BF16)
    crow = lax.broadcasted_iota(jnp.int32, (nu, tq), 0)
    m_c = (crow * CMP_STRIDE + CMP_BLOCK - 1 <= tpos) & (crow < nu - 1)
    q_hi, q_lo = _split2(jnp.concatenate([qrt[r * dk:(r + 1) * dk] for r in range(rep)], axis=1))
    s_all = _dot(jnp.concatenate([kc_hi, kc_hi, kc_lo], axis=1),
                 jnp.concatenate([q_hi, q_lo, q_hi], axis=0))

    qt = qrot_ref[0].astype(F32).T.astype(BF16)
    qt_all = jnp.concatenate([qt[r * dk:(r + 1) * dk] for r in range(rep)], axis=1)
    hi = (qs + tq) // tk

    def key_tile(ktc):
        return pl.ds(pl.multiple_of(ktc * tk, tk), tk)

    def scores(k_ref, ktc, dst_ref):
        dst_ref[...] = _dot(k_ref[0, 0, key_tile(ktc), :], qt_all)

    win_tiles = [hi - len(win_refs) + j for j in range(len(win_refs))]
    for kt, dst_ref in zip(win_tiles, win_refs):
        scores(kwn_ref, jnp.maximum(kt, 0), dst_ref)
    psum = jnp.zeros((nu, tq), F32)
    p_all = []
    has_block = tpos >= CMP_BLOCK - 1
    for r in range(rep):
        s = jnp.where(m_c, s_all[:, r * tq:(r + 1) * tq], NEG)
        e = jnp.exp2(s - jnp.max(s, axis=0, keepdims=True))
        p = e * jnp.where(has_block, 1.0 / jnp.sum(e, axis=0, keepdims=True), 0.0)
        psum = psum + p
        p_all.append(p.astype(BF16))
    o_c_all = _dot(vct, jnp.concatenate(p_all, axis=1))
    o_c = [o_c_all[:, r * tq:(r + 1) * tq] for r in range(rep)]

    p_hi, p_lo = _split2(psum)
    imp = _dot(aggt_ref[...], p_hi) + _dot(aggt_ref[...], p_lo)
    jrow = lax.broadcasted_iota(jnp.int32, (ns, tq), 0)
    dj = jnp.right_shift(tpos, SLC_BLOCK.bit_length() - 1) - jrow
    forced = (jrow == 0) | ((dj >= 0) & (dj < SLC_LOCAL))
    imp = jnp.where(forced, BIG, jnp.where(jrow * SLC_BLOCK <= tpos, imp, -BIG))
    sub8 = lax.broadcasted_iota(jnp.int32, (8, tq), 0)
    chunks = [imp[c * 8:(c + 1) * 8] for c in range(ns // 8)]
    ranks = [jnp.zeros((8, tq), F32) for _ in range(ns // 8)]
    for jp in range(ns):
        row = chunks[jp // 8][jp % 8:jp % 8 + 1]
        for c in range(ns // 8):
            if c < jp // 8:
                ahead = jnp.where(row > chunks[c], 1.0, 0.0)
            elif c > jp // 8:
                ahead = jnp.where(row >= chunks[c], 1.0, 0.0)
            else:
                tie = jnp.where(sub8 > jp % 8, 1.0, 0.0)
                ahead = jnp.where(row > chunks[c], 1.0, jnp.where(row == chunks[c], tie, 0.0))
            ranks[c] = ranks[c] + ahead
    selt = [jnp.where(ranks[c] < topk, 0.0, NEG) for c in range(ns // 8)]

    def consume(vt_ref, ktc, src_ref, carry, mask=None):
        vt = vt_ref[0, 0, :, key_tile(ktc)]
        out = []
        for r in range(rep):
            m_old, acc = carry[r]
            s = src_ref[:, r * tq:(r + 1) * tq]
            if mask is not None:
                s = jnp.where(mask, s, NEG)
            m_new = jnp.maximum(m_old, jnp.max(s, axis=0, keepdims=True))
            alpha = jnp.exp2(m_old - m_new)
            p = jnp.exp2(s - m_new).astype(BF16)
            out.append((m_new, acc * alpha + _dot(vt, p)))
        return tuple(out)

    def normalised(carry):
        return [acc[:dk] * (1.0 / acc[dk:dk + 1]) for _, acc in carry]

    init = tuple((jnp.full((1, tq), NEG, F32), jnp.zeros((dk + NSA_VPAD, tq), F32)) for _ in range(rep))
    krow = lax.broadcasted_iota(jnp.int32, (tk, tq), 0)
    unseen = 1 << 30

    carry = init
    for kt, src_ref in zip(win_tiles, win_refs):
        ktc = jnp.maximum(kt, 0)
        dlt = tpos - (jnp.where(kt >= 0, ktc * tk, unseen) + krow)
        carry = consume(vwnt_ref, ktc, src_ref, carry, mask=pltpu.bitcast(dlt, jnp.uint32) < jnp.uint32(WIN))
    o_w = normalised(carry)

    selb = jnp.concatenate([jnp.concatenate(selt, axis=0)] * rep, axis=1).astype(BF16)
    pad = jnp.zeros((LANES - dk - ns, rep * tq), BF16)
    qt_sel = jnp.concatenate([qt_all, selb] + ([pad] if LANES > dk + ns else []), axis=0)

    def sel_scores(ktc, dst_ref):
        dst_ref[...] = _dot(ksl_ref[0, 0, key_tile(ktc), :], qt_sel)

    def pair(i, carry):
        kt = 2 * i
        sel_scores(kt + 1, sb_ref)
        carry = consume(vslt_ref, kt, sa_ref, carry)
        sel_scores(kt + 2, sa_ref)
        return consume(vslt_ref, kt + 1, sb_ref, carry)

    past = hi - 1
    sel_scores(0, sa_ref)
    carry = lax.fori_loop(0, past // 2, pair, init)
    carry = lax.cond(past % 2 == 1, lambda c: consume(vslt_ref, past - 1, sa_ref, c), lambda c: c, carry)
    sel_scores(past, sb_ref)
    carry = consume(vslt_ref, past, sb_ref, carry, mask=past * tk + krow <= tpos)
    o_s = normalised(carry)

    gate = jax.nn.sigmoid(gate_ref[0, 0])
    o_t = [gate[3 * r:3 * r + 1] * o_c[r] + gate[3 * r + 1:3 * r + 2] * o_s[r] + gate[3 * r + 2:3 * r + 3] * o_w[r]
           for r in range(rep)]
    o_ref[0] = jnp.concatenate(o_t, axis=0).T.astype(o_ref.dtype)


def _nsa_attn(prec3, qrot, kc, vct, ksl, vslt, kwn, vwnt, gates_t, tq, tk):
    bsz, seq, _ = qrot.shape
    nu = seq // CMP_STRIDE
    ns = seq // SLC_BLOCK
    ci = np.arange(nu)[None, :]
    sj = np.arange(ns)[:, None]
    overlap = (ci * CMP_STRIDE < (sj + 1) * SLC_BLOCK) & (ci * CMP_STRIDE + CMP_BLOCK > sj * SLC_BLOCK) & (ci < nu - 1)
    aggt = jnp.asarray(overlap, BF16)
    gw = NSA_REP * NSA_DIM
    assert tq % tk == 0 and seq % tq == 0
    win_tiles = -(-(WIN - 1) // tk) + tq // tk

    def q_spec():
        return pl.BlockSpec((1, tq, gw), lambda b, g, t: (b, t, g))

    def per_group(rows, cols):
        return pl.BlockSpec((1, 1, rows, cols), lambda b, g, t: (b, g, 0, 0))

    return pl.pallas_call(
        functools.partial(_nsa_attn_kernel, topk=min(SLC_TOPK, ns), tk=tk),
        grid=(bsz, NSA_GROUPS, seq // tq),
        in_specs=[q_spec(), q_spec(), per_group(nu, NSA_DIM), per_group(NSA_DIM, nu),
                  per_group(seq, LANES), per_group(NSA_DIM + NSA_VPAD, seq),
                  per_group(seq, NSA_DIM), per_group(NSA_DIM + NSA_VPAD, seq),
                  pl.BlockSpec((1, 1, NSA_GATE_ROWS, tq), lambda b, g, t: (b, g, 0, t)),
                  pl.BlockSpec((ns, nu), lambda b, g, t: (0, 0))],
        out_specs=q_spec(),
        out_shape=jax.ShapeDtypeStruct((bsz, seq, NSA_WIDTH), BF16),
        scratch_shapes=[pltpu.VMEM((tk, NSA_REP * tq), F32)] * (2 + win_tiles),
        compiler_params=_params(("parallel", "parallel", "arbitrary")),
        name="nsa_attention",
    )(prec3, qrot, kc, vct, ksl, vslt, kwn, vwnt, gates_t, aggt)


def _nsa(prec3, rest3, kv_off, gate_off, cmp_k, cmp_v):
    seq = prec3.shape[1]
    qrot, ksl, vslt, kwn, vwnt, gates_t = _nsa_prep(prec3, rest3, kv_off + NSA_KV, gate_off, min(seq, NSA_PREP_SEQ))
    kc = _compress(prec3, NSA_WIDTH, *cmp_k, precise=True)
    vct = _compress(rest3, kv_off, *cmp_v, precise=False).transpose(0, 1, 3, 2)
    return _nsa_attn(prec3, qrot, kc, vct, ksl, vslt, kwn, vwnt, gates_t, NSA_TQ, NSA_TK)


def _out_proj_kernel(yh_ref, yn_ref, g_ref, w_ref, x_ref, o_ref, y_ref):
    @pl.when(pl.program_id(1) == 0)
    def _():
        wh = yh_ref.shape[1]
        y_ref[:, :wh] = yh_ref[...]
        y_ref[:, wh:] = _rms(yn_ref[...].astype(F32), g_ref[...]).astype(BF16)

    o_ref[...] = x_ref[...] + _dot(y_ref[...], w_ref[...])


def _out_proj(y_hg, y_nsa, nsa_gain, w_out, x, tm, tn):
    m, dm = x.shape
    wh, wn = y_hg.shape[1], y_nsa.shape[1]
    return pl.pallas_call(
        _out_proj_kernel,
        grid=(m // tm, dm // tn),
        in_specs=[
            pl.BlockSpec((tm, wh), lambda i, j: (i, 0)),
            pl.BlockSpec((tm, wn), lambda i, j: (i, 0)),
            pl.BlockSpec((1, wn), lambda i, j: (0, 0)),
            pl.BlockSpec((None, wh + wn, tn), lambda i, j: (j, 0, 0)),
            pl.BlockSpec((tm, tn), lambda i, j: (i, j)),
        ],
        out_specs=pl.BlockSpec((tm, tn), lambda i, j: (i, j)),
        out_shape=jax.ShapeDtypeStruct((m, dm), F32),
        scratch_shapes=[pltpu.VMEM((tm, wh + wn), BF16)],
        compiler_params=_params(("parallel", "arbitrary")),
        name="out_proj",
    )(y_hg, y_nsa, nsa_gain.reshape(1, wn), _column_blocks(w_out.astype(BF16), tn), x)


MOE_ROWS = 256


def _pack_bf16_pairs(hi):
    n = hi.shape[1] // 2
    bits = pltpu.bitcast(hi.astype(F32), jnp.uint32)
    return jnp.right_shift(bits[:, :n], jnp.uint32(16)) | (bits[:, n:] & jnp.uint32(0xFFFF0000))


def _unpack_f32_pairs(words):
    lo = pltpu.bitcast(jnp.left_shift(words, jnp.uint32(16)), F32)
    hi = pltpu.bitcast(words & jnp.uint32(0xFFFF0000), F32)
    return lo, hi


def _unpack_bf16_pairs(words):
    lo, hi = _unpack_f32_pairs(words)
    return lo.astype(BF16), hi.astype(BF16)


def _router_kernel(x_ref, g_ref, w_ref, b_ref, tri_ref, xn_ref, info_ref, infot_ref, cnt_ref, carry_ref):
    @pl.when(pl.program_id(0) == 0)
    def _():
        carry_ref[...] = jnp.zeros_like(carry_ref)

    xn = _rms(x_ref[...], g_ref[...])
    hi, lo = _split2(xn)
    xn_ref[...] = _pack_bf16_pairs(hi)
    logits = _dot(hi, w_ref[0]) + (_dot(hi, w_ref[1]) + _dot(lo, w_ref[0])) + b_ref[...]
    lane = lax.broadcasted_iota(jnp.int32, logits.shape, 1).astype(F32)
    none = float(LANES)

    def first_max(mask):
        top = jnp.max(jnp.where(mask, logits, -jnp.inf), axis=-1, keepdims=True)
        return top, jnp.min(jnp.where(mask & (logits == top), lane, none), axis=-1, keepdims=True)

    is_g = lane < MOE_GROUPS
    gmax, gsel = first_max(is_g)
    gw = 1.0 / jnp.sum(jnp.where(is_g, jnp.exp(logits - gmax), 0.0), axis=-1, keepdims=True)
    lo_lane = MOE_GROUPS + gsel * MOE_EPG
    in_grp = (lane >= lo_lane) & (lane < lo_lane + MOE_EPG)
    v1, i1 = first_max(in_grp)
    v2, i2 = first_max(in_grp & (lane != i1))
    e = jnp.exp(v2 - v1)
    w1 = gw / (1.0 + e)
    w2 = gw * e / (1.0 + e)
    e1 = i1 - MOE_GROUPS
    e2 = i2 - MOE_GROUPS
    onehot = jnp.where((lane == e1) | (lane == e2), 1.0, 0.0)
    before = _dot(tri_ref[...], onehot.astype(BF16)) + carry_ref[...]
    r1 = jnp.sum(jnp.where(lane == e1, before, 0.0), axis=-1, keepdims=True)
    r2 = jnp.sum(jnp.where(lane == e2, before, 0.0), axis=-1, keepdims=True)
    carry_ref[...] = carry_ref[...] + jnp.sum(onehot, axis=0, keepdims=True)
    cnt_ref[...] = carry_ref[...]
    info = jnp.zeros_like(logits)
    for idx, val in enumerate((e1, e2, w1, w2, r1, r2)):
        info = jnp.where(lane == idx, val, info)
    info_ref[...] = info
    infot_ref[...] = info.T[:8]


def _router(x2, gain, w_group, b_group, w_router, b_router, tm):
    m, dm = x2.shape
    wcat = jnp.pad(jnp.concatenate([w_group, w_router], axis=1), ((0, 0), (0, LANES - MOE_GROUPS - N_EXPERTS)))
    bcat = jnp.pad(jnp.concatenate([b_group, b_router]), (0, LANES - MOE_GROUPS - N_EXPERTS)).reshape(1, LANES)
    tri = jnp.asarray(np.tril(np.ones((tm, tm), np.float32), -1), BF16)
    return pl.pallas_call(
        _router_kernel,
        grid=(m // tm,),
        in_specs=[
            pl.BlockSpec((tm, dm), lambda i: (i, 0)),
            pl.BlockSpec((1, dm), lambda i: (0, 0)),
            pl.BlockSpec((2, dm, LANES), lambda i: (0, 0, 0)),
            pl.BlockSpec((1, LANES), lambda i: (0, 0)),
            pl.BlockSpec((tm, tm), lambda i: (0, 0)),
        ],
        out_specs=[
            pl.BlockSpec((tm, dm // 2), lambda i: (i, 0)),
            pl.BlockSpec((tm, LANES), lambda i: (i, 0)),
            pl.BlockSpec((8, tm), lambda i: (0, i)),
            pl.BlockSpec((1, LANES), lambda i: (0, 0)),
        ],
        out_shape=[
            jax.ShapeDtypeStruct((m, dm // 2), jnp.uint32),
            jax.ShapeDtypeStruct((m, LANES), F32),
            jax.ShapeDtypeStruct((8, m), F32),
            jax.ShapeDtypeStruct((1, LANES), F32),
        ],
        scratch_shapes=[pltpu.VMEM((1, LANES), F32)],
        compiler_params=_params(("arbitrary",)),
        name="moe_router",
    )(x2, gain.reshape(1, dm), jnp.stack(_split2(wcat)), bcat, tri)


def _dispatch_kernel(d1_ref, d2_ref, tail_ref, xn_ref, xs_ref, zero_ref, sem, zero_sem):
    tm = xn_ref.shape[0]
    base = pl.program_id(0) * tm

    @pl.when(pl.program_id(0) == 0)
    def _():
        zero_ref[...] = jnp.zeros_like(zero_ref)

        def fill(e):
            rows = zero_ref.shape[0]
            return pltpu.make_async_copy(zero_ref, xs_ref.at[pl.ds(jnp.maximum(tail_ref[e], 0) * rows, rows)], zero_sem)

        for e in range(N_EXPERTS):
            @pl.when(tail_ref[e] >= 0)
            def _():
                fill(e).start()

        def fill_unused(blk):
            rows = zero_ref.shape[0]
            return pltpu.make_async_copy(zero_ref, xs_ref.at[pl.ds(blk * rows, rows)], zero_sem)

        unused = (tail_ref[N_EXPERTS], xs_ref.shape[0] // zero_ref.shape[0])
        lax.fori_loop(*unused, lambda blk, c: (fill_unused(blk).start(), c)[1], 0)
        lax.fori_loop(*unused, lambda blk, c: (fill_unused(blk).wait(), c)[1], 0)
        for e in range(N_EXPERTS):
            @pl.when(tail_ref[e] >= 0)
            def _():
                fill(e).wait()

    def row_copy(i, dest):
        return pltpu.make_async_copy(xn_ref.at[pl.ds(i, 1)], xs_ref.at[pl.ds(dest, 1)], sem)

    for i in range(tm):
        row_copy(i, d1_ref[base + i]).start()
        row_copy(i, d2_ref[base + i]).start()
    for _ in range(2):
        pltpu.make_async_copy(xn_ref, xs_ref.at[pl.ds(0, tm)], sem).wait()


def _dispatch(xn, d1, d2, tail_block, rows, tm):
    m, words = xn.shape
    return pl.pallas_call(
        _dispatch_kernel,
        grid_spec=pltpu.PrefetchScalarGridSpec(
            num_scalar_prefetch=3,
            grid=(m // tm,),
            in_specs=[pl.BlockSpec((tm, words), lambda i, *_: (i, 0))],
            out_specs=pl.BlockSpec(memory_space=pl.ANY),
            scratch_shapes=[pltpu.VMEM((MOE_ROWS, words), jnp.uint32), pltpu.SemaphoreType.DMA(()),
                            pltpu.SemaphoreType.DMA(())],
        ),
        out_shape=jax.ShapeDtypeStruct((rows, words), jnp.uint32),
        compiler_params=_params(("arbitrary",)),
        name="moe_dispatch",
    )(d1, d2, tail_block, xn)


def _expert_kernel(be_ref, nused_ref, x_ref, wg_ref, wu_ref, wd_ref, o_ref, wgb_ref, wub_ref, wdb_ref):
    i = pl.program_id(0)

    @pl.when((i == 0) | (be_ref[i] != be_ref[jnp.maximum(i - 1, 0)]))
    def _():
        wgb_ref[...] = wg_ref[0].astype(BF16)
        wub_ref[...] = wu_ref[0].astype(BF16)
        wdb_ref[...] = wd_ref[0].astype(BF16)

    @pl.when(i < nused_ref[0])
    def _():
        half = wgb_ref.shape[0] // 2
        x_lo, x_hi = _unpack_bf16_pairs(x_ref[...])

        def up(w_ref):
            return _dot(x_lo, w_ref[:half]) + _dot(x_hi, w_ref[half:])

        hid = (jax.nn.silu(up(wgb_ref)) * up(wub_ref)).astype(BF16)
        for g in range(o_ref.shape[1] // LANES):
            y = _dot(hid, wdb_ref[:, g * 2 * LANES:(g + 1) * 2 * LANES])
            o_ref[:, g * LANES:(g + 1) * LANES] = _pack_bf16_pairs(y.astype(BF16))

    @pl.when(i >= nused_ref[0])
    def _():
        o_ref[...] = jnp.zeros_like(o_ref)


def _experts(xs, block_e, nused, w_gate, w_up, w_down):
    rows, words = xs.shape
    _, dm, ff = w_gate.shape
    return pl.pallas_call(
        _expert_kernel,
        grid_spec=pltpu.PrefetchScalarGridSpec(
            num_scalar_prefetch=2,
            grid=(rows // MOE_ROWS,),
            in_specs=[
                pl.BlockSpec((MOE_ROWS, words), lambda i, be, nu: (i, 0)),
                pl.BlockSpec((1, dm, ff), lambda i, be, nu: (be[i], 0, 0)),
                pl.BlockSpec((1, dm, ff), lambda i, be, nu: (be[i], 0, 0)),
                pl.BlockSpec((1, ff, dm), lambda i, be, nu: (be[i], 0, 0)),
            ],
            out_specs=pl.BlockSpec((MOE_ROWS, words), lambda i, be, nu: (i, 0)),
            scratch_shapes=[pltpu.VMEM((dm, ff), BF16), pltpu.VMEM((dm, ff), BF16), pltpu.VMEM((ff, dm), BF16)],
        ),
        out_shape=jax.ShapeDtypeStruct((rows, words), jnp.uint32),
        compiler_params=_params(("arbitrary",)),
        name="moe_experts",
    )(block_e, nused, xs, w_gate, w_up, w_down)


def _combine_kernel(d1_ref, d2_ref, x_ref, info_ref, g_ref, ys_ref, o_ref, buf_a, buf_b, sem_a, sem_b):
    ts = buf_a.shape[1]
    dm = x_ref.shape[1]
    step, nsteps = pl.program_id(0), pl.num_programs(0)
    base = step * 2 * ts

    def row_copy(buf_ref, sem, r, slot, src):
        return pltpu.make_async_copy(ys_ref.at[pl.ds(src, 1)], buf_ref.at[slot, pl.ds(r, 1)], sem)

    def issue(buf_ref, sem, tok0):
        for r in range(ts):
            row_copy(buf_ref, sem, r, 0, d1_ref[tok0 + r]).start()
            row_copy(buf_ref, sem, r, 1, d2_ref[tok0 + r]).start()

    def wait(buf_ref, sem):
        for slot in range(2):
            pltpu.make_async_copy(ys_ref.at[pl.ds(0, ts)], buf_ref.at[slot], sem).wait()

    def finish(buf_ref, rows):
        info = info_ref[rows, :]
        w1, w2 = info[:, 2:3], info[:, 3:4]
        ssq = jnp.zeros((ts, 1), F32)
        for g in range(dm // (2 * LANES)):
            a = _unpack_f32_pairs(buf_ref[0, :, g * LANES:(g + 1) * LANES])
            b = _unpack_f32_pairs(buf_ref[1, :, g * LANES:(g + 1) * LANES])
            for half in range(2):
                cols = slice((2 * g + half) * LANES, (2 * g + half + 1) * LANES)
                y = x_ref[rows, cols] + (w1 * a[half] + w2 * b[half])
                ssq = ssq + jnp.sum(y * y, axis=-1, keepdims=True)
                o_ref[rows, cols] = y
        o_ref[rows, :] = o_ref[rows, :] * lax.rsqrt(ssq / dm + EPS) * g_ref[...]

    @pl.when(step == 0)
    def _():
        issue(buf_a, sem_a, base)

    wait(buf_a, sem_a)
    issue(buf_b, sem_b, base + ts)
    finish(buf_a, slice(0, ts))
    wait(buf_b, sem_b)
    issue(buf_a, sem_a, jnp.where(step + 1 < nsteps, base + 2 * ts, base))
    finish(buf_b, slice(ts, 2 * ts))

    @pl.when(step + 1 == nsteps)
    def _():
        wait(buf_a, sem_a)


def _combine(x2, info, gain, ys, d1, d2, ts):
    m, dm = x2.shape
    tm = 2 * ts
    gather_buf = pltpu.VMEM((2, ts, dm // 2), jnp.uint32)
    return pl.pallas_call(
        _combine_kernel,
        grid_spec=pltpu.PrefetchScalarGridSpec(
            num_scalar_prefetch=2,
            grid=(m // tm,),
            in_specs=[
                pl.BlockSpec((tm, dm), lambda i, *_: (i, 0)),
                pl.BlockSpec((tm, LANES), lambda i, *_: (i, 0)),
                pl.BlockSpec((1, dm), lambda i, *_: (0, 0)),
                pl.BlockSpec(memory_space=pl.ANY),
            ],
            out_specs=pl.BlockSpec((tm, dm), lambda i, *_: (i, 0)),
            scratch_shapes=[gather_buf, gather_buf, pltpu.SemaphoreType.DMA(()), pltpu.SemaphoreType.DMA(())],
        ),
        out_shape=jax.ShapeDtypeStruct((m, dm), F32),
        compiler_params=_params(("arbitrary",)),
        name="moe_combine",
    )(d1, d2, x2, info, gain.reshape(1, dm), ys)


def _plan_kernel(pstart_ref, infot_ref, d_ref):
    fields = infot_ref[...]
    expert = fields[0:2]
    start = jnp.zeros_like(expert)
    for k in range(N_EXPERTS):
        start = jnp.where(expert == float(k), pstart_ref[k].astype(F32), start)
    d_ref[...] = (start + fields[4:6]).astype(jnp.int32)


def _plan(pstart, infot):
    m = infot.shape[1]
    return pl.pallas_call(
        _plan_kernel,
        grid_spec=pltpu.PrefetchScalarGridSpec(
            num_scalar_prefetch=1,
            grid=(1,),
            in_specs=[pl.BlockSpec(infot.shape, lambda i, *_: (0, 0))],
            out_specs=pl.BlockSpec((2, m), lambda i, *_: (0, 0)),
        ),
        out_shape=jax.ShapeDtypeStruct((2, m), jnp.int32),
        compiler_params=_params(("arbitrary",)),
        name="moe_plan",
    )(pstart, infot)


def _moe_and_final_norm(x2, ffn_gain, w_group, b_group, w_router, b_router, w_gate, w_up, w_down, final_gain, tm):
    m, _ = x2.shape
    xn, info, infot, cnt = _router(x2, ffn_gain, w_group, b_group, w_router, b_router, tm)
    counts = cnt[0, :N_EXPERTS].astype(jnp.int32)
    padded = (counts + MOE_ROWS - 1) // MOE_ROWS * MOE_ROWS
    pend = jnp.cumsum(padded)
    pstart = pend - padded
    d1, d2 = _plan(pstart.astype(jnp.int32), infot)
    nblocks = 2 * m // MOE_ROWS + N_EXPERTS
    first_row = jnp.arange(nblocks, dtype=jnp.int32) * MOE_ROWS
    block_e = jnp.minimum(jnp.sum(pend[None, :] <= first_row[:, None], axis=1), N_EXPERTS - 1).astype(jnp.int32)
    nused = (pend[-1:] // MOE_ROWS).astype(jnp.int32)
    tail_block = jnp.concatenate([jnp.where(padded > 0, pend // MOE_ROWS - 1, -1), pend[-1:] // MOE_ROWS]).astype(jnp.int32)
    xs = _dispatch(xn, d1, d2, tail_block, nblocks * MOE_ROWS, tm)
    ys = _experts(xs, block_e, nused, w_gate, w_up, w_down)
    return _combine(x2, info, final_gain, ys, d1, d2, tm)


def kernel(x, attn_norm, w_in, hg_lb_logits, hg_out_norm, cmp_pos_k, cmp_w1_k, cmp_w2_k, cmp_pos_v, cmp_w1_v,
           cmp_w2_v, nsa_out_norm, w_out, ffn_norm, moe_w_group, moe_b_group, moe_w_router, moe_b_router,
           moe_w_gate, moe_w_up, moe_w_down, final_norm):
    bsz, seq, dm = x.shape
    xt = x.reshape(bsz * seq, dm)
    w = w_in[0]
    p0, p1 = 4 * HG_QK, 4 * HG_QK + NSA_WIDTH + NSA_KV
    tn = PROJ_COLS
    assert p0 % tn == 0 and p1 % tn == 0
    w_all = _column_blocks(jnp.pad(w.astype(BF16), ((0, 0), (0, (-w.shape[1]) % tn))), tn)
    w_prec = _column_blocks(jnp.stack(_split2(w[:, p0:p1])), tn)
    proj3 = _normed_matmul(xt, attn_norm[0], w_prec, w_all, p0 // tn, PROJ_ROWS).reshape(bsz, seq, -1)
    hg0 = p1 - p0
    y_hg = _hgrn(proj3, hg0, hg_lb_logits, hg_out_norm[0], HG_SEQ)
    y_nsa = _nsa(proj3, proj3, hg0 + 4 * HG_QK, hg0 + 4 * HG_QK + 5 * NSA_KV,
                 (cmp_pos_k[0], cmp_w1_k[0], cmp_w2_k[0]), (cmp_pos_v[0], cmp_w1_v[0], cmp_w2_v[0]))
    x2 = _out_proj(y_hg.reshape(bsz * seq, -1), y_nsa.reshape(bsz * seq, -1), nsa_out_norm[0], w_out[0], xt,
                   PROJ_ROWS, OUT_COLS)
    out = _moe_and_final_norm(x2, ffn_norm[0], moe_w_group[0], moe_b_group[0], moe_w_router[0], moe_b_router[0],
                              moe_w_gate[0], moe_w_up[0], moe_w_down[0], final_norm, MOE_TOKENS)
    return out.reshape(bsz, seq, dm)
```

```python
import functools

import jax
import jax.numpy as jnp
import numpy as np
from jax import lax
from jax.experimental import pallas as pl
from jax.experimental.pallas import tpu as pltpu

F32 = jnp.float32
BF16 = jnp.bfloat16

EPS = 1e-6
ROPE_THETA = 10000.0
NEG = -1e30
BIG = 1e9
LOG2E = 1.4426950408889634

HG_HEADS = 8
HG_DIM = 128
HG_QK = HG_HEADS * HG_DIM
HG_CHUNK = 64
HG_SUB = 8

NSA_HEADS = 16
NSA_GROUPS = 4
NSA_REP = NSA_HEADS // NSA_GROUPS
NSA_DIM = 64
NSA_WIDTH = NSA_HEADS * NSA_DIM
NSA_KV = NSA_GROUPS * NSA_DIM
NSA_VPAD = 16
NSA_GATE_ROWS = 16
CMP_BLOCK = 32
CMP_STRIDE = 16
CMP_HIDDEN = 256
SLC_BLOCK = 64
SLC_TOPK = 16
SLC_LOCAL = 2
WIN = 512

MOE_GROUPS = 4
MOE_EPG = 8
N_EXPERTS = MOE_GROUPS * MOE_EPG

LANES = 128
VMEM_LIMIT = 56 * 1024 * 1024

PROJ_ROWS = 1024
PROJ_COLS = 256
OUT_COLS = 1024
HG_SEQ = 1024
HG_UNROLL = 8
NSA_PREP_SEQ = 512
NSA_TQ = 256
NSA_TK = 256
MOE_TOKENS = 256


def _params(semantics, **kw):
    return pltpu.CompilerParams(dimension_semantics=semantics, vmem_limit_bytes=VMEM_LIMIT, **kw)


def _split2(a):
    hi = a.astype(BF16)
    return hi, (a - hi.astype(F32)).astype(BF16)


def _split3(a):
    hi = a.astype(BF16)
    r = a - hi.astype(F32)
    mid = r.astype(BF16)
    return hi, mid, (r - mid.astype(F32)).astype(BF16)


def _dot(a, b):
    return jnp.dot(a, b, preferred_element_type=F32)


def _dot_nt(a, b):
    return lax.dot_general(a, b, (((1,), (1,)), ((), ())), preferred_element_type=F32)


def _dot3(a, b):
    a_hi, a_lo = _split2(a)
    b_hi, b_lo = _split2(b)
    return _dot(a_hi, b_hi) + (_dot(a_hi, b_lo) + _dot(a_lo, b_hi))


def _rms(x, gain):
    return x * lax.rsqrt(jnp.mean(x * x, axis=-1, keepdims=True) + EPS) * gain


def _normed_matmul_kernel(x_ref, g_ref, wp_ref, w_ref, o_ref, h_ref, *, nprec):
    j = pl.program_id(1)

    @pl.when(j == 0)
    def _():
        y = _rms(x_ref[...], g_ref[...])
        hi = y.astype(BF16)
        h_ref[0] = hi
        h_ref[1] = (y - hi.astype(F32)).astype(BF16)

    @pl.when(j < nprec)
    def _():
        o_ref[...] = _dot(h_ref[0], wp_ref[0]) + (_dot(h_ref[0], wp_ref[1]) + _dot(h_ref[1], wp_ref[0]))

    @pl.when(j >= nprec)
    def _():
        o_ref[...] = _dot(h_ref[0], w_ref[...])


def _column_blocks(w, tn):
    *lead, k, n = w.shape
    return jnp.moveaxis(w.reshape(*lead, k, n // tn, tn), -2, -3)


def _normed_matmul(x, gain, wp_blocks, w_blocks, first, tm):
    m, k = x.shape
    nb, _, tn = w_blocks.shape
    nprec = wp_blocks.shape[1]

    def single_pass_block(j):
        r = jnp.maximum(j - nprec, 0)
        return jnp.where(r < first, r, r + nprec)

    return pl.pallas_call(
        functools.partial(_normed_matmul_kernel, nprec=nprec),
        grid=(m // tm, nb),
        in_specs=[
            pl.BlockSpec((tm, k), lambda i, j: (i, 0)),
            pl.BlockSpec((1, k), lambda i, j: (0, 0)),
            pl.BlockSpec((2, None, k, tn), lambda i, j: (0, jnp.minimum(j, nprec - 1), 0, 0)),
            pl.BlockSpec((None, k, tn), lambda i, j: (single_pass_block(j), 0, 0)),
        ],
        out_specs=pl.BlockSpec((tm, tn), lambda i, j: (i, j)),
        out_shape=jax.ShapeDtypeStruct((m, nb * tn), F32),
        scratch_shapes=[pltpu.VMEM((2, tm, k), BF16)],
        compiler_params=_params(("parallel", "arbitrary")),
        name="normed_matmul",
    )(x, gain.reshape(1, k), wp_blocks, w_blocks)


def _hgrn_consts():
    c, sub = HG_CHUNK, HG_SUB
    tri = np.tile(np.tril(np.ones((c, c), np.float32)), (1, 3))
    gsum = (np.arange(c * sub)[None, :] // sub == np.arange(c)[:, None]).astype(np.float32)
    return jnp.asarray(tri, BF16), jnp.asarray(gsum, BF16)


HG_PAR = 2


def _hgrn_kernel(q_ref, f_ref, i_ref, g_ref, lbl_ref, gain_ref, tri_ref, gsum_ref, o_ref, *scratch):
    c, sub, d = HG_CHUNK, HG_SUB, HG_DIM
    nsub = c // sub
    heads = range(HG_PAR)
    st_refs, p_refs, cl_refs, qs_refs, k_refs = (scratch[i * HG_PAR:(i + 1) * HG_PAR] for i in range(5))

    @pl.when(pl.program_id(2) == 0)
    def _():
        for h in heads:
            st_refs[h][...] = jnp.zeros_like(st_refs[h])

    l0 = lbl_ref[0:1, :]
    l1 = lbl_ref[1:2, :]
    lmax = jnp.maximum(l0, l1)
    e0 = jnp.exp(l0 - lmax)
    lb_all = e0 / (e0 + jnp.exp(l1 - lmax))
    srow = lax.broadcasted_iota(jnp.int32, (sub, d), 0)
    ones = jnp.ones((d, d), BF16)

    def rows_at(x, start):
        parts = ([jnp.zeros((start, d), F32)] if start else []) + [x]
        if start + x.shape[0] < c:
            parts.append(jnp.zeros((c - start - x.shape[0], d), F32))
        return jnp.concatenate(parts, axis=0)

    nchunks = q_ref.shape[1] // c
    cols = [slice(h * d, (h + 1) * d) for h in heads]

    def load(ci):
        rows = pl.ds(pl.multiple_of(ci * c, c), c)
        out = []
        for h in heads:
            lb = lb_all[:, cols[h]]
            f = lb + (1.0 - lb) * jax.nn.sigmoid(f_ref[0, rows, cols[h]])
            bcum = _dot(tri_ref[...], jnp.concatenate(_split3(jnp.log(f)), axis=0))
            out.append((q_ref[0, rows, cols[h]] * (d ** -0.5), 1.0 - f, i_ref[0, rows, cols[h]], bcum))
        return tuple(out)

    def chunk(ci, cur):
        nxt = load(jnp.minimum(ci + 1, nchunks - 1))
        rows = pl.ds(pl.multiple_of(ci * c, c), c)
        q, k, v, bcum = ([cur[h][i] for h in heads] for i in range(4))
        a_off, o_inter = [], []
        for h in heads:
            b = bcum[h]
            edge = [b[i * sub - 1:i * sub] for i in range(1, nsub + 1)]
            cl = b - jnp.concatenate([jnp.zeros((sub, d), F32)] + [jnp.broadcast_to(e, (sub, d)) for e in edge[:-1]],
                                     axis=0)
            cl_refs[h][...] = cl * LOG2E
            qs_refs[h][...] = q[h]
            k_refs[h][...] = k[h]
            qe = q[h] * jnp.exp(cl)
            qcat, kcat = [], []
            for i in range(1, nsub):
                qcat.append(rows_at(qe[i * sub:(i + 1) * sub], i * sub))
                kcat.append(rows_at(k[h][:i * sub] * jnp.exp(edge[i - 1] - b[:i * sub]), 0))
            a_off.append(_dot_nt(jnp.concatenate(qcat, axis=1).astype(BF16),
                                 jnp.concatenate(kcat, axis=1).astype(BF16)))
            st = st_refs[h][...]
            o_inter.append(_dot_nt((q[h] * jnp.exp(b)).astype(BF16), st.astype(BF16)))
            kd = k[h] * jnp.exp(edge[-1] - b)
            st_refs[h][...] = st * jnp.exp(edge[-1]) + _dot(v[h].T.astype(BF16), kd.astype(BF16))
        r2 = []
        for h in heads:
            for t in range(c):
                j0 = (t // sub) * sub
                dlt = cl_refs[h][t:t + 1, :] - cl_refs[h][j0:j0 + sub, :]
                e = jnp.exp2(jnp.where(srow <= t - j0, dlt, NEG))
                p_refs[h][t * sub:(t + 1) * sub, :] = (qs_refs[h][t:t + 1, :] * k_refs[h][j0:j0 + sub, :] * e).astype(BF16)
            r2.append(_dot(p_refs[h][...], ones))
        o = []
        for h in heads:
            o.append(o_inter[h] + _dot(a_off[h].astype(BF16), v[h].astype(BF16)))
        for h in heads:
            x = r2[h].reshape(nsub, sub, sub, d) * v[h].reshape(nsub, 1, sub, d)
            o[h] = o[h] + _dot(gsum_ref[...], x.reshape(c * sub, d).astype(BF16))
        for h in heads:
            gate = jax.nn.silu(g_ref[0, rows, cols[h]])
            o_ref[0, rows, cols[h]] = (_rms(o[h], gain_ref[...]) * gate).astype(o_ref.dtype)
        return nxt

    lax.fori_loop(0, nchunks, chunk, load(0), unroll=HG_UNROLL)


def _hgrn(proj3, col0, lb_logits, out_gain, tseq):
    bsz, seq, _ = proj3.shape
    d, c, sub = HG_DIM, HG_CHUNK, HG_SUB
    wst, gsum = _hgrn_consts()
    groups = HG_HEADS // HG_PAR
    width = HG_PAR * d
    assert col0 % width == 0

    def col(off):
        return pl.BlockSpec((1, tseq, width), lambda b, h, t: (b, t, col0 // width + off * groups + h))

    per_head = [pltpu.VMEM((d, d), F32), pltpu.VMEM((c * sub, d), BF16), pltpu.VMEM((c, d), F32),
                pltpu.VMEM((c, d), F32), pltpu.VMEM((c, d), F32)]
    return pl.pallas_call(
        _hgrn_kernel,
        grid=(bsz, groups, seq // tseq),
        in_specs=[
            col(0), col(1), col(2), col(3),
            pl.BlockSpec((2, width), lambda b, h, t: (0, h)),
            pl.BlockSpec((1, d), lambda b, h, t: (0, 0)),
            pl.BlockSpec(wst.shape, lambda b, h, t: (0, 0)),
            pl.BlockSpec(gsum.shape, lambda b, h, t: (0, 0)),
        ],
        out_specs=pl.BlockSpec((1, tseq, width), lambda b, h, t: (b, t, h)),
        out_shape=jax.ShapeDtypeStruct((bsz, seq, HG_QK), BF16),
        scratch_shapes=[s for s in per_head for _ in range(HG_PAR)],
        compiler_params=_params(("parallel", "parallel", "arbitrary")),
        name="hgrn2",
    )(proj3, proj3, proj3, proj3, lb_logits, out_gain.reshape(1, d), wst, gsum)


def _rope(x, cs, sn):
    lane = lax.broadcasted_iota(jnp.int32, x.shape, 1)
    partner = jnp.where(lane % NSA_DIM < NSA_DIM // 2, pltpu.roll(x, LANES - NSA_DIM // 2, 1),
                        pltpu.roll(x, NSA_DIM // 2, 1))
    return x * cs + partner * sn


def _nsa_prep_kernel(q_ref, ksl_ref, vsl_ref, kwn_ref, vwn_ref, gate_ref, cs_ref, sn_ref,
                     qrot_ref, kslo_ref, vslo_ref, kwno_ref, vwno_ref, gateo_ref):
    cs = cs_ref[...]
    sn = sn_ref[...]
    gate_t = gate_ref[0].T
    per_group = 3 * NSA_REP
    gateo_ref[...] = jnp.zeros_like(gateo_ref)
    for g in range(NSA_GROUPS):
        gateo_ref[0, g, :per_group] = gate_t[g * per_group:(g + 1) * per_group]
    for c in range(NSA_WIDTH // LANES):
        cols = slice(c * LANES, (c + 1) * LANES)
        qrot_ref[0, :, cols] = (_rope(q_ref[0, :, cols], cs, sn) * (NSA_DIM ** -0.5 * LOG2E)).astype(BF16)
    tseq = q_ref.shape[1]
    lane = lax.broadcasted_iota(jnp.int32, (tseq, LANES), 1)
    block = (pl.program_id(1) * tseq + lax.broadcasted_iota(jnp.int32, (tseq, LANES), 0)) // SLC_BLOCK
    block_onehot = jnp.where(lane - NSA_DIM == block, 1.0, 0.0)
    ones_rows = jnp.where(lax.broadcasted_iota(jnp.int32, (NSA_VPAD, tseq), 0) == 0, 1.0, 0.0).astype(BF16)
    for c in range(NSA_KV // LANES):
        cols = slice(c * LANES, (c + 1) * LANES)
        ks = _rope(ksl_ref[0, :, cols], cs, sn)
        kw = _rope(kwn_ref[0, :, cols], cs, sn).astype(BF16)
        vs = vsl_ref[0, :, cols].T.astype(BF16)
        vw = vwn_ref[0, :, cols].T.astype(BF16)
        for half in range(LANES // NSA_DIM):
            g = c * (LANES // NSA_DIM) + half
            hs = slice(half * NSA_DIM, (half + 1) * NSA_DIM)
            ks_g = ks if half == 0 else pltpu.roll(ks, NSA_DIM, 1)
            kslo_ref[0, g] = jnp.where(lane < NSA_DIM, ks_g, block_onehot).astype(BF16)
            kwno_ref[0, g] = kw[:, hs]
            vslo_ref[0, g, :NSA_DIM] = vs[hs, :]
            vslo_ref[0, g, NSA_DIM:] = ones_rows
            vwno_ref[0, g, :NSA_DIM] = vw[hs, :]
            vwno_ref[0, g, NSA_DIM:] = ones_rows


def _nsa_prep(prec3, rest3, kv_off, gate_off, tseq):
    assert gate_off % LANES == 0 and 3 * NSA_HEADS <= LANES
    bsz, seq, _ = prec3.shape
    half = NSA_DIM // 2
    inv = 1.0 / (ROPE_THETA ** (jnp.arange(0, NSA_DIM, 2, dtype=F32) / NSA_DIM))
    ang = jnp.arange(seq, dtype=F32)[:, None] * inv[None, :]
    cs = jnp.tile(jnp.cos(ang), (1, LANES // half))
    sn = jnp.tile(jnp.concatenate([-jnp.sin(ang), jnp.sin(ang)], axis=1), (1, LANES // NSA_DIM))
    kvb = kv_off // NSA_KV

    def kv_in(i):
        return pl.BlockSpec((1, tseq, NSA_KV), lambda b, t: (b, t, kvb + i))

    assert NSA_DIM + seq // SLC_BLOCK <= LANES

    def k_out(width):
        return (pl.BlockSpec((1, NSA_GROUPS, tseq, width), lambda b, t: (b, 0, t, 0)),
                jax.ShapeDtypeStruct((bsz, NSA_GROUPS, seq, width), BF16))

    (ksl_out, ksl_shape), (kwn_out, kwn_shape) = k_out(LANES), k_out(NSA_DIM)
    v_out = pl.BlockSpec((1, NSA_GROUPS, NSA_DIM + NSA_VPAD, tseq), lambda b, t: (b, 0, 0, t))
    v_shape = jax.ShapeDtypeStruct((bsz, NSA_GROUPS, NSA_DIM + NSA_VPAD, seq), BF16)
    tab = pl.BlockSpec((tseq, LANES), lambda b, t: (t, 0))
    return pl.pallas_call(
        _nsa_prep_kernel,
        grid=(bsz, seq // tseq),
        in_specs=[pl.BlockSpec((1, tseq, NSA_WIDTH), lambda b, t: (b, t, 0)), kv_in(0), kv_in(1), kv_in(2), kv_in(3),
                  pl.BlockSpec((1, tseq, LANES), lambda b, t: (b, t, gate_off // LANES)), tab, tab],
        out_specs=[pl.BlockSpec((1, tseq, NSA_WIDTH), lambda b, t: (b, t, 0)), ksl_out, v_out, kwn_out, v_out,
                   pl.BlockSpec((1, NSA_GROUPS, NSA_GATE_ROWS, tseq), lambda b, t: (b, 0, 0, t))],
        out_shape=[jax.ShapeDtypeStruct((bsz, seq, NSA_WIDTH), BF16), ksl_shape, v_shape, kwn_shape, v_shape,
                   jax.ShapeDtypeStruct((bsz, NSA_GROUPS, NSA_GATE_ROWS, seq), F32)],
        compiler_params=_params(("parallel", "parallel")),
        name="nsa_prep",
    )(prec3, rest3, rest3, rest3, rest3, rest3, cs, sn)


def _compress_kernel(*refs, precise):
    *kv_refs, pos_ref, w1_ref, w2_ref, o_ref = refs
    mm = _dot3 if precise else (lambda a, b: _dot(a.astype(BF16), b.astype(BF16)))
    nu = kv_refs[0].shape[1] // CMP_STRIDE
    per_tile = LANES // NSA_DIM
    for g in range(NSA_GROUPS):
        kv_ref, lanes = kv_refs[g // per_tile], slice((g % per_tile) * NSA_DIM, (g % per_tile + 1) * NSA_DIM)
        u = jnp.concatenate([kv_ref[0, pl.ds(l, nu, stride=CMP_STRIDE), :][:, lanes] for l in range(CMP_STRIDE)],
                            axis=1)
        ya = mm(u + pos_ref[0:1, :], w1_ref[0])
        yb = mm(u + pos_ref[1:2, :], w1_ref[1])
        hid = ya + pltpu.roll(yb, nu - 1, 0)
        o_ref[0, g] = mm(jax.nn.gelu(hid), w2_ref[...])


def _compress(proj3, col, pos, w1, w2, precise):
    bsz, seq, _ = proj3.shape
    nu = seq // CMP_STRIDE
    width = CMP_STRIDE * NSA_DIM
    assert col % LANES == 0
    tiles = NSA_KV // LANES
    return pl.pallas_call(
        functools.partial(_compress_kernel, precise=precise),
        grid=(bsz,),
        in_specs=[pl.BlockSpec((1, seq, LANES), lambda b, t=t: (b, 0, col // LANES + t)) for t in range(tiles)] + [
            pl.BlockSpec((2, width), lambda b: (0, 0)),
            pl.BlockSpec((2, width, CMP_HIDDEN), lambda b: (0, 0, 0)),
            pl.BlockSpec((CMP_HIDDEN, NSA_DIM), lambda b: (0, 0)),
        ],
        out_specs=pl.BlockSpec((1, NSA_GROUPS, nu, NSA_DIM), lambda b: (b, 0, 0, 0)),
        out_shape=jax.ShapeDtypeStruct((bsz, NSA_GROUPS, nu, NSA_DIM), F32),
        compiler_params=_params(("parallel",)),
        name="nsa_compress",
    )(*([proj3] * tiles), pos.reshape(2, width), w1.reshape(2, width, CMP_HIDDEN), w2)


def _nsa_attn_kernel(qraw_ref, qrot_ref, kc_ref, vct_ref, ksl_ref, vslt_ref, kwn_ref, vwnt_ref, gate_ref, aggt_ref,
                     o_ref, sa_ref, sb_ref, *win_refs, topk, tk):
    tq = qraw_ref.shape[1]
    nu = kc_ref.shape[2]
    ns = aggt_ref.shape[0]
    rep, dk = NSA_REP, NSA_DIM
    qs = pl.program_id(2) * tq
    tpos = qs + lax.broadcasted_iota(jnp.int32, (1, tq), 1)

    qrt = (qraw_ref[0] * (dk ** -0.5 * LOG2E)).T
    kc_hi, kc_lo = _split2(kc_ref[0, 0])
    vct = vct_ref[0, 0].astype(BF16)
    crow = lax.broadcasted_iota(jnp.int32, (nu, tq), 0)
    m_c = (crow * CMP_STRIDE + CMP_BLOCK - 1 <= tpos) & (crow < nu - 1)
    q_hi, q_lo = _split2(jnp.concatenate([qrt[r * dk:(r + 1) * dk] for r in range(rep)], axis=1))
    s_all = _dot(jnp.concatenate([kc_hi, kc_hi, kc_lo], axis=1),
                 jnp.concatenate([q_hi, q_lo, q_hi], axis=0))

    qt = qrot_ref[0].astype(F32).T.astype(BF16)
    qt_all = jnp.concatenate([qt[r * dk:(r + 1) * dk] for r in range(rep)], axis=1)
    hi = (qs + tq) // tk

    def key_tile(ktc):
        return pl.ds(pl.multiple_of(ktc * tk, tk), tk)

    def scores(k_ref, ktc, dst_ref):
        dst_ref[...] = _dot(k_ref[0, 0, key_tile(ktc), :], qt_all)

    win_tiles = [hi - len(win_refs) + j for j in range(len(win_refs))]
    for kt, dst_ref in zip(win_tiles, win_refs):
        scores(kwn_ref, jnp.maximum(kt, 0), dst_ref)
    psum = jnp.zeros((nu, tq), F32)
    p_all = []
    has_block = tpos >= CMP_BLOCK - 1
    for r in range(rep):
        s = jnp.where(m_c, s_all[:, r * tq:(r + 1) * tq], NEG)
        e = jnp.exp2(s - jnp.max(s, axis=0, keepdims=True))
        p = e * jnp.where(has_block, 1.0 / jnp.sum(e, axis=0, keepdims=True), 0.0)
        psum = psum + p
        p_all.append(p.astype(BF16))
    o_c_all = _dot(vct, jnp.concatenate(p_all, axis=1))
    o_c = [o_c_all[:, r * tq:(r + 1) * tq] for r in range(rep)]

    p_hi, p_lo = _split2(psum)
    imp = _dot(aggt_ref[...], p_hi) + _dot(aggt_ref[...], p_lo)
    jrow = lax.broadcasted_iota(jnp.int32, (ns, tq), 0)
    dj = jnp.right_shift(tpos, SLC_BLOCK.bit_length() - 1) - jrow
    forced = (jrow == 0) | ((dj >= 0) & (dj < SLC_LOCAL))
    imp = jnp.where(forced, BIG, jnp.where(jrow * SLC_BLOCK <= tpos, imp, -BIG))
    sub8 = lax.broadcasted_iota(jnp.int32, (8, tq), 0)
    chunks = [imp[c * 8:(c + 1) * 8] for c in range(ns // 8)]
    ranks = [jnp.zeros((8, tq), F32) for _ in range(ns // 8)]
    for jp in range(ns):
        row = chunks[jp // 8][jp % 8:jp % 8 + 1]
        for c in range(ns // 8):
            if c < jp // 8:
                ahead = jnp.where(row > chunks[c], 1.0, 0.0)
            elif c > jp // 8:
                ahead = jnp.where(row >= chunks[c], 1.0, 0.0)
            else:
                tie = jnp.where(sub8 > jp % 8, 1.0, 0.0)
                ahead = jnp.where(row > chunks[c], 1.0, jnp.where(row == chunks[c], tie, 0.0))
            ranks[c] = ranks[c] + ahead
    selt = [jnp.where(ranks[c] < topk, 0.0, NEG) for c in range(ns // 8)]

    def consume(vt_ref, ktc, src_ref, carry, mask=None):
        vt = vt_ref[0, 0, :, key_tile(ktc)]
        out = []
        for r in range(rep):
            m_old, acc = carry[r]
            s = src_ref[:, r * tq:(r + 1) * tq]
            if mask is not None:
                s = jnp.where(mask, s, NEG)
            m_new = jnp.maximum(m_old, jnp.max(s, axis=0, keepdims=True))
            alpha = jnp.exp2(m_old - m_new)
            p = jnp.exp2(s - m_new).astype(BF16)
            out.append((m_new, acc * alpha + _dot(vt, p)))
        return tuple(out)

    def normalised(carry):
        return [acc[:dk] * (1.0 / acc[dk:dk + 1]) for _, acc in carry]

    init = tuple((jnp.full((1, tq), NEG, F32), jnp.zeros((dk + NSA_VPAD, tq), F32)) for _ in range(rep))
    krow = lax.broadcasted_iota(jnp.int32, (tk, tq), 0)
    unseen = 1 << 30

    carry = init
    for kt, src_ref in zip(win_tiles, win_refs):
        ktc = jnp.maximum(kt, 0)
        dlt = tpos - (jnp.where(kt >= 0, ktc * tk, unseen) + krow)
        carry = consume(vwnt_ref, ktc, src_ref, carry, mask=pltpu.bitcast(dlt, jnp.uint32) < jnp.uint32(WIN))
    o_w = normalised(carry)

    selb = jnp.concatenate([jnp.concatenate(selt, axis=0)] * rep, axis=1).astype(BF16)
    pad = jnp.zeros((LANES - dk - ns, rep * tq), BF16)
    qt_sel = jnp.concatenate([qt_all, selb] + ([pad] if LANES > dk + ns else []), axis=0)

    def sel_scores(ktc, dst_ref):
        dst_ref[...] = _dot(ksl_ref[0, 0, key_tile(ktc), :], qt_sel)

    def pair(i, carry):
        kt = 2 * i
        sel_scores(kt + 1, sb_ref)
        carry = consume(vslt_ref, kt, sa_ref, carry)
        sel_scores(kt + 2, sa_ref)
        return consume(vslt_ref, kt + 1, sb_ref, carry)

    past = hi - 1
    sel_scores(0, sa_ref)
    carry = lax.fori_loop(0, past // 2, pair, init)
    carry = lax.cond(past % 2 == 1, lambda c: consume(vslt_ref, past - 1, sa_ref, c), lambda c: c, carry)
    sel_scores(past, sb_ref)
    carry = consume(vslt_ref, past, sb_ref, carry, mask=past * tk + krow <= tpos)
    o_s = normalised(carry)

    gate = jax.nn.sigmoid(gate_ref[0, 0])
    o_t = [gate[3 * r:3 * r + 1] * o_c[r] + gate[3 * r + 1:3 * r + 2] * o_s[r] + gate[3 * r + 2:3 * r + 3] * o_w[r]
           for r in range(rep)]
    o_ref[0] = jnp.concatenate(o_t, axis=0).T.astype(o_ref.dtype)


def _nsa_attn(prec3, qrot, kc, vct, ksl, vslt, kwn, vwnt, gates_t, tq, tk):
    bsz, seq, _ = qrot.shape
    nu = seq // CMP_STRIDE
    ns = seq // SLC_BLOCK
    ci = np.arange(nu)[None, :]
    sj = np.arange(ns)[:, None]
    overlap = (ci * CMP_STRIDE < (sj + 1) * SLC_BLOCK) & (ci * CMP_STRIDE + CMP_BLOCK > sj * SLC_BLOCK) & (ci < nu - 1)
    aggt = jnp.asarray(overlap, BF16)
    gw = NSA_REP * NSA_DIM
    assert tq % tk == 0 and seq % tq == 0
    win_tiles = -(-(WIN - 1) // tk) + tq // tk

    def q_spec():
        return pl.BlockSpec((1, tq, gw), lambda b, g, t: (b, t, g))

    def per_group(rows, cols):
        return pl.BlockSpec((1, 1, rows, cols), lambda b, g, t: (b, g, 0, 0))

    return pl.pallas_call(
        functools.partial(_nsa_attn_kernel, topk=min(SLC_TOPK, ns), tk=tk),
        grid=(bsz, NSA_GROUPS, seq // tq),
        in_specs=[q_spec(), q_spec(), per_group(nu, NSA_DIM), per_group(NSA_DIM, nu),
                  per_group(seq, LANES), per_group(NSA_DIM + NSA_VPAD, seq),
                  per_group(seq, NSA_DIM), per_group(NSA_DIM + NSA_VPAD, seq),
                  pl.BlockSpec((1, 1, NSA_GATE_ROWS, tq), lambda b, g, t: (b, g, 0, t)),
                  pl.BlockSpec((ns, nu), lambda b, g, t: (0, 0))],
        out_specs=q_spec(),
        out_shape=jax.ShapeDtypeStruct((bsz, seq, NSA_WIDTH), BF16),
        scratch_shapes=[pltpu.VMEM((tk, NSA_REP * tq), F32)] * (2 + win_tiles),
        compiler_params=_params(("parallel", "parallel", "arbitrary")),
        name="nsa_attention",
    )(prec3, qrot, kc, vct, ksl, vslt, kwn, vwnt, gates_t, aggt)


def _nsa(prec3, rest3, kv_off, gate_off, cmp_k, cmp_v):
    seq = prec3.shape[1]
    qrot, ksl, vslt, kwn, vwnt, gates_t = _nsa_prep(prec3, rest3, kv_off + NSA_KV, gate_off, min(seq, NSA_PREP_SEQ))
    kc = _compress(prec3, NSA_WIDTH, *cmp_k, precise=True)
    vct = _compress(rest3, kv_off, *cmp_v, precise=False).transpose(0, 1, 3, 2)
    return _nsa_attn(prec3, qrot, kc, vct, ksl, vslt, kwn, vwnt, gates_t, NSA_TQ, NSA_TK)


def _out_proj_kernel(yh_ref, yn_ref, g_ref, w_ref, x_ref, o_ref, y_ref):
    @pl.when(pl.program_id(1) == 0)
    def _():
        wh = yh_ref.shape[1]
        y_ref[:, :wh] = yh_ref[...]
        y_ref[:, wh:] = _rms(yn_ref[...].astype(F32), g_ref[...]).astype(BF16)

    o_ref[...] = x_ref[...] + _dot(y_ref[...], w_ref[...])


def _out_proj(y_hg, y_nsa, nsa_gain, w_out, x, tm, tn):
    m, dm = x.shape
    wh, wn = y_hg.shape[1], y_nsa.shape[1]
    return pl.pallas_call(
        _out_proj_kernel,
        grid=(m // tm, dm // tn),
        in_specs=[
            pl.BlockSpec((tm, wh), lambda i, j: (i, 0)),
            pl.BlockSpec((tm, wn), lambda i, j: (i, 0)),
            pl.BlockSpec((1, wn), lambda i, j: (0, 0)),
            pl.BlockSpec((None, wh + wn, tn), lambda i, j: (j, 0, 0)),
            pl.BlockSpec((tm, tn), lambda i, j: (i, j)),
        ],
        out_specs=pl.BlockSpec((tm, tn), lambda i, j: (i, j)),
        out_shape=jax.ShapeDtypeStruct((m, dm), F32),
        scratch_shapes=[pltpu.VMEM((tm, wh + wn), BF16)],
        compiler_params=_params(("parallel", "arbitrary")),
        name="out_proj",
    )(y_hg, y_nsa, nsa_gain.reshape(1, wn), _column_blocks(w_out.astype(BF16), tn), x)


MOE_ROWS = 256


def _pack_bf16_pairs(hi):
    n = hi.shape[1] // 2
    bits = pltpu.bitcast(hi.astype(F32), jnp.uint32)
    return jnp.right_shift(bits[:, :n], jnp.uint32(16)) | (bits[:, n:] & jnp.uint32(0xFFFF0000))


def _unpack_f32_pairs(words):
    lo = pltpu.bitcast(jnp.left_shift(words, jnp.uint32(16)), F32)
    hi = pltpu.bitcast(words & jnp.uint32(0xFFFF0000), F32)
    return lo, hi


def _unpack_bf16_pairs(words):
    lo, hi = _unpack_f32_pairs(words)
    return lo.astype(BF16), hi.astype(BF16)


def _router_kernel(x_ref, g_ref, w_ref, b_ref, tri_ref, xn_ref, info_ref, infot_ref, cnt_ref, carry_ref):
    @pl.when(pl.program_id(0) == 0)
    def _():
        carry_ref[...] = jnp.zeros_like(carry_ref)

    xn = _rms(x_ref[...], g_ref[...])
    hi, lo = _split2(xn)
    xn_ref[...] = _pack_bf16_pairs(hi)
    logits = _dot(hi, w_ref[0]) + (_dot(hi, w_ref[1]) + _dot(lo, w_ref[0])) + b_ref[...]
    lane = lax.broadcasted_iota(jnp.int32, logits.shape, 1).astype(F32)
    none = float(LANES)

    def first_max(mask):
        top = jnp.max(jnp.where(mask, logits, -jnp.inf), axis=-1, keepdims=True)
        return top, jnp.min(jnp.where(mask & (logits == top), lane, none), axis=-1, keepdims=True)

    is_g = lane < MOE_GROUPS
    gmax, gsel = first_max(is_g)
    gw = 1.0 / jnp.sum(jnp.where(is_g, jnp.exp(logits - gmax), 0.0), axis=-1, keepdims=True)
    lo_lane = MOE_GROUPS + gsel * MOE_EPG
    in_grp = (lane >= lo_lane) & (lane < lo_lane + MOE_EPG)
    v1, i1 = first_max(in_grp)
    v2, i2 = first_max(in_grp & (lane != i1))
    e = jnp.exp(v2 - v1)
    w1 = gw / (1.0 + e)
    w2 = gw * e / (1.0 + e)
    e1 = i1 - MOE_GROUPS
    e2 = i2 - MOE_GROUPS
    onehot = jnp.where((lane == e1) | (lane == e2), 1.0, 0.0)
    before = _dot(tri_ref[...], onehot.astype(BF16)) + carry_ref[...]
    r1 = jnp.sum(jnp.where(lane == e1, before, 0.0), axis=-1, keepdims=True)
    r2 = jnp.sum(jnp.where(lane == e2, before, 0.0), axis=-1, keepdims=True)
    carry_ref[...] = carry_ref[...] + jnp.sum(onehot, axis=0, keepdims=True)
    cnt_ref[...] = carry_ref[...]
    info = jnp.zeros_like(logits)
    for idx, val in enumerate((e1, e2, w1, w2, r1, r2)):
        info = jnp.where(lane == idx, val, info)
    info_ref[...] = info
    infot_ref[...] = info.T[:8]


def _router(x2, gain, w_group, b_group, w_router, b_router, tm):
    m, dm = x2.shape
    wcat = jnp.pad(jnp.concatenate([w_group, w_router], axis=1), ((0, 0), (0, LANES - MOE_GROUPS - N_EXPERTS)))
    bcat = jnp.pad(jnp.concatenate([b_group, b_router]), (0, LANES - MOE_GROUPS - N_EXPERTS)).reshape(1, LANES)
    tri = jnp.asarray(np.tril(np.ones((tm, tm), np.float32), -1), BF16)
    return pl.pallas_call(
        _router_kernel,
        grid=(m // tm,),
        in_specs=[
            pl.BlockSpec((tm, dm), lambda i: (i, 0)),
            pl.BlockSpec((1, dm), lambda i: (0, 0)),
            pl.BlockSpec((2, dm, LANES), lambda i: (0, 0, 0)),
            pl.BlockSpec((1, LANES), lambda i: (0, 0)),
            pl.BlockSpec((tm, tm), lambda i: (0, 0)),
        ],
        out_specs=[
            pl.BlockSpec((tm, dm // 2), lambda i: (i, 0)),
            pl.BlockSpec((tm, LANES), lambda i: (i, 0)),
            pl.BlockSpec((8, tm), lambda i: (0, i)),
            pl.BlockSpec((1, LANES), lambda i: (0, 0)),
        ],
        out_shape=[
            jax.ShapeDtypeStruct((m, dm // 2), jnp.uint32),
            jax.ShapeDtypeStruct((m, LANES), F32),
            jax.ShapeDtypeStruct((8, m), F32),
            jax.ShapeDtypeStruct((1, LANES), F32),
        ],
        scratch_shapes=[pltpu.VMEM((1, LANES), F32)],
        compiler_params=_params(("arbitrary",)),
        name="moe_router",
    )(x2, gain.reshape(1, dm), jnp.stack(_split2(wcat)), bcat, tri)


def _dispatch_kernel(d1_ref, d2_ref, tail_ref, xn_ref, xs_ref, zero_ref, sem, zero_sem):
    tm = xn_ref.shape[0]
    base = pl.program_id(0) * tm

    @pl.when(pl.program_id(0) == 0)
    def _():
        zero_ref[...] = jnp.zeros_like(zero_ref)

        def fill(e):
            rows = zero_ref.shape[0]
            return pltpu.make_async_copy(zero_ref, xs_ref.at[pl.ds(jnp.maximum(tail_ref[e], 0) * rows, rows)], zero_sem)

        for e in range(N_EXPERTS):
            @pl.when(tail_ref[e] >= 0)
            def _():
                fill(e).start()

        def fill_unused(blk):
            rows = zero_ref.shape[0]
            return pltpu.make_async_copy(zero_ref, xs_ref.at[pl.ds(blk * rows, rows)], zero_sem)

        unused = (tail_ref[N_EXPERTS], xs_ref.shape[0] // zero_ref.shape[0])
        lax.fori_loop(*unused, lambda blk, c: (fill_unused(blk).start(), c)[1], 0)
        lax.fori_loop(*unused, lambda blk, c: (fill_unused(blk).wait(), c)[1], 0)
        for e in range(N_EXPERTS):
            @pl.when(tail_ref[e] >= 0)
            def _():
                fill(e).wait()

    def row_copy(i, dest):
        return pltpu.make_async_copy(xn_ref.at[pl.ds(i, 1)], xs_ref.at[pl.ds(dest, 1)], sem)

    for i in range(tm):
        row_copy(i, d1_ref[base + i]).start()
        row_copy(i, d2_ref[base + i]).start()
    for _ in range(2):
        pltpu.make_async_copy(xn_ref, xs_ref.at[pl.ds(0, tm)], sem).wait()


def _dispatch(xn, d1, d2, tail_block, rows, tm):
    m, words = xn.shape
    return pl.pallas_call(
        _dispatch_kernel,
        grid_spec=pltpu.PrefetchScalarGridSpec(
            num_scalar_prefetch=3,
            grid=(m // tm,),
            in_specs=[pl.BlockSpec((tm, words), lambda i, *_: (i, 0))],
            out_specs=pl.BlockSpec(memory_space=pl.ANY),
            scratch_shapes=[pltpu.VMEM((MOE_ROWS, words), jnp.uint32), pltpu.SemaphoreType.DMA(()),
                            pltpu.SemaphoreType.DMA(())],
        ),
        out_shape=jax.ShapeDtypeStruct((rows, words), jnp.uint32),
        compiler_params=_params(("arbitrary",)),
        name="moe_dispatch",
    )(d1, d2, tail_block, xn)


def _expert_kernel(be_ref, nused_ref, x_ref, wg_ref, wu_ref, wd_ref, o_ref, wgb_ref, wub_ref, wdb_ref):
    i = pl.program_id(0)

    @pl.when((i == 0) | (be_ref[i] != be_ref[jnp.maximum(i - 1, 0)]))
    def _():
        wgb_ref[...] = wg_ref[0].astype(BF16)
        wub_ref[...] = wu_ref[0].astype(BF16)
        wdb_ref[...] = wd_ref[0].astype(BF16)

    @pl.when(i < nused_ref[0])
    def _():
        half = wgb_ref.shape[0] // 2
        x_lo, x_hi = _unpack_bf16_pairs(x_ref[...])

        def up(w_ref):
            return _dot(x_lo, w_ref[:half]) + _dot(x_hi, w_ref[half:])

        hid = (jax.nn.silu(up(wgb_ref)) * up(wub_ref)).astype(BF16)
        for g in range(o_ref.shape[1] // LANES):
            y = _dot(hid, wdb_ref[:, g * 2 * LANES:(g + 1) * 2 * LANES])
            o_ref[:, g * LANES:(g + 1) * LANES] = _pack_bf16_pairs(y.astype(BF16))

    @pl.when(i >= nused_ref[0])
    def _():
        o_ref[...] = jnp.zeros_like(o_ref)


def _experts(xs, block_e, nused, w_gate, w_up, w_down):
    rows, words = xs.shape
    _, dm, ff = w_gate.shape
    return pl.pallas_call(
        _expert_kernel,
        grid_spec=pltpu.PrefetchScalarGridSpec(
            num_scalar_prefetch=2,
            grid=(rows // MOE_ROWS,),
            in_specs=[
                pl.BlockSpec((MOE_ROWS, words), lambda i, be, nu: (i, 0)),
                pl.BlockSpec((1, dm, ff), lambda i, be, nu: (be[i], 0, 0)),
                pl.BlockSpec((1, dm, ff), lambda i, be, nu: (be[i], 0, 0)),
                pl.BlockSpec((1, ff, dm), lambda i, be, nu: (be[i], 0, 0)),
            ],
            out_specs=pl.BlockSpec((MOE_ROWS, words), lambda i, be, nu: (i, 0)),
            scratch_shapes=[pltpu.VMEM((dm, ff), BF16), pltpu.VMEM((dm, ff), BF16), pltpu.VMEM((ff, dm), BF16)],
        ),
        out_shape=jax.ShapeDtypeStruct((rows, words), jnp.uint32),
        compiler_params=_params(("arbitrary",)),
        name="moe_experts",
    )(block_e, nused, xs, w_gate, w_up, w_down)


def _combine_kernel(d1_ref, d2_ref, x_ref, info_ref, g_ref, ys_ref, o_ref, buf_a, buf_b, sem_a, sem_b):
    ts = buf_a.shape[1]
    dm = x_ref.shape[1]
    step, nsteps = pl.program_id(0), pl.num_programs(0)
    base = step * 2 * ts

    def row_copy(buf_ref, sem, r, slot, src):
        return pltpu.make_async_copy(ys_ref.at[pl.ds(src, 1)], buf_ref.at[slot, pl.ds(r, 1)], sem)

    def issue(buf_ref, sem, tok0):
        for r in range(ts):
            row_copy(buf_ref, sem, r, 0, d1_ref[tok0 + r]).start()
            row_copy(buf_ref, sem, r, 1, d2_ref[tok0 + r]).start()

    def wait(buf_ref, sem):
        for slot in range(2):
            pltpu.make_async_copy(ys_ref.at[pl.ds(0, ts)], buf_ref.at[slot], sem).wait()

    def finish(buf_ref, rows):
        info = info_ref[rows, :]
        w1, w2 = info[:, 2:3], info[:, 3:4]
        ssq = jnp.zeros((ts, 1), F32)
        for g in range(dm // (2 * LANES)):
            a = _unpack_f32_pairs(buf_ref[0, :, g * LANES:(g + 1) * LANES])
            b = _unpack_f32_pairs(buf_ref[1, :, g * LANES:(g + 1) * LANES])
            for half in range(2):
                cols = slice((2 * g + half) * LANES, (2 * g + half + 1) * LANES)
                y = x_ref[rows, cols] + (w1 * a[half] + w2 * b[half])
                ssq = ssq + jnp.sum(y * y, axis=-1, keepdims=True)
                o_ref[rows, cols] = y
        o_ref[rows, :] = o_ref[rows, :] * lax.rsqrt(ssq / dm + EPS) * g_ref[...]

    @pl.when(step == 0)
    def _():
        issue(buf_a, sem_a, base)

    wait(buf_a, sem_a)
    issue(buf_b, sem_b, base + ts)
    finish(buf_a, slice(0, ts))
    wait(buf_b, sem_b)
    issue(buf_a, sem_a, jnp.where(step + 1 < nsteps, base + 2 * ts, base))
    finish(buf_b, slice(ts, 2 * ts))

    @pl.when(step + 1 == nsteps)
    def _():
        wait(buf_a, sem_a)


def _combine(x2, info, gain, ys, d1, d2, ts):
    m, dm = x2.shape
    tm = 2 * ts
    gather_buf = pltpu.VMEM((2, ts, dm // 2), jnp.uint32)
    return pl.pallas_call(
        _combine_kernel,
        grid_spec=pltpu.PrefetchScalarGridSpec(
            num_scalar_prefetch=2,
            grid=(m // tm,),
            in_specs=[
                pl.BlockSpec((tm, dm), lambda i, *_: (i, 0)),
                pl.BlockSpec((tm, LANES), lambda i, *_: (i, 0)),
                pl.BlockSpec((1, dm), lambda i, *_: (0, 0)),
                pl.BlockSpec(memory_space=pl.ANY),
            ],
            out_specs=pl.BlockSpec((tm, dm), lambda i, *_: (i, 0)),
            scratch_shapes=[gather_buf, gather_buf, pltpu.SemaphoreType.DMA(()), pltpu.SemaphoreType.DMA(())],
        ),
        out_shape=jax.ShapeDtypeStruct((m, dm), F32),
        compiler_params=_params(("arbitrary",)),
        name="moe_combine",
    )(d1, d2, x2, info, gain.reshape(1, dm), ys)


def _plan_kernel(pstart_ref, infot_ref, d_ref):
    fields = infot_ref[...]
    expert = fields[0:2]
    start = jnp.zeros_like(expert)
    for k in range(N_EXPERTS):
        start = jnp.where(expert == float(k), pstart_ref[k].astype(F32), start)
    d_ref[...] = (start + fields[4:6]).astype(jnp.int32)


def _plan(pstart, infot):
    m = infot.shape[1]
    return pl.pallas_call(
        _plan_kernel,
        grid_spec=pltpu.PrefetchScalarGridSpec(
            num_scalar_prefetch=1,
            grid=(1,),
            in_specs=[pl.BlockSpec(infot.shape, lambda i, *_: (0, 0))],
            out_specs=pl.BlockSpec((2, m), lambda i, *_: (0, 0)),
        ),
        out_shape=jax.ShapeDtypeStruct((2, m), jnp.int32),
        compiler_params=_params(("arbitrary",)),
        name="moe_plan",
    )(pstart, infot)


def _moe_and_final_norm(x2, ffn_gain, w_group, b_group, w_router, b_router, w_gate, w_up, w_down, final_gain, tm):
    m, _ = x2.shape
    xn, info, infot, cnt = _router(x2, ffn_gain, w_group, b_group, w_router, b_router, tm)
    counts = cnt[0, :N_EXPERTS].astype(jnp.int32)
    padded = (counts + MOE_ROWS - 1) // MOE_ROWS * MOE_ROWS
    pend = jnp.cumsum(padded)
    pstart = pend - padded
    d1, d2 = _plan(pstart.astype(jnp.int32), infot)
    nblocks = 2 * m // MOE_ROWS + N_EXPERTS
    first_row = jnp.arange(nblocks, dtype=jnp.int32) * MOE_ROWS
    block_e = jnp.minimum(jnp.sum(pend[None, :] <= first_row[:, None], axis=1), N_EXPERTS - 1).astype(jnp.int32)
    nused = (pend[-1:] // MOE_ROWS).astype(jnp.int32)
    tail_block = jnp.concatenate([jnp.where(padded > 0, pend // MOE_ROWS - 1, -1), pend[-1:] // MOE_ROWS]).astype(jnp.int32)
    xs = _dispatch(xn, d1, d2, tail_block, nblocks * MOE_ROWS, tm)
    ys = _experts(xs, block_e, nused, w_gate, w_up, w_down)
    return _combine(x2, info, final_gain, ys, d1, d2, tm)


def kernel(x, attn_norm, w_in, hg_lb_logits, hg_out_norm, cmp_pos_k, cmp_w1_k, cmp_w2_k, cmp_pos_v, cmp_w1_v,
           cmp_w2_v, nsa_out_norm, w_out, ffn_norm, moe_w_group, moe_b_group, moe_w_router, moe_b_router,
           moe_w_gate, moe_w_up, moe_w_down, final_norm):
    bsz, seq, dm = x.shape
    xt = x.reshape(bsz * seq, dm)
    w = w_in[0]
    p0, p1 = 4 * HG_QK, 4 * HG_QK + NSA_WIDTH + NSA_KV
    tn = PROJ_COLS
    assert p0 % tn == 0 and p1 % tn == 0
    w_all = _column_blocks(jnp.pad(w.astype(BF16), ((0, 0), (0, (-w.shape[1]) % tn))), tn)
    w_prec = _column_blocks(jnp.stack(_split2(w[:, p0:p1])), tn)
    proj3 = _normed_matmul(xt, attn_norm[0], w_prec, w_all, p0 // tn, PROJ_ROWS).reshape(bsz, seq, -1)
    hg0 = p1 - p0
    y_hg = _hgrn(proj3, hg0, hg_lb_logits, hg_out_norm[0], HG_SEQ)
    y_nsa = _nsa(proj3, proj3, hg0 + 4 * HG_QK, hg0 + 4 * HG_QK + 5 * NSA_KV,
                 (cmp_pos_k[0], cmp_w1_k[0], cmp_w2_k[0]), (cmp_pos_v[0], cmp_w1_v[0], cmp_w2_v[0]))
    x2 = _out_proj(y_hg.reshape(bsz * seq, -1), y_nsa.reshape(bsz * seq, -1), nsa_out_norm[0], w_out[0], xt,
                   PROJ_ROWS, OUT_COLS)
    out = _moe_and_final_norm(x2, ffn_norm[0], moe_w_group[0], moe_b_group[0], moe_w_router[0], moe_b_router[0],
                              moe_w_gate[0], moe_w_up[0], moe_w_down[0], final_norm, MOE_TOKENS)
    return out.reshape(bsz, seq, dm)
```
